```python
import math
import jax, jax.numpy as jnp
from jax import lax
import numpy as np

D_MODEL = 1024
BATCH = 8
SEQ = 4096
DEPTH = 1

GRID_W = 64
CTX_LEN = 256
EPS = 1e-6
RG_WIDTH = D_MODEL
RG_BLOCKS = 16
RG_BLOCK = RG_WIDTH // RG_BLOCKS
RG_C = 8.0
CONV_W = 4
CONV_PAD_LEFT = 2
ML_WIDTH = D_MODEL
ML_HEADS = 4
ML_HEAD_DIM = ML_WIDTH // ML_HEADS
ML_QKV_BLOCK = 4
ML_CHUNK = 64
N_IN = 2 * RG_WIDTH + 2 * ML_WIDTH + 2 * D_MODEL
D_FF = int(math.ceil(8 * D_MODEL / 3 / 256)) * 256

kernel_name = "hybrid_rglru_mlstm_dit_block"


def _rmsnorm(x, g):
    xf = x.astype(jnp.float32)
    y = xf * lax.rsqrt(jnp.mean(xf * xf, axis=-1, keepdims=True) + EPS)
    return (y * g.astype(jnp.float32)).astype(x.dtype)


def _modulate(x, shift, scale):
    return x * (1 + scale) + shift


def _conv_centred(x, w, b):
    L = x.shape[1]
    xp = jnp.pad(x, ((0, 0), (CONV_PAD_LEFT, CONV_W - 1 - CONV_PAD_LEFT), (0, 0)))
    y = b
    for j in range(CONV_W):
        y = y + w[j] * xp[:, j:j + L]
    return y


def _block_diag(x, w):
    nb, bi, bo = w.shape
    xs = x.reshape(x.shape[:-1] + (nb, bi))
    return jnp.einsum('...ni,nio->...no', xs, w).reshape(x.shape[:-1] + (nb * bo,))


def _to_col_major(t, rows):
    B, L, C = t.shape
    return t.reshape(B, rows, GRID_W, C).transpose(0, 2, 1, 3).reshape(B, L, C)


def _to_row_major(t, rows):
    B, L, C = t.shape
    return t.reshape(B, GRID_W, rows, C).transpose(0, 2, 1, 3).reshape(B, L, C)


def _rglru_coeffs(xc, wa, ba, wx, bx, lam):
    r = jax.nn.sigmoid(_block_diag(xc, wa) + ba).astype(jnp.float32)
    i = jax.nn.sigmoid(_block_diag(xc, wx) + bx).astype(jnp.float32)
    log_a = -RG_C * jax.nn.softplus(-lam.astype(jnp.float32)) * r
    a = jnp.exp(log_a)
    b = jnp.sqrt(-jnp.expm1(2.0 * log_a)) * (i * xc.astype(jnp.float32))
    return a, b


def _linear_scan(a, b, h0):
    def comb(l, r):
        return l[0] * r[0], r[0] * l[1] + r[1]
    A, H = lax.associative_scan(comb, (a, b), axis=1)
    return H + A * h0[:, None, :]


def _rglru_bidir(xc_lat, xc_ctx, wa, ba, wx, bx, lam):
    outs_l, outs_c = [], []
    for d in range(2):
        rev = d == 1
        seq_c = jnp.flip(xc_ctx, 1) if rev else xc_ctx
        seq_l = jnp.flip(xc_lat, 1) if rev else xc_lat
        a_c, b_c = _rglru_coeffs(seq_c, wa[d], ba[d], wx[d], bx[d], lam[d])
        h_c = _linear_scan(a_c, b_c, jnp.zeros_like(b_c[:, 0]))
        a_l, b_l = _rglru_coeffs(seq_l, wa[d], ba[d], wx[d], bx[d], lam[d])
        h_l = _linear_scan(a_l, b_l, h_c[:, -1])
        if rev:
            h_c, h_l = jnp.flip(h_c, 1), jnp.flip(h_l, 1)
        outs_c.append(h_c)
        outs_l.append(h_l)
    return outs_l[0] + outs_l[1], outs_c[0] + outs_c[1]


def _mlstm_inputs(u, conv_w, conv_b, wq, wk, wv):
    xc = jax.nn.silu(_conv_centred(u, conv_w, conv_b))
    q = _block_diag(xc, wq)
    k = _block_diag(xc, wk) * (ML_HEAD_DIM ** -0.5)
    v = _block_diag(u, wv)
    gin = jnp.concatenate([q, k, v], axis=-1)
    return xc, q, k, v, gin


def _heads(t):
    B, L, _ = t.shape
    return t.astype(jnp.float32).reshape(B, L, ML_HEADS, ML_HEAD_DIM).transpose(0, 2, 1, 3)


def _mlstm_gates(gin, wi, bi, wf, bf):
    log_i = (gin @ wi + bi).astype(jnp.float32)
    log_f = jax.nn.log_sigmoid((gin @ wf + bf).astype(jnp.float32))
    return log_i.transpose(0, 2, 1), log_f.transpose(0, 2, 1)


def _mlstm_chunkwise(q, k, v, log_i, log_f, state):
    B, H, L, dh = q.shape
    nc = L // ML_CHUNK
    to_c = lambda t: jnp.moveaxis(t.reshape((B, H, nc, ML_CHUNK) + t.shape[3:]), 2, 0)
    causal = jnp.tril(jnp.ones((ML_CHUNK, ML_CHUNK), dtype=bool))

    def step(carry, inp):
        C, n, m = carry
        qc, kc, vc, lic, lfc = inp
        bcum = jnp.cumsum(lfc, axis=-1)
        dmat = bcum[..., :, None] - bcum[..., None, :] + lic[..., None, :]
        dmat = jnp.where(causal, dmat, -jnp.inf)
        inter = bcum + m[..., None]
        m_t = jnp.maximum(inter, jnp.max(dmat, axis=-1))
        wmat = jnp.exp(dmat - m_t[..., None])
        s_inter = jnp.exp(inter - m_t)
        s = jnp.einsum('bhtd,bhsd->bhts', qc, kc) * wmat
        num = jnp.einsum('bhts,bhsv->bhtv', s, vc) + s_inter[..., None] * jnp.einsum('bhvk,bhtk->bhtv', C, qc)
        den = jnp.sum(s, axis=-1) + s_inter * jnp.einsum('bhk,bhtk->bht', n, qc)
        h = num / jnp.maximum(jnp.abs(den), jnp.exp(-m_t))[..., None]
        bL = bcum[..., -1]
        g = bL[..., None] - bcum + lic
        m_new = jnp.maximum(bL + m, jnp.max(g, axis=-1))
        decay = jnp.exp(bL + m - m_new)
        ws = jnp.exp(g - m_new[..., None])
        C_new = decay[..., None, None] * C + jnp.einsum('bhs,bhsv,bhsk->bhvk', ws, vc, kc)
        n_new = decay[..., None] * n + jnp.einsum('bhs,bhsk->bhk', ws, kc)
        return (C_new, n_new, m_new), h

    state, hs = lax.scan(step, state, (to_c(q), to_c(k), to_c(v), to_c(log_i), to_c(log_f)))
    h = jnp.moveaxis(hs, 0, 2).reshape(B, H, L, dh)
    return h, state


def _mlstm_bidir(q_l, k_l, v_l, gin_l, q_c, k_c, v_c, gin_c, wi, bi, wf, bf):
    B = q_l.shape[0]
    outs_l, outs_c = [], []
    for d in range(2):
        rev = d == 1
        fl = (lambda t: jnp.flip(t, axis=2)) if rev else (lambda t: t)
        li_c, lf_c = _mlstm_gates(gin_c, wi[d], bi[d], wf[d], bf[d])
        li_l, lf_l = _mlstm_gates(gin_l, wi[d], bi[d], wf[d], bf[d])
        st0 = (jnp.zeros((B, ML_HEADS, ML_HEAD_DIM, ML_HEAD_DIM), jnp.float32),
               jnp.zeros((B, ML_HEADS, ML_HEAD_DIM), jnp.float32),
               jnp.zeros((B, ML_HEADS), jnp.float32))
        h_c, st = _mlstm_chunkwise(fl(q_c), fl(k_c), fl(v_c), fl(li_c), fl(lf_c), st0)
        h_l, _ = _mlstm_chunkwise(fl(q_l), fl(k_l), fl(v_l), fl(li_l), fl(lf_l), st)
        outs_c.append(fl(h_c))
        outs_l.append(fl(h_l))
    return outs_l[0] + outs_l[1], outs_c[0] + outs_c[1]


def _mlstm_out(h, xc, norm_g, skip):
    mu = jnp.mean(h, axis=-1, keepdims=True)
    var = jnp.mean(jnp.square(h - mu), axis=-1, keepdims=True)
    hn = (h - mu) * lax.rsqrt(var + EPS)
    B, H, L, dh = h.shape
    hn = hn.transpose(0, 2, 1, 3).reshape(B, L, H * dh) * norm_g.astype(jnp.float32)
    return (hn + skip.astype(jnp.float32) * xc.astype(jnp.float32)).astype(xc.dtype)


def _merge(y_rg, y_ml, g_rg, g_ml, w_branch_rg, w_branch_ml, w_out):
    mix = jax.nn.sigmoid(g_rg) * (y_rg @ w_branch_rg) + jax.nn.sigmoid(g_ml) * (y_ml @ w_branch_ml)
    return mix @ w_out


def _mixer(u_lat, u_ctx, need_ctx, w_in, rg_conv_w, rg_conv_b, rg_wa, rg_ba, rg_wx, rg_bx, rg_lambda,
           ml_conv_w, ml_conv_b, ml_wq, ml_wk, ml_wv, ml_wi, ml_bi, ml_wf, ml_bf, ml_norm_g, ml_skip,
           w_branch_rg, w_branch_ml, w_out):
    rows = u_lat.shape[1] // GRID_W
    splits = [RG_WIDTH, 2 * RG_WIDTH, 2 * RG_WIDTH + ML_WIDTH, 2 * RG_WIDTH + 2 * ML_WIDTH,
              2 * RG_WIDTH + 2 * ML_WIDTH + D_MODEL]
    rgx_l, rgg_l, mlx_l, mlo_l, gr_l, gm_l = jnp.split(u_lat @ w_in, splits, axis=-1)
    rgx_c, rgg_c, mlx_c, mlo_c, gr_c, gm_c = jnp.split(u_ctx @ w_in, splits, axis=-1)

    xc_l = _conv_centred(rgx_l, rg_conv_w, rg_conv_b)
    xc_c = _conv_centred(rgx_c, rg_conv_w, rg_conv_b)
    h_rg_l, h_rg_c = _rglru_bidir(xc_l, xc_c, rg_wa, rg_ba, rg_wx, rg_bx, rg_lambda)
    y_rg_l = h_rg_l.astype(u_lat.dtype) * jax.nn.gelu(rgg_l)

    xm_l, q_l, k_l, v_l, gin_l = _mlstm_inputs(_to_col_major(mlx_l, rows), ml_conv_w, ml_conv_b, ml_wq, ml_wk, ml_wv)
    xm_c, q_c, k_c, v_c, gin_c = _mlstm_inputs(mlx_c, ml_conv_w, ml_conv_b, ml_wq, ml_wk, ml_wv)
    h_ml_l, h_ml_c = _mlstm_bidir(_heads(q_l), _heads(k_l), _heads(v_l), gin_l,
                                  _heads(q_c), _heads(k_c), _heads(v_c), gin_c,
                                  ml_wi, ml_bi, ml_wf, ml_bf)
    y_ml_l = _to_row_major(_mlstm_out(h_ml_l, xm_l, ml_norm_g, ml_skip), rows) * jax.nn.sigmoid(mlo_l)

    y_lat = _merge(y_rg_l, y_ml_l, gr_l, gm_l, w_branch_rg, w_branch_ml, w_out)
    y_ctx = None
    if need_ctx:
        y_rg_c = h_rg_c.astype(u_ctx.dtype) * jax.nn.gelu(rgg_c)
        y_ml_c = _mlstm_out(h_ml_c, xm_c, ml_norm_g, ml_skip) * jax.nn.sigmoid(mlo_c)
        y_ctx = _merge(y_rg_c, y_ml_c, gr_c, gm_c, w_branch_rg, w_branch_ml, w_out)
    return y_lat, y_ctx


def _swiglu(h, w_ffn_in, w_ffn_out):
    gate, up = jnp.split(h @ w_ffn_in, 2, axis=-1)
    return (jax.nn.silu(gate) * up) @ w_ffn_out


def setup_inputs(seed: int = 0) -> dict:
    key = jax.random.key(seed)
    ks = iter(jax.random.split(key, 48))
    nrm = lambda shape, s: jax.random.normal(next(ks), shape, jnp.float32) * s
    L = DEPTH
    u = jax.random.uniform(next(ks), (L, 2, RG_WIDTH), jnp.float32, minval=0.9, maxval=0.999)
    a0 = u ** (1.0 / RG_C)
    return {
        "x": nrm((BATCH, SEQ, D_MODEL), 1.0),
        "c": nrm((BATCH, D_MODEL), 1.0),
        "ctx": nrm((BATCH, CTX_LEN, D_MODEL), 1.0),
        "c_ctx": nrm((D_MODEL,), 1.0),
        "w_mod": nrm((L, D_MODEL, 6 * D_MODEL), 0.5 * D_MODEL ** -0.5),
        "b_mod": nrm((L, 6 * D_MODEL), 0.02),
        "norm1_g": 1.0 + nrm((L, D_MODEL), 0.02),
        "norm2_g": 1.0 + nrm((L, D_MODEL), 0.02),
        "w_in": nrm((L, D_MODEL, N_IN), D_MODEL ** -0.5),
        "rg_conv_w": nrm((L, CONV_W, RG_WIDTH), CONV_W ** -0.5),
        "rg_conv_b": nrm((L, RG_WIDTH), 0.02),
        "rg_wa": nrm((L, 2, RG_BLOCKS, RG_BLOCK, RG_BLOCK), RG_BLOCK ** -0.5),
        "rg_ba": nrm((L, 2, RG_WIDTH), 0.02),
        "rg_wx": nrm((L, 2, RG_BLOCKS, RG_BLOCK, RG_BLOCK), RG_BLOCK ** -0.5),
        "rg_bx": nrm((L, 2, RG_WIDTH), 0.02),
        "rg_lambda": jnp.log(a0) - jnp.log1p(-a0),
        "ml_conv_w": nrm((L, CONV_W, ML_WIDTH), CONV_W ** -0.5),
        "ml_conv_b": nrm((L, ML_WIDTH), 0.02),
        "ml_wq": nrm((L, ML_WIDTH // ML_QKV_BLOCK, ML_QKV_BLOCK, ML_QKV_BLOCK), ML_QKV_BLOCK ** -0.5),
        "ml_wk": nrm((L, ML_WIDTH // ML_QKV_BLOCK, ML_QKV_BLOCK, ML_QKV_BLOCK), ML_QKV_BLOCK ** -0.5),
        "ml_wv": nrm((L, ML_WIDTH // ML_QKV_BLOCK, ML_QKV_BLOCK, ML_QKV_BLOCK), ML_QKV_BLOCK ** -0.5),
        "ml_wi": nrm((L, 2, 3 * ML_WIDTH, ML_HEADS), (3 * ML_WIDTH) ** -0.5),
        "ml_bi": nrm((L, 2, ML_HEADS), 0.1),
        "ml_wf": nrm((L, 2, 3 * ML_WIDTH, ML_HEADS), (3 * ML_WIDTH) ** -0.5),
        "ml_bf": jnp.linspace(3.0, 6.0, ML_HEADS, dtype=jnp.float32)[None, None, :] + nrm((L, 2, ML_HEADS), 0.1),
        "ml_norm_g": 1.0 + nrm((L, ML_WIDTH), 0.02),
        "ml_skip": 1.0 + nrm((L, ML_WIDTH), 0.02),
        "w_branch_rg": nrm((L, RG_WIDTH, D_MODEL), RG_WIDTH ** -0.5),
        "w_branch_ml": nrm((L, ML_WIDTH, D_MODEL), ML_WIDTH ** -0.5),
        "w_out": nrm((L, D_MODEL, D_MODEL), D_MODEL ** -0.5),
        "w_ffn_in": nrm((L, D_MODEL, 2 * D_FF), D_MODEL ** -0.5),
        "w_ffn_out": nrm((L, D_FF, D_MODEL), D_FF ** -0.5),
        "final_norm_g": 1.0 + nrm((D_MODEL,), 0.02),
    }


def reference(x, c, ctx, c_ctx, w_mod, b_mod, norm1_g, norm2_g, w_in, rg_conv_w, rg_conv_b, rg_wa, rg_ba,
              rg_wx, rg_bx, rg_lambda, ml_conv_w, ml_conv_b, ml_wq, ml_wk, ml_wv, ml_wi, ml_bi, ml_wf, ml_bf,
              ml_norm_g, ml_skip, w_branch_rg, w_branch_ml, w_out, w_ffn_in, w_ffn_out, final_norm_g):
    s_lat = jax.nn.silu(c)
    s_ctx = jax.nn.silu(c_ctx)
    for l in range(DEPTH):
        last = l == DEPTH - 1
        mod_l = (s_lat @ w_mod[l] + b_mod[l])[:, None, :]
        mod_c = s_ctx @ w_mod[l] + b_mod[l]
        sh1, sc1, g1, sh2, sc2, g2 = jnp.split(mod_l, 6, axis=-1)
        csh1, csc1, cg1, csh2, csc2, cg2 = jnp.split(mod_c, 6, axis=-1)

        u_lat = _modulate(_rmsnorm(x, norm1_g[l]), sh1, sc1)
        u_ctx = _modulate(_rmsnorm(ctx, norm1_g[l]), csh1, csc1)
        y_lat, y_ctx = _mixer(u_lat, u_ctx, not last, w_in[l], rg_conv_w[l], rg_conv_b[l], rg_wa[l], rg_ba[l],
                              rg_wx[l], rg_bx[l], rg_lambda[l], ml_conv_w[l], ml_conv_b[l], ml_wq[l], ml_wk[l],
                              ml_wv[l], ml_wi[l], ml_bi[l], ml_wf[l], ml_bf[l], ml_norm_g[l], ml_skip[l],
                              w_branch_rg[l], w_branch_ml[l], w_out[l])
        x = x + g1 * y_lat
        h = _modulate(_rmsnorm(x, norm2_g[l]), sh2, sc2)
        x = x + g2 * _swiglu(h, w_ffn_in[l], w_ffn_out[l])
        if not last:
            ctx = ctx + cg1 * y_ctx
            hc = _modulate(_rmsnorm(ctx, norm2_g[l]), csh2, csc2)
            ctx = ctx + cg2 * _swiglu(hc, w_ffn_in[l], w_ffn_out[l])
    return _rmsnorm(x, final_norm_g)
```

```python
import functools

import jax
import jax.numpy as jnp
from jax import lax
from jax.experimental import pallas as pl
from jax.experimental.pallas import tpu as pltpu

F32 = jnp.float32
BF16 = jnp.bfloat16

D = 1024
NB = 8
SEQ = 4096
GRID_W = 64
CTX = 256
EPS = 1e-6
RG_C = 8.0
HEADS = 4
DH = D // HEADS
D_FF = 2816
N_IN = 6 * D

VMEM_LIMIT = 60 * 1024 * 1024

PROJ_ROWS = 512
RG_STEPS = 256
RG_SUB = 256
ML_CHUNK = 256
ML_COLS = ML_CHUNK // GRID_W


def _sigmoid(x):
    return 0.5 * (jnp.tanh(0.5 * x) + 1.0)


def _silu(x):
    return x * _sigmoid(x)


def _softplus(x):
    return jnp.maximum(x, 0.0) + jnp.log(1.0 + jnp.exp(-jnp.abs(x)))


def _log_sigmoid(x):
    return jnp.minimum(x, 0.0) - jnp.log(1.0 + jnp.exp(-jnp.abs(x)))


def _split3(x):
    hi = x.astype(BF16)
    r1 = x - hi.astype(F32)
    mid = r1.astype(BF16)
    lo = (r1 - mid.astype(F32)).astype(BF16)
    return hi, mid, lo


def _params(sem):
    return pltpu.CompilerParams(dimension_semantics=sem, vmem_limit_bytes=VMEM_LIMIT)


def _const_spec(shape):
    nd = len(shape)
    return pl.BlockSpec(shape, lambda *_: (0,) * nd, pipeline_mode=pl.Buffered(1))


def _mod_kernel(c_ref, cc_ref, w_ref, b_ref, o_ref):
    s = jnp.concatenate([c_ref[...], jnp.broadcast_to(cc_ref[...], (NB, D))], axis=0)
    s = _silu(s)
    s_hi, s_mid, _ = _split3(s)
    w_hi, w_mid, _ = _split3(w_ref[...])
    dot = functools.partial(jnp.dot, preferred_element_type=F32)
    o_ref[...] = dot(s_hi, w_hi) + dot(s_mid, w_hi) + dot(s_hi, w_mid) + b_ref[...]


def _mod_call(c, c_ctx, w_mod, b_mod):
    return pl.pallas_call(
        _mod_kernel,
        grid=(6,),
        in_specs=[
            pl.BlockSpec((NB, D), lambda g: (0, 0)),
            pl.BlockSpec((1, D), lambda g: (0, 0)),
            pl.BlockSpec((D, D), lambda g: (0, g)),
            pl.BlockSpec((1, D), lambda g: (0, g)),
        ],
        out_specs=pl.BlockSpec((2 * NB, D), lambda g: (0, g)),
        out_shape=jax.ShapeDtypeStruct((2 * NB, 6 * D), F32),
        compiler_params=_params(("arbitrary",)),
        name="mod",
    )(c, c_ctx.reshape(1, D), w_mod, b_mod.reshape(1, 6 * D))


def _gelu_tanh(x):
    return jax.nn.gelu(x, approximate=True)


def _identity(x):
    return x


_PROJ_FULL = ((0, _identity), (1, _gelu_tanh), (2, _identity), (3, _sigmoid), (4, _sigmoid), (5, _sigmoid))
_PROJ_CTX = ((0, _identity), (2, _identity))


def _proj_kernel(x_ref, sh_ref, sc_ref, g_ref, w_ref, *o_refs, groups):
    x = x_ref[...]
    ms = jnp.mean(x * x, axis=-1, keepdims=True)
    y = x * lax.rsqrt(ms + EPS) * g_ref[...]
    u = (y * (1.0 + sc_ref[...]) + sh_ref[...]).astype(BF16)
    for o_ref, (g, act) in zip(o_refs, groups):
        p = jnp.dot(u, w_ref[:, g * D:(g + 1) * D], preferred_element_type=F32)
        o_ref[...] = act(p).astype(o_ref.dtype)


def _proj_call(x, mod3, norm_g, w_in_bf, *, ctx):
    L = x.shape[1]
    rows = min(PROJ_ROWS, L)
    groups = _PROJ_CTX if ctx else _PROJ_FULL
    mod_row = (lambda b: NB) if ctx else (lambda b: b)
    tm_spec = pl.BlockSpec((rows, D), lambda b, i: (i, b))
    bm_spec = pl.BlockSpec((None, rows, D), lambda b, i: (b, i, 0))
    tm_shape = jax.ShapeDtypeStruct((L, NB * D), BF16)
    bm_shape = jax.ShapeDtypeStruct((NB, L, D), BF16)
    if ctx:
        out_specs, out_shape = [tm_spec, bm_spec], [tm_shape, bm_shape]
    else:
        out_specs = [tm_spec, tm_spec, bm_spec, bm_spec, bm_spec, bm_spec]
        out_shape = [tm_shape, tm_shape, bm_shape, bm_shape, bm_shape, bm_shape]
    return pl.pallas_call(
        functools.partial(_proj_kernel, groups=groups),
        grid=(NB, L // rows),
        in_specs=[
            pl.BlockSpec((None, rows, D), lambda b, i: (b, i, 0)),
            pl.BlockSpec((None, 1, D), lambda b, i: (mod_row(b), 0, 0)),
            pl.BlockSpec((None, 1, D), lambda b, i: (mod_row(b), 0, 1)),
            _const_spec((1, D)),
            _const_spec((D, N_IN)),
        ],
        out_specs=out_specs,
        out_shape=out_shape,
        compiler_params=_params(("arbitrary", "arbitrary")),
        name="proj_ctx" if ctx else "proj",
    )(x, mod3, mod3, norm_g, w_in_bf)


def _rg_kernel(*refs, reverse, n_tiles, merge):
    if merge:
        (xc_ref, xp_ref, xn_ref, h0_ref, cw_ref, cb_ref, wbd_ref, bias_ref, lam_ref,
         hf_ref, gate_ref, o_ref, hlast_ref, xe_scr, a_scr, b_scr, h_scr) = refs
    else:
        (xc_ref, xp_ref, xn_ref, h0_ref, cw_ref, cb_ref, wbd_ref, bias_ref, lam_ref,
         o_ref, hlast_ref, xe_scr, a_scr, b_scr, h_scr) = refs
    i = pl.program_id(0)
    tile = (n_tiles - 1 - i) if reverse else i
    rows = RG_STEPS * NB

    @pl.when(i == 0)
    def _():
        h_scr[...] = h0_ref[...]

    has_prev = tile > 0
    has_next = tile < n_tiles - 1
    prev = xp_ref[...].astype(F32)
    nxt = xn_ref[...].astype(F32)
    xe_scr[0:2 * NB, :] = jnp.where(has_prev, prev, 0.0)
    xe_scr[2 * NB:2 * NB + rows, :] = xc_ref[...].astype(F32)
    xe_scr[2 * NB + rows:4 * NB + rows, :] = jnp.where(has_next, nxt, 0.0)

    cw = cw_ref[...]
    cb = cb_ref[...]
    lamc = -RG_C * _softplus(-lam_ref[...])

    def coef_block(s, carry):
        r0 = pl.multiple_of(s * RG_SUB, RG_SUB)
        xc = cb + cw[0:1] * xe_scr[pl.ds(r0, RG_SUB), :]
        xc = xc + cw[1:2] * xe_scr[pl.ds(r0 + NB, RG_SUB), :]
        xc = xc + cw[2:3] * xe_scr[pl.ds(r0 + 2 * NB, RG_SUB), :]
        xc = xc + cw[3:4] * xe_scr[pl.ds(r0 + 3 * NB, RG_SUB), :]
        for p in range(D // 128):
            lo, hi = p * 128, (p + 1) * 128
            xcp = xc[:, lo:hi]
            pre = jnp.dot(xcp.astype(BF16), wbd_ref[p], preferred_element_type=F32) + bias_ref[p]
            r = _sigmoid(pre[:, 0:128])
            ig = _sigmoid(pre[:, 128:256])
            log_a = lamc[:, lo:hi] * r
            a = jnp.exp(log_a)
            a_scr[pl.ds(r0, RG_SUB), lo:hi] = a
            one_minus_a2 = -jnp.tanh(log_a) * (a * a + 1.0)
            b_scr[pl.ds(r0, RG_SUB), lo:hi] = jnp.sqrt(one_minus_a2) * (ig * xcp)
        return carry

    lax.fori_loop(0, rows // RG_SUB, coef_block, 0)

    def scan_step(t, h):
        tt = (RG_STEPS - 1 - t) if reverse else t
        r0 = pl.multiple_of(tt * NB, NB)
        h = a_scr[pl.ds(r0, NB), :] * h + b_scr[pl.ds(r0, NB), :]
        b_scr[pl.ds(r0, NB), :] = h
        return h

    h_last = lax.fori_loop(0, RG_STEPS, scan_step, h_scr[...], unroll=8)
    h_scr[...] = h_last
    hlast_ref[...] = h_last

    if merge:
        def out_block(s, carry):
            r0 = pl.multiple_of(s * RG_SUB, RG_SUB)
            h = b_scr[pl.ds(r0, RG_SUB), :] + hf_ref[pl.ds(r0, RG_SUB), :].astype(F32)
            y = h * gate_ref[pl.ds(r0, RG_SUB), :].astype(F32)
            o_ref[pl.ds(r0, RG_SUB), :] = y.astype(o_ref.dtype)
            return carry
        lax.fori_loop(0, rows // RG_SUB, out_block, 0)
    else:
        o_ref[...] = b_scr[...].astype(o_ref.dtype)


def _rg_call(rgx_rows, h0, cw, cb, wbd, bias, lam, *, reverse, merge_with=None, name):
    total = rgx_rows.shape[0]
    rows = RG_STEPS * NB
    n_tiles = total // rows
    halo = 2 * NB
    per = rows // halo
    n_halo = total // halo
    tile = (lambda i: n_tiles - 1 - i) if reverse else (lambda i: i)
    row_spec = pl.BlockSpec((rows, D), lambda i: (tile(i), 0))
    in_specs = [
        row_spec,
        pl.BlockSpec((halo, D), lambda i: (jnp.maximum(tile(i) * per - 1, 0), 0)),
        pl.BlockSpec((halo, D), lambda i: (jnp.minimum((tile(i) + 1) * per, n_halo - 1), 0)),
        _const_spec((NB, D)),
        _const_spec((4, D)),
        _const_spec((1, D)),
        _const_spec((D // 128, 128, 256)),
        _const_spec((D // 128, 1, 256)),
        _const_spec((1, D)),
    ]
    args = [rgx_rows, rgx_rows, rgx_rows, h0, cw, cb, wbd, bias, lam]
    if merge_with is not None:
        in_specs += [row_spec, row_spec]
        args += list(merge_with)
    return pl.pallas_call(
        functools.partial(_rg_kernel, reverse=reverse, n_tiles=n_tiles, merge=merge_with is not None),
        grid=(n_tiles,),
        in_specs=in_specs,
        out_specs=[row_spec, pl.BlockSpec((NB, D), lambda i: (0, 0))],
        out_shape=[jax.ShapeDtypeStruct((total, D), BF16), jax.ShapeDtypeStruct((NB, D), F32)],
        scratch_shapes=[
            pltpu.VMEM((rows + 4 * NB, D), F32),
            pltpu.VMEM((rows, D), F32),
            pltpu.VMEM((rows, D), F32),
            pltpu.VMEM((NB, D), F32),
        ],
        compiler_params=_params(("arbitrary",)),
        name=name,
    )(*args)


def _scan_lanes(x, op, fill, reverse):
    n = x.shape[1]
    lane = lax.broadcasted_iota(jnp.int32, x.shape, 1)
    sh = 1
    while sh < n:
        if reverse:
            shifted = jnp.where(lane < n - sh, pltpu.roll(x, n - sh, axis=1), fill)
        else:
            shifted = jnp.where(lane >= sh, pltpu.roll(x, sh, axis=1), fill)
        x = op(x, shifted)
        sh *= 2
    return x


def _ml_kernel(*refs, reverse, n_chunks, mode):
    L = ML_CHUNK
    if mode == "ctx":
        (x_ref, cw_ref, cb_ref, wqk_ref, wv_ref, wg_ref, bg_ref,
         c_out, n_out, m_out, xe_scr, h_scr, c_scr, n_scr, m_scr) = refs
    elif mode == "fwd":
        (x_ref, xp_ref, xn_ref, c0_ref, n0_ref, m0_ref, cw_ref, cb_ref, wqk_ref, wv_ref, wg_ref, bg_ref,
         o_ref, xe_scr, h_scr, c_scr, n_scr, m_scr) = refs
    else:
        (x_ref, xp_ref, xn_ref, c0_ref, n0_ref, m0_ref, cw_ref, cb_ref, wqk_ref, wv_ref, wg_ref, bg_ref,
         hf_ref, og_ref, ng_ref, sk_ref, o_ref, xe_scr, h_scr, c_scr, n_scr, m_scr) = refs
    j = pl.program_id(1)
    chunk = (n_chunks - 1 - j) if reverse else j

    @pl.when(j == 0)
    def _():
        if mode == "ctx":
            c_scr[...] = jnp.zeros_like(c_scr)
            n_scr[...] = jnp.zeros_like(n_scr)
            m_scr[...] = jnp.zeros_like(m_scr)
        else:
            c_scr[...] = c0_ref[...]
            n_scr[...] = n0_ref[...]
            m_scr[...] = m0_ref[...]

    if mode == "ctx":
        xe_scr[0:16, :] = jnp.zeros((16, D), F32)
        xe_scr[16:16 + L, :] = x_ref[...].astype(F32)
        xe_scr[16 + L:32 + L, :] = jnp.zeros((16, D), F32)
    else:
        has_prev = chunk > 0
        has_next = chunk < n_chunks - 1
        xe_scr[0:16, :] = jnp.where(has_prev, xp_ref[...].astype(F32), 0.0)
        for c in range(ML_COLS):
            xe_scr[16 + c * GRID_W:16 + (c + 1) * GRID_W, :] = x_ref[:, c * D:(c + 1) * D].astype(F32)
        xe_scr[16 + L:32 + L, :] = jnp.where(has_next, xn_ref[...].astype(F32), 0.0)

    cw = cw_ref[...]
    xt = xe_scr[16:16 + L, :]
    xc = cb_ref[...] + cw[0:1] * xe_scr[14:14 + L, :] + cw[1:2] * xe_scr[15:15 + L, :]
    xc = xc + cw[2:3] * xt + cw[3:4] * xe_scr[17:17 + L, :]
    xc = _silu(xc)

    qs, ks, vs = [], [], []
    gates = jnp.zeros((L, 128), F32) + bg_ref[...]
    for h in range(HEADS):
        lo, hi = h * DH, (h + 1) * DH
        qk = jnp.dot(xc[:, lo:hi].astype(BF16), wqk_ref[h], preferred_element_type=F32)
        q = qk[:, 0:DH].astype(BF16)
        k = qk[:, DH:2 * DH]
        v = jnp.dot(xt[:, lo:hi].astype(BF16), wv_ref[h], preferred_element_type=F32).astype(BF16)
        kb = k.astype(BF16)
        gates = gates + jnp.dot(q, wg_ref[h], preferred_element_type=F32)
        gates = gates + jnp.dot(kb, wg_ref[HEADS + h], preferred_element_type=F32)
        gates = gates + jnp.dot(v, wg_ref[2 * HEADS + h], preferred_element_type=F32)
        qs.append(q)
        ks.append(kb)
        vs.append(v)

    gt = gates.T
    li = gt[0:8, :]
    lf = _log_sigmoid(gt[8:16, :])
    cum = _scan_lanes(lf, jnp.add, 0.0, reverse)
    g = li - cum
    m_prev = m_scr[:, 0:1]
    inter = cum + m_prev
    b_tot = cum[:, 0:1] if reverse else cum[:, L - 1:L]
    gmax = jnp.max(g, axis=1, keepdims=True)
    m_new = jnp.maximum(b_tot + m_prev, b_tot + gmax)
    ws = jnp.exp(b_tot + g - m_new)
    dec = jnp.exp(b_tot + m_prev - m_new)
    pieces = [ws, jnp.broadcast_to(dec, (8, L))]
    if mode != "ctx":
        m_loc = cum + _scan_lanes(g, jnp.maximum, -jnp.inf, reverse)
        m_t = jnp.maximum(inter, m_loc)
        pieces += [cum - m_t, jnp.exp(inter - m_t), jnp.exp(-m_t)]
    pieces.append(jnp.zeros((128 - 8 * len(pieces), L), F32))
    cols = jnp.concatenate(pieces, axis=0).T

    if mode != "ctx":
        row_id = lax.broadcasted_iota(jnp.int32, (L, L), 0)
        col_id = lax.broadcasted_iota(jnp.int32, (L, L), 1)
        keep = (col_id >= row_id) if reverse else (col_id <= row_id)

    for h in range(HEADS):
        lo, hi = h * DH, (h + 1) * DH
        q, kb, v = qs[h], ks[h], vs[h]
        ws_c = cols[:, h:h + 1]
        dec_c = cols[:, 8 + h:9 + h]
        c_old = c_scr[h]
        n_old = n_scr[h]
        if mode != "ctx":
            c_c = cols[:, 16 + h:17 + h]
            si_c = cols[:, 24 + h:25 + h]
            fl_c = cols[:, 32 + h:33 + h]
            s = lax.dot_general(q, kb, (((1,), (1,)), ((), ())), preferred_element_type=F32)
            arg = jnp.where(keep, c_c + g[h:h + 1, :], -jnp.inf)
            p = s * jnp.exp(arg)
            num = jnp.dot(p.astype(BF16), v, preferred_element_type=F32)
            num = num + si_c * jnp.dot(q, c_old.astype(BF16), preferred_element_type=F32)
            den = jnp.sum(p, axis=1, keepdims=True)
            den = den + si_c * jnp.sum(q.astype(F32) * n_old, axis=1, keepdims=True)
            h_scr[:, lo:hi] = num / jnp.maximum(jnp.abs(den), fl_c)
        kw = kb.astype(F32) * ws_c
        upd = lax.dot_general(kw.astype(BF16), v, (((0,), (0,)), ((), ())), preferred_element_type=F32)
        c_scr[h] = dec_c * c_old + upd
        n_scr[h] = dec_c[0:1, :] * n_old + jnp.sum(kw, axis=0, keepdims=True)
    m_scr[...] = jnp.broadcast_to(m_new, m_scr.shape)

    if mode == "ctx":
        c_out[...] = c_scr[...]
        n_out[...] = n_scr[...]
        m_out[...] = m_scr[...]
    elif mode == "fwd":
        o_ref[...] = h_scr[...].astype(o_ref.dtype)
    else:
        ht = h_scr[...] + hf_ref[...].astype(F32)
        for h in range(HEADS):
            lo, hi = h * DH, (h + 1) * DH
            hh = ht[:, lo:hi]
            mu = jnp.mean(hh, axis=-1, keepdims=True)
            var = jnp.mean(jnp.square(hh - mu), axis=-1, keepdims=True)
            hn = (hh - mu) * lax.rsqrt(var + EPS)
            h_scr[:, lo:hi] = hn * ng_ref[:, lo:hi] + sk_ref[:, lo:hi] * xc[:, lo:hi]
        for c in range(ML_COLS):
            y = h_scr[c * GRID_W:(c + 1) * GRID_W, :] * og_ref[:, c * D:(c + 1) * D].astype(F32)
            o_ref[:, c * D:(c + 1) * D] = y.astype(o_ref.dtype)


def _ml_weight_specs():
    return [
        _const_spec((4, D)),
        _const_spec((1, D)),
        _const_spec((HEADS, DH, 2 * DH)),
        _const_spec((HEADS, DH, DH)),
        _const_spec((3 * HEADS, DH, 128)),
        _const_spec((1, 128)),
    ]


def _ml_scratch():
    return [
        pltpu.VMEM((ML_CHUNK + 32, D), F32),
        pltpu.VMEM((ML_CHUNK, D), F32),
        pltpu.VMEM((HEADS, DH, DH), F32),
        pltpu.VMEM((HEADS, 1, DH), F32),
        pltpu.VMEM((8, 128), F32),
    ]


_STATE_SHAPES = [
    jax.ShapeDtypeStruct((NB, HEADS, DH, DH), F32),
    jax.ShapeDtypeStruct((NB, HEADS, 1, DH), F32),
    jax.ShapeDtypeStruct((NB, 8, 128), F32),
]


def _state_specs():
    return [
        pl.BlockSpec((None, HEADS, DH, DH), lambda b, j: (b, 0, 0, 0)),
        pl.BlockSpec((None, HEADS, 1, DH), lambda b, j: (b, 0, 0, 0)),
        pl.BlockSpec((None, 8, 128), lambda b, j: (b, 0, 0)),
    ]


def _ml_ctx_call(mlx_ctx, weights, *, reverse, name):
    n_chunks = CTX // ML_CHUNK
    chunk = (lambda j: n_chunks - 1 - j) if reverse else (lambda j: j)
    return pl.pallas_call(
        functools.partial(_ml_kernel, reverse=reverse, n_chunks=n_chunks, mode="ctx"),
        grid=(NB, n_chunks),
        in_specs=[pl.BlockSpec((None, ML_CHUNK, D), lambda b, j: (b, chunk(j), 0))] + _ml_weight_specs(),
        out_specs=_state_specs(),
        out_shape=_STATE_SHAPES,
        scratch_shapes=_ml_scratch(),
        compiler_params=_params(("arbitrary", "arbitrary")),
        name=name,
    )(mlx_ctx, *weights)


def _ml_lat_call(mlx_cm, state, weights, *, reverse, merge_with=None, name):
    n_chunks = SEQ // ML_CHUNK
    chunk = (lambda j: n_chunks - 1 - j) if reverse else (lambda j: j)
    col_spec = pl.BlockSpec((None, GRID_W, ML_COLS * D), lambda b, j: (b, 0, chunk(j)))
    in_specs = [
        col_spec,
        pl.BlockSpec((None, 16, D), lambda b, j: (b, GRID_W // 16 - 1, jnp.maximum(chunk(j) * ML_COLS - 1, 0))),
        pl.BlockSpec((None, 16, D), lambda b, j: (b, 0, jnp.minimum((chunk(j) + 1) * ML_COLS, GRID_W - 1))),
    ] + _state_specs() + _ml_weight_specs()
    args = [mlx_cm, mlx_cm, mlx_cm, *state, *weights]
    tok_spec = pl.BlockSpec((None, ML_CHUNK, D), lambda b, j: (b, chunk(j), 0))
    if merge_with is None:
        mode = "fwd"
        out_spec, out_shape = tok_spec, jax.ShapeDtypeStruct((NB, SEQ, D), BF16)
    else:
        mode = "rev"
        h_fwd, out_gate, norm_g, skip = merge_with
        in_specs += [tok_spec, col_spec, _const_spec((1, D)), _const_spec((1, D))]
        args += [h_fwd, out_gate, norm_g, skip]
        out_spec, out_shape = col_spec, jax.ShapeDtypeStruct((NB, GRID_W, GRID_W * D), BF16)
    return pl.pallas_call(
        functools.partial(_ml_kernel, reverse=reverse, n_chunks=n_chunks, mode=mode),
        grid=(NB, n_chunks),
        in_specs=in_specs,
        out_specs=out_spec,
        out_shape=out_shape,
        scratch_shapes=_ml_scratch(),
        compiler_params=_params(("arbitrary", "arbitrary")),
        name=name,
    )(*args)


def _final_kernel(x_ref, yrg_ref, yml_ref, sgr_ref, sgm_ref, g1_ref, sh2_ref, sc2_ref, g2_ref,
                  n2_ref, nf_ref, wbr_ref, wbm_ref, wo_ref, wfi_ref, wfo_ref, o_ref, act_scr):
    dot = functools.partial(jnp.dot, preferred_element_type=F32)
    mix = sgr_ref[...].astype(F32) * dot(yrg_ref[...], wbr_ref[...])
    mix = mix + sgm_ref[...].astype(F32) * dot(yml_ref[...], wbm_ref[...])
    x1 = x_ref[...] + g1_ref[...] * dot(mix.astype(BF16), wo_ref[...])
    ms = jnp.mean(x1 * x1, axis=-1, keepdims=True)
    hn = x1 * lax.rsqrt(ms + EPS) * n2_ref[...]
    hb = (hn * (1.0 + sc2_ref[...]) + sh2_ref[...]).astype(BF16)
    step = 256
    for c in range(D_FF // step):
        gate = dot(hb, wfi_ref[:, c * step:(c + 1) * step])
        up = dot(hb, wfi_ref[:, D_FF + c * step:D_FF + (c + 1) * step])
        act_scr[:, c * step:(c + 1) * step] = (_silu(gate) * up).astype(BF16)
    x2 = x1 + g2_ref[...] * dot(act_scr[...], wfo_ref[...])
    ms2 = jnp.mean(x2 * x2, axis=-1, keepdims=True)
    o_ref[...] = x2 * lax.rsqrt(ms2 + EPS) * nf_ref[...]


def _final_call(x, y_rg_tm, y_ml, sgr, sgm, mod3, norm2_g, final_g, wbr, wbm, wo, wfi, wfo):
    rows = PROJ_ROWS
    bm_spec = pl.BlockSpec((None, rows, D), lambda b, i: (b, i, 0))
    tm_spec = pl.BlockSpec((rows, D), lambda b, i: (i, b))
    mod_spec = lambda g: pl.BlockSpec((None, 1, D), lambda b, i: (b, 0, g))
    return pl.pallas_call(
        _final_kernel,
        grid=(NB, SEQ // rows),
        in_specs=[
            bm_spec, tm_spec, bm_spec, bm_spec, bm_spec,
            mod_spec(2), mod_spec(3), mod_spec(4), mod_spec(5),
            _const_spec((1, D)), _const_spec((1, D)),
            _const_spec((D, D)), _const_spec((D, D)), _const_spec((D, D)),
            _const_spec((D, 2 * D_FF)), _const_spec((D_FF, D)),
        ],
        out_specs=bm_spec,
        out_shape=jax.ShapeDtypeStruct((NB, SEQ, D), F32),
        scratch_shapes=[pltpu.VMEM((rows, D_FF), BF16)],
        compiler_params=_params(("arbitrary", "arbitrary")),
        name="final",
    )(x, y_rg_tm, y_ml, sgr, sgm, mod3, mod3, mod3, mod3, norm2_g, final_g, wbr, wbm, wo, wfi, wfo)


def _pair_blockdiag(w):
    w = w.reshape(8, 2, 64, 64)
    z = jnp.zeros((8, 64, 64), w.dtype)
    top = jnp.concatenate([w[:, 0], z], axis=2)
    bot = jnp.concatenate([z, w[:, 1]], axis=2)
    return jnp.concatenate([top, bot], axis=1)


def _rg_weights(wa, ba, wx, bx):
    wbd = jnp.concatenate([_pair_blockdiag(wa), _pair_blockdiag(wx)], axis=2).astype(BF16)
    bias = jnp.concatenate([ba.reshape(8, 1, 128), bx.reshape(8, 1, 128)], axis=2)
    return wbd, bias


def _head_blockdiag(w):
    w = w.reshape(HEADS, DH // 4, 4, 4)
    eye = jnp.eye(DH // 4, dtype=w.dtype)
    dense = w[:, :, :, None, :] * eye[None, :, None, :, None]
    return dense.reshape(HEADS, DH, DH)


def _gate_weights(wi, bi, wf, bf):
    w = jnp.zeros((3 * D, 128), F32).at[:, 0:HEADS].set(wi).at[:, 8:8 + HEADS].set(wf)
    b = jnp.zeros((1, 128), F32).at[0, 0:HEADS].set(bi).at[0, 8:8 + HEADS].set(bf)
    return w.reshape(3 * HEADS, DH, 128).astype(BF16), b


def kernel(x, c, ctx, c_ctx, w_mod, b_mod, norm1_g, norm2_g, w_in, rg_conv_w, rg_conv_b, rg_wa, rg_ba, rg_wx,
           rg_bx, rg_lambda, ml_conv_w, ml_conv_b, ml_wq, ml_wk, ml_wv, ml_wi, ml_bi, ml_wf, ml_bf,
           ml_norm_g, ml_skip, w_branch_rg, w_branch_ml, w_out, w_ffn_in, w_ffn_out, final_norm_g):
    mod = _mod_call(c, c_ctx, w_mod[0], b_mod[0])
    mod3 = mod.reshape(2 * NB, 1, 6 * D)
    w_in_bf = w_in[0].astype(BF16)
    norm1 = norm1_g[0].reshape(1, D)

    rgx, grg, mlx, smlo, sgr, sgm = _proj_call(x, mod3, norm1, w_in_bf, ctx=False)
    rgx_c, mlx_c = _proj_call(ctx, mod3, norm1, w_in_bf, ctx=True)

    rg_cw = rg_conv_w[0]
    rg_cb = rg_conv_b[0].reshape(1, D)
    rgx_rows = rgx.reshape(SEQ * NB, D)
    rgx_c_rows = rgx_c.reshape(CTX * NB, D)
    zero_h = jnp.zeros((NB, D), F32)
    rg_w = [_rg_weights(rg_wa[0, d], rg_ba[0, d], rg_wx[0, d], rg_bx[0, d]) + (rg_lambda[0, d].reshape(1, D),)
            for d in range(2)]
    _, h0_f = _rg_call(rgx_c_rows, zero_h, rg_cw, rg_cb, *rg_w[0], reverse=False, name="rg_ctx_fwd")
    _, h0_r = _rg_call(rgx_c_rows, zero_h, rg_cw, rg_cb, *rg_w[1], reverse=True, name="rg_ctx_rev")
    h_f, _ = _rg_call(rgx_rows, h0_f, rg_cw, rg_cb, *rg_w[0], reverse=False, name="rg_fwd")
    y_rg, _ = _rg_call(rgx_rows, h0_r, rg_cw, rg_cb, *rg_w[1], reverse=True,
                       merge_with=(h_f, grg.reshape(SEQ * NB, D)), name="rg_rev")

    wqk = jnp.concatenate([_head_blockdiag(ml_wq[0]), _head_blockdiag(ml_wk[0]) * (DH ** -0.5)], axis=2).astype(BF16)
    wv = _head_blockdiag(ml_wv[0]).astype(BF16)
    ml_cw = ml_conv_w[0]
    ml_cb = ml_conv_b[0].reshape(1, D)
    ml_w = [(ml_cw, ml_cb, wqk, wv) + _gate_weights(ml_wi[0, d], ml_bi[0, d], ml_wf[0, d], ml_bf[0, d])
            for d in range(2)]
    mlx_cm = mlx.reshape(NB, GRID_W, GRID_W * D)
    st_f = _ml_ctx_call(mlx_c, ml_w[0], reverse=False, name="ml_ctx_fwd")
    st_r = _ml_ctx_call(mlx_c, ml_w[1], reverse=True, name="ml_ctx_rev")
    hm_f = _ml_lat_call(mlx_cm, st_f, ml_w[0], reverse=False, name="ml_fwd")
    y_ml = _ml_lat_call(mlx_cm, st_r, ml_w[1], reverse=True,
                        merge_with=(hm_f, smlo.reshape(NB, GRID_W, GRID_W * D),
                                    ml_norm_g[0].reshape(1, D), ml_skip[0].reshape(1, D)),
                        name="ml_rev")

    return _final_call(
        x, y_rg.reshape(SEQ, NB * D), y_ml.reshape(NB, SEQ, D), sgr, sgm, mod3,
        norm2_g[0].reshape(1, D), final_norm_g.reshape(1, D),
        w_branch_rg[0].astype(BF16), w_branch_ml[0].astype(BF16), w_out[0].astype(BF16),
        w_ffn_in[0].astype(BF16), w_ffn_out[0].astype(BF16))
```

```python
import functools

import jax
import jax.numpy as jnp
from jax import lax
from jax.experimental import pallas as pl
from jax.experimental.pallas import tpu as pltpu

F32 = jnp.float32
BF16 = jnp.bfloat16

D = 1024
NB = 8
SEQ = 4096
GRID_W = 64
CTX = 256
EPS = 1e-6
RG_C = 8.0
HEADS = 4
DH = D // HEADS
D_FF = 2816
N_IN = 6 * D
LOG2E = 1.4426950408889634

VMEM_LIMIT = 60 * 1024 * 1024
BF16_ROWS = 16

PROJ_ROWS = 512
RG_STEPS = 128
RG_SUB = 256
RG_SUB_STEPS = RG_SUB // NB
ML_CHUNK = 256
ML_GROUP_COLS = 16
ML_TILE = ML_GROUP_COLS * GRID_W
ML_A_ROWS = 512

assert ML_CHUNK == DH


def _sigmoid(x):
    return 0.5 * (jnp.tanh(0.5 * x) + 1.0)


def _silu(x):
    return x * _sigmoid(x)


def _softplus(x):
    return jnp.maximum(x, 0.0) + jnp.log(1.0 + jnp.exp(-jnp.abs(x)))


def _log_sigmoid(x):
    return jnp.minimum(x, 0.0) - jnp.log(1.0 + jnp.exp(-jnp.abs(x)))


def _split3(x):
    hi = x.astype(BF16)
    r1 = x - hi.astype(F32)
    mid = r1.astype(BF16)
    lo = (r1 - mid.astype(F32)).astype(BF16)
    return hi, mid, lo


def _params(sem):
    return pltpu.CompilerParams(dimension_semantics=sem, vmem_limit_bytes=VMEM_LIMIT)


def _const_spec(shape):
    nd = len(shape)
    return pl.BlockSpec(shape, lambda *_: (0,) * nd, pipeline_mode=pl.Buffered(1))


def _swap_perm(a, b):
    n = a * b
    out_row = jnp.arange(n)
    src = (out_row % a) * b + out_row // a
    return (src[:, None] == jnp.arange(n)[None, :]).astype(BF16)


def _mod_kernel(c_ref, cc_ref, w_ref, b_ref, o_ref):
    s = jnp.concatenate([c_ref[...], jnp.broadcast_to(cc_ref[...], (NB, D))], axis=0)
    s = _silu(s)
    s_hi, s_mid, _ = _split3(s)
    w_hi, w_mid, _ = _split3(w_ref[...])
    dot = functools.partial(jnp.dot, preferred_element_type=F32)
    o_ref[...] = dot(s_hi, w_hi) + dot(s_mid, w_hi) + dot(s_hi, w_mid) + b_ref[...]


def _mod_call(c, c_ctx, w_mod, b_mod):
    return pl.pallas_call(
        _mod_kernel,
        grid=(6,),
        in_specs=[
            pl.BlockSpec((NB, D), lambda g: (0, 0)),
            pl.BlockSpec((1, D), lambda g: (0, 0)),
            pl.BlockSpec((D, D), lambda g: (0, g)),
            pl.BlockSpec((1, D), lambda g: (0, g)),
        ],
        out_specs=pl.BlockSpec((2 * NB, D), lambda g: (0, g)),
        out_shape=jax.ShapeDtypeStruct((2 * NB, 6 * D), F32),
        compiler_params=_params(("arbitrary",)),
        name="mod",
    )(c, c_ctx.reshape(1, D), w_mod, b_mod.reshape(1, 6 * D))


def _gelu_tanh(x):
    return jax.nn.gelu(x, approximate=True)


def _identity(x):
    return x


_PROJ_FULL = ((0, _identity), (1, _gelu_tanh), (2, _identity), (3, _sigmoid), (4, _sigmoid), (5, _sigmoid))
_PROJ_CTX = ((0, _identity), (2, _identity))


def _proj_kernel(x_ref, sh_ref, sc_ref, g_ref, w_ref, *o_refs, groups):
    x = x_ref[...]
    ms = jnp.mean(x * x, axis=-1, keepdims=True)
    y = x * lax.rsqrt(ms + EPS) * g_ref[...]
    u = (y * (1.0 + sc_ref[...]) + sh_ref[...]).astype(BF16)
    for o_ref, (g, act) in zip(o_refs, groups):
        p = jnp.dot(u, w_ref[:, g * D:(g + 1) * D], preferred_element_type=F32)
        o_ref[...] = act(p).astype(o_ref.dtype)


def _proj_call(x, mod3, norm_g, w_in_bf, *, ctx):
    L = x.shape[1]
    rows = min(PROJ_ROWS, L)
    groups = _PROJ_CTX if ctx else _PROJ_FULL
    mod_row = (lambda b: NB) if ctx else (lambda b: b)
    row_spec = pl.BlockSpec((None, rows, D), lambda b, i: (b, i, 0))
    return pl.pallas_call(
        functools.partial(_proj_kernel, groups=groups),
        grid=(NB, L // rows),
        in_specs=[
            row_spec,
            pl.BlockSpec((None, 1, D), lambda b, i: (mod_row(b), 0, 0)),
            pl.BlockSpec((None, 1, D), lambda b, i: (mod_row(b), 0, 1)),
            _const_spec((1, D)),
            _const_spec((D, N_IN)),
        ],
        out_specs=[row_spec] * len(groups),
        out_shape=[jax.ShapeDtypeStruct((NB, L, D), BF16)] * len(groups),
        compiler_params=_params(("arbitrary", "arbitrary")),
        name="proj_ctx" if ctx else "proj",
    )(x, mod3, mod3, norm_g, w_in_bf)


def _rg_kernel(*refs, reverse, n_tiles, merge):
    if merge:
        (x_ref, xp_ref, xn_ref, h0_ref, cw_ref, cb_ref, wbd_ref, bias_ref, lam_ref, pin_ref, phalo_ref,
         hf_ref, pout_ref, o_ref, hlast_ref, xe_scr, a_scr, b_scr, h_scr) = refs
    else:
        (x_ref, xp_ref, xn_ref, h0_ref, cw_ref, cb_ref, wbd_ref, bias_ref, lam_ref, pin_ref, phalo_ref,
         o_ref, hlast_ref, xe_scr, a_scr, b_scr, h_scr) = refs
    i = pl.program_id(0)
    tile = (n_tiles - 1 - i) if reverse else i
    rows = RG_STEPS * NB
    halo = 2 * NB

    @pl.when(i == 0)
    def _():
        h_scr[...] = h0_ref[...]

    def halo_rows(ref):
        xb = jnp.concatenate([ref[b] for b in range(NB)], axis=0)
        return jnp.dot(phalo_ref[...], xb, preferred_element_type=F32)

    xe_scr[0:halo, :] = jnp.where(tile > 0, halo_rows(xp_ref)[(BF16_ROWS - 2) * NB:BF16_ROWS * NB], 0.0)
    xe_scr[halo + rows:2 * halo + rows, :] = jnp.where(tile < n_tiles - 1, halo_rows(xn_ref)[0:halo], 0.0)

    def fill_block(s, carry):
        t0 = pl.multiple_of(s * RG_SUB_STEPS, RG_SUB_STEPS)
        xb = jnp.concatenate([x_ref[b, pl.ds(t0, RG_SUB_STEPS), :] for b in range(NB)], axis=0)
        r0 = pl.multiple_of(s * RG_SUB, RG_SUB)
        xe_scr[pl.ds(halo + r0, RG_SUB), :] = jnp.dot(pin_ref[...], xb, preferred_element_type=F32)
        return carry

    lax.fori_loop(0, rows // RG_SUB, fill_block, 0)

    cw = 0.5 * cw_ref[...]
    cb = 0.5 * cb_ref[...]
    lam2 = (-0.5 * RG_C * LOG2E) * _softplus(-lam_ref[...])

    def coef_block(s, carry):
        r0 = pl.multiple_of(s * RG_SUB, RG_SUB)
        xh = cb + cw[0:1] * xe_scr[pl.ds(r0, RG_SUB), :]
        xh = xh + cw[1:2] * xe_scr[pl.ds(r0 + NB, RG_SUB), :]
        xh = xh + cw[2:3] * xe_scr[pl.ds(r0 + 2 * NB, RG_SUB), :]
        xh = xh + cw[3:4] * xe_scr[pl.ds(r0 + 3 * NB, RG_SUB), :]
        for p in range(D // 128):
            lo, hi = p * 128, (p + 1) * 128
            xhp = xh[:, lo:hi]
            pre = jnp.dot(xhp.astype(BF16), wbd_ref[p], preferred_element_type=F32) + bias_ref[p]
            t_r = jnp.tanh(pre[:, 0:128])
            t_i = jnp.tanh(pre[:, 128:256])
            a = jnp.exp2(lam2[:, lo:hi] * (t_r + 1.0))
            gain = jnp.sqrt((1.0 - a) * (1.0 + a))
            a_scr[pl.ds(r0, RG_SUB), lo:hi] = a
            b_scr[pl.ds(r0, RG_SUB), lo:hi] = gain * ((t_i + 1.0) * xhp)
        return carry

    lax.fori_loop(0, rows // RG_SUB, coef_block, 0)

    def scan_step(t, h):
        tt = (RG_STEPS - 1 - t) if reverse else t
        r0 = pl.multiple_of(tt * NB, NB)
        h = a_scr[pl.ds(r0, NB), :] * h + b_scr[pl.ds(r0, NB), :]
        b_scr[pl.ds(r0, NB), :] = h
        return h

    h_last = lax.fori_loop(0, RG_STEPS, scan_step, h_scr[...], unroll=8)
    h_scr[...] = h_last
    hlast_ref[...] = h_last

    if merge:
        def out_block(s, carry):
            r0 = pl.multiple_of(s * RG_SUB, RG_SUB)
            t0 = pl.multiple_of(s * RG_SUB_STEPS, RG_SUB_STEPS)
            h = (b_scr[pl.ds(r0, RG_SUB), :] + hf_ref[pl.ds(r0, RG_SUB), :].astype(F32)).astype(BF16)
            hb = jnp.dot(pout_ref[...], h, preferred_element_type=F32).astype(o_ref.dtype)
            for b in range(NB):
                o_ref[b, pl.ds(t0, RG_SUB_STEPS), :] = hb[b * RG_SUB_STEPS:(b + 1) * RG_SUB_STEPS]
            return carry
        lax.fori_loop(0, rows // RG_SUB, out_block, 0)
    else:
        o_ref[...] = b_scr[...].astype(o_ref.dtype)


def _rg_call(rgx, h0, cw, cb, wbd, bias, lam, *, reverse, h_fwd=None, name):
    L = rgx.shape[1]
    rows = RG_STEPS * NB
    n_tiles = L // RG_STEPS
    per = RG_STEPS // BF16_ROWS
    n_halo = L // BF16_ROWS
    tile = (lambda i: n_tiles - 1 - i) if reverse else (lambda i: i)
    bm_spec = pl.BlockSpec((NB, RG_STEPS, D), lambda i: (0, tile(i), 0))
    tm_spec = pl.BlockSpec((rows, D), lambda i: (tile(i), 0))
    in_specs = [
        bm_spec,
        pl.BlockSpec((NB, BF16_ROWS, D), lambda i: (0, jnp.maximum(tile(i) * per - 1, 0), 0)),
        pl.BlockSpec((NB, BF16_ROWS, D), lambda i: (0, jnp.minimum((tile(i) + 1) * per, n_halo - 1), 0)),
        _const_spec((NB, D)),
        _const_spec((4, D)),
        _const_spec((1, D)),
        _const_spec((D // 128, 128, 256)),
        _const_spec((D // 128, 1, 256)),
        _const_spec((1, D)),
        _const_spec((RG_SUB, RG_SUB)),
        _const_spec((NB * BF16_ROWS, NB * BF16_ROWS)),
    ]
    args = [rgx, rgx, rgx, h0, cw, cb, wbd, bias, lam,
            _swap_perm(NB, RG_SUB_STEPS), _swap_perm(NB, BF16_ROWS)]
    if h_fwd is None:
        out_spec, out_shape = tm_spec, jax.ShapeDtypeStruct((L * NB, D), BF16)
    else:
        in_specs += [tm_spec, _const_spec((RG_SUB, RG_SUB))]
        args += [h_fwd, _swap_perm(RG_SUB_STEPS, NB)]
        out_spec, out_shape = bm_spec, jax.ShapeDtypeStruct((NB, L, D), BF16)
    return pl.pallas_call(
        functools.partial(_rg_kernel, reverse=reverse, n_tiles=n_tiles, merge=h_fwd is not None),
        grid=(n_tiles,),
        in_specs=in_specs,
        out_specs=[out_spec, pl.BlockSpec((NB, D), lambda i: (0, 0))],
        out_shape=[out_shape, jax.ShapeDtypeStruct((NB, D), F32)],
        scratch_shapes=[
            pltpu.VMEM((rows + 4 * NB, D), F32),
            pltpu.VMEM((rows, D), F32),
            pltpu.VMEM((rows, D), F32),
            pltpu.VMEM((NB, D), F32),
        ],
        compiler_params=_params(("arbitrary",)),
        name=name,
    )(*args)


def _scan_rows(x, op, fill, reverse):
    n = x.shape[0]
    row = lax.broadcasted_iota(jnp.int32, x.shape, 0)
    sh = 1
    while sh < n:
        if reverse:
            shifted = jnp.where(row < n - sh, pltpu.roll(x, n - sh, axis=0), fill)
        else:
            shifted = jnp.where(row >= sh, pltpu.roll(x, sh, axis=0), fill)
        x = op(x, shifted)
        sh *= 2
    return x


def _ml_kernel(*refs, reverse, n_tiles, n_tok, mode):
    L = ML_CHUNK
    n_chunks = n_tok // L
    if mode == "ctx":
        (x_ref, cw_ref, cb_ref, wqk_ref, wv_ref, wg_ref, bg_ref,
         c_out, n_out, m_out, xt_scr, q_scr, k_scr, v_scr, g_scr, c_scr, n_scr, m_scr) = refs
    elif mode == "fwd":
        (x_ref, xp_ref, xn_ref, perm_ref, c0_ref, n0_ref, m0_ref, cw_ref, cb_ref, wqk_ref, wv_ref, wg_ref, bg_ref,
         o_ref, xt_scr, q_scr, k_scr, v_scr, g_scr, c_scr, n_scr, m_scr, h_scr) = refs
    else:
        (x_ref, xp_ref, xn_ref, perm_ref, c0_ref, n0_ref, m0_ref, cw_ref, cb_ref, wqk_ref, wv_ref, wg_ref, bg_ref,
         hf_ref, ng_ref, sk_ref, o_ref, xt_scr, q_scr, k_scr, v_scr, g_scr, c_scr, n_scr, m_scr, h_scr, xc_scr) = refs
    j = pl.program_id(1)
    tile = (n_tiles - 1 - j) if reverse else j

    @pl.when(j == 0)
    def _():
        if mode == "ctx":
            c_scr[...] = jnp.zeros_like(c_scr)
            n_scr[...] = jnp.zeros_like(n_scr)
            m_scr[...] = jnp.zeros_like(m_scr)
        else:
            c_scr[...] = c0_ref[...]
            n_scr[...] = n0_ref[...]
            m_scr[...] = m0_ref[...]

    if mode == "ctx":
        xt_scr[8:16, :] = jnp.zeros((8, D), F32)
        xt_scr[16:16 + n_tok, :] = x_ref[...].astype(F32)
        xt_scr[16 + n_tok:24 + n_tok, :] = jnp.zeros((8, D), F32)
    else:
        for g in range(GRID_W // BF16_ROWS):
            xg = x_ref[g * BF16_ROWS:(g + 1) * BF16_ROWS].reshape(BF16_ROWS * ML_GROUP_COLS, D)
            yg = jnp.dot(perm_ref[...], xg, preferred_element_type=F32)
            for w in range(ML_GROUP_COLS):
                dst = 16 + w * GRID_W + g * BF16_ROWS
                xt_scr[dst:dst + BF16_ROWS, :] = yg[w * BF16_ROWS:(w + 1) * BF16_ROWS]
        last = ML_GROUP_COLS - 1
        prev = jnp.concatenate([xp_ref[0].astype(F32)[last:last + 1], xp_ref[1].astype(F32)[last:last + 1]], axis=0)
        xt_scr[14:16, :] = jnp.where(tile > 0, prev, 0.0)
        xt_scr[16 + n_tok:17 + n_tok, :] = jnp.where(tile < n_tiles - 1, xn_ref[0].astype(F32)[0:1], 0.0)

    cw = cw_ref[...]
    a_rows = min(ML_A_ROWS, n_tok)
    for sb in range(n_tok // a_rows):
        r0 = sb * a_rows
        xt = xt_scr[16 + r0:16 + r0 + a_rows, :]
        xc = cb_ref[...] + cw[0:1] * xt_scr[14 + r0:14 + r0 + a_rows, :] + cw[1:2] * xt_scr[15 + r0:15 + r0 + a_rows, :]
        xc = xc + cw[2:3] * xt + cw[3:4] * xt_scr[17 + r0:17 + r0 + a_rows, :]
        xc = _silu(xc)
        if mode == "rev":
            xc_scr[r0:r0 + a_rows, :] = xc
        gates = jnp.zeros((a_rows, 256), F32) + bg_ref[...]
        for h in range(HEADS):
            lo, hi = h * DH, (h + 1) * DH
            qk = jnp.dot(xc[:, lo:hi].astype(BF16), wqk_ref[h], preferred_element_type=F32)
            q = qk[:, 0:DH].astype(BF16)
            k = qk[:, DH:2 * DH].astype(BF16)
            v = jnp.dot(xt[:, lo:hi].astype(BF16), wv_ref[h], preferred_element_type=F32).astype(BF16)
            gates = gates + jnp.dot(q, wg_ref[h], preferred_element_type=F32)
            gates = gates + jnp.dot(k, wg_ref[HEADS + h], preferred_element_type=F32)
            gates = gates + jnp.dot(v, wg_ref[2 * HEADS + h], preferred_element_type=F32)
            q_scr[r0:r0 + a_rows, lo:hi] = q
            k_scr[r0:r0 + a_rows, lo:hi] = k
            v_scr[r0:r0 + a_rows, lo:hi] = v
        g_scr[r0:r0 + a_rows, :] = gates

    if mode != "ctx":
        row_id = lax.broadcasted_iota(jnp.int32, (L, L), 0)
        col_id = lax.broadcasted_iota(jnp.int32, (L, L), 1)
        keep = (col_id >= row_id) if reverse else (col_id <= row_id)

    def chunk_step(ci, carry):
        ck = (n_chunks - 1 - ci) if reverse else ci
        r0 = pl.multiple_of(ck * L, L)
        li = g_scr[pl.ds(r0, L), 0:128]
        lf = _log_sigmoid(g_scr[pl.ds(r0, L), 128:256])
        cum = _scan_rows(lf, jnp.add, 0.0, reverse)
        g = li - cum
        m_prev = m_scr[0:1, :]
        b_tot = cum[0:1, :] if reverse else cum[L - 1:L, :]
        gmax = jnp.max(g, axis=0, keepdims=True)
        m_new = jnp.maximum(b_tot + m_prev, b_tot + gmax)
        ws = jnp.exp(b_tot + g - m_new)
        dec = jnp.broadcast_to(jnp.exp(b_tot + m_prev - m_new), (L, 128))
        if mode != "ctx":
            inter = cum + m_prev
            m_t = jnp.maximum(inter, cum + _scan_rows(g, jnp.maximum, -jnp.inf, reverse))
            c_all = cum - m_t
            si_all = jnp.exp(inter - m_t)
            fl_all = jnp.exp(-m_t)
            g_row = g.T
        for h in range(HEADS):
            lo, hi = h * DH, (h + 1) * DH
            q = q_scr[pl.ds(r0, L), lo:hi]
            kb = k_scr[pl.ds(r0, L), lo:hi]
            v = v_scr[pl.ds(r0, L), lo:hi]
            dec_c = dec[:, h:h + 1]
            c_old = c_scr[h]
            n_old = n_scr[h]
            if mode != "ctx":
                si_c = si_all[:, h:h + 1]
                s = lax.dot_general(q, kb, (((1,), (1,)), ((), ())), preferred_element_type=F32)
                arg = jnp.where(keep, c_all[:, h:h + 1] + g_row[h:h + 1, :], -jnp.inf)
                p = s * jnp.exp(arg)
                num = jnp.dot(p.astype(BF16), v, preferred_element_type=F32)
                num = num + si_c * jnp.dot(q, c_old.astype(BF16), preferred_element_type=F32)
                den = jnp.sum(p, axis=1, keepdims=True)
                den = den + si_c * jnp.sum(q.astype(F32) * n_old, axis=1, keepdims=True)
                h_scr[pl.ds(r0, L), lo:hi] = num / jnp.maximum(jnp.abs(den), fl_all[:, h:h + 1])
            kw = kb.astype(F32) * ws[:, h:h + 1]
            upd = lax.dot_general(kw.astype(BF16), v, (((0,), (0,)), ((), ())), preferred_element_type=F32)
            c_scr[h] = dec_c * c_old + upd
            n_scr[h] = dec_c[0:1, :] * n_old + jnp.sum(kw, axis=0, keepdims=True)
        m_scr[...] = jnp.broadcast_to(m_new, m_scr.shape)
        return carry

    lax.fori_loop(0, n_chunks, chunk_step, 0)

    if mode == "ctx":
        c_out[...] = c_scr[...]
        n_out[...] = n_scr[...]
        m_out[...] = m_scr[...]
    elif mode == "fwd":
        o_ref[...] = h_scr[...].astype(o_ref.dtype)
    else:
        for sb in range(n_tok // a_rows):
            r0 = sb * a_rows
            ht = h_scr[r0:r0 + a_rows, :] + hf_ref[r0:r0 + a_rows, :].astype(F32)
            for h in range(HEADS):
                lo, hi = h * DH, (h + 1) * DH
                hh = ht[:, lo:hi]
                mu = jnp.mean(hh, axis=-1, keepdims=True)
                var = jnp.mean(jnp.square(hh - mu), axis=-1, keepdims=True)
                hn = (hh - mu) * lax.rsqrt(var + EPS)
                h_scr[r0:r0 + a_rows, lo:hi] = hn * ng_ref[:, lo:hi] + sk_ref[:, lo:hi] * xc_scr[r0:r0 + a_rows, lo:hi]
        for g in range(GRID_W // BF16_ROWS):
            zg = jnp.concatenate(
                [h_scr[w * GRID_W + g * BF16_ROWS:w * GRID_W + (g + 1) * BF16_ROWS, :] for w in range(ML_GROUP_COLS)],
                axis=0).astype(BF16)
            og = jnp.dot(perm_ref[...], zg, preferred_element_type=F32).astype(o_ref.dtype)
            o_ref[g * BF16_ROWS:(g + 1) * BF16_ROWS] = og.reshape(BF16_ROWS, ML_GROUP_COLS, D)


def _ml_weight_specs():
    return [
        _const_spec((4, D)),
        _const_spec((1, D)),
        _const_spec((HEADS, DH, 2 * DH)),
        _const_spec((HEADS, DH, DH)),
        _const_spec((3 * HEADS, DH, 256)),
        _const_spec((1, 256)),
    ]


def _ml_scratch(n_tok, mode):
    scratch = [
        pltpu.VMEM((n_tok + 32, D), F32),
        pltpu.VMEM((n_tok, D), BF16),
        pltpu.VMEM((n_tok, D), BF16),
        pltpu.VMEM((n_tok, D), BF16),
        pltpu.VMEM((n_tok, 256), F32),
        pltpu.VMEM((HEADS, DH, DH), F32),
        pltpu.VMEM((HEADS, 1, DH), F32),
        pltpu.VMEM((8, 128), F32),
    ]
    if mode != "ctx":
        scratch.append(pltpu.VMEM((n_tok, D), F32))
    if mode == "rev":
        scratch.append(pltpu.VMEM((n_tok, D), F32))
    return scratch


_STATE_SHAPES = [
    jax.ShapeDtypeStruct((NB, HEADS, DH, DH), F32),
    jax.ShapeDtypeStruct((NB, HEADS, 1, DH), F32),
    jax.ShapeDtypeStruct((NB, 8, 128), F32),
]


def _state_specs():
    return [
        pl.BlockSpec((None, HEADS, DH, DH), lambda b, j: (b, 0, 0, 0)),
        pl.BlockSpec((None, HEADS, 1, DH), lambda b, j: (b, 0, 0, 0)),
        pl.BlockSpec((None, 8, 128), lambda b, j: (b, 0, 0)),
    ]


def _ml_ctx_call(mlx_ctx, weights, *, reverse, name):
    return pl.pallas_call(
        functools.partial(_ml_kernel, reverse=reverse, n_tiles=1, n_tok=CTX, mode="ctx"),
        grid=(NB, 1),
        in_specs=[pl.BlockSpec((None, CTX, D), lambda b, j: (b, 0, 0))] + _ml_weight_specs(),
        out_specs=_state_specs(),
        out_shape=_STATE_SHAPES,
        scratch_shapes=_ml_scratch(CTX, "ctx"),
        compiler_params=_params(("arbitrary", "arbitrary")),
        name=name,
    )(mlx_ctx, *weights)


def _ml_lat_call(mlx_grid, state, weights, *, reverse, merge_with=None, name):
    n_tiles = GRID_W // ML_GROUP_COLS
    tile = (lambda j: n_tiles - 1 - j) if reverse else (lambda j: j)
    grid_spec = pl.BlockSpec((None, GRID_W, ML_GROUP_COLS, D), lambda b, j: (b, 0, tile(j), 0))
    tok_spec = pl.BlockSpec((None, ML_TILE, D), lambda b, j: (b, tile(j), 0))
    in_specs = [
        grid_spec,
        pl.BlockSpec((None, 2, ML_GROUP_COLS, D), lambda b, j: (b, GRID_W // 2 - 1, jnp.maximum(tile(j) - 1, 0), 0)),
        pl.BlockSpec((None, 1, ML_GROUP_COLS, D), lambda b, j: (b, 0, jnp.minimum(tile(j) + 1, n_tiles - 1), 0)),
        _const_spec((BF16_ROWS * ML_GROUP_COLS, BF16_ROWS * ML_GROUP_COLS)),
    ] + _state_specs() + _ml_weight_specs()
    args = [mlx_grid, mlx_grid, mlx_grid, _swap_perm(BF16_ROWS, ML_GROUP_COLS), *state, *weights]
    if merge_with is None:
        mode = "fwd"
        out_spec, out_shape = tok_spec, jax.ShapeDtypeStruct((NB, SEQ, D), BF16)
    else:
        mode = "rev"
        h_fwd, norm_g, skip = merge_with
        in_specs += [tok_spec, _const_spec((1, D)), _const_spec((1, D))]
        args += [h_fwd, norm_g, skip]
        out_spec, out_shape = grid_spec, jax.ShapeDtypeStruct((NB, GRID_W, GRID_W, D), BF16)
    return pl.pallas_call(
        functools.partial(_ml_kernel, reverse=reverse, n_tiles=n_tiles, n_tok=ML_TILE, mode=mode),
        grid=(NB, n_tiles),
        in_specs=in_specs,
        out_specs=out_spec,
        out_shape=out_shape,
        scratch_shapes=_ml_scratch(ML_TILE, mode),
        compiler_params=_params(("arbitrary", "arbitrary")),
        name=name,
    )(*args)


def _final_kernel(x_ref, hrg_ref, grg_ref, hml_ref, smlo_ref, sgr_ref, sgm_ref, g1_ref, sh2_ref, sc2_ref, g2_ref,
                  n2_ref, nf_ref, wbr_ref, wbm_ref, wo_ref, wfi_ref, wfo_ref, o_ref, act_scr):
    dot = functools.partial(jnp.dot, preferred_element_type=F32)
    y_rg = (hrg_ref[...].astype(F32) * grg_ref[...].astype(F32)).astype(BF16)
    y_ml = (hml_ref[...].astype(F32) * smlo_ref[...].astype(F32)).astype(BF16)
    mix = sgr_ref[...].astype(F32) * dot(y_rg, wbr_ref[...])
    mix = mix + sgm_ref[...].astype(F32) * dot(y_ml, wbm_ref[...])
    x1 = x_ref[...] + g1_ref[...] * dot(mix.astype(BF16), wo_ref[...])
    ms = jnp.mean(x1 * x1, axis=-1, keepdims=True)
    hn = x1 * lax.rsqrt(ms + EPS) * n2_ref[...]
    hb = (hn * (1.0 + sc2_ref[...]) + sh2_ref[...]).astype(BF16)
    step = 256
    for c in range(D_FF // step):
        gate = dot(hb, wfi_ref[:, c * step:(c + 1) * step])
        up = dot(hb, wfi_ref[:, D_FF + c * step:D_FF + (c + 1) * step])
        act_scr[:, c * step:(c + 1) * step] = (_silu(gate) * up).astype(BF16)
    x2 = x1 + g2_ref[...] * dot(act_scr[...], wfo_ref[...])
    ms2 = jnp.mean(x2 * x2, axis=-1, keepdims=True)
    o_ref[...] = x2 * lax.rsqrt(ms2 + EPS) * nf_ref[...]


def _final_call(x, h_rg, grg, h_ml, smlo, sgr, sgm, mod3, norm2_g, final_g, wbr, wbm, wo, wfi, wfo):
    rows = PROJ_ROWS
    row_spec = pl.BlockSpec((None, rows, D), lambda b, i: (b, i, 0))
    mod_spec = lambda g: pl.BlockSpec((None, 1, D), lambda b, i: (b, 0, g))
    return pl.pallas_call(
        _final_kernel,
        grid=(NB, SEQ // rows),
        in_specs=[
            row_spec, row_spec, row_spec, row_spec, row_spec, row_spec, row_spec,
            mod_spec(2), mod_spec(3), mod_spec(4), mod_spec(5),
            _const_spec((1, D)), _const_spec((1, D)),
            _const_spec((D, D)), _const_spec((D, D)), _const_spec((D, D)),
            _const_spec((D, 2 * D_FF)), _const_spec((D_FF, D)),
        ],
        out_specs=row_spec,
        out_shape=jax.ShapeDtypeStruct((NB, SEQ, D), F32),
        scratch_shapes=[pltpu.VMEM((rows, D_FF), BF16)],
        compiler_params=_params(("arbitrary", "arbitrary")),
        name="final",
    )(x, h_rg, grg, h_ml, smlo, sgr, sgm, mod3, mod3, mod3, mod3, norm2_g, final_g, wbr, wbm, wo, wfi, wfo)


def _pair_blockdiag(w):
    w = w.reshape(8, 2, 64, 64)
    z = jnp.zeros((8, 64, 64), w.dtype)
    top = jnp.concatenate([w[:, 0], z], axis=2)
    bot = jnp.concatenate([z, w[:, 1]], axis=2)
    return jnp.concatenate([top, bot], axis=1)


def _rg_weights(wa, ba, wx, bx):
    wbd = jnp.concatenate([_pair_blockdiag(wa), _pair_blockdiag(wx)], axis=2).astype(BF16)
    bias = 0.5 * jnp.concatenate([ba.reshape(8, 1, 128), bx.reshape(8, 1, 128)], axis=2)
    return wbd, bias


def _head_blockdiag(w):
    w = w.reshape(HEADS, DH // 4, 4, 4)
    eye = jnp.eye(DH // 4, dtype=w.dtype)
    dense = w[:, :, :, None, :] * eye[None, :, None, :, None]
    return dense.reshape(HEADS, DH, DH)


def _gate_weights(wi, bi, wf, bf):
    w = jnp.zeros((3 * D, 256), F32).at[:, 0:HEADS].set(wi).at[:, 128:128 + HEADS].set(wf)
    b = jnp.zeros((1, 256), F32).at[0, 0:HEADS].set(bi).at[0, 128:128 + HEADS].set(bf)
    return w.reshape(3 * HEADS, DH, 256).astype(BF16), b


def kernel(x, c, ctx, c_ctx, w_mod, b_mod, norm1_g, norm2_g, w_in, rg_conv_w, rg_conv_b, rg_wa, rg_ba, rg_wx,
           rg_bx, rg_lambda, ml_conv_w, ml_conv_b, ml_wq, ml_wk, ml_wv, ml_wi, ml_bi, ml_wf, ml_bf,
           ml_norm_g, ml_skip, w_branch_rg, w_branch_ml, w_out, w_ffn_in, w_ffn_out, final_norm_g):
    mod = _mod_call(c, c_ctx, w_mod[0], b_mod[0])
    mod3 = mod.reshape(2 * NB, 1, 6 * D)
    w_in_bf = w_in[0].astype(BF16)
    norm1 = norm1_g[0].reshape(1, D)

    rgx, grg, mlx, smlo, sgr, sgm = _proj_call(x, mod3, norm1, w_in_bf, ctx=False)
    rgx_c, mlx_c = _proj_call(ctx, mod3, norm1, w_in_bf, ctx=True)

    rg_cw = rg_conv_w[0]
    rg_cb = rg_conv_b[0].reshape(1, D)
    zero_h = jnp.zeros((NB, D), F32)
    rg_w = [_rg_weights(rg_wa[0, d], rg_ba[0, d], rg_wx[0, d], rg_bx[0, d]) + (rg_lambda[0, d].reshape(1, D),)
            for d in range(2)]
    _, h0_f = _rg_call(rgx_c, zero_h, rg_cw, rg_cb, *rg_w[0], reverse=False, name="rg_ctx_fwd")
    _, h0_r = _rg_call(rgx_c, zero_h, rg_cw, rg_cb, *rg_w[1], reverse=True, name="rg_ctx_rev")
    h_f, _ = _rg_call(rgx, h0_f, rg_cw, rg_cb, *rg_w[0], reverse=False, name="rg_fwd")
    h_rg, _ = _rg_call(rgx, h0_r, rg_cw, rg_cb, *rg_w[1], reverse=True, h_fwd=h_f, name="rg_rev")

    wqk = jnp.concatenate([_head_blockdiag(ml_wq[0]), _head_blockdiag(ml_wk[0]) * (DH ** -0.5)], axis=2).astype(BF16)
    wv = _head_blockdiag(ml_wv[0]).astype(BF16)
    ml_cw = ml_conv_w[0]
    ml_cb = ml_conv_b[0].reshape(1, D)
    ml_w = [(ml_cw, ml_cb, wqk, wv) + _gate_weights(ml_wi[0, d], ml_bi[0, d], ml_wf[0, d], ml_bf[0, d])
            for d in range(2)]
    mlx_grid = mlx.reshape(NB, GRID_W, GRID_W, D)
    st_f = _ml_ctx_call(mlx_c, ml_w[0], reverse=False, name="ml_ctx_fwd")
    st_r = _ml_ctx_call(mlx_c, ml_w[1], reverse=True, name="ml_ctx_rev")
    hm_f = _ml_lat_call(mlx_grid, st_f, ml_w[0], reverse=False, name="ml_fwd")
    h_ml = _ml_lat_call(mlx_grid, st_r, ml_w[1], reverse=True,
                        merge_with=(hm_f, ml_norm_g[0].reshape(1, D), ml_skip[0].reshape(1, D)), name="ml_rev")

    return _final_call(
        x, h_rg, grg, h_ml.reshape(NB, SEQ, D), smlo, sgr, sgm, mod3,
        norm2_g[0].reshape(1, D), final_norm_g.reshape(1, D),
        w_branch_rg[0].astype(BF16), w_branch_ml[0].astype(BF16), w_out[0].astype(BF16),
        w_ffn_in[0].astype(BF16), w_ffn_out[0].astype(BF16))
```

```python
import functools

import jax
import jax.numpy as jnp
from jax import lax
from jax.experimental import pallas as pl
from jax.experimental.pallas import tpu as pltpu

F32 = jnp.float32
BF16 = jnp.bfloat16

D = 1024
NB = 8
SEQ = 4096
GRID_W = 64
CTX = 256
EPS = 1e-6
RG_C = 8.0
HEADS = 4
DH = D // HEADS
D_FF = 2816
N_IN = 6 * D
LOG2E = 1.4426950408889634

VMEM_LIMIT = 60 * 1024 * 1024
BF16_ROWS = 16

PROJ_ROWS = 512
RG_STEPS = 128
RG_SUB = 256
RG_SUB_STEPS = RG_SUB // NB
ML_CHUNK = 256
ML_GROUP_COLS = 16
ML_TILE = ML_GROUP_COLS * GRID_W
ML_A_ROWS = 512

assert ML_CHUNK == DH


def _sigmoid(x):
    return 0.5 * (jnp.tanh(0.5 * x) + 1.0)


def _silu(x):
    return x * _sigmoid(x)


def _softplus(x):
    return jnp.maximum(x, 0.0) + jnp.log(1.0 + jnp.exp(-jnp.abs(x)))


def _log_sigmoid(x):
    return jnp.minimum(x, 0.0) - jnp.log(1.0 + jnp.exp(-jnp.abs(x)))


def _split3(x):
    hi = x.astype(BF16)
    r1 = x - hi.astype(F32)
    mid = r1.astype(BF16)
    lo = (r1 - mid.astype(F32)).astype(BF16)
    return hi, mid, lo


def _params(sem):
    return pltpu.CompilerParams(dimension_semantics=sem, vmem_limit_bytes=VMEM_LIMIT)


def _const_spec(shape):
    nd = len(shape)
    return pl.BlockSpec(shape, lambda *_: (0,) * nd, pipeline_mode=pl.Buffered(1))


def _swap_perm(a, b):
    n = a * b
    out_row = jnp.arange(n)
    src = (out_row % a) * b + out_row // a
    return (src[:, None] == jnp.arange(n)[None, :]).astype(BF16)


def _mod_kernel(c_ref, cc_ref, w_ref, b_ref, o_ref):
    s = jnp.concatenate([c_ref[...], jnp.broadcast_to(cc_ref[...], (NB, D))], axis=0)
    s = _silu(s)
    s_hi, s_mid, _ = _split3(s)
    w_hi, w_mid, _ = _split3(w_ref[...])
    dot = functools.partial(jnp.dot, preferred_element_type=F32)
    o_ref[...] = dot(s_hi, w_hi) + dot(s_mid, w_hi) + dot(s_hi, w_mid) + b_ref[...]


def _mod_call(c, c_ctx, w_mod, b_mod):
    return pl.pallas_call(
        _mod_kernel,
        grid=(6,),
        in_specs=[
            pl.BlockSpec((NB, D), lambda g: (0, 0)),
            pl.BlockSpec((1, D), lambda g: (0, 0)),
            pl.BlockSpec((D, D), lambda g: (0, g)),
            pl.BlockSpec((1, D), lambda g: (0, g)),
        ],
        out_specs=pl.BlockSpec((2 * NB, D), lambda g: (0, g)),
        out_shape=jax.ShapeDtypeStruct((2 * NB, 6 * D), F32),
        compiler_params=_params(("arbitrary",)),
        name="mod",
    )(c, c_ctx.reshape(1, D), w_mod, b_mod.reshape(1, 6 * D))


def _gelu_tanh(x):
    return jax.nn.gelu(x, approximate=True)


def _identity(x):
    return x


_PROJ_FULL = ((0, _identity), (1, _gelu_tanh), (2, _identity), (3, _sigmoid), (4, _sigmoid), (5, _sigmoid))
_PROJ_CTX = ((0, _identity), (2, _identity))


def _proj_kernel(x_ref, sh_ref, sc_ref, g_ref, w_ref, *o_refs, groups):
    x = x_ref[...]
    ms = jnp.mean(x * x, axis=-1, keepdims=True)
    y = x * lax.rsqrt(ms + EPS) * g_ref[...]
    u = (y * (1.0 + sc_ref[...]) + sh_ref[...]).astype(BF16)
    for o_ref, (g, act) in zip(o_refs, groups):
        p = jnp.dot(u, w_ref[:, g * D:(g + 1) * D], preferred_element_type=F32)
        o_ref[...] = act(p).astype(o_ref.dtype)


def _proj_call(x, mod3, norm_g, w_in_bf, *, ctx):
    L = x.shape[1]
    rows = min(PROJ_ROWS, L)
    groups = _PROJ_CTX if ctx else _PROJ_FULL
    mod_row = (lambda b: NB) if ctx else (lambda b: b)
    row_spec = pl.BlockSpec((None, rows, D), lambda b, i: (b, i, 0))
    return pl.pallas_call(
        functools.partial(_proj_kernel, groups=groups),
        grid=(NB, L // rows),
        in_specs=[
            row_spec,
            pl.BlockSpec((None, 1, D), lambda b, i: (mod_row(b), 0, 0)),
            pl.BlockSpec((None, 1, D), lambda b, i: (mod_row(b), 0, 1)),
            _const_spec((1, D)),
            _const_spec((D, N_IN)),
        ],
        out_specs=[row_spec] * len(groups),
        out_shape=[jax.ShapeDtypeStruct((NB, L, D), BF16)] * len(groups),
        compiler_params=_params(("arbitrary", "arbitrary")),
        name="proj_ctx" if ctx else "proj",
    )(x, mod3, mod3, norm_g, w_in_bf)


def _rg_kernel(*refs, reverse, n_tiles, merge):
    if merge:
        (x_ref, xp_ref, xn_ref, h0_ref, cw_ref, cb_ref, wbd_ref, lam_ref, pin_ref, phalo_ref,
         hf_ref, pout_ref, o_ref, hlast_ref, xe_scr, a_scr, b_scr, h_scr) = refs
    else:
        (x_ref, xp_ref, xn_ref, h0_ref, cw_ref, cb_ref, wbd_ref, lam_ref, pin_ref, phalo_ref,
         o_ref, hlast_ref, xe_scr, a_scr, b_scr, h_scr) = refs
    i = pl.program_id(0)
    tile = (n_tiles - 1 - i) if reverse else i
    rows = RG_STEPS * NB
    halo = 2 * NB

    @pl.when(i == 0)
    def _():
        h_scr[...] = h0_ref[...]

    def halo_rows(ref):
        xb = jnp.concatenate([ref[b] for b in range(NB)], axis=0)
        return jnp.dot(phalo_ref[...], xb, preferred_element_type=F32)

    xe_scr[0:halo, :] = jnp.where(tile > 0, halo_rows(xp_ref)[(BF16_ROWS - 2) * NB:BF16_ROWS * NB], 0.0)
    xe_scr[halo + rows:2 * halo + rows, :] = jnp.where(tile < n_tiles - 1, halo_rows(xn_ref)[0:halo], 0.0)

    def fill_block(s, carry):
        t0 = pl.multiple_of(s * RG_SUB_STEPS, RG_SUB_STEPS)
        xb = jnp.concatenate([x_ref[b, pl.ds(t0, RG_SUB_STEPS), :] for b in range(NB)], axis=0)
        r0 = pl.multiple_of(s * RG_SUB, RG_SUB)
        xe_scr[pl.ds(halo + r0, RG_SUB), :] = jnp.dot(pin_ref[...], xb, preferred_element_type=F32)
        return carry

    lax.fori_loop(0, rows // RG_SUB, fill_block, 0)

    cw = 0.5 * cw_ref[...]
    cb = 0.5 * cb_ref[...]
    lam2 = (-0.5 * RG_C * LOG2E) * _softplus(-lam_ref[...])

    ones = jnp.ones((RG_SUB, 128), BF16)

    def coef_block(s, carry):
        r0 = pl.multiple_of(s * RG_SUB, RG_SUB)
        xh = cb + cw[0:1] * xe_scr[pl.ds(r0, RG_SUB), :]
        xh = xh + cw[1:2] * xe_scr[pl.ds(r0 + NB, RG_SUB), :]
        xh = xh + cw[2:3] * xe_scr[pl.ds(r0 + 2 * NB, RG_SUB), :]
        xh = xh + cw[3:4] * xe_scr[pl.ds(r0 + 3 * NB, RG_SUB), :]
        for p in range(D // 128):
            lo, hi = p * 128, (p + 1) * 128
            xhp = xh[:, lo:hi]
            lhs = jnp.concatenate([xhp.astype(BF16), ones], axis=1)
            pre = jnp.dot(lhs, wbd_ref[p], preferred_element_type=F32)
            t_r = jnp.tanh(pre[:, 0:128])
            t_i = jnp.tanh(pre[:, 128:256])
            a = jnp.exp2(lam2[:, lo:hi] * (t_r + 1.0))
            y = (1.0 - a) * (1.0 + a)
            gain = jnp.where(y > 0.0, y * lax.rsqrt(y), 0.0)
            a_scr[pl.ds(r0, RG_SUB), lo:hi] = a
            b_scr[pl.ds(r0, RG_SUB), lo:hi] = gain * ((t_i + 1.0) * xhp)
        return carry

    lax.fori_loop(0, rows // RG_SUB, coef_block, 0)

    def scan_step(t, h):
        tt = (RG_STEPS - 1 - t) if reverse else t
        r0 = pl.multiple_of(tt * NB, NB)
        h = a_scr[pl.ds(r0, NB), :] * h + b_scr[pl.ds(r0, NB), :]
        b_scr[pl.ds(r0, NB), :] = h
        return h

    h_last = lax.fori_loop(0, RG_STEPS, scan_step, h_scr[...], unroll=8)
    h_scr[...] = h_last
    hlast_ref[...] = h_last

    if merge:
        def out_block(s, carry):
            r0 = pl.multiple_of(s * RG_SUB, RG_SUB)
            t0 = pl.multiple_of(s * RG_SUB_STEPS, RG_SUB_STEPS)
            h = (b_scr[pl.ds(r0, RG_SUB), :] + hf_ref[pl.ds(r0, RG_SUB), :].astype(F32)).astype(BF16)
            hb = jnp.dot(pout_ref[...], h, preferred_element_type=F32).astype(o_ref.dtype)
            for b in range(NB):
                o_ref[b, pl.ds(t0, RG_SUB_STEPS), :] = hb[b * RG_SUB_STEPS:(b + 1) * RG_SUB_STEPS]
            return carry
        lax.fori_loop(0, rows // RG_SUB, out_block, 0)
    else:
        o_ref[...] = b_scr[...].astype(o_ref.dtype)


def _rg_call(rgx, h0, cw, cb, wbd, lam, *, reverse, h_fwd=None, name):
    L = rgx.shape[1]
    rows = RG_STEPS * NB
    n_tiles = L // RG_STEPS
    per = RG_STEPS // BF16_ROWS
    n_halo = L // BF16_ROWS
    tile = (lambda i: n_tiles - 1 - i) if reverse else (lambda i: i)
    bm_spec = pl.BlockSpec((NB, RG_STEPS, D), lambda i: (0, tile(i), 0))
    tm_spec = pl.BlockSpec((rows, D), lambda i: (tile(i), 0))
    in_specs = [
        bm_spec,
        pl.BlockSpec((NB, BF16_ROWS, D), lambda i: (0, jnp.maximum(tile(i) * per - 1, 0), 0)),
        pl.BlockSpec((NB, BF16_ROWS, D), lambda i: (0, jnp.minimum((tile(i) + 1) * per, n_halo - 1), 0)),
        _const_spec((NB, D)),
        _const_spec((4, D)),
        _const_spec((1, D)),
        _const_spec((D // 128, 256, 256)),
        _const_spec((1, D)),
        _const_spec((RG_SUB, RG_SUB)),
        _const_spec((NB * BF16_ROWS, NB * BF16_ROWS)),
    ]
    args = [rgx, rgx, rgx, h0, cw, cb, wbd, lam,
            _swap_perm(NB, RG_SUB_STEPS), _swap_perm(NB, BF16_ROWS)]
    if h_fwd is None:
        out_spec, out_shape = tm_spec, jax.ShapeDtypeStruct((L * NB, D), BF16)
    else:
        in_specs += [tm_spec, _const_spec((RG_SUB, RG_SUB))]
        args += [h_fwd, _swap_perm(RG_SUB_STEPS, NB)]
        out_spec, out_shape = bm_spec, jax.ShapeDtypeStruct((NB, L, D), BF16)
    return pl.pallas_call(
        functools.partial(_rg_kernel, reverse=reverse, n_tiles=n_tiles, merge=h_fwd is not None),
        grid=(n_tiles,),
        in_specs=in_specs,
        out_specs=[out_spec, pl.BlockSpec((NB, D), lambda i: (0, 0))],
        out_shape=[out_shape, jax.ShapeDtypeStruct((NB, D), F32)],
        scratch_shapes=[
            pltpu.VMEM((rows + 4 * NB, D), F32),
            pltpu.VMEM((rows, D), F32),
            pltpu.VMEM((rows, D), F32),
            pltpu.VMEM((NB, D), F32),
        ],
        compiler_params=_params(("arbitrary",)),
        name=name,
    )(*args)


def _scan_rows(x, op, fill, reverse):
    n = x.shape[0]
    row = lax.broadcasted_iota(jnp.int32, x.shape, 0)
    sh = 1
    while sh < n:
        if reverse:
            shifted = jnp.where(row < n - sh, pltpu.roll(x, n - sh, axis=0), fill)
        else:
            shifted = jnp.where(row >= sh, pltpu.roll(x, sh, axis=0), fill)
        x = op(x, shifted)
        sh *= 2
    return x


def _ml_kernel(*refs, reverse, n_tiles, n_tok, mode):
    L = ML_CHUNK
    n_chunks = n_tok // L
    if mode == "ctx":
        (x_ref, cw_ref, cb_ref, dq_ref, dk_ref, dv_ref, wg_ref, bg_ref,
         c_out, n_out, m_out, xt_scr, q_scr, k_scr, v_scr, g_scr, c_scr, n_scr, m_scr, wqk_scr, wv_scr) = refs
    elif mode == "fwd":
        (x_ref, xp_ref, xn_ref, perm_ref, c0_ref, n0_ref, m0_ref, cw_ref, cb_ref, dq_ref, dk_ref, dv_ref, wg_ref,
         bg_ref, o_ref, xt_scr, q_scr, k_scr, v_scr, g_scr, c_scr, n_scr, m_scr, wqk_scr, wv_scr, h_scr) = refs
    else:
        (x_ref, xp_ref, xn_ref, perm_ref, c0_ref, n0_ref, m0_ref, cw_ref, cb_ref, dq_ref, dk_ref, dv_ref, wg_ref,
         bg_ref, hf_ref, ng_ref, sk_ref, o_ref, xt_scr, q_scr, k_scr, v_scr, g_scr, c_scr, n_scr, m_scr,
         wqk_scr, wv_scr, h_scr, xc_scr) = refs
    j = pl.program_id(1)
    tile = (n_tiles - 1 - j) if reverse else j

    @pl.when((pl.program_id(0) == 0) & (j == 0))
    def _():
        diff = lax.broadcasted_iota(jnp.int32, (DH, DH), 1) - lax.broadcasted_iota(jnp.int32, (DH, DH), 0)

        def dense(d_ref, h):
            diag = d_ref[h * DH:(h + 1) * DH, :]
            out = jnp.zeros((DH, DH), F32)
            for d in range(-3, 4):
                out = jnp.where(diff == d, diag[:, 3 + d:4 + d], out)
            return out

        dot = functools.partial(jnp.dot, preferred_element_type=F32)
        for h in range(HEADS):
            wq = dense(dq_ref, h).astype(BF16)
            wk = (dense(dk_ref, h) * (DH ** -0.5)).astype(BF16)
            wv = dense(dv_ref, h).astype(BF16)
            wqk_scr[h, :, 0:DH] = wq
            wqk_scr[h, :, DH:2 * DH] = wk
            wqk_scr[h, :, 2 * DH:3 * DH] = (dot(wq, wg_ref[h]) + dot(wk, wg_ref[HEADS + h])).astype(BF16)
            wv_scr[h, :, 0:DH] = wv
            wv_scr[h, :, DH:2 * DH] = dot(wv, wg_ref[2 * HEADS + h]).astype(BF16)

    @pl.when(j == 0)
    def _():
        if mode == "ctx":
            c_scr[...] = jnp.zeros_like(c_scr)
            n_scr[...] = jnp.zeros_like(n_scr)
            m_scr[...] = jnp.zeros_like(m_scr)
        else:
            c_scr[...] = c0_ref[...]
            n_scr[...] = n0_ref[...]
            m_scr[...] = m0_ref[...]

    if mode == "ctx":
        xt_scr[8:16, :] = jnp.zeros((8, D), F32)
        xt_scr[16:16 + n_tok, :] = x_ref[...].astype(F32)
        xt_scr[16 + n_tok:24 + n_tok, :] = jnp.zeros((8, D), F32)
    else:
        for g in range(GRID_W // BF16_ROWS):
            xg = x_ref[g * BF16_ROWS:(g + 1) * BF16_ROWS].reshape(BF16_ROWS * ML_GROUP_COLS, D)
            yg = jnp.dot(perm_ref[...], xg, preferred_element_type=F32)
            for w in range(ML_GROUP_COLS):
                dst = 16 + w * GRID_W + g * BF16_ROWS
                xt_scr[dst:dst + BF16_ROWS, :] = yg[w * BF16_ROWS:(w + 1) * BF16_ROWS]
        last = ML_GROUP_COLS - 1
        prev = jnp.concatenate([xp_ref[0].astype(F32)[last:last + 1], xp_ref[1].astype(F32)[last:last + 1]], axis=0)
        xt_scr[14:16, :] = jnp.where(tile > 0, prev, 0.0)
        xt_scr[16 + n_tok:17 + n_tok, :] = jnp.where(tile < n_tiles - 1, xn_ref[0].astype(F32)[0:1], 0.0)

    cw = cw_ref[...]
    a_rows = min(ML_A_ROWS, n_tok)
    for sb in range(n_tok // a_rows):
        r0 = sb * a_rows
        xt = xt_scr[16 + r0:16 + r0 + a_rows, :]
        xc = cb_ref[...] + cw[0:1] * xt_scr[14 + r0:14 + r0 + a_rows, :] + cw[1:2] * xt_scr[15 + r0:15 + r0 + a_rows, :]
        xc = xc + cw[2:3] * xt + cw[3:4] * xt_scr[17 + r0:17 + r0 + a_rows, :]
        xc = _silu(xc)
        if mode == "rev":
            xc_scr[r0:r0 + a_rows, :] = xc
        gates = jnp.zeros((a_rows, 256), F32) + bg_ref[...]
        for h in range(HEADS):
            lo, hi = h * DH, (h + 1) * DH
            qkg = jnp.dot(xc[:, lo:hi].astype(BF16), wqk_scr[h], preferred_element_type=F32)
            vg = jnp.dot(xt[:, lo:hi].astype(BF16), wv_scr[h], preferred_element_type=F32)
            gates = gates + qkg[:, 2 * DH:3 * DH] + vg[:, DH:2 * DH]
            q_scr[r0:r0 + a_rows, lo:hi] = qkg[:, 0:DH].astype(BF16)
            k_scr[r0:r0 + a_rows, lo:hi] = qkg[:, DH:2 * DH].astype(BF16)
            v_scr[r0:r0 + a_rows, lo:hi] = vg[:, 0:DH].astype(BF16)
        g_scr[r0:r0 + a_rows, :] = gates

    if mode != "ctx":
        row_id = lax.broadcasted_iota(jnp.int32, (L, L), 0)
        col_id = lax.broadcasted_iota(jnp.int32, (L, L), 1)
        keep = (col_id >= row_id) if reverse else (col_id <= row_id)

    def chunk_step(ci, carry):
        ck = (n_chunks - 1 - ci) if reverse else ci
        r0 = pl.multiple_of(ck * L, L)
        li = g_scr[pl.ds(r0, L), 0:128]
        lf = _log_sigmoid(g_scr[pl.ds(r0, L), 128:256])
        cum = _scan_rows(lf, jnp.add, 0.0, reverse)
        g = li - cum
        m_prev = m_scr[0:1, :]
        b_tot = cum[0:1, :] if reverse else cum[L - 1:L, :]
        gmax = jnp.max(g, axis=0, keepdims=True)
        m_new = jnp.maximum(b_tot + m_prev, b_tot + gmax)
        ws = jnp.exp(b_tot + g - m_new)
        dec = jnp.exp(b_tot + m_prev - m_new)
        if mode != "ctx":
            inter = cum + m_prev
            m_t = jnp.maximum(inter, cum + _scan_rows(g, jnp.maximum, -jnp.inf, reverse))
            c_all = cum - m_t
            si_all = jnp.exp(inter - m_t)
            fl_all = jnp.exp(-m_t)
            g_row = g.T
        for h in range(HEADS):
            lo, hi = h * DH, (h + 1) * DH
            q = q_scr[pl.ds(r0, L), lo:hi]
            kb = k_scr[pl.ds(r0, L), lo:hi]
            v = v_scr[pl.ds(r0, L), lo:hi]
            dec_h = dec[:, h:h + 1]
            c_old = c_scr[h]
            n_old = n_scr[h]
            if mode != "ctx":
                si_c = si_all[:, h:h + 1]
                s = lax.dot_general(q, kb, (((1,), (1,)), ((), ())), preferred_element_type=F32)
                arg = jnp.where(keep, c_all[:, h:h + 1] + g_row[h:h + 1, :], -jnp.inf)
                p = s * jnp.exp(arg)
                num = jnp.dot(p.astype(BF16), v, preferred_element_type=F32)
                num = num + si_c * jnp.dot(q, c_old.astype(BF16), preferred_element_type=F32)
                qn = q.astype(F32) * n_old
                den = jnp.sum(p[:, 0:128] + p[:, 128:256], axis=1, keepdims=True)
                den = den + si_c * jnp.sum(qn[:, 0:128] + qn[:, 128:256], axis=1, keepdims=True)
                h_scr[pl.ds(r0, L), lo:hi] = num / jnp.maximum(jnp.abs(den), fl_all[:, h:h + 1])
            kw = kb.astype(F32) * ws[:, h:h + 1]
            upd = lax.dot_general(kw.astype(BF16), v, (((0,), (0,)), ((), ())), preferred_element_type=F32)
            c_scr[h] = dec_h * c_old + upd
            n_scr[h] = dec_h * n_old + jnp.sum(kw, axis=0, keepdims=True)
        m_scr[...] = jnp.broadcast_to(m_new, m_scr.shape)
        return carry

    lax.fori_loop(0, n_chunks, chunk_step, 0)

    if mode == "ctx":
        c_out[...] = c_scr[...]
        n_out[...] = n_scr[...]
        m_out[...] = m_scr[...]
    elif mode == "fwd":
        o_ref[...] = h_scr[...].astype(o_ref.dtype)
    else:
        for sb in range(n_tok // a_rows):
            r0 = sb * a_rows
            ht = h_scr[r0:r0 + a_rows, :] + hf_ref[r0:r0 + a_rows, :].astype(F32)
            for h in range(HEADS):
                lo, hi = h * DH, (h + 1) * DH
                hh = ht[:, lo:hi]
                mu = jnp.mean(hh, axis=-1, keepdims=True)
                var = jnp.mean(jnp.square(hh - mu), axis=-1, keepdims=True)
                hn = (hh - mu) * lax.rsqrt(var + EPS)
                h_scr[r0:r0 + a_rows, lo:hi] = hn * ng_ref[:, lo:hi] + sk_ref[:, lo:hi] * xc_scr[r0:r0 + a_rows, lo:hi]
        for g in range(GRID_W // BF16_ROWS):
            zg = jnp.concatenate(
                [h_scr[w * GRID_W + g * BF16_ROWS:w * GRID_W + (g + 1) * BF16_ROWS, :] for w in range(ML_GROUP_COLS)],
                axis=0).astype(BF16)
            og = jnp.dot(perm_ref[...], zg, preferred_element_type=F32).astype(o_ref.dtype)
            o_ref[g * BF16_ROWS:(g + 1) * BF16_ROWS] = og.reshape(BF16_ROWS, ML_GROUP_COLS, D)


def _ml_weight_specs():
    return [
        _const_spec((4, D)),
        _const_spec((1, D)),
        _const_spec((D, 8)),
        _const_spec((D, 8)),
        _const_spec((D, 8)),
        _const_spec((3 * HEADS, DH, 256)),
        _const_spec((1, 256)),
    ]


def _ml_scratch(n_tok, mode):
    scratch = [
        pltpu.VMEM((n_tok + 32, D), F32),
        pltpu.VMEM((n_tok, D), BF16),
        pltpu.VMEM((n_tok, D), BF16),
        pltpu.VMEM((n_tok, D), BF16),
        pltpu.VMEM((n_tok, 256), F32),
        pltpu.VMEM((HEADS, DH, DH), F32),
        pltpu.VMEM((HEADS, 1, DH), F32),
        pltpu.VMEM((8, 128), F32),
        pltpu.VMEM((HEADS, DH, 3 * DH), BF16),
        pltpu.VMEM((HEADS, DH, 2 * DH), BF16),
    ]
    if mode != "ctx":
        scratch.append(pltpu.VMEM((n_tok, D), F32))
    if mode == "rev":
        scratch.append(pltpu.VMEM((n_tok, D), F32))
    return scratch


_STATE_SHAPES = [
    jax.ShapeDtypeStruct((NB, HEADS, DH, DH), F32),
    jax.ShapeDtypeStruct((NB, HEADS, 1, DH), F32),
    jax.ShapeDtypeStruct((NB, 8, 128), F32),
]


def _state_specs():
    return [
        pl.BlockSpec((None, HEADS, DH, DH), lambda b, j: (b, 0, 0, 0)),
        pl.BlockSpec((None, HEADS, 1, DH), lambda b, j: (b, 0, 0, 0)),
        pl.BlockSpec((None, 8, 128), lambda b, j: (b, 0, 0)),
    ]


def _ml_ctx_call(mlx_ctx, weights, *, reverse, name):
    return pl.pallas_call(
        functools.partial(_ml_kernel, reverse=reverse, n_tiles=1, n_tok=CTX, mode="ctx"),
        grid=(NB, 1),
        in_specs=[pl.BlockSpec((None, CTX, D), lambda b, j: (b, 0, 0))] + _ml_weight_specs(),
        out_specs=_state_specs(),
        out_shape=_STATE_SHAPES,
        scratch_shapes=_ml_scratch(CTX, "ctx"),
        compiler_params=_params(("arbitrary", "arbitrary")),
        name=name,
    )(mlx_ctx, *weights)


def _ml_lat_call(mlx_grid, state, weights, *, reverse, merge_with=None, name):
    n_tiles = GRID_W // ML_GROUP_COLS
    tile = (lambda j: n_tiles - 1 - j) if reverse else (lambda j: j)
    grid_spec = pl.BlockSpec((None, GRID_W, ML_GROUP_COLS, D), lambda b, j: (b, 0, tile(j), 0))
    tok_spec = pl.BlockSpec((None, ML_TILE, D), lambda b, j: (b, tile(j), 0))
    in_specs = [
        grid_spec,
        pl.BlockSpec((None, 2, ML_GROUP_COLS, D), lambda b, j: (b, GRID_W // 2 - 1, jnp.maximum(tile(j) - 1, 0), 0)),
        pl.BlockSpec((None, 1, ML_GROUP_COLS, D), lambda b, j: (b, 0, jnp.minimum(tile(j) + 1, n_tiles - 1), 0)),
        _const_spec((BF16_ROWS * ML_GROUP_COLS, BF16_ROWS * ML_GROUP_COLS)),
    ] + _state_specs() + _ml_weight_specs()
    args = [mlx_grid, mlx_grid, mlx_grid, _swap_perm(BF16_ROWS, ML_GROUP_COLS), *state, *weights]
    if merge_with is None:
        mode = "fwd"
        out_spec, out_shape = tok_spec, jax.ShapeDtypeStruct((NB, SEQ, D), BF16)
    else:
        mode = "rev"
        h_fwd, norm_g, skip = merge_with
        in_specs += [tok_spec, _const_spec((1, D)), _const_spec((1, D))]
        args += [h_fwd, norm_g, skip]
        out_spec, out_shape = grid_spec, jax.ShapeDtypeStruct((NB, GRID_W, GRID_W, D), BF16)
    return pl.pallas_call(
        functools.partial(_ml_kernel, reverse=reverse, n_tiles=n_tiles, n_tok=ML_TILE, mode=mode),
        grid=(NB, n_tiles),
        in_specs=in_specs,
        out_specs=out_spec,
        out_shape=out_shape,
        scratch_shapes=_ml_scratch(ML_TILE, mode),
        compiler_params=_params(("arbitrary", "arbitrary")),
        name=name,
    )(*args)


def _final_kernel(x_ref, hrg_ref, grg_ref, hml_ref, smlo_ref, sgr_ref, sgm_ref, g1_ref, sh2_ref, sc2_ref, g2_ref,
                  n2_ref, nf_ref, wbr_ref, wbm_ref, wo_ref, wfi_ref, wfo_ref, o_ref, act_scr):
    dot = functools.partial(jnp.dot, preferred_element_type=F32)
    y_rg = (hrg_ref[...].astype(F32) * grg_ref[...].astype(F32)).astype(BF16)
    y_ml = (hml_ref[...].astype(F32) * smlo_ref[...].astype(F32)).astype(BF16)
    mix = sgr_ref[...].astype(F32) * dot(y_rg, wbr_ref[...])
    mix = mix + sgm_ref[...].astype(F32) * dot(y_ml, wbm_ref[...])
    x1 = x_ref[...] + g1_ref[...] * dot(mix.astype(BF16), wo_ref[...])
    ms = jnp.mean(x1 * x1, axis=-1, keepdims=True)
    hn = x1 * lax.rsqrt(ms + EPS) * n2_ref[...]
    hb = (hn * (1.0 + sc2_ref[...]) + sh2_ref[...]).astype(BF16)
    step = 256
    for c in range(D_FF // step):
        gate = dot(hb, wfi_ref[:, c * step:(c + 1) * step])
        up = dot(hb, wfi_ref[:, D_FF + c * step:D_FF + (c + 1) * step])
        act_scr[:, c * step:(c + 1) * step] = (_silu(gate) * up).astype(BF16)
    x2 = x1 + g2_ref[...] * dot(act_scr[...], wfo_ref[...])
    ms2 = jnp.mean(x2 * x2, axis=-1, keepdims=True)
    o_ref[...] = x2 * lax.rsqrt(ms2 + EPS) * nf_ref[...]


def _final_call(x, h_rg, grg, h_ml, smlo, sgr, sgm, mod3, norm2_g, final_g, wbr, wbm, wo, wfi, wfo):
    rows = PROJ_ROWS
    row_spec = pl.BlockSpec((None, rows, D), lambda b, i: (b, i, 0))
    mod_spec = lambda g: pl.BlockSpec((None, 1, D), lambda b, i: (b, 0, g))
    return pl.pallas_call(
        _final_kernel,
        grid=(NB, SEQ // rows),
        in_specs=[
            row_spec, row_spec, row_spec, row_spec, row_spec, row_spec, row_spec,
            mod_spec(2), mod_spec(3), mod_spec(4), mod_spec(5),
            _const_spec((1, D)), _const_spec((1, D)),
            _const_spec((D, D)), _const_spec((D, D)), _const_spec((D, D)),
            _const_spec((D, 2 * D_FF)), _const_spec((D_FF, D)),
        ],
        out_specs=row_spec,
        out_shape=jax.ShapeDtypeStruct((NB, SEQ, D), F32),
        scratch_shapes=[pltpu.VMEM((rows, D_FF), BF16)],
        compiler_params=_params(("arbitrary", "arbitrary")),
        name="final",
    )(x, h_rg, grg, h_ml, smlo, sgr, sgm, mod3, mod3, mod3, mod3, norm2_g, final_g, wbr, wbm, wo, wfi, wfo)


def _pair_blockdiag(w):
    w = w.reshape(8, 2, 64, 64)
    z = jnp.zeros((8, 64, 64), w.dtype)
    top = jnp.concatenate([w[:, 0], z], axis=2)
    bot = jnp.concatenate([z, w[:, 1]], axis=2)
    return jnp.concatenate([top, bot], axis=1)


def _rg_weights(wa, ba, wx, bx):
    w = jnp.concatenate([_pair_blockdiag(wa), _pair_blockdiag(wx)], axis=2).astype(BF16)
    bias = 0.5 * jnp.concatenate([ba.reshape(8, 1, 128), bx.reshape(8, 1, 128)], axis=2)
    b_hi = bias.astype(BF16)
    b_lo = (bias - b_hi.astype(F32)).astype(BF16)
    return (jnp.concatenate([w, b_hi, b_lo, jnp.zeros((8, 126, 256), BF16)], axis=1),)


def _block_diagonals(w):
    rows = [jnp.pad(w[:, i, :], ((0, 0), (3 - i, 1 + i))) for i in range(4)]
    return jnp.stack(rows, axis=1).reshape(D, 8)


def _gate_weights(wi, bi, wf, bf):
    w = jnp.concatenate([jnp.pad(wi, ((0, 0), (0, 128 - HEADS))), jnp.pad(wf, ((0, 0), (0, 128 - HEADS)))], axis=1)
    b = jnp.concatenate([jnp.pad(bi, (0, 128 - HEADS)), jnp.pad(bf, (0, 128 - HEADS))]).reshape(1, 256)
    return w.reshape(3 * HEADS, DH, 256).astype(BF16), b


def kernel(x, c, ctx, c_ctx, w_mod, b_mod, norm1_g, norm2_g, w_in, rg_conv_w, rg_conv_b, rg_wa, rg_ba, rg_wx,
           rg_bx, rg_lambda, ml_conv_w, ml_conv_b, ml_wq, ml_wk, ml_wv, ml_wi, ml_bi, ml_wf, ml_bf,
           ml_norm_g, ml_skip, w_branch_rg, w_branch_ml, w_out, w_ffn_in, w_ffn_out, final_norm_g):
    mod = _mod_call(c, c_ctx, w_mod[0], b_mod[0])
    mod3 = mod.reshape(2 * NB, 1, 6 * D)
    w_in_bf = w_in[0].astype(BF16)
    norm1 = norm1_g[0].reshape(1, D)

    rgx, grg, mlx, smlo, sgr, sgm = _proj_call(x, mod3, norm1, w_in_bf, ctx=False)
    rgx_c, mlx_c = _proj_call(ctx, mod3, norm1, w_in_bf, ctx=True)

    rg_cw = rg_conv_w[0]
    rg_cb = rg_conv_b[0].reshape(1, D)
    zero_h = jnp.zeros((NB, D), F32)
    rg_w = [_rg_weights(rg_wa[0, d], rg_ba[0, d], rg_wx[0, d], rg_bx[0, d]) + (rg_lambda[0, d].reshape(1, D),)
            for d in range(2)]
    _, h0_f = _rg_call(rgx_c, zero_h, rg_cw, rg_cb, *rg_w[0], reverse=False, name="rg_ctx_fwd")
    _, h0_r = _rg_call(rgx_c, zero_h, rg_cw, rg_cb, *rg_w[1], reverse=True, name="rg_ctx_rev")
    h_f, _ = _rg_call(rgx, h0_f, rg_cw, rg_cb, *rg_w[0], reverse=False, name="rg_fwd")
    h_rg, _ = _rg_call(rgx, h0_r, rg_cw, rg_cb, *rg_w[1], reverse=True, h_fwd=h_f, name="rg_rev")

    ml_cw = ml_conv_w[0]
    ml_cb = ml_conv_b[0].reshape(1, D)
    ml_qkv = (_block_diagonals(ml_wq[0]), _block_diagonals(ml_wk[0]), _block_diagonals(ml_wv[0]))
    ml_w = [(ml_cw, ml_cb) + ml_qkv + _gate_weights(ml_wi[0, d], ml_bi[0, d], ml_wf[0, d], ml_bf[0, d])
            for d in range(2)]
    mlx_grid = mlx.reshape(NB, GRID_W, GRID_W, D)
    st_f = _ml_ctx_call(mlx_c, ml_w[0], reverse=False, name="ml_ctx_fwd")
    st_r = _ml_ctx_call(mlx_c, ml_w[1], reverse=True, name="ml_ctx_rev")
    hm_f = _ml_lat_call(mlx_grid, st_f, ml_w[0], reverse=False, name="ml_fwd")
    h_ml = _ml_lat_call(mlx_grid, st_r, ml_w[1], reverse=True,
                        merge_with=(hm_f, ml_norm_g[0].reshape(1, D), ml_skip[0].reshape(1, D)), name="ml_rev")

    return _final_call(
        x, h_rg, grg, h_ml.reshape(NB, SEQ, D), smlo, sgr, sgm, mod3,
        norm2_g[0].reshape(1, D), final_norm_g.reshape(1, D),
        w_branch_rg[0].astype(BF16), w_branch_ml[0].astype(BF16), w_out[0].astype(BF16),
        w_ffn_in[0].astype(BF16), w_ffn_out[0].astype(BF16))
```

```python
import functools

import jax
import jax.numpy as jnp
from jax import lax
from jax.experimental import pallas as pl
from jax.experimental.pallas import tpu as pltpu

F32 = jnp.float32
BF16 = jnp.bfloat16

D = 1024
NB = 8
SEQ = 4096
GRID_W = 64
CTX = 256
EPS = 1e-6
RG_C = 8.0
HEADS = 4
DH = D // HEADS
D_FF = 2816
N_IN = 6 * D
LOG2E = 1.4426950408889634

VMEM_LIMIT = 60 * 1024 * 1024
BF16_ROWS = 16

PROJ_ROWS = 512
RG_STEPS = 128
RG_SUB = 256
RG_SUB_STEPS = RG_SUB // NB
ML_CHUNK = 256
ML_GROUP_COLS = 16
ML_TILE = ML_GROUP_COLS * GRID_W
ML_A_ROWS = 512

assert ML_CHUNK == DH


def _sigmoid(x):
    return 0.5 * (jnp.tanh(0.5 * x) + 1.0)


def _silu(x):
    return x * _sigmoid(x)


def _softplus(x):
    return jnp.maximum(x, 0.0) + jnp.log(1.0 + jnp.exp(-jnp.abs(x)))


def _log_sigmoid(x):
    return jnp.minimum(x, 0.0) - jnp.log(1.0 + jnp.exp(-jnp.abs(x)))


def _split3(x):
    hi = x.astype(BF16)
    r1 = x - hi.astype(F32)
    mid = r1.astype(BF16)
    lo = (r1 - mid.astype(F32)).astype(BF16)
    return hi, mid, lo


def _params(sem):
    return pltpu.CompilerParams(dimension_semantics=sem, vmem_limit_bytes=VMEM_LIMIT)


def _const_spec(shape):
    nd = len(shape)
    return pl.BlockSpec(shape, lambda *_: (0,) * nd, pipeline_mode=pl.Buffered(1))


def _swap_perm(a, b):
    n = a * b
    out_row = jnp.arange(n)
    src = (out_row % a) * b + out_row // a
    return (src[:, None] == jnp.arange(n)[None, :]).astype(BF16)


def _mod_kernel(c_ref, cc_ref, w_ref, b_ref, o_ref):
    s = jnp.concatenate([c_ref[...], jnp.broadcast_to(cc_ref[...], (NB, D))], axis=0)
    s = _silu(s)
    s_hi, s_mid, _ = _split3(s)
    w_hi, w_mid, _ = _split3(w_ref[...])
    dot = functools.partial(jnp.dot, preferred_element_type=F32)
    o_ref[...] = dot(s_hi, w_hi) + dot(s_mid, w_hi) + dot(s_hi, w_mid) + b_ref[...]


def _mod_call(c, c_ctx, w_mod, b_mod):
    return pl.pallas_call(
        _mod_kernel,
        grid=(6,),
        in_specs=[
            pl.BlockSpec((NB, D), lambda g: (0, 0)),
            pl.BlockSpec((1, D), lambda g: (0, 0)),
            pl.BlockSpec((D, D), lambda g: (0, g)),
            pl.BlockSpec((1, D), lambda g: (0, g)),
        ],
        out_specs=pl.BlockSpec((2 * NB, D), lambda g: (0, g)),
        out_shape=jax.ShapeDtypeStruct((2 * NB, 6 * D), F32),
        compiler_params=_params(("arbitrary",)),
        name="mod",
    )(c, c_ctx.reshape(1, D), w_mod, b_mod.reshape(1, 6 * D))


def _gelu_tanh(x):
    return jax.nn.gelu(x, approximate=True)


def _identity(x):
    return x


_PROJ_FULL = ((0, _identity), (1, _gelu_tanh), (2, _identity), (3, _sigmoid), (4, _sigmoid), (5, _sigmoid))
_PROJ_CTX = ((0, _identity), (2, _identity))


def _proj_kernel(x_ref, sh_ref, sc_ref, g_ref, w_ref, *o_refs, groups):
    x = x_ref[...]
    ms = jnp.mean(x * x, axis=-1, keepdims=True)
    y = x * lax.rsqrt(ms + EPS) * g_ref[...]
    u = (y * (1.0 + sc_ref[...]) + sh_ref[...]).astype(BF16)
    for o_ref, (g, act) in zip(o_refs, groups):
        p = jnp.dot(u, w_ref[:, g * D:(g + 1) * D], preferred_element_type=F32)
        o_ref[...] = act(p).astype(o_ref.dtype)


def _proj_call(x, mod3, norm_g, w_in_bf, *, ctx):
    L = x.shape[1]
    rows = min(PROJ_ROWS, L)
    groups = _PROJ_CTX if ctx else _PROJ_FULL
    mod_row = (lambda b: NB) if ctx else (lambda b: b)
    row_spec = pl.BlockSpec((None, rows, D), lambda b, i: (b, i, 0))
    return pl.pallas_call(
        functools.partial(_proj_kernel, groups=groups),
        grid=(NB, L // rows),
        in_specs=[
            row_spec,
            pl.BlockSpec((None, 1, D), lambda b, i: (mod_row(b), 0, 0)),
            pl.BlockSpec((None, 1, D), lambda b, i: (mod_row(b), 0, 1)),
            _const_spec((1, D)),
            _const_spec((D, N_IN)),
        ],
        out_specs=[row_spec] * len(groups),
        out_shape=[jax.ShapeDtypeStruct((NB, L, D), BF16)] * len(groups),
        compiler_params=_params(("arbitrary", "arbitrary")),
        name="proj_ctx" if ctx else "proj",
    )(x, mod3, mod3, norm_g, w_in_bf)


def _rg_kernel(*refs, reverse, n_tiles, merge):
    if merge:
        (x_ref, xp_ref, xn_ref, h0_ref, cw_ref, cb_ref, wbd_ref, lam_ref, pin_ref, phalo_ref,
         hf_ref, pout_ref, o_ref, hlast_ref, xe_scr, a_scr, b_scr, h_scr) = refs
    else:
        (x_ref, xp_ref, xn_ref, h0_ref, cw_ref, cb_ref, wbd_ref, lam_ref, pin_ref, phalo_ref,
         o_ref, hlast_ref, xe_scr, a_scr, b_scr, h_scr) = refs
    i = pl.program_id(0)
    tile = (n_tiles - 1 - i) if reverse else i
    rows = RG_STEPS * NB
    halo = 2 * NB

    @pl.when(i == 0)
    def _():
        h_scr[...] = h0_ref[...]

    def halo_rows(ref):
        xb = jnp.concatenate([ref[b] for b in range(NB)], axis=0)
        return jnp.dot(phalo_ref[...], xb, preferred_element_type=F32)

    xe_scr[0:halo, :] = jnp.where(tile > 0, halo_rows(xp_ref)[(BF16_ROWS - 2) * NB:BF16_ROWS * NB], 0.0)
    xe_scr[halo + rows:2 * halo + rows, :] = jnp.where(tile < n_tiles - 1, halo_rows(xn_ref)[0:halo], 0.0)

    def fill_block(s, carry):
        t0 = pl.multiple_of(s * RG_SUB_STEPS, RG_SUB_STEPS)
        xb = jnp.concatenate([x_ref[b, pl.ds(t0, RG_SUB_STEPS), :] for b in range(NB)], axis=0)
        r0 = pl.multiple_of(s * RG_SUB, RG_SUB)
        xe_scr[pl.ds(halo + r0, RG_SUB), :] = jnp.dot(pin_ref[...], xb, preferred_element_type=F32)
        return carry

    lax.fori_loop(0, rows // RG_SUB, fill_block, 0)

    cw = 0.5 * cw_ref[...]
    cb = 0.5 * cb_ref[...]
    lam2 = (-0.5 * RG_C * LOG2E) * _softplus(-lam_ref[...])

    ones = jnp.ones((RG_SUB, 128), BF16)

    def coef_block(s, carry):
        r0 = pl.multiple_of(s * RG_SUB, RG_SUB)
        xh = cb + cw[0:1] * xe_scr[pl.ds(r0, RG_SUB), :]
        xh = xh + cw[1:2] * xe_scr[pl.ds(r0 + NB, RG_SUB), :]
        xh = xh + cw[2:3] * xe_scr[pl.ds(r0 + 2 * NB, RG_SUB), :]
        xh = xh + cw[3:4] * xe_scr[pl.ds(r0 + 3 * NB, RG_SUB), :]
        for p in range(D // 128):
            lo, hi = p * 128, (p + 1) * 128
            xhp = xh[:, lo:hi]
            lhs = jnp.concatenate([xhp.astype(BF16), ones], axis=1)
            pre = jnp.dot(lhs, wbd_ref[p], preferred_element_type=F32)
            t_r = jnp.tanh(pre[:, 0:128])
            t_i = jnp.tanh(pre[:, 128:256])
            a = jnp.exp2(lam2[:, lo:hi] * (t_r + 1.0))
            y = (1.0 - a) * (1.0 + a)
            gain = jnp.where(y > 0.0, y * lax.rsqrt(y), 0.0)
            a_scr[pl.ds(r0, RG_SUB), lo:hi] = a
            b_scr[pl.ds(r0, RG_SUB), lo:hi] = gain * ((t_i + 1.0) * xhp)
        return carry

    lax.fori_loop(0, rows // RG_SUB, coef_block, 0)

    def scan_step(t, h):
        tt = (RG_STEPS - 1 - t) if reverse else t
        r0 = pl.multiple_of(tt * NB, NB)
        h = a_scr[pl.ds(r0, NB), :] * h + b_scr[pl.ds(r0, NB), :]
        b_scr[pl.ds(r0, NB), :] = h
        return h

    h_last = lax.fori_loop(0, RG_STEPS, scan_step, h_scr[...], unroll=8)
    h_scr[...] = h_last
    hlast_ref[...] = h_last

    if merge:
        def out_block(s, carry):
            r0 = pl.multiple_of(s * RG_SUB, RG_SUB)
            t0 = pl.multiple_of(s * RG_SUB_STEPS, RG_SUB_STEPS)
            h = (b_scr[pl.ds(r0, RG_SUB), :] + hf_ref[pl.ds(r0, RG_SUB), :].astype(F32)).astype(BF16)
            hb = jnp.dot(pout_ref[...], h, preferred_element_type=F32).astype(o_ref.dtype)
            for b in range(NB):
                o_ref[b, pl.ds(t0, RG_SUB_STEPS), :] = hb[b * RG_SUB_STEPS:(b + 1) * RG_SUB_STEPS]
            return carry
        lax.fori_loop(0, rows // RG_SUB, out_block, 0)
    else:
        o_ref[...] = b_scr[...].astype(o_ref.dtype)


def _rg_call(rgx, h0, cw, cb, wbd, lam, *, reverse, h_fwd=None, name):
    L = rgx.shape[1]
    rows = RG_STEPS * NB
    n_tiles = L // RG_STEPS
    per = RG_STEPS // BF16_ROWS
    n_halo = L // BF16_ROWS
    tile = (lambda i: n_tiles - 1 - i) if reverse else (lambda i: i)
    bm_spec = pl.BlockSpec((NB, RG_STEPS, D), lambda i: (0, tile(i), 0))
    tm_spec = pl.BlockSpec((rows, D), lambda i: (tile(i), 0))
    in_specs = [
        bm_spec,
        pl.BlockSpec((NB, BF16_ROWS, D), lambda i: (0, jnp.maximum(tile(i) * per - 1, 0), 0)),
        pl.BlockSpec((NB, BF16_ROWS, D), lambda i: (0, jnp.minimum((tile(i) + 1) * per, n_halo - 1), 0)),
        _const_spec((NB, D)),
        _const_spec((4, D)),
        _const_spec((1, D)),
        _const_spec((D // 128, 256, 256)),
        _const_spec((1, D)),
        _const_spec((RG_SUB, RG_SUB)),
        _const_spec((NB * BF16_ROWS, NB * BF16_ROWS)),
    ]
    args = [rgx, rgx, rgx, h0, cw, cb, wbd, lam,
            _swap_perm(NB, RG_SUB_STEPS), _swap_perm(NB, BF16_ROWS)]
    if h_fwd is None:
        out_spec, out_shape = tm_spec, jax.ShapeDtypeStruct((L * NB, D), BF16)
    else:
        in_specs += [tm_spec, _const_spec((RG_SUB, RG_SUB))]
        args += [h_fwd, _swap_perm(RG_SUB_STEPS, NB)]
        out_spec, out_shape = bm_spec, jax.ShapeDtypeStruct((NB, L, D), BF16)
    return pl.pallas_call(
        functools.partial(_rg_kernel, reverse=reverse, n_tiles=n_tiles, merge=h_fwd is not None),
        grid=(n_tiles,),
        in_specs=in_specs,
        out_specs=[out_spec, pl.BlockSpec((NB, D), lambda i: (0, 0))],
        out_shape=[out_shape, jax.ShapeDtypeStruct((NB, D), F32)],
        scratch_shapes=[
            pltpu.VMEM((rows + 4 * NB, D), F32),
            pltpu.VMEM((rows, D), F32),
            pltpu.VMEM((rows, D), F32),
            pltpu.VMEM((NB, D), F32),
        ],
        compiler_params=_params(("arbitrary",)),
        name=name,
    )(*args)


ML_REV_LANE = 8


def _scan_lanes(x, op, fill, reverse):
    n = x.shape[1]
    lane = lax.broadcasted_iota(jnp.int32, x.shape, 1)
    sh = 1
    while sh < n:
        if reverse:
            shifted = jnp.where(lane < n - sh, pltpu.roll(x, n - sh, axis=1), fill)
        else:
            shifted = jnp.where(lane >= sh, pltpu.roll(x, sh, axis=1), fill)
        x = op(x, shifted)
        sh *= 2
    return x


def _ml_prep_kernel(*refs, n_tiles, n_tok, latent):
    L = ML_CHUNK
    n_chunks = n_tok // L
    if latent:
        (x_ref, xp_ref, xn_ref, perm_ref, cw_ref, cb_ref, dq_ref, dk_ref, dv_ref, wg_ref, bg_ref,
         q_ref, k_ref, v_ref, xc_ref, gv_ref, grf_ref, grr_ref, xt_scr, wqk_scr, wv_scr) = refs
    else:
        (x_ref, cw_ref, cb_ref, dq_ref, dk_ref, dv_ref, wg_ref, bg_ref,
         q_ref, k_ref, v_ref, xc_ref, gv_ref, grf_ref, grr_ref, xt_scr, wqk_scr, wv_scr) = refs
    tile = pl.program_id(1)

    @pl.when((pl.program_id(0) == 0) & (tile == 0))
    def _():
        diff = lax.broadcasted_iota(jnp.int32, (DH, DH), 1) - lax.broadcasted_iota(jnp.int32, (DH, DH), 0)

        def dense(d_ref, h):
            diag = d_ref[h * DH:(h + 1) * DH, :]
            out = jnp.zeros((DH, DH), F32)
            for d in range(-3, 4):
                out = jnp.where(diff == d, diag[:, 3 + d:4 + d], out)
            return out

        dot = functools.partial(jnp.dot, preferred_element_type=F32)
        for h in range(HEADS):
            wq = dense(dq_ref, h).astype(BF16)
            wk = (dense(dk_ref, h) * (DH ** -0.5)).astype(BF16)
            wv = dense(dv_ref, h).astype(BF16)
            wqk_scr[h, :, 0:DH] = wq
            wqk_scr[h, :, DH:2 * DH] = wk
            wqk_scr[h, :, 2 * DH:3 * DH] = (dot(wq, wg_ref[h]) + dot(wk, wg_ref[HEADS + h])).astype(BF16)
            wv_scr[h, :, 0:DH] = wv
            wv_scr[h, :, DH:2 * DH] = dot(wv, wg_ref[2 * HEADS + h]).astype(BF16)

    if not latent:
        xt_scr[8:16, :] = jnp.zeros((8, D), F32)
        xt_scr[16:16 + n_tok, :] = x_ref[...].astype(F32)
        xt_scr[16 + n_tok:24 + n_tok, :] = jnp.zeros((8, D), F32)
    else:
        for g in range(GRID_W // BF16_ROWS):
            xg = x_ref[g * BF16_ROWS:(g + 1) * BF16_ROWS].reshape(BF16_ROWS * ML_GROUP_COLS, D)
            yg = jnp.dot(perm_ref[...], xg, preferred_element_type=F32)
            for w in range(ML_GROUP_COLS):
                dst = 16 + w * GRID_W + g * BF16_ROWS
                xt_scr[dst:dst + BF16_ROWS, :] = yg[w * BF16_ROWS:(w + 1) * BF16_ROWS]
        last = ML_GROUP_COLS - 1
        prev = jnp.concatenate([xp_ref[0].astype(F32)[last:last + 1], xp_ref[1].astype(F32)[last:last + 1]], axis=0)
        xt_scr[14:16, :] = jnp.where(tile > 0, prev, 0.0)
        xt_scr[16 + n_tok:17 + n_tok, :] = jnp.where(tile < n_tiles - 1, xn_ref[0].astype(F32)[0:1], 0.0)

    cw = cw_ref[...]
    for ck in range(n_chunks):
        r0 = ck * L
        xt = xt_scr[16 + r0:16 + r0 + L, :]
        xc = cb_ref[...] + cw[0:1] * xt_scr[14 + r0:14 + r0 + L, :] + cw[1:2] * xt_scr[15 + r0:15 + r0 + L, :]
        xc = xc + cw[2:3] * xt + cw[3:4] * xt_scr[17 + r0:17 + r0 + L, :]
        xc = _silu(xc)
        xc_ref[r0:r0 + L, :] = xc.astype(BF16)
        gates = jnp.zeros((L, 256), F32) + bg_ref[...]
        for h in range(HEADS):
            lo, hi = h * DH, (h + 1) * DH
            qkg = jnp.dot(xc[:, lo:hi].astype(BF16), wqk_scr[h], preferred_element_type=F32)
            vg = jnp.dot(xt[:, lo:hi].astype(BF16), wv_scr[h], preferred_element_type=F32)
            gates = gates + qkg[:, 2 * DH:3 * DH] + vg[:, DH:2 * DH]
            q_ref[r0:r0 + L, lo:hi] = qkg[:, 0:DH].astype(BF16)
            k_ref[r0:r0 + L, lo:hi] = qkg[:, DH:2 * DH].astype(BF16)
            v_ref[r0:r0 + L, lo:hi] = vg[:, 0:DH].astype(BF16)
        li_col = gates[:, 0:128]
        li_rows = li_col.T
        lf_rows = gates[:, 128:256].T
        cum_rows, pm_rows = [], []
        for gr_ref, rev in ((grf_ref, False), (grr_ref, True)):
            r8 = ML_REV_LANE if rev else 0
            lf_row = _log_sigmoid(lf_rows[r8:r8 + 8])
            cum_row = _scan_lanes(lf_row, jnp.add, 0.0, rev)
            g_row = li_rows[r8:r8 + 8] - cum_row
            gr_ref[ck] = g_row
            cum_rows.append(cum_row)
            pm_rows.append(_scan_lanes(g_row, jnp.maximum, -jnp.inf, rev))
        pad = jnp.zeros((128 - 16, L), F32)
        gv_ref[0, r0:r0 + L, :] = li_col
        gv_ref[1, r0:r0 + L, :] = jnp.concatenate(cum_rows + [pad], axis=0).T
        gv_ref[2, r0:r0 + L, :] = jnp.concatenate(pm_rows + [pad], axis=0).T


def _ml_rec_kernel(*refs, reverse, n_tok, mode):
    L = ML_CHUNK
    n_chunks = n_tok // L
    lane0 = ML_REV_LANE if reverse else 0
    if mode == "ctx":
        (q_ref, k_ref, v_ref, gv_ref, gr_ref, c_out, n_out, m_out, c_scr, n_scr, m_scr) = refs
    elif mode == "fwd":
        (q_ref, k_ref, v_ref, gv_ref, gr_ref, c0_ref, n0_ref, m0_ref, o_ref, c_scr, n_scr, m_scr, h_scr) = refs
    else:
        (q_ref, k_ref, v_ref, gv_ref, gr_ref, c0_ref, n0_ref, m0_ref, hf_ref, xc_ref, ng_ref, sk_ref, perm_ref,
         o_ref, c_scr, n_scr, m_scr, h_scr) = refs

    @pl.when(pl.program_id(1) == 0)
    def _():
        if mode == "ctx":
            c_scr[...] = jnp.zeros_like(c_scr)
            n_scr[...] = jnp.zeros_like(n_scr)
            m_scr[...] = jnp.zeros_like(m_scr)
        else:
            c_scr[...] = c0_ref[...]
            n_scr[...] = n0_ref[...]
            m_scr[...] = m0_ref[...]

    if mode != "ctx":
        row_id = lax.broadcasted_iota(jnp.int32, (L, L), 0)
        col_id = lax.broadcasted_iota(jnp.int32, (L, L), 1)
        keep = (col_id >= row_id) if reverse else (col_id <= row_id)

    def stage_b(ck):
        r0 = ck * L
        li = gv_ref[0, r0:r0 + L, :]
        cum = gv_ref[1, r0:r0 + L, :]
        pm = gv_ref[2, r0:r0 + L, :]
        g = li - cum
        m_prev = m_scr[0:1, :]
        edge = 0 if reverse else L - 1
        b_tot = cum[edge:edge + 1, :]
        gmax = pm[edge:edge + 1, :]
        m_new = jnp.maximum(b_tot + m_prev, b_tot + gmax)
        ws = jnp.exp(b_tot + g - m_new)
        dec = jnp.exp(b_tot + m_prev - m_new)
        if mode != "ctx":
            inter = cum + m_prev
            m_t = jnp.maximum(inter, cum + pm)
            c_all = cum - m_t
            si_all = jnp.exp(inter - m_t)
            fl_all = jnp.exp(-m_t)
            g_row = gr_ref[ck]
        for h in range(HEADS):
            lo, hi = h * DH, (h + 1) * DH
            ln = lane0 + h
            q = q_ref[r0:r0 + L, lo:hi]
            kb = k_ref[r0:r0 + L, lo:hi]
            v = v_ref[r0:r0 + L, lo:hi]
            dec_h = dec[:, ln:ln + 1]
            c_old = c_scr[h]
            n_old = n_scr[h]
            if mode != "ctx":
                si_c = si_all[:, ln:ln + 1]
                s = lax.dot_general(q, kb, (((1,), (1,)), ((), ())), preferred_element_type=F32)
                arg = jnp.where(keep, c_all[:, ln:ln + 1] + g_row[h:h + 1, :], -jnp.inf)
                p = s * jnp.exp(arg)
                num = jnp.dot(p.astype(BF16), v, preferred_element_type=F32)
                num = num + si_c * jnp.dot(q, c_old.astype(BF16), preferred_element_type=F32)
                qn = q.astype(F32) * n_old
                den = jnp.sum(p[:, 0:128] + p[:, 128:256], axis=1, keepdims=True)
                den = den + si_c * jnp.sum(qn[:, 0:128] + qn[:, 128:256], axis=1, keepdims=True)
                h_scr[r0:r0 + L, lo:hi] = num / jnp.maximum(jnp.abs(den), fl_all[:, ln:ln + 1])
            kw = kb.astype(F32) * ws[:, ln:ln + 1]
            upd = lax.dot_general(kw.astype(BF16), v, (((0,), (0,)), ((), ())), preferred_element_type=F32)
            c_scr[h] = dec_h * c_old + upd
            n_scr[h] = dec_h * n_old + jnp.sum(kw, axis=0, keepdims=True)
        m_scr[...] = jnp.broadcast_to(m_new, m_scr.shape)

    for ci in range(n_chunks):
        stage_b((n_chunks - 1 - ci) if reverse else ci)

    a_rows = min(ML_A_ROWS, n_tok)
    if mode == "ctx":
        c_out[...] = c_scr[...]
        n_out[...] = n_scr[...]
        m_out[...] = m_scr[...]
    elif mode == "fwd":
        o_ref[...] = h_scr[...].astype(o_ref.dtype)
    else:
        for sb in range(n_tok // a_rows):
            r0 = sb * a_rows
            ht = h_scr[r0:r0 + a_rows, :] + hf_ref[r0:r0 + a_rows, :].astype(F32)
            for h in range(HEADS):
                lo, hi = h * DH, (h + 1) * DH
                hh = ht[:, lo:hi]
                mu = jnp.mean(hh, axis=-1, keepdims=True)
                var = jnp.mean(jnp.square(hh - mu), axis=-1, keepdims=True)
                hn = (hh - mu) * lax.rsqrt(var + EPS)
                xc = xc_ref[r0:r0 + a_rows, lo:hi].astype(F32)
                h_scr[r0:r0 + a_rows, lo:hi] = hn * ng_ref[:, lo:hi] + sk_ref[:, lo:hi] * xc
        for g in range(GRID_W // BF16_ROWS):
            zg = jnp.concatenate(
                [h_scr[w * GRID_W + g * BF16_ROWS:w * GRID_W + (g + 1) * BF16_ROWS, :] for w in range(ML_GROUP_COLS)],
                axis=0).astype(BF16)
            og = jnp.dot(perm_ref[...], zg, preferred_element_type=F32).astype(o_ref.dtype)
            o_ref[g * BF16_ROWS:(g + 1) * BF16_ROWS] = og.reshape(BF16_ROWS, ML_GROUP_COLS, D)


def _ml_weight_specs():
    return [
        _const_spec((4, D)),
        _const_spec((1, D)),
        _const_spec((D, 8)),
        _const_spec((D, 8)),
        _const_spec((D, 8)),
        _const_spec((3 * HEADS, DH, 256)),
        _const_spec((1, 256)),
    ]


def _ml_prep_call(mlx, weights, *, latent, name):
    if latent:
        n_tiles, n_tok, seq = GRID_W // ML_GROUP_COLS, ML_TILE, SEQ
        in_specs = [
            pl.BlockSpec((None, GRID_W, ML_GROUP_COLS, D), lambda b, j: (b, 0, j, 0)),
            pl.BlockSpec((None, 2, ML_GROUP_COLS, D), lambda b, j: (b, GRID_W // 2 - 1, jnp.maximum(j - 1, 0), 0)),
            pl.BlockSpec((None, 1, ML_GROUP_COLS, D), lambda b, j: (b, 0, jnp.minimum(j + 1, n_tiles - 1), 0)),
            _const_spec((BF16_ROWS * ML_GROUP_COLS, BF16_ROWS * ML_GROUP_COLS)),
        ]
        args = [mlx, mlx, mlx, _swap_perm(BF16_ROWS, ML_GROUP_COLS)]
    else:
        n_tiles, n_tok, seq = 1, CTX, CTX
        in_specs = [pl.BlockSpec((None, CTX, D), lambda b, j: (b, 0, 0))]
        args = [mlx]
    n_chunks = n_tok // ML_CHUNK
    tok_spec = pl.BlockSpec((None, n_tok, D), lambda b, j: (b, j, 0))
    tok_shape = jax.ShapeDtypeStruct((NB, seq, D), BF16)
    row_spec = pl.BlockSpec((None, n_chunks, 8, ML_CHUNK), lambda b, j: (b, j, 0, 0))
    row_shape = jax.ShapeDtypeStruct((NB, seq // ML_CHUNK, 8, ML_CHUNK), F32)
    return pl.pallas_call(
        functools.partial(_ml_prep_kernel, n_tiles=n_tiles, n_tok=n_tok, latent=latent),
        grid=(NB, n_tiles),
        in_specs=in_specs + _ml_weight_specs(),
        out_specs=[tok_spec] * 4 + [pl.BlockSpec((None, 3, n_tok, 128), lambda b, j: (b, 0, j, 0)), row_spec, row_spec],
        out_shape=[tok_shape] * 4 + [jax.ShapeDtypeStruct((NB, 3, seq, 128), F32), row_shape, row_shape],
        scratch_shapes=[
            pltpu.VMEM((n_tok + 32, D), F32),
            pltpu.VMEM((HEADS, DH, 3 * DH), BF16),
            pltpu.VMEM((HEADS, DH, 2 * DH), BF16),
        ],
        compiler_params=_params(("arbitrary", "arbitrary")),
        name=name,
    )(*args, *weights)


_STATE_SHAPES = [
    jax.ShapeDtypeStruct((NB, HEADS, DH, DH), F32),
    jax.ShapeDtypeStruct((NB, HEADS, 1, DH), F32),
    jax.ShapeDtypeStruct((NB, 8, 128), F32),
]


def _ml_rec_call(q, k, v, gv, g_rows, *, reverse, state=None, merge_with=None, name):
    seq = q.shape[1]
    n_tok = min(ML_TILE, seq)
    n_tiles = seq // n_tok
    n_chunks = n_tok // ML_CHUNK
    tile = (lambda j: n_tiles - 1 - j) if reverse else (lambda j: j)
    tok_spec = pl.BlockSpec((None, n_tok, D), lambda b, j: (b, tile(j), 0))
    state_specs = [
        pl.BlockSpec((None, HEADS, DH, DH), lambda b, j: (b, 0, 0, 0)),
        pl.BlockSpec((None, HEADS, 1, DH), lambda b, j: (b, 0, 0, 0)),
        pl.BlockSpec((None, 8, 128), lambda b, j: (b, 0, 0)),
    ]
    in_specs = [
        tok_spec, tok_spec, tok_spec,
        pl.BlockSpec((None, 3, n_tok, 128), lambda b, j: (b, 0, tile(j), 0)),
        pl.BlockSpec((None, n_chunks, 8, ML_CHUNK), lambda b, j: (b, tile(j), 0, 0)),
    ]
    args = [q, k, v, gv, g_rows]
    scratch = [
        pltpu.VMEM((HEADS, DH, DH), F32),
        pltpu.VMEM((HEADS, 1, DH), F32),
        pltpu.VMEM((8, 128), F32),
    ]
    if state is None:
        mode = "ctx"
        out_specs, out_shape = state_specs, _STATE_SHAPES
    else:
        in_specs += state_specs
        args += list(state)
        scratch.append(pltpu.VMEM((n_tok, D), F32))
        if merge_with is None:
            mode = "fwd"
            out_specs, out_shape = tok_spec, jax.ShapeDtypeStruct((NB, seq, D), BF16)
        else:
            mode = "rev"
            h_fwd, xc, norm_g, skip = merge_with
            in_specs += [tok_spec, tok_spec, _const_spec((1, D)), _const_spec((1, D)),
                         _const_spec((BF16_ROWS * ML_GROUP_COLS, BF16_ROWS * ML_GROUP_COLS))]
            args += [h_fwd, xc, norm_g, skip, _swap_perm(BF16_ROWS, ML_GROUP_COLS)]
            out_specs = pl.BlockSpec((None, GRID_W, ML_GROUP_COLS, D), lambda b, j: (b, 0, tile(j), 0))
            out_shape = jax.ShapeDtypeStruct((NB, GRID_W, GRID_W, D), BF16)
    return pl.pallas_call(
        functools.partial(_ml_rec_kernel, reverse=reverse, n_tok=n_tok, mode=mode),
        grid=(NB, n_tiles),
        in_specs=in_specs,
        out_specs=out_specs,
        out_shape=out_shape,
        scratch_shapes=scratch,
        compiler_params=_params(("arbitrary", "arbitrary")),
        name=name,
    )(*args)


def _final_kernel(x_ref, hrg_ref, grg_ref, hml_ref, smlo_ref, sgr_ref, sgm_ref, g1_ref, sh2_ref, sc2_ref, g2_ref,
                  n2_ref, nf_ref, wbr_ref, wbm_ref, wo_ref, wfi_ref, wfo_ref, o_ref, act_scr):
    dot = functools.partial(jnp.dot, preferred_element_type=F32)
    y_rg = (hrg_ref[...].astype(F32) * grg_ref[...].astype(F32)).astype(BF16)
    y_ml = (hml_ref[...].astype(F32) * smlo_ref[...].astype(F32)).astype(BF16)
    mix = sgr_ref[...].astype(F32) * dot(y_rg, wbr_ref[...])
    mix = mix + sgm_ref[...].astype(F32) * dot(y_ml, wbm_ref[...])
    x1 = x_ref[...] + g1_ref[...] * dot(mix.astype(BF16), wo_ref[...])
    ms = jnp.mean(x1 * x1, axis=-1, keepdims=True)
    hn = x1 * lax.rsqrt(ms + EPS) * n2_ref[...]
    hb = (hn * (1.0 + sc2_ref[...]) + sh2_ref[...]).astype(BF16)
    step = 256
    for c in range(D_FF // step):
        gate = dot(hb, wfi_ref[:, c * step:(c + 1) * step])
        up = dot(hb, wfi_ref[:, D_FF + c * step:D_FF + (c + 1) * step])
        act_scr[:, c * step:(c + 1) * step] = (_silu(gate) * up).astype(BF16)
    x2 = x1 + g2_ref[...] * dot(act_scr[...], wfo_ref[...])
    ms2 = jnp.mean(x2 * x2, axis=-1, keepdims=True)
    o_ref[...] = x2 * lax.rsqrt(ms2 + EPS) * nf_ref[...]


def _final_call(x, h_rg, grg, h_ml, smlo, sgr, sgm, mod3, norm2_g, final_g, wbr, wbm, wo, wfi, wfo):
    rows = PROJ_ROWS
    row_spec = pl.BlockSpec((None, rows, D), lambda b, i: (b, i, 0))
    mod_spec = lambda g: pl.BlockSpec((None, 1, D), lambda b, i: (b, 0, g))
    return pl.pallas_call(
        _final_kernel,
        grid=(NB, SEQ // rows),
        in_specs=[
            row_spec, row_spec, row_spec, row_spec, row_spec, row_spec, row_spec,
            mod_spec(2), mod_spec(3), mod_spec(4), mod_spec(5),
            _const_spec((1, D)), _const_spec((1, D)),
            _const_spec((D, D)), _const_spec((D, D)), _const_spec((D, D)),
            _const_spec((D, 2 * D_FF)), _const_spec((D_FF, D)),
        ],
        out_specs=row_spec,
        out_shape=jax.ShapeDtypeStruct((NB, SEQ, D), F32),
        scratch_shapes=[pltpu.VMEM((rows, D_FF), BF16)],
        compiler_params=_params(("arbitrary", "arbitrary")),
        name="final",
    )(x, h_rg, grg, h_ml, smlo, sgr, sgm, mod3, mod3, mod3, mod3, norm2_g, final_g, wbr, wbm, wo, wfi, wfo)


def _pair_blockdiag(w):
    w = w.reshape(8, 2, 64, 64)
    z = jnp.zeros((8, 64, 64), w.dtype)
    top = jnp.concatenate([w[:, 0], z], axis=2)
    bot = jnp.concatenate([z, w[:, 1]], axis=2)
    return jnp.concatenate([top, bot], axis=1)


def _rg_weights(wa, ba, wx, bx):
    w = jnp.concatenate([_pair_blockdiag(wa), _pair_blockdiag(wx)], axis=2).astype(BF16)
    bias = 0.5 * jnp.concatenate([ba.reshape(8, 1, 128), bx.reshape(8, 1, 128)], axis=2)
    b_hi = bias.astype(BF16)
    b_lo = (bias - b_hi.astype(F32)).astype(BF16)
    return (jnp.concatenate([w, b_hi, b_lo, jnp.zeros((8, 126, 256), BF16)], axis=1),)


def _block_diagonals(w):
    rows = [jnp.pad(w[:, i, :], ((0, 0), (3 - i, 1 + i))) for i in range(4)]
    return jnp.stack(rows, axis=1).reshape(D, 8)


def _gate_weights(wi, bi, wf, bf):
    def lanes(x):
        gap = [(0, 0)] * (x.ndim - 2)
        fwd = jnp.pad(x[0], gap + [(0, ML_REV_LANE - HEADS)])
        rev = jnp.pad(x[1], gap + [(0, 128 - ML_REV_LANE - HEADS)])
        return jnp.concatenate([fwd, rev], axis=-1)
    w = jnp.concatenate([lanes(wi), lanes(wf)], axis=-1)
    b = jnp.concatenate([lanes(bi), lanes(bf)], axis=-1).reshape(1, 256)
    return w.reshape(3 * HEADS, DH, 256).astype(BF16), b


def kernel(x, c, ctx, c_ctx, w_mod, b_mod, norm1_g, norm2_g, w_in, rg_conv_w, rg_conv_b, rg_wa, rg_ba, rg_wx,
           rg_bx, rg_lambda, ml_conv_w, ml_conv_b, ml_wq, ml_wk, ml_wv, ml_wi, ml_bi, ml_wf, ml_bf,
           ml_norm_g, ml_skip, w_branch_rg, w_branch_ml, w_out, w_ffn_in, w_ffn_out, final_norm_g):
    mod = _mod_call(c, c_ctx, w_mod[0], b_mod[0])
    mod3 = mod.reshape(2 * NB, 1, 6 * D)
    w_in_bf = w_in[0].astype(BF16)
    norm1 = norm1_g[0].reshape(1, D)

    rgx, grg, mlx, smlo, sgr, sgm = _proj_call(x, mod3, norm1, w_in_bf, ctx=False)
    rgx_c, mlx_c = _proj_call(ctx, mod3, norm1, w_in_bf, ctx=True)

    rg_cw = rg_conv_w[0]
    rg_cb = rg_conv_b[0].reshape(1, D)
    zero_h = jnp.zeros((NB, D), F32)
    rg_w = [_rg_weights(rg_wa[0, d], rg_ba[0, d], rg_wx[0, d], rg_bx[0, d]) + (rg_lambda[0, d].reshape(1, D),)
            for d in range(2)]
    _, h0_f = _rg_call(rgx_c, zero_h, rg_cw, rg_cb, *rg_w[0], reverse=False, name="rg_ctx_fwd")
    _, h0_r = _rg_call(rgx_c, zero_h, rg_cw, rg_cb, *rg_w[1], reverse=True, name="rg_ctx_rev")
    h_f, _ = _rg_call(rgx, h0_f, rg_cw, rg_cb, *rg_w[0], reverse=False, name="rg_fwd")
    h_rg, _ = _rg_call(rgx, h0_r, rg_cw, rg_cb, *rg_w[1], reverse=True, h_fwd=h_f, name="rg_rev")

    ml_cw = ml_conv_w[0]
    ml_cb = ml_conv_b[0].reshape(1, D)
    ml_qkv = (_block_diagonals(ml_wq[0]), _block_diagonals(ml_wk[0]), _block_diagonals(ml_wv[0]))
    ml_w = (ml_cw, ml_cb) + ml_qkv + _gate_weights(ml_wi[0], ml_bi[0], ml_wf[0], ml_bf[0])
    q_c, k_c, v_c, _, gv_c, grf_c, grr_c = _ml_prep_call(mlx_c, ml_w, latent=False, name="ml_prep_ctx")
    st_f = _ml_rec_call(q_c, k_c, v_c, gv_c, grf_c, reverse=False, name="ml_ctx_fwd")
    st_r = _ml_rec_call(q_c, k_c, v_c, gv_c, grr_c, reverse=True, name="ml_ctx_rev")
    q_l, k_l, v_l, xc_l, gv_l, grf_l, grr_l = _ml_prep_call(
        mlx.reshape(NB, GRID_W, GRID_W, D), ml_w, latent=True, name="ml_prep")
    hm_f = _ml_rec_call(q_l, k_l, v_l, gv_l, grf_l, reverse=False, state=st_f, name="ml_fwd")
    h_ml = _ml_rec_call(q_l, k_l, v_l, gv_l, grr_l, reverse=True, state=st_r,
                        merge_with=(hm_f, xc_l, ml_norm_g[0].reshape(1, D), ml_skip[0].reshape(1, D)), name="ml_rev")

    return _final_call(
        x, h_rg, grg, h_ml.reshape(NB, SEQ, D), smlo, sgr, sgm, mod3,
        norm2_g[0].reshape(1, D), final_norm_g.reshape(1, D),
        w_branch_rg[0].astype(BF16), w_branch_ml[0].astype(BF16), w_out[0].astype(BF16),
        w_ffn_in[0].astype(BF16), w_ffn_out[0].astype(BF16))
```

```python
import functools

import jax
import jax.numpy as jnp
from jax import lax
from jax.experimental import pallas as pl
from jax.experimental.pallas import tpu as pltpu

F32 = jnp.float32
BF16 = jnp.bfloat16

D = 1024
NB = 8
SEQ = 4096
GRID_W = 64
CTX = 256
EPS = 1e-6
RG_C = 8.0
HEADS = 4
DH = D // HEADS
D_FF = 2816
N_IN = 6 * D
LOG2E = 1.4426950408889634

VMEM_LIMIT = 60 * 1024 * 1024
BF16_ROWS = 16

PROJ_ROWS = 1024
FINAL_ROWS = 512
RG_STEPS = 128
RG_SUB = 256
RG_SUB_STEPS = RG_SUB // NB
ML_CHUNK = 256
ML_GROUP_COLS = 16
ML_TILE = ML_GROUP_COLS * GRID_W
ML_A_ROWS = 512

assert ML_CHUNK == DH


def _sigmoid(x):
    return 0.5 * (jnp.tanh(0.5 * x) + 1.0)


def _silu(x):
    return x * _sigmoid(x)


def _softplus(x):
    return jnp.maximum(x, 0.0) + jnp.log(1.0 + jnp.exp(-jnp.abs(x)))


def _log_sigmoid(x):
    return jnp.minimum(x, 0.0) - jnp.log(1.0 + jnp.exp(-jnp.abs(x)))


def _split3(x):
    hi = x.astype(BF16)
    r1 = x - hi.astype(F32)
    mid = r1.astype(BF16)
    lo = (r1 - mid.astype(F32)).astype(BF16)
    return hi, mid, lo


def _params(sem):
    return pltpu.CompilerParams(dimension_semantics=sem, vmem_limit_bytes=VMEM_LIMIT)


def _const_spec(shape):
    nd = len(shape)
    return pl.BlockSpec(shape, lambda *_: (0,) * nd, pipeline_mode=pl.Buffered(1))


def _swap_perm(a, b):
    n = a * b
    out_row = jnp.arange(n)
    src = (out_row % a) * b + out_row // a
    return (src[:, None] == jnp.arange(n)[None, :]).astype(BF16)


def _mod_kernel(c_ref, cc_ref, w_ref, b_ref, o_ref):
    s = jnp.concatenate([c_ref[...], jnp.broadcast_to(cc_ref[...], (NB, D))], axis=0)
    s = _silu(s)
    s_hi, s_mid, _ = _split3(s)
    w_hi, w_mid, _ = _split3(w_ref[...])
    dot = functools.partial(jnp.dot, preferred_element_type=F32)
    o_ref[...] = dot(s_hi, w_hi) + dot(s_mid, w_hi) + dot(s_hi, w_mid) + b_ref[...]


def _mod_call(c, c_ctx, w_mod, b_mod):
    return pl.pallas_call(
        _mod_kernel,
        grid=(6,),
        in_specs=[
            pl.BlockSpec((NB, D), lambda g: (0, 0)),
            pl.BlockSpec((1, D), lambda g: (0, 0)),
            pl.BlockSpec((D, D), lambda g: (0, g)),
            pl.BlockSpec((1, D), lambda g: (0, g)),
        ],
        out_specs=pl.BlockSpec((2 * NB, D), lambda g: (0, g)),
        out_shape=jax.ShapeDtypeStruct((2 * NB, 6 * D), F32),
        compiler_params=_params(("arbitrary",)),
        name="mod",
    )(c, c_ctx.reshape(1, D), w_mod, b_mod.reshape(1, 6 * D))


def _gelu_tanh(x):
    return jax.nn.gelu(x, approximate=True)


def _identity(x):
    return x


_PROJ_FULL = ((0, _identity), (1, _gelu_tanh), (2, _identity), (3, _sigmoid), (4, _sigmoid), (5, _sigmoid))
_PROJ_CTX = ((0, _identity), (2, _identity))


def _proj_kernel(x_ref, sh_ref, sc_ref, g_ref, w_ref, *o_refs, groups):
    x = x_ref[...]
    ms = jnp.mean(x * x, axis=-1, keepdims=True)
    y = x * lax.rsqrt(ms + EPS) * g_ref[...]
    u = (y * (1.0 + sc_ref[...]) + sh_ref[...]).astype(BF16)
    for o_ref, (g, act) in zip(o_refs, groups):
        p = jnp.dot(u, w_ref[:, g * D:(g + 1) * D], preferred_element_type=F32)
        o_ref[...] = act(p).astype(o_ref.dtype)


def _proj_call(x, mod3, norm_g, w_in_bf, *, ctx):
    L = x.shape[1]
    rows = min(PROJ_ROWS, L)
    groups = _PROJ_CTX if ctx else _PROJ_FULL
    mod_row = (lambda b: NB) if ctx else (lambda b: b)
    row_spec = pl.BlockSpec((None, rows, D), lambda b, i: (b, i, 0))
    return pl.pallas_call(
        functools.partial(_proj_kernel, groups=groups),
        grid=(NB, L // rows),
        in_specs=[
            row_spec,
            pl.BlockSpec((None, 1, D), lambda b, i: (mod_row(b), 0, 0)),
            pl.BlockSpec((None, 1, D), lambda b, i: (mod_row(b), 0, 1)),
            _const_spec((1, D)),
            _const_spec((D, N_IN)),
        ],
        out_specs=[row_spec] * len(groups),
        out_shape=[jax.ShapeDtypeStruct((NB, L, D), BF16)] * len(groups),
        compiler_params=_params(("arbitrary", "arbitrary")),
        name="proj_ctx" if ctx else "proj",
    )(x, mod3, mod3, norm_g, w_in_bf)


def _rg_kernel(*refs, reverse, n_tiles, mode):
    if mode == "rev":
        (xh_ref, h0_ref, wbd_ref, lam_ref, hf_ref, pout_ref, o_ref, hlast_ref, a_scr, b_scr, h_scr) = refs
    elif mode == "fwd":
        (x_ref, xp_ref, xn_ref, h0_ref, cw_ref, cb_ref, wbd_ref, lam_ref, pin_ref, phalo_ref,
         o_ref, xh_out, hlast_ref, xe_scr, a_scr, b_scr, h_scr) = refs
    else:
        (x_ref, xp_ref, xn_ref, h0_ref, cw_ref, cb_ref, wbd_ref, lam_ref, pin_ref, phalo_ref,
         o_ref, hlast_ref, xe_scr, a_scr, b_scr, h_scr) = refs
    i = pl.program_id(0)
    tile = (n_tiles - 1 - i) if reverse else i
    rows = RG_STEPS * NB
    halo = 2 * NB

    @pl.when(i == 0)
    def _():
        h_scr[...] = h0_ref[...]

    if mode != "rev":
        def halo_rows(ref):
            xb = jnp.concatenate([ref[b] for b in range(NB)], axis=0)
            return jnp.dot(phalo_ref[...], xb, preferred_element_type=F32)

        xe_scr[0:halo, :] = jnp.where(tile > 0, halo_rows(xp_ref)[(BF16_ROWS - 2) * NB:BF16_ROWS * NB], 0.0)
        xe_scr[halo + rows:2 * halo + rows, :] = jnp.where(tile < n_tiles - 1, halo_rows(xn_ref)[0:halo], 0.0)

        def fill_block(s, carry):
            t0 = pl.multiple_of(s * RG_SUB_STEPS, RG_SUB_STEPS)
            xb = jnp.concatenate([x_ref[b, pl.ds(t0, RG_SUB_STEPS), :] for b in range(NB)], axis=0)
            r0 = pl.multiple_of(s * RG_SUB, RG_SUB)
            xe_scr[pl.ds(halo + r0, RG_SUB), :] = jnp.dot(pin_ref[...], xb, preferred_element_type=F32)
            return carry

        lax.fori_loop(0, rows // RG_SUB, fill_block, 0)
        cw = 0.5 * cw_ref[...]
        cb = 0.5 * cb_ref[...]

    lam2 = (-0.5 * RG_C * LOG2E) * _softplus(-lam_ref[...])

    ones = jnp.ones((RG_SUB, 128), BF16)

    def coef_block(s, carry):
        r0 = pl.multiple_of(s * RG_SUB, RG_SUB)
        if mode == "rev":
            xh = xh_ref[pl.ds(r0, RG_SUB), :].astype(F32)
        else:
            xh = cb + cw[0:1] * xe_scr[pl.ds(r0, RG_SUB), :]
            xh = xh + cw[1:2] * xe_scr[pl.ds(r0 + NB, RG_SUB), :]
            xh = xh + cw[2:3] * xe_scr[pl.ds(r0 + 2 * NB, RG_SUB), :]
            xh = xh + cw[3:4] * xe_scr[pl.ds(r0 + 3 * NB, RG_SUB), :]
        if mode == "fwd":
            xh_out[pl.ds(r0, RG_SUB), :] = xh.astype(BF16)
        for p in range(D // 128):
            lo, hi = p * 128, (p + 1) * 128
            xhp = xh[:, lo:hi]
            lhs = jnp.concatenate([xhp.astype(BF16), ones], axis=1)
            pre = jnp.dot(lhs, wbd_ref[p], preferred_element_type=F32)
            t_r = jnp.tanh(pre[:, 0:128])
            t_i = jnp.tanh(pre[:, 128:256])
            a = jnp.exp2(lam2[:, lo:hi] * (t_r + 1.0))
            y = (1.0 - a) * (1.0 + a)
            gain = jnp.where(y > 0.0, y * lax.rsqrt(y), 0.0)
            a_scr[pl.ds(r0, RG_SUB), lo:hi] = a
            b_scr[pl.ds(r0, RG_SUB), lo:hi] = gain * ((t_i + 1.0) * xhp)
        return carry

    lax.fori_loop(0, rows // RG_SUB, coef_block, 0)

    def scan_step(t, h):
        tt = (RG_STEPS - 1 - t) if reverse else t
        r0 = pl.multiple_of(tt * NB, NB)
        h = a_scr[pl.ds(r0, NB), :] * h + b_scr[pl.ds(r0, NB), :]
        b_scr[pl.ds(r0, NB), :] = h
        return h

    h_last = lax.fori_loop(0, RG_STEPS, scan_step, h_scr[...], unroll=8)
    h_scr[...] = h_last
    hlast_ref[...] = h_last

    if mode == "rev":
        def out_block(s, carry):
            r0 = pl.multiple_of(s * RG_SUB, RG_SUB)
            t0 = pl.multiple_of(s * RG_SUB_STEPS, RG_SUB_STEPS)
            h = (b_scr[pl.ds(r0, RG_SUB), :] + hf_ref[pl.ds(r0, RG_SUB), :].astype(F32)).astype(BF16)
            hb = jnp.dot(pout_ref[...], h, preferred_element_type=F32).astype(o_ref.dtype)
            for b in range(NB):
                o_ref[b, pl.ds(t0, RG_SUB_STEPS), :] = hb[b * RG_SUB_STEPS:(b + 1) * RG_SUB_STEPS]
            return carry
        lax.fori_loop(0, rows // RG_SUB, out_block, 0)
    else:
        o_ref[...] = b_scr[...].astype(o_ref.dtype)


def _rg_call(src, h0, cw, cb, wbd, lam, *, reverse, mode, h_fwd=None, name):
    L = src.shape[0] // NB if mode == "rev" else src.shape[1]
    rows = RG_STEPS * NB
    n_tiles = L // RG_STEPS
    per = RG_STEPS // BF16_ROWS
    n_halo = L // BF16_ROWS
    tile = (lambda i: n_tiles - 1 - i) if reverse else (lambda i: i)
    bm_spec = pl.BlockSpec((NB, RG_STEPS, D), lambda i: (0, tile(i), 0))
    tm_spec = pl.BlockSpec((rows, D), lambda i: (tile(i), 0))
    tm_shape = jax.ShapeDtypeStruct((L * NB, D), BF16)
    state_spec = pl.BlockSpec((NB, D), lambda i: (0, 0))
    state_shape = jax.ShapeDtypeStruct((NB, D), F32)
    scratch = [pltpu.VMEM((rows, D), F32), pltpu.VMEM((rows, D), F32), pltpu.VMEM((NB, D), F32)]
    if mode == "rev":
        in_specs = [tm_spec, _const_spec((NB, D)), _const_spec((D // 128, 256, 256)), _const_spec((1, D)),
                    tm_spec, _const_spec((RG_SUB, RG_SUB))]
        args = [src, h0, wbd, lam, h_fwd, _swap_perm(RG_SUB_STEPS, NB)]
        out_specs = [bm_spec, state_spec]
        out_shape = [jax.ShapeDtypeStruct((NB, L, D), BF16), state_shape]
    else:
        in_specs = [
            bm_spec,
            pl.BlockSpec((NB, BF16_ROWS, D), lambda i: (0, jnp.maximum(tile(i) * per - 1, 0), 0)),
            pl.BlockSpec((NB, BF16_ROWS, D), lambda i: (0, jnp.minimum((tile(i) + 1) * per, n_halo - 1), 0)),
            _const_spec((NB, D)),
            _const_spec((4, D)),
            _const_spec((1, D)),
            _const_spec((D // 128, 256, 256)),
            _const_spec((1, D)),
            _const_spec((RG_SUB, RG_SUB)),
            _const_spec((NB * BF16_ROWS, NB * BF16_ROWS)),
        ]
        args = [src, src, src, h0, cw, cb, wbd, lam, _swap_perm(NB, RG_SUB_STEPS), _swap_perm(NB, BF16_ROWS)]
        n_tm = 2 if mode == "fwd" else 1
        out_specs = [tm_spec] * n_tm + [state_spec]
        out_shape = [tm_shape] * n_tm + [state_shape]
        scratch = [pltpu.VMEM((rows + 4 * NB, D), F32)] + scratch
    return pl.pallas_call(
        functools.partial(_rg_kernel, reverse=reverse, n_tiles=n_tiles, mode=mode),
        grid=(n_tiles,),
        in_specs=in_specs,
        out_specs=out_specs,
        out_shape=out_shape,
        scratch_shapes=scratch,
        compiler_params=_params(("arbitrary",)),
        name=name,
    )(*args)


ML_REV_LANE = 8


def _scan_lanes(x, op, fill, reverse):
    n = x.shape[1]
    lane = lax.broadcasted_iota(jnp.int32, x.shape, 1)
    sh = 1
    while sh < n:
        if reverse:
            shifted = jnp.where(lane < n - sh, pltpu.roll(x, n - sh, axis=1), fill)
        else:
            shifted = jnp.where(lane >= sh, pltpu.roll(x, sh, axis=1), fill)
        x = op(x, shifted)
        sh *= 2
    return x


def _ml_prep_kernel(*refs, n_tiles, n_tok, latent):
    L = ML_CHUNK
    n_chunks = n_tok // L
    if latent:
        (x_ref, xp_ref, xn_ref, perm_ref, cw_ref, cb_ref, dq_ref, dk_ref, dv_ref, wg_ref, bg_ref,
         q_ref, k_ref, v_ref, xc_ref, gv_ref, grf_ref, grr_ref, xt_scr, wqk_scr, wv_scr) = refs
    else:
        (x_ref, cw_ref, cb_ref, dq_ref, dk_ref, dv_ref, wg_ref, bg_ref,
         q_ref, k_ref, v_ref, xc_ref, gv_ref, grf_ref, grr_ref, xt_scr, wqk_scr, wv_scr) = refs
    tile = pl.program_id(1)

    @pl.when((pl.program_id(0) == 0) & (tile == 0))
    def _():
        diff = lax.broadcasted_iota(jnp.int32, (DH, DH), 1) - lax.broadcasted_iota(jnp.int32, (DH, DH), 0)

        def dense(d_ref, h):
            diag = d_ref[h * DH:(h + 1) * DH, :]
            out = jnp.zeros((DH, DH), F32)
            for d in range(-3, 4):
                out = jnp.where(diff == d, diag[:, 3 + d:4 + d], out)
            return out

        dot = functools.partial(jnp.dot, preferred_element_type=F32)
        for h in range(HEADS):
            wq = dense(dq_ref, h).astype(BF16)
            wk = (dense(dk_ref, h) * (DH ** -0.5)).astype(BF16)
            wv = dense(dv_ref, h).astype(BF16)
            wqk_scr[h, :, 0:DH] = wq
            wqk_scr[h, :, DH:2 * DH] = wk
            wqk_scr[h, :, 2 * DH:3 * DH] = (dot(wq, wg_ref[h]) + dot(wk, wg_ref[HEADS + h])).astype(BF16)
            wv_scr[h, :, 0:DH] = wv
            wv_scr[h, :, DH:2 * DH] = dot(wv, wg_ref[2 * HEADS + h]).astype(BF16)

    if not latent:
        xt_scr[8:16, :] = jnp.zeros((8, D), F32)
        xt_scr[16:16 + n_tok, :] = x_ref[...].astype(F32)
        xt_scr[16 + n_tok:24 + n_tok, :] = jnp.zeros((8, D), F32)
    else:
        for g in range(GRID_W // BF16_ROWS):
            xg = x_ref[g * BF16_ROWS:(g + 1) * BF16_ROWS].reshape(BF16_ROWS * ML_GROUP_COLS, D)
            yg = jnp.dot(perm_ref[...], xg, preferred_element_type=F32)
            for w in range(ML_GROUP_COLS):
                dst = 16 + w * GRID_W + g * BF16_ROWS
                xt_scr[dst:dst + BF16_ROWS, :] = yg[w * BF16_ROWS:(w + 1) * BF16_ROWS]
        last = ML_GROUP_COLS - 1
        prev = jnp.concatenate([xp_ref[0].astype(F32)[last:last + 1], xp_ref[1].astype(F32)[last:last + 1]], axis=0)
        xt_scr[14:16, :] = jnp.where(tile > 0, prev, 0.0)
        xt_scr[16 + n_tok:17 + n_tok, :] = jnp.where(tile < n_tiles - 1, xn_ref[0].astype(F32)[0:1], 0.0)

    cw = cw_ref[...]
    for ck in range(n_chunks):
        r0 = ck * L
        xt = xt_scr[16 + r0:16 + r0 + L, :]
        xc = cb_ref[...] + cw[0:1] * xt_scr[14 + r0:14 + r0 + L, :] + cw[1:2] * xt_scr[15 + r0:15 + r0 + L, :]
        xc = xc + cw[2:3] * xt + cw[3:4] * xt_scr[17 + r0:17 + r0 + L, :]
        xc = _silu(xc)
        xc_ref[r0:r0 + L, :] = xc.astype(BF16)
        gates = jnp.zeros((L, 256), F32) + bg_ref[...]
        for h in range(HEADS):
            lo, hi = h * DH, (h + 1) * DH
            qkg = jnp.dot(xc[:, lo:hi].astype(BF16), wqk_scr[h], preferred_element_type=F32)
            vg = jnp.dot(xt[:, lo:hi].astype(BF16), wv_scr[h], preferred_element_type=F32)
            gates = gates + qkg[:, 2 * DH:3 * DH] + vg[:, DH:2 * DH]
            q_ref[r0:r0 + L, lo:hi] = qkg[:, 0:DH].astype(BF16)
            k_ref[r0:r0 + L, lo:hi] = qkg[:, DH:2 * DH].astype(BF16)
            v_ref[r0:r0 + L, lo:hi] = vg[:, 0:DH].astype(BF16)
        li_col = gates[:, 0:128]
        li_rows = li_col.T
        lf_rows = gates[:, 128:256].T
        cum_rows, pm_rows = [], []
        for gr_ref, rev in ((grf_ref, False), (grr_ref, True)):
            r8 = ML_REV_LANE if rev else 0
            lf_row = _log_sigmoid(lf_rows[r8:r8 + 8])
            cum_row = _scan_lanes(lf_row, jnp.add, 0.0, rev)
            g_row = li_rows[r8:r8 + 8] - cum_row
            gr_ref[ck] = g_row
            cum_rows.append(cum_row)
            pm_rows.append(_scan_lanes(g_row, jnp.maximum, -jnp.inf, rev))
        pad = jnp.zeros((128 - 16, L), F32)
        gv_ref[0, r0:r0 + L, :] = li_col
        gv_ref[1, r0:r0 + L, :] = jnp.concatenate(cum_rows + [pad], axis=0).T
        gv_ref[2, r0:r0 + L, :] = jnp.concatenate(pm_rows + [pad], axis=0).T


def _ml_rec_kernel(*refs, reverse, n_tok, mode):
    L = ML_CHUNK
    n_chunks = n_tok // L
    lane0 = ML_REV_LANE if reverse else 0
    if mode == "ctx":
        (q_ref, k_ref, v_ref, gv_ref, gr_ref, c_out, n_out, m_out, c_scr, n_scr, m_scr) = refs
    elif mode == "fwd":
        (q_ref, k_ref, v_ref, gv_ref, gr_ref, c0_ref, n0_ref, m0_ref, o_ref, c_scr, n_scr, m_scr, h_scr) = refs
    else:
        (q_ref, k_ref, v_ref, gv_ref, gr_ref, c0_ref, n0_ref, m0_ref, hf_ref, xc_ref, ng_ref, sk_ref, perm_ref,
         o_ref, c_scr, n_scr, m_scr, h_scr) = refs

    @pl.when(pl.program_id(1) == 0)
    def _():
        if mode == "ctx":
            c_scr[...] = jnp.zeros_like(c_scr)
            n_scr[...] = jnp.zeros_like(n_scr)
            m_scr[...] = jnp.zeros_like(m_scr)
        else:
            c_scr[...] = c0_ref[...]
            n_scr[...] = n0_ref[...]
            m_scr[...] = m0_ref[...]

    if mode != "ctx":
        row_id = lax.broadcasted_iota(jnp.int32, (L, L), 0)
        col_id = lax.broadcasted_iota(jnp.int32, (L, L), 1)
        keep = (col_id >= row_id) if reverse else (col_id <= row_id)

    def stage_b(ck):
        r0 = ck * L
        li = gv_ref[0, r0:r0 + L, :]
        cum = gv_ref[1, r0:r0 + L, :]
        pm = gv_ref[2, r0:r0 + L, :]
        g = li - cum
        m_prev = m_scr[0:1, :]
        edge = 0 if reverse else L - 1
        b_tot = cum[edge:edge + 1, :]
        gmax = pm[edge:edge + 1, :]
        m_new = jnp.maximum(b_tot + m_prev, b_tot + gmax)
        ws = jnp.exp(b_tot + g - m_new)
        dec = jnp.exp(b_tot + m_prev - m_new)
        if mode != "ctx":
            inter = cum + m_prev
            m_t = jnp.maximum(inter, cum + pm)
            c_all = cum - m_t
            si_all = jnp.exp(inter - m_t)
            fl_all = jnp.exp(-m_t)
            g_row = gr_ref[ck]
        for h in range(HEADS):
            lo, hi = h * DH, (h + 1) * DH
            ln = lane0 + h
            q = q_ref[r0:r0 + L, lo:hi]
            kb = k_ref[r0:r0 + L, lo:hi]
            v = v_ref[r0:r0 + L, lo:hi]
            dec_h = dec[:, ln:ln + 1]
            c_old = c_scr[h]
            n_old = n_scr[h]
            if mode != "ctx":
                si_c = si_all[:, ln:ln + 1]
                s = lax.dot_general(q, kb, (((1,), (1,)), ((), ())), preferred_element_type=F32)
                arg = jnp.where(keep, c_all[:, ln:ln + 1] + g_row[h:h + 1, :], -jnp.inf)
                p = s * jnp.exp(arg)
                num = jnp.dot(p.astype(BF16), v, preferred_element_type=F32)
                num = num + si_c * jnp.dot(q, c_old.astype(BF16), preferred_element_type=F32)
                qn = q.astype(F32) * n_old
                den = jnp.sum(p[:, 0:128] + p[:, 128:256], axis=1, keepdims=True)
                den = den + si_c * jnp.sum(qn[:, 0:128] + qn[:, 128:256], axis=1, keepdims=True)
                h_scr[r0:r0 + L, lo:hi] = num / jnp.maximum(jnp.abs(den), fl_all[:, ln:ln + 1])
            kw = kb.astype(F32) * ws[:, ln:ln + 1]
            upd = lax.dot_general(kw.astype(BF16), v, (((0,), (0,)), ((), ())), preferred_element_type=F32)
            c_scr[h] = dec_h * c_old + upd
            n_scr[h] = dec_h * n_old + jnp.sum(kw, axis=0, keepdims=True)
        m_scr[...] = jnp.broadcast_to(m_new, m_scr.shape)

    for ci in range(n_chunks):
        stage_b((n_chunks - 1 - ci) if reverse else ci)

    a_rows = min(ML_A_ROWS, n_tok)
    if mode == "ctx":
        c_out[...] = c_scr[...]
        n_out[...] = n_scr[...]
        m_out[...] = m_scr[...]
    elif mode == "fwd":
        o_ref[...] = h_scr[...].astype(o_ref.dtype)
    else:
        for sb in range(n_tok // a_rows):
            r0 = sb * a_rows
            ht = h_scr[r0:r0 + a_rows, :] + hf_ref[r0:r0 + a_rows, :].astype(F32)
            for h in range(HEADS):
                lo, hi = h * DH, (h + 1) * DH
                hh = ht[:, lo:hi]
                mu = jnp.mean(hh, axis=-1, keepdims=True)
                var = jnp.mean(jnp.square(hh - mu), axis=-1, keepdims=True)
                hn = (hh - mu) * lax.rsqrt(var + EPS)
                xc = xc_ref[r0:r0 + a_rows, lo:hi].astype(F32)
                h_scr[r0:r0 + a_rows, lo:hi] = hn * ng_ref[:, lo:hi] + sk_ref[:, lo:hi] * xc
        for g in range(GRID_W // BF16_ROWS):
            zg = jnp.concatenate(
                [h_scr[w * GRID_W + g * BF16_ROWS:w * GRID_W + (g + 1) * BF16_ROWS, :] for w in range(ML_GROUP_COLS)],
                axis=0).astype(BF16)
            og = jnp.dot(perm_ref[...], zg, preferred_element_type=F32).astype(o_ref.dtype)
            o_ref[g * BF16_ROWS:(g + 1) * BF16_ROWS] = og.reshape(BF16_ROWS, ML_GROUP_COLS, D)


def _ml_weight_specs():
    return [
        _const_spec((4, D)),
        _const_spec((1, D)),
        _const_spec((D, 8)),
        _const_spec((D, 8)),
        _const_spec((D, 8)),
        _const_spec((3 * HEADS, DH, 256)),
        _const_spec((1, 256)),
    ]


def _ml_prep_call(mlx, weights, *, latent, name):
    if latent:
        n_tiles, n_tok, seq = GRID_W // ML_GROUP_COLS, ML_TILE, SEQ
        in_specs = [
            pl.BlockSpec((None, GRID_W, ML_GROUP_COLS, D), lambda b, j: (b, 0, j, 0)),
            pl.BlockSpec((None, 2, ML_GROUP_COLS, D), lambda b, j: (b, GRID_W // 2 - 1, jnp.maximum(j - 1, 0), 0)),
            pl.BlockSpec((None, 1, ML_GROUP_COLS, D), lambda b, j: (b, 0, jnp.minimum(j + 1, n_tiles - 1), 0)),
            _const_spec((BF16_ROWS * ML_GROUP_COLS, BF16_ROWS * ML_GROUP_COLS)),
        ]
        args = [mlx, mlx, mlx, _swap_perm(BF16_ROWS, ML_GROUP_COLS)]
    else:
        n_tiles, n_tok, seq = 1, CTX, CTX
        in_specs = [pl.BlockSpec((None, CTX, D), lambda b, j: (b, 0, 0))]
        args = [mlx]
    n_chunks = n_tok // ML_CHUNK
    tok_spec = pl.BlockSpec((None, n_tok, D), lambda b, j: (b, j, 0))
    tok_shape = jax.ShapeDtypeStruct((NB, seq, D), BF16)
    row_spec = pl.BlockSpec((None, n_chunks, 8, ML_CHUNK), lambda b, j: (b, j, 0, 0))
    row_shape = jax.ShapeDtypeStruct((NB, seq // ML_CHUNK, 8, ML_CHUNK), F32)
    return pl.pallas_call(
        functools.partial(_ml_prep_kernel, n_tiles=n_tiles, n_tok=n_tok, latent=latent),
        grid=(NB, n_tiles),
        in_specs=in_specs + _ml_weight_specs(),
        out_specs=[tok_spec] * 4 + [pl.BlockSpec((None, 3, n_tok, 128), lambda b, j: (b, 0, j, 0)), row_spec, row_spec],
        out_shape=[tok_shape] * 4 + [jax.ShapeDtypeStruct((NB, 3, seq, 128), F32), row_shape, row_shape],
        scratch_shapes=[
            pltpu.VMEM((n_tok + 32, D), F32),
            pltpu.VMEM((HEADS, DH, 3 * DH), BF16),
            pltpu.VMEM((HEADS, DH, 2 * DH), BF16),
        ],
        compiler_params=_params(("arbitrary", "arbitrary")),
        name=name,
    )(*args, *weights)


_STATE_SHAPES = [
    jax.ShapeDtypeStruct((NB, HEADS, DH, DH), F32),
    jax.ShapeDtypeStruct((NB, HEADS, 1, DH), F32),
    jax.ShapeDtypeStruct((NB, 8, 128), F32),
]


def _ml_rec_call(q, k, v, gv, g_rows, *, reverse, state=None, merge_with=None, name):
    seq = q.shape[1]
    n_tok = min(ML_TILE, seq)
    n_tiles = seq // n_tok
    n_chunks = n_tok // ML_CHUNK
    tile = (lambda j: n_tiles - 1 - j) if reverse else (lambda j: j)
    tok_spec = pl.BlockSpec((None, n_tok, D), lambda b, j: (b, tile(j), 0))
    state_specs = [
        pl.BlockSpec((None, HEADS, DH, DH), lambda b, j: (b, 0, 0, 0)),
        pl.BlockSpec((None, HEADS, 1, DH), lambda b, j: (b, 0, 0, 0)),
        pl.BlockSpec((None, 8, 128), lambda b, j: (b, 0, 0)),
    ]
    in_specs = [
        tok_spec, tok_spec, tok_spec,
        pl.BlockSpec((None, 3, n_tok, 128), lambda b, j: (b, 0, tile(j), 0)),
        pl.BlockSpec((None, n_chunks, 8, ML_CHUNK), lambda b, j: (b, tile(j), 0, 0)),
    ]
    args = [q, k, v, gv, g_rows]
    scratch = [
        pltpu.VMEM((HEADS, DH, DH), F32),
        pltpu.VMEM((HEADS, 1, DH), F32),
        pltpu.VMEM((8, 128), F32),
    ]
    if state is None:
        mode = "ctx"
        out_specs, out_shape = state_specs, _STATE_SHAPES
    else:
        in_specs += state_specs
        args += list(state)
        scratch.append(pltpu.VMEM((n_tok, D), F32))
        if merge_with is None:
            mode = "fwd"
            out_specs, out_shape = tok_spec, jax.ShapeDtypeStruct((NB, seq, D), BF16)
        else:
            mode = "rev"
            h_fwd, xc, norm_g, skip = merge_with
            in_specs += [tok_spec, tok_spec, _const_spec((1, D)), _const_spec((1, D)),
                         _const_spec((BF16_ROWS * ML_GROUP_COLS, BF16_ROWS * ML_GROUP_COLS))]
            args += [h_fwd, xc, norm_g, skip, _swap_perm(BF16_ROWS, ML_GROUP_COLS)]
            out_specs = pl.BlockSpec((None, GRID_W, ML_GROUP_COLS, D), lambda b, j: (b, 0, tile(j), 0))
            out_shape = jax.ShapeDtypeStruct((NB, GRID_W, GRID_W, D), BF16)
    return pl.pallas_call(
        functools.partial(_ml_rec_kernel, reverse=reverse, n_tok=n_tok, mode=mode),
        grid=(NB, n_tiles),
        in_specs=in_specs,
        out_specs=out_specs,
        out_shape=out_shape,
        scratch_shapes=scratch,
        compiler_params=_params(("arbitrary", "arbitrary")),
        name=name,
    )(*args)


def _final_kernel(x_ref, hrg_ref, grg_ref, hml_ref, smlo_ref, sgr_ref, sgm_ref, g1_ref, sh2_ref, sc2_ref, g2_ref,
                  n2_ref, nf_ref, wbr_ref, wbm_ref, wo_ref, wfi_ref, wfo_ref, o_ref, act_scr):
    dot = functools.partial(jnp.dot, preferred_element_type=F32)
    y_rg = (hrg_ref[...].astype(F32) * grg_ref[...].astype(F32)).astype(BF16)
    y_ml = (hml_ref[...].astype(F32) * smlo_ref[...].astype(F32)).astype(BF16)
    mix = sgr_ref[...].astype(F32) * dot(y_rg, wbr_ref[...])
    mix = mix + sgm_ref[...].astype(F32) * dot(y_ml, wbm_ref[...])
    x1 = x_ref[...] + g1_ref[...] * dot(mix.astype(BF16), wo_ref[...])
    ms = jnp.mean(x1 * x1, axis=-1, keepdims=True)
    hn = x1 * lax.rsqrt(ms + EPS) * n2_ref[...]
    hb = (hn * (1.0 + sc2_ref[...]) + sh2_ref[...]).astype(BF16)
    step = 256
    for c in range(D_FF // step):
        gate = dot(hb, wfi_ref[:, c * step:(c + 1) * step])
        up = dot(hb, wfi_ref[:, D_FF + c * step:D_FF + (c + 1) * step])
        act_scr[:, c * step:(c + 1) * step] = (_silu(gate) * up).astype(BF16)
    x2 = x1 + g2_ref[...] * dot(act_scr[...], wfo_ref[...])
    ms2 = jnp.mean(x2 * x2, axis=-1, keepdims=True)
    o_ref[...] = x2 * lax.rsqrt(ms2 + EPS) * nf_ref[...]


def _final_call(x, h_rg, grg, h_ml, smlo, sgr, sgm, mod3, norm2_g, final_g, wbr, wbm, wo, wfi, wfo):
    rows = FINAL_ROWS
    row_spec = pl.BlockSpec((None, rows, D), lambda b, i: (b, i, 0))
    mod_spec = lambda g: pl.BlockSpec((None, 1, D), lambda b, i: (b, 0, g))
    return pl.pallas_call(
        _final_kernel,
        grid=(NB, SEQ // rows),
        in_specs=[
            row_spec, row_spec, row_spec, row_spec, row_spec, row_spec, row_spec,
            mod_spec(2), mod_spec(3), mod_spec(4), mod_spec(5),
            _const_spec((1, D)), _const_spec((1, D)),
            _const_spec((D, D)), _const_spec((D, D)), _const_spec((D, D)),
            _const_spec((D, 2 * D_FF)), _const_spec((D_FF, D)),
        ],
        out_specs=row_spec,
        out_shape=jax.ShapeDtypeStruct((NB, SEQ, D), F32),
        scratch_shapes=[pltpu.VMEM((rows, D_FF), BF16)],
        compiler_params=_params(("arbitrary", "arbitrary")),
        name="final",
    )(x, h_rg, grg, h_ml, smlo, sgr, sgm, mod3, mod3, mod3, mod3, norm2_g, final_g, wbr, wbm, wo, wfi, wfo)


def _pair_blockdiag(w):
    w = w.reshape(8, 2, 64, 64)
    z = jnp.zeros((8, 64, 64), w.dtype)
    top = jnp.concatenate([w[:, 0], z], axis=2)
    bot = jnp.concatenate([z, w[:, 1]], axis=2)
    return jnp.concatenate([top, bot], axis=1)


def _rg_weights(wa, ba, wx, bx):
    w = jnp.concatenate([_pair_blockdiag(wa), _pair_blockdiag(wx)], axis=2).astype(BF16)
    bias = 0.5 * jnp.concatenate([ba.reshape(8, 1, 128), bx.reshape(8, 1, 128)], axis=2)
    b_hi = bias.astype(BF16)
    b_lo = (bias - b_hi.astype(F32)).astype(BF16)
    return (jnp.concatenate([w, b_hi, b_lo, jnp.zeros((8, 126, 256), BF16)], axis=1),)


def _block_diagonals(w):
    rows = [jnp.pad(w[:, i, :], ((0, 0), (3 - i, 1 + i))) for i in range(4)]
    return jnp.stack(rows, axis=1).reshape(D, 8)


def _gate_weights(wi, bi, wf, bf):
    def lanes(x):
        gap = [(0, 0)] * (x.ndim - 2)
        fwd = jnp.pad(x[0], gap + [(0, ML_REV_LANE - HEADS)])
        rev = jnp.pad(x[1], gap + [(0, 128 - ML_REV_LANE - HEADS)])
        return jnp.concatenate([fwd, rev], axis=-1)
    w = jnp.concatenate([lanes(wi), lanes(wf)], axis=-1)
    b = jnp.concatenate([lanes(bi), lanes(bf)], axis=-1).reshape(1, 256)
    return w.reshape(3 * HEADS, DH, 256).astype(BF16), b


def kernel(x, c, ctx, c_ctx, w_mod, b_mod, norm1_g, norm2_g, w_in, rg_conv_w, rg_conv_b, rg_wa, rg_ba, rg_wx,
           rg_bx, rg_lambda, ml_conv_w, ml_conv_b, ml_wq, ml_wk, ml_wv, ml_wi, ml_bi, ml_wf, ml_bf,
           ml_norm_g, ml_skip, w_branch_rg, w_branch_ml, w_out, w_ffn_in, w_ffn_out, final_norm_g):
    mod = _mod_call(c, c_ctx, w_mod[0], b_mod[0])
    mod3 = mod.reshape(2 * NB, 1, 6 * D)
    w_in_bf = w_in[0].astype(BF16)
    norm1 = norm1_g[0].reshape(1, D)

    rgx, grg, mlx, smlo, sgr, sgm = _proj_call(x, mod3, norm1, w_in_bf, ctx=False)
    rgx_c, mlx_c = _proj_call(ctx, mod3, norm1, w_in_bf, ctx=True)

    rg_cw = rg_conv_w[0]
    rg_cb = rg_conv_b[0].reshape(1, D)
    zero_h = jnp.zeros((NB, D), F32)
    rg_w = [_rg_weights(rg_wa[0, d], rg_ba[0, d], rg_wx[0, d], rg_bx[0, d]) + (rg_lambda[0, d].reshape(1, D),)
            for d in range(2)]
    _, h0_f = _rg_call(rgx_c, zero_h, rg_cw, rg_cb, *rg_w[0], reverse=False, mode="ctx", name="rg_ctx_fwd")
    _, h0_r = _rg_call(rgx_c, zero_h, rg_cw, rg_cb, *rg_w[1], reverse=True, mode="ctx", name="rg_ctx_rev")
    h_f, rg_xh, _ = _rg_call(rgx, h0_f, rg_cw, rg_cb, *rg_w[0], reverse=False, mode="fwd", name="rg_fwd")
    h_rg, _ = _rg_call(rg_xh, h0_r, rg_cw, rg_cb, *rg_w[1], reverse=True, mode="rev", h_fwd=h_f, name="rg_rev")

    ml_cw = ml_conv_w[0]
    ml_cb = ml_conv_b[0].reshape(1, D)
    ml_qkv = (_block_diagonals(ml_wq[0]), _block_diagonals(ml_wk[0]), _block_diagonals(ml_wv[0]))
    ml_w = (ml_cw, ml_cb) + ml_qkv + _gate_weights(ml_wi[0], ml_bi[0], ml_wf[0], ml_bf[0])
    q_c, k_c, v_c, _, gv_c, grf_c, grr_c = _ml_prep_call(mlx_c, ml_w, latent=False, name="ml_prep_ctx")
    st_f = _ml_rec_call(q_c, k_c, v_c, gv_c, grf_c, reverse=False, name="ml_ctx_fwd")
    st_r = _ml_rec_call(q_c, k_c, v_c, gv_c, grr_c, reverse=True, name="ml_ctx_rev")
    q_l, k_l, v_l, xc_l, gv_l, grf_l, grr_l = _ml_prep_call(
        mlx.reshape(NB, GRID_W, GRID_W, D), ml_w, latent=True, name="ml_prep")
    hm_f = _ml_rec_call(q_l, k_l, v_l, gv_l, grf_l, reverse=False, state=st_f, name="ml_fwd")
    h_ml = _ml_rec_call(q_l, k_l, v_l, gv_l, grr_l, reverse=True, state=st_r,
                        merge_with=(hm_f, xc_l, ml_norm_g[0].reshape(1, D), ml_skip[0].reshape(1, D)), name="ml_rev")

    return _final_call(
        x, h_rg, grg, h_ml.reshape(NB, SEQ, D), smlo, sgr, sgm, mod3,
        norm2_g[0].reshape(1, D), final_norm_g.reshape(1, D),
        w_branch_rg[0].astype(BF16), w_branch_ml[0].astype(BF16), w_out[0].astype(BF16),
        w_ffn_in[0].astype(BF16), w_ffn_out[0].astype(BF16))
```

```python
import functools

import jax
import jax.numpy as jnp
from jax import lax
from jax.experimental import pallas as pl
from jax.experimental.pallas import tpu as pltpu

F32 = jnp.float32
BF16 = jnp.bfloat16

D = 1024
NB = 8
SEQ = 4096
GRID_W = 64
CTX = 256
EPS = 1e-6
RG_C = 8.0
HEADS = 4
DH = D // HEADS
D_FF = 2816
N_IN = 6 * D
LOG2E = 1.4426950408889634

VMEM_LIMIT = 60 * 1024 * 1024
BF16_ROWS = 16

PROJ_ROWS = 1024
FINAL_ROWS = 512
RG_STEPS = 128
RG_SUB = 256
RG_SUB_STEPS = RG_SUB // NB
ML_CHUNK = 256
ML_GROUP_COLS = 16
ML_TILE = ML_GROUP_COLS * GRID_W
ML_A_ROWS = 512

assert ML_CHUNK == DH


def _sigmoid(x):
    return 0.5 * (jnp.tanh(0.5 * x) + 1.0)


def _silu(x):
    return x * _sigmoid(x)


def _softplus(x):
    return jnp.maximum(x, 0.0) + jnp.log(1.0 + jnp.exp(-jnp.abs(x)))


def _log_sigmoid(x):
    return jnp.minimum(x, 0.0) - jnp.log(1.0 + jnp.exp(-jnp.abs(x)))


def _split3(x):
    hi = x.astype(BF16)
    r1 = x - hi.astype(F32)
    mid = r1.astype(BF16)
    lo = (r1 - mid.astype(F32)).astype(BF16)
    return hi, mid, lo


def _params(sem):
    return pltpu.CompilerParams(dimension_semantics=sem, vmem_limit_bytes=VMEM_LIMIT)


def _const_spec(shape):
    nd = len(shape)
    return pl.BlockSpec(shape, lambda *_: (0,) * nd, pipeline_mode=pl.Buffered(1))


def _swap_perm(a, b):
    n = a * b
    out_row = jnp.arange(n)
    src = (out_row % a) * b + out_row // a
    return (src[:, None] == jnp.arange(n)[None, :]).astype(BF16)


def _mod_kernel(c_ref, cc_ref, w_ref, b_ref, o_ref):
    s = jnp.concatenate([c_ref[...], jnp.broadcast_to(cc_ref[...], (NB, D))], axis=0)
    s = _silu(s)
    s_hi, s_mid, _ = _split3(s)
    w_hi, w_mid, _ = _split3(w_ref[...])
    dot = functools.partial(jnp.dot, preferred_element_type=F32)
    o_ref[...] = dot(s_hi, w_hi) + dot(s_mid, w_hi) + dot(s_hi, w_mid) + b_ref[...]


def _mod_call(c, c_ctx, w_mod, b_mod):
    return pl.pallas_call(
        _mod_kernel,
        grid=(6,),
        in_specs=[
            pl.BlockSpec((NB, D), lambda g: (0, 0)),
            pl.BlockSpec((1, D), lambda g: (0, 0)),
            pl.BlockSpec((D, D), lambda g: (0, g)),
            pl.BlockSpec((1, D), lambda g: (0, g)),
        ],
        out_specs=pl.BlockSpec((2 * NB, D), lambda g: (0, g)),
        out_shape=jax.ShapeDtypeStruct((2 * NB, 6 * D), F32),
        compiler_params=_params(("arbitrary",)),
        name="mod",
    )(c, c_ctx.reshape(1, D), w_mod, b_mod.reshape(1, 6 * D))


def _gelu_tanh(x):
    return jax.nn.gelu(x, approximate=True)


def _identity(x):
    return x


_PROJ_FULL = ((0, _identity), (1, _gelu_tanh), (2, _identity), (3, _sigmoid), (4, _sigmoid), (5, _sigmoid))
_PROJ_CTX = ((0, _identity), (2, _identity))


def _proj_kernel(x_ref, sh_ref, sc_ref, g_ref, w_ref, *o_refs, groups):
    x = x_ref[...]
    ms = jnp.mean(x * x, axis=-1, keepdims=True)
    y = x * lax.rsqrt(ms + EPS) * g_ref[...]
    u = (y * (1.0 + sc_ref[...]) + sh_ref[...]).astype(BF16)
    for o_ref, (g, act) in zip(o_refs, groups):
        p = jnp.dot(u, w_ref[:, g * D:(g + 1) * D], preferred_element_type=F32)
        o_ref[...] = act(p).astype(o_ref.dtype)


def _proj_call(x, mod3, norm_g, w_in_bf, *, ctx):
    L = x.shape[1]
    rows = min(PROJ_ROWS, L)
    groups = _PROJ_CTX if ctx else _PROJ_FULL
    mod_row = (lambda b: NB) if ctx else (lambda b: b)
    row_spec = pl.BlockSpec((None, rows, D), lambda b, i: (b, i, 0))
    return pl.pallas_call(
        functools.partial(_proj_kernel, groups=groups),
        grid=(NB, L // rows),
        in_specs=[
            row_spec,
            pl.BlockSpec((None, 1, D), lambda b, i: (mod_row(b), 0, 0)),
            pl.BlockSpec((None, 1, D), lambda b, i: (mod_row(b), 0, 1)),
            _const_spec((1, D)),
            _const_spec((D, N_IN)),
        ],
        out_specs=[row_spec] * len(groups),
        out_shape=[jax.ShapeDtypeStruct((NB, L, D), BF16)] * len(groups),
        compiler_params=_params(("arbitrary", "arbitrary")),
        name="proj_ctx" if ctx else "proj",
    )(x, mod3, mod3, norm_g, w_in_bf)


def _rg_kernel(*refs, reverse, n_tiles, mode):
    if mode == "rev":
        (xh_ref, h0_ref, wbd_ref, lam_ref, hf_ref, pout_ref, o_ref, hlast_ref, a_scr, b_scr, h_scr) = refs
    elif mode == "fwd":
        (x_ref, xp_ref, xn_ref, h0_ref, cw_ref, cb_ref, wbd_ref, lam_ref, pin_ref, phalo_ref,
         o_ref, xh_out, hlast_ref, xe_scr, a_scr, b_scr, h_scr) = refs
    else:
        (x_ref, xp_ref, xn_ref, h0_ref, cw_ref, cb_ref, wbd_ref, lam_ref, pin_ref, phalo_ref,
         o_ref, hlast_ref, xe_scr, a_scr, b_scr, h_scr) = refs
    i = pl.program_id(0)
    tile = (n_tiles - 1 - i) if reverse else i
    rows = RG_STEPS * NB
    halo = 2 * NB

    @pl.when(i == 0)
    def _():
        h_scr[...] = h0_ref[...]

    if mode != "rev":
        def halo_rows(ref):
            xb = jnp.concatenate([ref[b] for b in range(NB)], axis=0)
            return jnp.dot(phalo_ref[...], xb, preferred_element_type=F32)

        xe_scr[0:halo, :] = jnp.where(tile > 0, halo_rows(xp_ref)[(BF16_ROWS - 2) * NB:BF16_ROWS * NB], 0.0)
        xe_scr[halo + rows:2 * halo + rows, :] = jnp.where(tile < n_tiles - 1, halo_rows(xn_ref)[0:halo], 0.0)

        def fill_block(s, carry):
            t0 = pl.multiple_of(s * RG_SUB_STEPS, RG_SUB_STEPS)
            xb = jnp.concatenate([x_ref[b, pl.ds(t0, RG_SUB_STEPS), :] for b in range(NB)], axis=0)
            r0 = pl.multiple_of(s * RG_SUB, RG_SUB)
            xe_scr[pl.ds(halo + r0, RG_SUB), :] = jnp.dot(pin_ref[...], xb, preferred_element_type=F32)
            return carry

        lax.fori_loop(0, rows // RG_SUB, fill_block, 0)
        cw = 0.5 * cw_ref[...]
        cb = 0.5 * cb_ref[...]

    lam2 = (-0.5 * RG_C * LOG2E) * _softplus(-lam_ref[...])

    ones = jnp.ones((RG_SUB, 128), BF16)

    def coef_block(s, carry):
        r0 = pl.multiple_of(s * RG_SUB, RG_SUB)
        if mode == "rev":
            xh = xh_ref[pl.ds(r0, RG_SUB), :].astype(F32)
        else:
            xh = cb + cw[0:1] * xe_scr[pl.ds(r0, RG_SUB), :]
            xh = xh + cw[1:2] * xe_scr[pl.ds(r0 + NB, RG_SUB), :]
            xh = xh + cw[2:3] * xe_scr[pl.ds(r0 + 2 * NB, RG_SUB), :]
            xh = xh + cw[3:4] * xe_scr[pl.ds(r0 + 3 * NB, RG_SUB), :]
        if mode == "fwd":
            xh_out[pl.ds(r0, RG_SUB), :] = xh.astype(BF16)
        for p in range(D // 128):
            lo, hi = p * 128, (p + 1) * 128
            xhp = xh[:, lo:hi]
            lhs = jnp.concatenate([xhp.astype(BF16), ones], axis=1)
            pre = jnp.dot(lhs, wbd_ref[p], preferred_element_type=F32)
            t_r = jnp.tanh(pre[:, 0:128])
            t_i = jnp.tanh(pre[:, 128:256])
            a = jnp.exp2(lam2[:, lo:hi] * (t_r + 1.0))
            y = (1.0 - a) * (1.0 + a)
            gain = jnp.where(y > 0.0, y * lax.rsqrt(y), 0.0)
            a_scr[pl.ds(r0, RG_SUB), lo:hi] = a
            b_scr[pl.ds(r0, RG_SUB), lo:hi] = gain * ((t_i + 1.0) * xhp)
        return carry

    lax.fori_loop(0, rows // RG_SUB, coef_block, 0)

    def scan_step(t, h):
        tt = (RG_STEPS - 1 - t) if reverse else t
        r0 = pl.multiple_of(tt * NB, NB)
        h = a_scr[pl.ds(r0, NB), :] * h + b_scr[pl.ds(r0, NB), :]
        b_scr[pl.ds(r0, NB), :] = h
        return h

    h_last = lax.fori_loop(0, RG_STEPS, scan_step, h_scr[...], unroll=8)
    h_scr[...] = h_last
    hlast_ref[...] = h_last

    if mode == "rev":
        def out_block(s, carry):
            r0 = pl.multiple_of(s * RG_SUB, RG_SUB)
            t0 = pl.multiple_of(s * RG_SUB_STEPS, RG_SUB_STEPS)
            h = (b_scr[pl.ds(r0, RG_SUB), :] + hf_ref[pl.ds(r0, RG_SUB), :].astype(F32)).astype(BF16)
            hb = jnp.dot(pout_ref[...], h, preferred_element_type=F32).astype(o_ref.dtype)
            for b in range(NB):
                o_ref[b, pl.ds(t0, RG_SUB_STEPS), :] = hb[b * RG_SUB_STEPS:(b + 1) * RG_SUB_STEPS]
            return carry
        lax.fori_loop(0, rows // RG_SUB, out_block, 0)
    else:
        o_ref[...] = b_scr[...].astype(o_ref.dtype)


def _rg_call(src, h0, cw, cb, wbd, lam, *, reverse, mode, h_fwd=None, name):
    L = src.shape[0] // NB if mode == "rev" else src.shape[1]
    rows = RG_STEPS * NB
    n_tiles = L // RG_STEPS
    per = RG_STEPS // BF16_ROWS
    n_halo = L // BF16_ROWS
    tile = (lambda i: n_tiles - 1 - i) if reverse else (lambda i: i)
    bm_spec = pl.BlockSpec((NB, RG_STEPS, D), lambda i: (0, tile(i), 0))
    tm_spec = pl.BlockSpec((rows, D), lambda i: (tile(i), 0))
    tm_shape = jax.ShapeDtypeStruct((L * NB, D), BF16)
    state_spec = pl.BlockSpec((NB, D), lambda i: (0, 0))
    state_shape = jax.ShapeDtypeStruct((NB, D), F32)
    scratch = [pltpu.VMEM((rows, D), F32), pltpu.VMEM((rows, D), F32), pltpu.VMEM((NB, D), F32)]
    if mode == "rev":
        in_specs = [tm_spec, _const_spec((NB, D)), _const_spec((D // 128, 256, 256)), _const_spec((1, D)),
                    tm_spec, _const_spec((RG_SUB, RG_SUB))]
        args = [src, h0, wbd, lam, h_fwd, _swap_perm(RG_SUB_STEPS, NB)]
        out_specs = [bm_spec, state_spec]
        out_shape = [jax.ShapeDtypeStruct((NB, L, D), BF16), state_shape]
    else:
        in_specs = [
            bm_spec,
            pl.BlockSpec((NB, BF16_ROWS, D), lambda i: (0, jnp.maximum(tile(i) * per - 1, 0), 0)),
            pl.BlockSpec((NB, BF16_ROWS, D), lambda i: (0, jnp.minimum((tile(i) + 1) * per, n_halo - 1), 0)),
            _const_spec((NB, D)),
            _const_spec((4, D)),
            _const_spec((1, D)),
            _const_spec((D // 128, 256, 256)),
            _const_spec((1, D)),
            _const_spec((RG_SUB, RG_SUB)),
            _const_spec((NB * BF16_ROWS, NB * BF16_ROWS)),
        ]
        args = [src, src, src, h0, cw, cb, wbd, lam, _swap_perm(NB, RG_SUB_STEPS), _swap_perm(NB, BF16_ROWS)]
        n_tm = 2 if mode == "fwd" else 1
        out_specs = [tm_spec] * n_tm + [state_spec]
        out_shape = [tm_shape] * n_tm + [state_shape]
        scratch = [pltpu.VMEM((rows + 4 * NB, D), F32)] + scratch
    return pl.pallas_call(
        functools.partial(_rg_kernel, reverse=reverse, n_tiles=n_tiles, mode=mode),
        grid=(n_tiles,),
        in_specs=in_specs,
        out_specs=out_specs,
        out_shape=out_shape,
        scratch_shapes=scratch,
        compiler_params=_params(("arbitrary",)),
        name=name,
    )(*args)


ML_REV_LANE = 8


def _scan_lanes_both(x, op, fill):
    n = x.shape[1]
    lane = lax.broadcasted_iota(jnp.int32, x.shape, 1)
    is_prefix = lax.broadcasted_iota(jnp.int32, x.shape, 0) < ML_REV_LANE
    sh = 1
    while sh < n:
        before = jnp.where(lane >= sh, pltpu.roll(x, sh, axis=1), fill)
        after = jnp.where(lane < n - sh, pltpu.roll(x, n - sh, axis=1), fill)
        x = op(x, jnp.where(is_prefix, before, after))
        sh *= 2
    return x


def _ml_prep_kernel(*refs, n_tiles, n_tok, latent):
    L = ML_CHUNK
    n_chunks = n_tok // L
    if latent:
        (x_ref, xp_ref, xn_ref, perm_ref, cw_ref, cb_ref, dq_ref, dk_ref, dv_ref, wg_ref, bg_ref,
         q_ref, k_ref, kt_ref, v_ref, xc_ref, gv_ref, grf_ref, grr_ref, xt_scr, wqk_scr, wv_scr, wkt_scr) = refs
    else:
        (x_ref, cw_ref, cb_ref, dq_ref, dk_ref, dv_ref, wg_ref, bg_ref,
         q_ref, k_ref, kt_ref, v_ref, xc_ref, gv_ref, grf_ref, grr_ref, xt_scr, wqk_scr, wv_scr, wkt_scr) = refs
    tile = pl.program_id(1)

    @pl.when((pl.program_id(0) == 0) & (tile == 0))
    def _():
        diff = lax.broadcasted_iota(jnp.int32, (DH, DH), 1) - lax.broadcasted_iota(jnp.int32, (DH, DH), 0)

        def dense(d_ref, h):
            diag = d_ref[h * DH:(h + 1) * DH, :]
            out = jnp.zeros((DH, DH), F32)
            for d in range(-3, 4):
                out = jnp.where(diff == d, diag[:, 3 + d:4 + d], out)
            return out

        dot = functools.partial(jnp.dot, preferred_element_type=F32)
        for h in range(HEADS):
            wq = dense(dq_ref, h).astype(BF16)
            wk_f32 = dense(dk_ref, h) * (DH ** -0.5)
            wk = wk_f32.astype(BF16)
            wv = dense(dv_ref, h).astype(BF16)
            wkt_scr[h] = wk_f32.T.astype(BF16)
            wqk_scr[h, :, 0:DH] = wq
            wqk_scr[h, :, DH:2 * DH] = wk
            wqk_scr[h, :, 2 * DH:3 * DH] = (dot(wq, wg_ref[h]) + dot(wk, wg_ref[HEADS + h])).astype(BF16)
            wv_scr[h, :, 0:DH] = wv
            wv_scr[h, :, DH:2 * DH] = dot(wv, wg_ref[2 * HEADS + h]).astype(BF16)

    if not latent:
        xt_scr[8:16, :] = jnp.zeros((8, D), F32)
        xt_scr[16:16 + n_tok, :] = x_ref[...].astype(F32)
        xt_scr[16 + n_tok:24 + n_tok, :] = jnp.zeros((8, D), F32)
    else:
        for g in range(GRID_W // BF16_ROWS):
            xg = x_ref[g * BF16_ROWS:(g + 1) * BF16_ROWS].reshape(BF16_ROWS * ML_GROUP_COLS, D)
            yg = jnp.dot(perm_ref[...], xg, preferred_element_type=F32)
            for w in range(ML_GROUP_COLS):
                dst = 16 + w * GRID_W + g * BF16_ROWS
                xt_scr[dst:dst + BF16_ROWS, :] = yg[w * BF16_ROWS:(w + 1) * BF16_ROWS]
        last = ML_GROUP_COLS - 1
        prev = jnp.concatenate([xp_ref[0].astype(F32)[last:last + 1], xp_ref[1].astype(F32)[last:last + 1]], axis=0)
        xt_scr[14:16, :] = jnp.where(tile > 0, prev, 0.0)
        xt_scr[16 + n_tok:17 + n_tok, :] = jnp.where(tile < n_tiles - 1, xn_ref[0].astype(F32)[0:1], 0.0)

    cw = cw_ref[...]
    src_t = lax.broadcasted_iota(jnp.int32, (L, L), 0)
    dst_t = lax.broadcasted_iota(jnp.int32, (L, L), 1)
    tri_fwd = (src_t <= dst_t).astype(BF16)
    tri_rev = (src_t >= dst_t).astype(BF16)
    for ck in range(n_chunks):
        r0 = ck * L
        xt = xt_scr[16 + r0:16 + r0 + L, :]
        xc = cb_ref[...] + cw[0:1] * xt_scr[14 + r0:14 + r0 + L, :] + cw[1:2] * xt_scr[15 + r0:15 + r0 + L, :]
        xc = xc + cw[2:3] * xt + cw[3:4] * xt_scr[17 + r0:17 + r0 + L, :]
        xc = _silu(xc)
        xc_ref[r0:r0 + L, :] = xc.astype(BF16)
        gates = jnp.zeros((L, 256), F32) + bg_ref[...]
        for h in range(HEADS):
            lo, hi = h * DH, (h + 1) * DH
            xc_h = xc[:, lo:hi].astype(BF16)
            qkg = jnp.dot(xc_h, wqk_scr[h], preferred_element_type=F32)
            vg = jnp.dot(xt[:, lo:hi].astype(BF16), wv_scr[h], preferred_element_type=F32)
            gates = gates + qkg[:, 2 * DH:3 * DH] + vg[:, DH:2 * DH]
            q_ref[r0:r0 + L, lo:hi] = qkg[:, 0:DH].astype(BF16)
            k_ref[r0:r0 + L, lo:hi] = qkg[:, DH:2 * DH].astype(BF16)
            kt = lax.dot_general(wkt_scr[h], xc_h, (((1,), (1,)), ((), ())), preferred_element_type=F32)
            kt_ref[lo:hi, r0:r0 + L] = kt.astype(BF16)
            v_ref[r0:r0 + L, lo:hi] = vg[:, 0:DH].astype(BF16)
        li_rows = gates[:, 0:128].T[0:2 * ML_REV_LANE]
        lf_rows = _log_sigmoid(gates[:, 128:256].T[0:2 * ML_REV_LANE])
        split = jnp.concatenate(_split3(lf_rows), axis=0)
        to_t = jnp.dot(split, tri_fwd, preferred_element_type=F32)
        from_t = jnp.dot(split, tri_rev, preferred_element_type=F32)
        cum_rows = jnp.concatenate([
            to_t[0:8] + to_t[BF16_ROWS:BF16_ROWS + 8] + to_t[2 * BF16_ROWS:2 * BF16_ROWS + 8],
            from_t[8:16] + from_t[BF16_ROWS + 8:2 * BF16_ROWS] + from_t[2 * BF16_ROWS + 8:3 * BF16_ROWS]], axis=0)
        g_rows = li_rows - cum_rows
        grf_ref[ck] = g_rows[0:ML_REV_LANE]
        grr_ref[ck] = g_rows[ML_REV_LANE:2 * ML_REV_LANE]
        pm_rows = _scan_lanes_both(g_rows, jnp.maximum, -jnp.inf)
        pad = jnp.zeros((128 - 2 * ML_REV_LANE, L), F32)
        gv_ref[0, r0:r0 + L, :] = jnp.concatenate([cum_rows, pad], axis=0).T
        gv_ref[1, r0:r0 + L, :] = jnp.concatenate([pm_rows, pad], axis=0).T
        gv_ref[2, r0:r0 + L, :] = jnp.concatenate([g_rows, pad], axis=0).T


def _ml_rec_kernel(*refs, reverse, n_tok, mode):
    L = ML_CHUNK
    n_chunks = n_tok // L
    lane0 = ML_REV_LANE if reverse else 0
    if mode == "ctx":
        (q_ref, k_ref, kt_ref, v_ref, gv_ref, gr_ref, c_out, n_out, m_out, c_scr, n_scr, m_scr) = refs
    elif mode == "fwd":
        (q_ref, k_ref, kt_ref, v_ref, gv_ref, gr_ref, c0_ref, n0_ref, m0_ref, o_ref,
         c_scr, n_scr, m_scr, h_scr) = refs
    else:
        (q_ref, k_ref, kt_ref, v_ref, gv_ref, gr_ref, c0_ref, n0_ref, m0_ref, hf_ref, xc_ref, ng_ref, sk_ref,
         perm_ref, o_ref, c_scr, n_scr, m_scr, h_scr) = refs

    @pl.when(pl.program_id(1) == 0)
    def _():
        if mode == "ctx":
            c_scr[...] = jnp.zeros_like(c_scr)
            n_scr[...] = jnp.zeros_like(n_scr)
            m_scr[...] = jnp.zeros_like(m_scr)
        else:
            c_scr[...] = c0_ref[...]
            n_scr[...] = n0_ref[...]
            m_scr[...] = m0_ref[...]

    if mode != "ctx":
        row_id = lax.broadcasted_iota(jnp.int32, (L, L), 0)
        col_id = lax.broadcasted_iota(jnp.int32, (L, L), 1)
        keep = (col_id >= row_id) if reverse else (col_id <= row_id)

    def stage_b(ck):
        r0 = ck * L
        m_prev = m_scr[0:1, :]
        edge = 0 if reverse else L - 1
        b_tot = gv_ref[0, r0 + edge:r0 + edge + 1, :]
        gmax = gv_ref[1, r0 + edge:r0 + edge + 1, :]
        m_new = jnp.maximum(b_tot + m_prev, b_tot + gmax)
        dec = jnp.exp(b_tot + m_prev - m_new)
        ws = jnp.exp(b_tot + gv_ref[2, r0:r0 + L, :] - m_new)
        if mode != "ctx":
            cum = gv_ref[0, r0:r0 + L, :]
            inter = cum + m_prev
            m_t = jnp.maximum(inter, cum + gv_ref[1, r0:r0 + L, :])
            c_all = cum - m_t
            si_all = jnp.exp(inter - m_t)
            fl_all = jnp.exp(-m_t)
            g_row = gr_ref[ck]
        for h in range(HEADS):
            lo, hi = h * DH, (h + 1) * DH
            ln = lane0 + h
            q = q_ref[r0:r0 + L, lo:hi]
            kb = k_ref[r0:r0 + L, lo:hi]
            kt = kt_ref[lo:hi, r0:r0 + L]
            v = v_ref[r0:r0 + L, lo:hi]
            dec_h = dec[:, ln:ln + 1]
            c_old = c_scr[h]
            n_old = n_scr[h]
            if mode != "ctx":
                si_c = si_all[:, ln:ln + 1]
                s = jnp.dot(q, kt, preferred_element_type=F32)
                arg = jnp.where(keep, c_all[:, ln:ln + 1] + g_row[h:h + 1, :], -jnp.inf)
                p = s * jnp.exp(arg)
                num = jnp.dot(p.astype(BF16), v, preferred_element_type=F32)
                num = num + si_c * jnp.dot(q, c_old.astype(BF16), preferred_element_type=F32)
                qn = q.astype(F32) * n_old
                den = jnp.sum(p[:, 0:128] + p[:, 128:256], axis=1, keepdims=True)
                den = den + si_c * jnp.sum(qn[:, 0:128] + qn[:, 128:256], axis=1, keepdims=True)
                h_scr[r0:r0 + L, lo:hi] = num / jnp.maximum(jnp.abs(den), fl_all[:, ln:ln + 1])
            ws_row = jnp.exp((b_tot[:, ln:ln + 1] - m_new[:, ln:ln + 1]) + gr_ref[ck][h:h + 1, :]).astype(BF16)
            kw_t = kt * ws_row
            upd = jnp.dot(kw_t, v, preferred_element_type=F32)
            n_upd = jnp.sum(kb.astype(F32) * ws[:, ln:ln + 1], axis=0, keepdims=True)
            c_scr[h] = dec_h * c_old + upd
            n_scr[h] = dec_h * n_old + n_upd
        m_scr[...] = jnp.broadcast_to(m_new, m_scr.shape)

    for ci in range(n_chunks):
        stage_b((n_chunks - 1 - ci) if reverse else ci)

    a_rows = min(ML_A_ROWS, n_tok)
    if mode == "ctx":
        c_out[...] = c_scr[...]
        n_out[...] = n_scr[...]
        m_out[...] = m_scr[...]
    elif mode == "fwd":
        o_ref[...] = h_scr[...].astype(o_ref.dtype)
    else:
        for sb in range(n_tok // a_rows):
            r0 = sb * a_rows
            ht = h_scr[r0:r0 + a_rows, :] + hf_ref[r0:r0 + a_rows, :].astype(F32)
            for h in range(HEADS):
                lo, hi = h * DH, (h + 1) * DH
                hh = ht[:, lo:hi]
                mu = jnp.mean(hh, axis=-1, keepdims=True)
                var = jnp.mean(jnp.square(hh - mu), axis=-1, keepdims=True)
                hn = (hh - mu) * lax.rsqrt(var + EPS)
                xc = xc_ref[r0:r0 + a_rows, lo:hi].astype(F32)
                h_scr[r0:r0 + a_rows, lo:hi] = hn * ng_ref[:, lo:hi] + sk_ref[:, lo:hi] * xc
        for g in range(GRID_W // BF16_ROWS):
            zg = jnp.concatenate(
                [h_scr[w * GRID_W + g * BF16_ROWS:w * GRID_W + (g + 1) * BF16_ROWS, :] for w in range(ML_GROUP_COLS)],
                axis=0).astype(BF16)
            og = jnp.dot(perm_ref[...], zg, preferred_element_type=F32).astype(o_ref.dtype)
            o_ref[g * BF16_ROWS:(g + 1) * BF16_ROWS] = og.reshape(BF16_ROWS, ML_GROUP_COLS, D)


def _ml_weight_specs():
    return [
        _const_spec((4, D)),
        _const_spec((1, D)),
        _const_spec((D, 8)),
        _const_spec((D, 8)),
        _const_spec((D, 8)),
        _const_spec((3 * HEADS, DH, 256)),
        _const_spec((1, 256)),
    ]


def _ml_prep_call(mlx, weights, *, latent, name):
    if latent:
        n_tiles, n_tok, seq = GRID_W // ML_GROUP_COLS, ML_TILE, SEQ
        in_specs = [
            pl.BlockSpec((None, GRID_W, ML_GROUP_COLS, D), lambda b, j: (b, 0, j, 0)),
            pl.BlockSpec((None, 2, ML_GROUP_COLS, D), lambda b, j: (b, GRID_W // 2 - 1, jnp.maximum(j - 1, 0), 0)),
            pl.BlockSpec((None, 1, ML_GROUP_COLS, D), lambda b, j: (b, 0, jnp.minimum(j + 1, n_tiles - 1), 0)),
            _const_spec((BF16_ROWS * ML_GROUP_COLS, BF16_ROWS * ML_GROUP_COLS)),
        ]
        args = [mlx, mlx, mlx, _swap_perm(BF16_ROWS, ML_GROUP_COLS)]
    else:
        n_tiles, n_tok, seq = 1, CTX, CTX
        in_specs = [pl.BlockSpec((None, CTX, D), lambda b, j: (b, 0, 0))]
        args = [mlx]
    n_chunks = n_tok // ML_CHUNK
    tok_spec = pl.BlockSpec((None, n_tok, D), lambda b, j: (b, j, 0))
    tok_shape = jax.ShapeDtypeStruct((NB, seq, D), BF16)
    row_spec = pl.BlockSpec((None, n_chunks, 8, ML_CHUNK), lambda b, j: (b, j, 0, 0))
    row_shape = jax.ShapeDtypeStruct((NB, seq // ML_CHUNK, 8, ML_CHUNK), F32)
    return pl.pallas_call(
        functools.partial(_ml_prep_kernel, n_tiles=n_tiles, n_tok=n_tok, latent=latent),
        grid=(NB, n_tiles),
        in_specs=in_specs + _ml_weight_specs(),
        out_specs=[tok_spec, tok_spec, pl.BlockSpec((None, D, n_tok), lambda b, j: (b, 0, j)), tok_spec, tok_spec,
                   pl.BlockSpec((None, 3, n_tok, 128), lambda b, j: (b, 0, j, 0)), row_spec, row_spec],
        out_shape=[tok_shape, tok_shape, jax.ShapeDtypeStruct((NB, D, seq), BF16), tok_shape, tok_shape,
                   jax.ShapeDtypeStruct((NB, 3, seq, 128), F32), row_shape, row_shape],
        scratch_shapes=[
            pltpu.VMEM((n_tok + 32, D), F32),
            pltpu.VMEM((HEADS, DH, 3 * DH), BF16),
            pltpu.VMEM((HEADS, DH, 2 * DH), BF16),
            pltpu.VMEM((HEADS, DH, DH), BF16),
        ],
        compiler_params=_params(("arbitrary", "arbitrary")),
        name=name,
    )(*args, *weights)


_STATE_SHAPES = [
    jax.ShapeDtypeStruct((NB, HEADS, DH, DH), F32),
    jax.ShapeDtypeStruct((NB, HEADS, 1, DH), F32),
    jax.ShapeDtypeStruct((NB, 8, 128), F32),
]


def _ml_rec_call(q, k, kt, v, gv, g_rows, *, reverse, state=None, merge_with=None, name):
    seq = q.shape[1]
    n_tok = min(ML_TILE, seq)
    n_tiles = seq // n_tok
    n_chunks = n_tok // ML_CHUNK
    tile = (lambda j: n_tiles - 1 - j) if reverse else (lambda j: j)
    tok_spec = pl.BlockSpec((None, n_tok, D), lambda b, j: (b, tile(j), 0))
    state_specs = [
        pl.BlockSpec((None, HEADS, DH, DH), lambda b, j: (b, 0, 0, 0)),
        pl.BlockSpec((None, HEADS, 1, DH), lambda b, j: (b, 0, 0, 0)),
        pl.BlockSpec((None, 8, 128), lambda b, j: (b, 0, 0)),
    ]
    in_specs = [
        tok_spec, tok_spec, pl.BlockSpec((None, D, n_tok), lambda b, j: (b, 0, tile(j))), tok_spec,
        pl.BlockSpec((None, 3, n_tok, 128), lambda b, j: (b, 0, tile(j), 0)),
        pl.BlockSpec((None, n_chunks, 8, ML_CHUNK), lambda b, j: (b, tile(j), 0, 0)),
    ]
    args = [q, k, kt, v, gv, g_rows]
    scratch = [
        pltpu.VMEM((HEADS, DH, DH), F32),
        pltpu.VMEM((HEADS, 1, DH), F32),
        pltpu.VMEM((8, 128), F32),
    ]
    if state is None:
        mode = "ctx"
        out_specs, out_shape = state_specs, _STATE_SHAPES
    else:
        in_specs += state_specs
        args += list(state)
        scratch.append(pltpu.VMEM((n_tok, D), F32))
        if merge_with is None:
            mode = "fwd"
            out_specs, out_shape = tok_spec, jax.ShapeDtypeStruct((NB, seq, D), BF16)
        else:
            mode = "rev"
            h_fwd, xc, norm_g, skip = merge_with
            in_specs += [tok_spec, tok_spec, _const_spec((1, D)), _const_spec((1, D)),
                         _const_spec((BF16_ROWS * ML_GROUP_COLS, BF16_ROWS * ML_GROUP_COLS))]
            args += [h_fwd, xc, norm_g, skip, _swap_perm(BF16_ROWS, ML_GROUP_COLS)]
            out_specs = pl.BlockSpec((None, GRID_W, ML_GROUP_COLS, D), lambda b, j: (b, 0, tile(j), 0))
            out_shape = jax.ShapeDtypeStruct((NB, GRID_W, GRID_W, D), BF16)
    return pl.pallas_call(
        functools.partial(_ml_rec_kernel, reverse=reverse, n_tok=n_tok, mode=mode),
        grid=(NB, n_tiles),
        in_specs=in_specs,
        out_specs=out_specs,
        out_shape=out_shape,
        scratch_shapes=scratch,
        compiler_params=_params(("arbitrary", "arbitrary")),
        name=name,
    )(*args)


def _final_kernel(x_ref, hrg_ref, grg_ref, hml_ref, smlo_ref, sgr_ref, sgm_ref, g1_ref, sh2_ref, sc2_ref, g2_ref,
                  n2_ref, nf_ref, wbr_ref, wbm_ref, wo_ref, wfi_ref, wfo_ref, o_ref, act_scr):
    dot = functools.partial(jnp.dot, preferred_element_type=F32)
    y_rg = (hrg_ref[...].astype(F32) * grg_ref[...].astype(F32)).astype(BF16)
    y_ml = (hml_ref[...].astype(F32) * smlo_ref[...].astype(F32)).astype(BF16)
    mix = sgr_ref[...].astype(F32) * dot(y_rg, wbr_ref[...])
    mix = mix + sgm_ref[...].astype(F32) * dot(y_ml, wbm_ref[...])
    x1 = x_ref[...] + g1_ref[...] * dot(mix.astype(BF16), wo_ref[...])
    ms = jnp.mean(x1 * x1, axis=-1, keepdims=True)
    hn = x1 * lax.rsqrt(ms + EPS) * n2_ref[...]
    hb = (hn * (1.0 + sc2_ref[...]) + sh2_ref[...]).astype(BF16)
    step = 256
    for c in range(D_FF // step):
        gate = dot(hb, wfi_ref[:, c * step:(c + 1) * step])
        up = dot(hb, wfi_ref[:, D_FF + c * step:D_FF + (c + 1) * step])
        act_scr[:, c * step:(c + 1) * step] = (_silu(gate) * up).astype(BF16)
    x2 = x1 + g2_ref[...] * dot(act_scr[...], wfo_ref[...])
    ms2 = jnp.mean(x2 * x2, axis=-1, keepdims=True)
    o_ref[...] = x2 * lax.rsqrt(ms2 + EPS) * nf_ref[...]


def _final_call(x, h_rg, grg, h_ml, smlo, sgr, sgm, mod3, norm2_g, final_g, wbr, wbm, wo, wfi, wfo):
    rows = FINAL_ROWS
    row_spec = pl.BlockSpec((None, rows, D), lambda b, i: (b, i, 0))
    mod_spec = lambda g: pl.BlockSpec((None, 1, D), lambda b, i: (b, 0, g))
    return pl.pallas_call(
        _final_kernel,
        grid=(NB, SEQ // rows),
        in_specs=[
            row_spec, row_spec, row_spec, row_spec, row_spec, row_spec, row_spec,
            mod_spec(2), mod_spec(3), mod_spec(4), mod_spec(5),
            _const_spec((1, D)), _const_spec((1, D)),
            _const_spec((D, D)), _const_spec((D, D)), _const_spec((D, D)),
            _const_spec((D, 2 * D_FF)), _const_spec((D_FF, D)),
        ],
        out_specs=row_spec,
        out_shape=jax.ShapeDtypeStruct((NB, SEQ, D), F32),
        scratch_shapes=[pltpu.VMEM((rows, D_FF), BF16)],
        compiler_params=_params(("arbitrary", "arbitrary")),
        name="final",
    )(x, h_rg, grg, h_ml, smlo, sgr, sgm, mod3, mod3, mod3, mod3, norm2_g, final_g, wbr, wbm, wo, wfi, wfo)


def _pair_blockdiag(w):
    w = w.reshape(8, 2, 64, 64)
    z = jnp.zeros((8, 64, 64), w.dtype)
    top = jnp.concatenate([w[:, 0], z], axis=2)
    bot = jnp.concatenate([z, w[:, 1]], axis=2)
    return jnp.concatenate([top, bot], axis=1)


def _rg_weights(wa, ba, wx, bx):
    w = jnp.concatenate([_pair_blockdiag(wa), _pair_blockdiag(wx)], axis=2).astype(BF16)
    bias = 0.5 * jnp.concatenate([ba.reshape(8, 1, 128), bx.reshape(8, 1, 128)], axis=2)
    b_hi = bias.astype(BF16)
    b_lo = (bias - b_hi.astype(F32)).astype(BF16)
    return (jnp.concatenate([w, b_hi, b_lo, jnp.zeros((8, 126, 256), BF16)], axis=1),)


def _block_diagonals(w):
    rows = [jnp.pad(w[:, i, :], ((0, 0), (3 - i, 1 + i))) for i in range(4)]
    return jnp.stack(rows, axis=1).reshape(D, 8)


def _gate_weights(wi, bi, wf, bf):
    def lanes(x):
        gap = [(0, 0)] * (x.ndim - 2)
        fwd = jnp.pad(x[0], gap + [(0, ML_REV_LANE - HEADS)])
        rev = jnp.pad(x[1], gap + [(0, 128 - ML_REV_LANE - HEADS)])
        return jnp.concatenate([fwd, rev], axis=-1)
    w = jnp.concatenate([lanes(wi), lanes(wf)], axis=-1)
    b = jnp.concatenate([lanes(bi), lanes(bf)], axis=-1).reshape(1, 256)
    return w.reshape(3 * HEADS, DH, 256).astype(BF16), b


def kernel(x, c, ctx, c_ctx, w_mod, b_mod, norm1_g, norm2_g, w_in, rg_conv_w, rg_conv_b, rg_wa, rg_ba, rg_wx,
           rg_bx, rg_lambda, ml_conv_w, ml_conv_b, ml_wq, ml_wk, ml_wv, ml_wi, ml_bi, ml_wf, ml_bf,
           ml_norm_g, ml_skip, w_branch_rg, w_branch_ml, w_out, w_ffn_in, w_ffn_out, final_norm_g):
    mod = _mod_call(c, c_ctx, w_mod[0], b_mod[0])
    mod3 = mod.reshape(2 * NB, 1, 6 * D)
    w_in_bf = w_in[0].astype(BF16)
    norm1 = norm1_g[0].reshape(1, D)

    rgx, grg, mlx, smlo, sgr, sgm = _proj_call(x, mod3, norm1, w_in_bf, ctx=False)
    rgx_c, mlx_c = _proj_call(ctx, mod3, norm1, w_in_bf, ctx=True)

    rg_cw = rg_conv_w[0]
    rg_cb = rg_conv_b[0].reshape(1, D)
    zero_h = jnp.zeros((NB, D), F32)
    rg_w = [_rg_weights(rg_wa[0, d], rg_ba[0, d], rg_wx[0, d], rg_bx[0, d]) + (rg_lambda[0, d].reshape(1, D),)
            for d in range(2)]
    _, h0_f = _rg_call(rgx_c, zero_h, rg_cw, rg_cb, *rg_w[0], reverse=False, mode="ctx", name="rg_ctx_fwd")
    _, h0_r = _rg_call(rgx_c, zero_h, rg_cw, rg_cb, *rg_w[1], reverse=True, mode="ctx", name="rg_ctx_rev")
    h_f, rg_xh, _ = _rg_call(rgx, h0_f, rg_cw, rg_cb, *rg_w[0], reverse=False, mode="fwd", name="rg_fwd")
    h_rg, _ = _rg_call(rg_xh, h0_r, rg_cw, rg_cb, *rg_w[1], reverse=True, mode="rev", h_fwd=h_f, name="rg_rev")

    ml_cw = ml_conv_w[0]
    ml_cb = ml_conv_b[0].reshape(1, D)
    ml_qkv = (_block_diagonals(ml_wq[0]), _block_diagonals(ml_wk[0]), _block_diagonals(ml_wv[0]))
    ml_w = (ml_cw, ml_cb) + ml_qkv + _gate_weights(ml_wi[0], ml_bi[0], ml_wf[0], ml_bf[0])
    q_c, k_c, kt_c, v_c, _, gv_c, grf_c, grr_c = _ml_prep_call(mlx_c, ml_w, latent=False, name="ml_prep_ctx")
    st_f = _ml_rec_call(q_c, k_c, kt_c, v_c, gv_c, grf_c, reverse=False, name="ml_ctx_fwd")
    st_r = _ml_rec_call(q_c, k_c, kt_c, v_c, gv_c, grr_c, reverse=True, name="ml_ctx_rev")
    q_l, k_l, kt_l, v_l, xc_l, gv_l, grf_l, grr_l = _ml_prep_call(
        mlx.reshape(NB, GRID_W, GRID_W, D), ml_w, latent=True, name="ml_prep")
    hm_f = _ml_rec_call(q_l, k_l, kt_l, v_l, gv_l, grf_l, reverse=False, state=st_f, name="ml_fwd")
    h_ml = _ml_rec_call(q_l, k_l, kt_l, v_l, gv_l, grr_l, reverse=True, state=st_r,
                        merge_with=(hm_f, xc_l, ml_norm_g[0].reshape(1, D), ml_skip[0].reshape(1, D)), name="ml_rev")

    return _final_call(
        x, h_rg, grg, h_ml.reshape(NB, SEQ, D), smlo, sgr, sgm, mod3,
        norm2_g[0].reshape(1, D), final_norm_g.reshape(1, D),
        w_branch_rg[0].astype(BF16), w_branch_ml[0].astype(BF16), w_out[0].astype(BF16),
        w_ffn_in[0].astype(BF16), w_ffn_out[0].astype(BF16))
```

```python
import functools

import jax
import jax.numpy as jnp
from jax import lax
from jax.experimental import pallas as pl
from jax.experimental.pallas import tpu as pltpu

F32 = jnp.float32
BF16 = jnp.bfloat16

D = 1024
NB = 8
SEQ = 4096
GRID_W = 64
CTX = 256
EPS = 1e-6
RG_C = 8.0
HEADS = 4
DH = D // HEADS
D_FF = 2816
N_IN = 6 * D
LOG2E = 1.4426950408889634

VMEM_LIMIT = 60 * 1024 * 1024
BF16_ROWS = 16

PROJ_ROWS = 1024
FINAL_ROWS = 512
RG_STEPS = 128
RG_SUB = 256
RG_SUB_STEPS = RG_SUB // NB
ML_CHUNK = 256
ML_GROUP_COLS = 16
ML_TILE = ML_GROUP_COLS * GRID_W
ML_A_ROWS = 512

assert ML_CHUNK == DH


def _sigmoid(x):
    return 0.5 * (jnp.tanh(0.5 * x) + 1.0)


def _silu(x):
    return x * _sigmoid(x)


def _softplus(x):
    return jnp.maximum(x, 0.0) + jnp.log(1.0 + jnp.exp(-jnp.abs(x)))


def _log_sigmoid(x):
    return jnp.minimum(x, 0.0) - jnp.log(1.0 + jnp.exp(-jnp.abs(x)))


def _split3(x):
    hi = x.astype(BF16)
    r1 = x - hi.astype(F32)
    mid = r1.astype(BF16)
    lo = (r1 - mid.astype(F32)).astype(BF16)
    return hi, mid, lo


def _params(sem):
    return pltpu.CompilerParams(dimension_semantics=sem, vmem_limit_bytes=VMEM_LIMIT)


def _const_spec(shape):
    nd = len(shape)
    return pl.BlockSpec(shape, lambda *_: (0,) * nd, pipeline_mode=pl.Buffered(1))


def _swap_perm(a, b):
    n = a * b
    out_row = jnp.arange(n)
    src = (out_row % a) * b + out_row // a
    return (src[:, None] == jnp.arange(n)[None, :]).astype(BF16)


def _mod_kernel(c_ref, cc_ref, w_ref, b_ref, o_ref):
    s = jnp.concatenate([c_ref[...], jnp.broadcast_to(cc_ref[...], (NB, D))], axis=0)
    s = _silu(s)
    s_hi, s_mid, _ = _split3(s)
    w_hi, w_mid, _ = _split3(w_ref[...])
    dot = functools.partial(jnp.dot, preferred_element_type=F32)
    o_ref[...] = dot(s_hi, w_hi) + dot(s_mid, w_hi) + dot(s_hi, w_mid) + b_ref[...]


def _mod_call(c, c_ctx, w_mod, b_mod):
    return pl.pallas_call(
        _mod_kernel,
        grid=(6,),
        in_specs=[
            pl.BlockSpec((NB, D), lambda g: (0, 0)),
            pl.BlockSpec((1, D), lambda g: (0, 0)),
            pl.BlockSpec((D, D), lambda g: (0, g)),
            pl.BlockSpec((1, D), lambda g: (0, g)),
        ],
        out_specs=pl.BlockSpec((2 * NB, D), lambda g: (0, g)),
        out_shape=jax.ShapeDtypeStruct((2 * NB, 6 * D), F32),
        compiler_params=_params(("arbitrary",)),
        name="mod",
    )(c, c_ctx.reshape(1, D), w_mod, b_mod.reshape(1, 6 * D))


def _gelu_tanh(x):
    return jax.nn.gelu(x, approximate=True)


def _identity(x):
    return x


_PROJ_FULL = ((0, _identity), (1, _gelu_tanh), (2, _identity), (3, _sigmoid), (4, _sigmoid), (5, _sigmoid))
_PROJ_CTX = ((0, _identity), (2, _identity))


def _proj_kernel(x_ref, sh_ref, sc_ref, g_ref, w_ref, *o_refs, groups):
    x = x_ref[...]
    ms = jnp.mean(x * x, axis=-1, keepdims=True)
    y = x * lax.rsqrt(ms + EPS) * g_ref[...]
    u = (y * (1.0 + sc_ref[...]) + sh_ref[...]).astype(BF16)
    for o_ref, (g, act) in zip(o_refs, groups):
        p = jnp.dot(u, w_ref[:, g * D:(g + 1) * D], preferred_element_type=F32)
        o_ref[...] = act(p).astype(o_ref.dtype)


def _proj_call(x, mod3, norm_g, w_in_bf, *, ctx):
    L = x.shape[1]
    rows = min(PROJ_ROWS, L)
    groups = _PROJ_CTX if ctx else _PROJ_FULL
    mod_row = (lambda b: NB) if ctx else (lambda b: b)
    row_spec = pl.BlockSpec((None, rows, D), lambda b, i: (b, i, 0))
    return pl.pallas_call(
        functools.partial(_proj_kernel, groups=groups),
        grid=(NB, L // rows),
        in_specs=[
            row_spec,
            pl.BlockSpec((None, 1, D), lambda b, i: (mod_row(b), 0, 0)),
            pl.BlockSpec((None, 1, D), lambda b, i: (mod_row(b), 0, 1)),
            _const_spec((1, D)),
            _const_spec((D, N_IN)),
        ],
        out_specs=[row_spec] * len(groups),
        out_shape=[jax.ShapeDtypeStruct((NB, L, D), BF16)] * len(groups),
        compiler_params=_params(("arbitrary", "arbitrary")),
        name="proj_ctx" if ctx else "proj",
    )(x, mod3, mod3, norm_g, w_in_bf)


def _rg_kernel(*refs, reverse, n_tiles, mode):
    if mode == "rev":
        (xh_ref, h0_ref, wbd_ref, lam_ref, hf_ref, pout_ref, o_ref, hlast_ref, a_scr, b_scr, h_scr) = refs
    elif mode == "fwd":
        (x_ref, xp_ref, xn_ref, h0_ref, cw_ref, cb_ref, wbd_ref, lam_ref, pin_ref, phalo_ref,
         o_ref, xh_out, hlast_ref, xe_scr, a_scr, b_scr, h_scr) = refs
    else:
        (x_ref, xp_ref, xn_ref, h0_ref, cw_ref, cb_ref, wbd_ref, lam_ref, pin_ref, phalo_ref,
         o_ref, hlast_ref, xe_scr, a_scr, b_scr, h_scr) = refs
    i = pl.program_id(0)
    tile = (n_tiles - 1 - i) if reverse else i
    rows = RG_STEPS * NB
    halo = 2 * NB

    @pl.when(i == 0)
    def _():
        h_scr[...] = h0_ref[...]

    if mode != "rev":
        def halo_rows(ref):
            xb = jnp.concatenate([ref[b] for b in range(NB)], axis=0)
            return jnp.dot(phalo_ref[...], xb, preferred_element_type=F32)

        xe_scr[0:halo, :] = jnp.where(tile > 0, halo_rows(xp_ref)[(BF16_ROWS - 2) * NB:BF16_ROWS * NB], 0.0)
        xe_scr[halo + rows:2 * halo + rows, :] = jnp.where(tile < n_tiles - 1, halo_rows(xn_ref)[0:halo], 0.0)

        def fill_block(s, carry):
            t0 = pl.multiple_of(s * RG_SUB_STEPS, RG_SUB_STEPS)
            xb = jnp.concatenate([x_ref[b, pl.ds(t0, RG_SUB_STEPS), :] for b in range(NB)], axis=0)
            r0 = pl.multiple_of(s * RG_SUB, RG_SUB)
            xe_scr[pl.ds(halo + r0, RG_SUB), :] = jnp.dot(pin_ref[...], xb, preferred_element_type=F32)
            return carry

        lax.fori_loop(0, rows // RG_SUB, fill_block, 0)
        cw = 0.5 * cw_ref[...]
        cb = 0.5 * cb_ref[...]

    lam2 = (-0.5 * RG_C * LOG2E) * _softplus(-lam_ref[...])

    ones = jnp.ones((RG_SUB, 128), BF16)

    def coef_block(s, carry):
        r0 = pl.multiple_of(s * RG_SUB, RG_SUB)
        if mode == "rev":
            xh = xh_ref[pl.ds(r0, RG_SUB), :].astype(F32)
        else:
            xh = cb + cw[0:1] * xe_scr[pl.ds(r0, RG_SUB), :]
            xh = xh + cw[1:2] * xe_scr[pl.ds(r0 + NB, RG_SUB), :]
            xh = xh + cw[2:3] * xe_scr[pl.ds(r0 + 2 * NB, RG_SUB), :]
            xh = xh + cw[3:4] * xe_scr[pl.ds(r0 + 3 * NB, RG_SUB), :]
        if mode == "fwd":
            xh_out[pl.ds(r0, RG_SUB), :] = xh.astype(BF16)
        for p in range(D // 128):
            lo, hi = p * 128, (p + 1) * 128
            xhp = xh[:, lo:hi]
            lhs = jnp.concatenate([xhp.astype(BF16), ones], axis=1)
            pre = jnp.dot(lhs, wbd_ref[p], preferred_element_type=F32)
            t_r = jnp.tanh(pre[:, 0:128])
            t_i = jnp.tanh(pre[:, 128:256])
            a = jnp.exp2(lam2[:, lo:hi] * (t_r + 1.0))
            y = (1.0 - a) * (1.0 + a)
            gain = jnp.where(y > 0.0, y * lax.rsqrt(y), 0.0)
            a_scr[pl.ds(r0, RG_SUB), lo:hi] = a
            b_scr[pl.ds(r0, RG_SUB), lo:hi] = gain * ((t_i + 1.0) * xhp)
        return carry

    lax.fori_loop(0, rows // RG_SUB, coef_block, 0)

    def scan_step(t, h):
        tt = (RG_STEPS - 1 - t) if reverse else t
        r0 = pl.multiple_of(tt * NB, NB)
        h = a_scr[pl.ds(r0, NB), :] * h + b_scr[pl.ds(r0, NB), :]
        b_scr[pl.ds(r0, NB), :] = h
        return h

    h_last = lax.fori_loop(0, RG_STEPS, scan_step, h_scr[...], unroll=8)
    h_scr[...] = h_last
    hlast_ref[...] = h_last

    if mode == "rev":
        def out_block(s, carry):
            r0 = pl.multiple_of(s * RG_SUB, RG_SUB)
            t0 = pl.multiple_of(s * RG_SUB_STEPS, RG_SUB_STEPS)
            h = (b_scr[pl.ds(r0, RG_SUB), :] + hf_ref[pl.ds(r0, RG_SUB), :].astype(F32)).astype(BF16)
            hb = jnp.dot(pout_ref[...], h, preferred_element_type=F32).astype(o_ref.dtype)
            for b in range(NB):
                o_ref[b, pl.ds(t0, RG_SUB_STEPS), :] = hb[b * RG_SUB_STEPS:(b + 1) * RG_SUB_STEPS]
            return carry
        lax.fori_loop(0, rows // RG_SUB, out_block, 0)
    else:
        o_ref[...] = b_scr[...].astype(o_ref.dtype)


def _rg_call(src, h0, cw, cb, wbd, lam, *, reverse, mode, h_fwd=None, name):
    L = src.shape[0] // NB if mode == "rev" else src.shape[1]
    rows = RG_STEPS * NB
    n_tiles = L // RG_STEPS
    per = RG_STEPS // BF16_ROWS
    n_halo = L // BF16_ROWS
    tile = (lambda i: n_tiles - 1 - i) if reverse else (lambda i: i)
    bm_spec = pl.BlockSpec((NB, RG_STEPS, D), lambda i: (0, tile(i), 0))
    tm_spec = pl.BlockSpec((rows, D), lambda i: (tile(i), 0))
    tm_shape = jax.ShapeDtypeStruct((L * NB, D), BF16)
    state_spec = pl.BlockSpec((NB, D), lambda i: (0, 0))
    state_shape = jax.ShapeDtypeStruct((NB, D), F32)
    scratch = [pltpu.VMEM((rows, D), F32), pltpu.VMEM((rows, D), F32), pltpu.VMEM((NB, D), F32)]
    if mode == "rev":
        in_specs = [tm_spec, _const_spec((NB, D)), _const_spec((D // 128, 256, 256)), _const_spec((1, D)),
                    tm_spec, _const_spec((RG_SUB, RG_SUB))]
        args = [src, h0, wbd, lam, h_fwd, _swap_perm(RG_SUB_STEPS, NB)]
        out_specs = [bm_spec, state_spec]
        out_shape = [jax.ShapeDtypeStruct((NB, L, D), BF16), state_shape]
    else:
        in_specs = [
            bm_spec,
            pl.BlockSpec((NB, BF16_ROWS, D), lambda i: (0, jnp.maximum(tile(i) * per - 1, 0), 0)),
            pl.BlockSpec((NB, BF16_ROWS, D), lambda i: (0, jnp.minimum((tile(i) + 1) * per, n_halo - 1), 0)),
            _const_spec((NB, D)),
            _const_spec((4, D)),
            _const_spec((1, D)),
            _const_spec((D // 128, 256, 256)),
            _const_spec((1, D)),
            _const_spec((RG_SUB, RG_SUB)),
            _const_spec((NB * BF16_ROWS, NB * BF16_ROWS)),
        ]
        args = [src, src, src, h0, cw, cb, wbd, lam, _swap_perm(NB, RG_SUB_STEPS), _swap_perm(NB, BF16_ROWS)]
        n_tm = 2 if mode == "fwd" else 1
        out_specs = [tm_spec] * n_tm + [state_spec]
        out_shape = [tm_shape] * n_tm + [state_shape]
        scratch = [pltpu.VMEM((rows + 4 * NB, D), F32)] + scratch
    return pl.pallas_call(
        functools.partial(_rg_kernel, reverse=reverse, n_tiles=n_tiles, mode=mode),
        grid=(n_tiles,),
        in_specs=in_specs,
        out_specs=out_specs,
        out_shape=out_shape,
        scratch_shapes=scratch,
        compiler_params=_params(("arbitrary",)),
        name=name,
    )(*args)


ML_REV_LANE = 8


def _scan_lanes_both(x, op, fill):
    n = x.shape[1]
    lane = lax.broadcasted_iota(jnp.int32, x.shape, 1)
    is_prefix = lax.broadcasted_iota(jnp.int32, x.shape, 0) < ML_REV_LANE
    sh = 1
    while sh < n:
        before = jnp.where(lane >= sh, pltpu.roll(x, sh, axis=1), fill)
        after = jnp.where(lane < n - sh, pltpu.roll(x, n - sh, axis=1), fill)
        x = op(x, jnp.where(is_prefix, before, after))
        sh *= 2
    return x


def _ml_prep_kernel(*refs, n_tiles, n_tok, latent):
    L = ML_CHUNK
    n_chunks = n_tok // L
    if latent:
        (x_ref, xp_ref, xn_ref, perm_ref, cw_ref, cb_ref, dq_ref, dk_ref, dv_ref, wg_ref, bg_ref,
         q_ref, k_ref, kt_ref, v_ref, xc_ref, gv_ref, grf_ref, grr_ref,
         xt_scr, wqk_scr, wv_scr, wkt_scr, g_scr) = refs
        step = pl.program_id(0)
        tile = jnp.minimum(step, NB * n_tiles - 1) % n_tiles
        first = step == 0
        slot_new = step % 2
        slot_old = 1 - slot_new

        @pl.when(first)
        def _():
            g_scr[1] = jnp.zeros(g_scr.shape[1:], F32)
    else:
        (x_ref, cw_ref, cb_ref, dq_ref, dk_ref, dv_ref, wg_ref, bg_ref,
         q_ref, k_ref, kt_ref, v_ref, xc_ref, gv_ref, grf_ref, grr_ref, xt_scr, wqk_scr, wv_scr, wkt_scr) = refs
        first = pl.program_id(0) == 0

    src_t = lax.broadcasted_iota(jnp.int32, (L, L), 0)
    dst_t = lax.broadcasted_iota(jnp.int32, (L, L), 1)
    tri_fwd = (src_t <= dst_t).astype(BF16)
    tri_rev = (src_t >= dst_t).astype(BF16)

    def gate_vectors(gates, ck):
        r0 = ck * L
        li_rows = gates[:, 0:128].T[0:2 * ML_REV_LANE]
        lf_rows = _log_sigmoid(gates[:, 128:256].T[0:2 * ML_REV_LANE])
        split = jnp.concatenate(_split3(lf_rows), axis=0)
        to_t = jnp.dot(split, tri_fwd, preferred_element_type=F32)
        from_t = jnp.dot(split, tri_rev, preferred_element_type=F32)
        cum_rows = jnp.concatenate([
            to_t[0:8] + to_t[BF16_ROWS:BF16_ROWS + 8] + to_t[2 * BF16_ROWS:2 * BF16_ROWS + 8],
            from_t[8:16] + from_t[BF16_ROWS + 8:2 * BF16_ROWS] + from_t[2 * BF16_ROWS + 8:3 * BF16_ROWS]], axis=0)
        g_rows = li_rows - cum_rows
        grf_ref[ck] = g_rows[0:ML_REV_LANE]
        grr_ref[ck] = g_rows[ML_REV_LANE:2 * ML_REV_LANE]
        pm_rows = _scan_lanes_both(g_rows, jnp.maximum, -jnp.inf)
        pad = jnp.zeros((128 - 2 * ML_REV_LANE, L), F32)
        gv_ref[0, r0:r0 + L, :] = jnp.concatenate([cum_rows, pad], axis=0).T
        gv_ref[1, r0:r0 + L, :] = jnp.concatenate([pm_rows, pad], axis=0).T
        gv_ref[2, r0:r0 + L, :] = jnp.concatenate([g_rows, pad], axis=0).T

    @pl.when(first)
    def _():
        diff = lax.broadcasted_iota(jnp.int32, (DH, DH), 1) - lax.broadcasted_iota(jnp.int32, (DH, DH), 0)

        def dense(d_ref, h):
            diag = d_ref[h * DH:(h + 1) * DH, :]
            out = jnp.zeros((DH, DH), F32)
            for d in range(-3, 4):
                out = jnp.where(diff == d, diag[:, 3 + d:4 + d], out)
            return out

        dot = functools.partial(jnp.dot, preferred_element_type=F32)
        for h in range(HEADS):
            wq = dense(dq_ref, h).astype(BF16)
            wk_f32 = dense(dk_ref, h) * (DH ** -0.5)
            wk = wk_f32.astype(BF16)
            wv = dense(dv_ref, h).astype(BF16)
            wkt_scr[h] = wk_f32.T.astype(BF16)
            wqk_scr[h, :, 0:DH] = wq
            wqk_scr[h, :, DH:2 * DH] = wk
            wqk_scr[h, :, 2 * DH:3 * DH] = (dot(wq, wg_ref[h]) + dot(wk, wg_ref[HEADS + h])).astype(BF16)
            wv_scr[h, :, 0:DH] = wv
            wv_scr[h, :, DH:2 * DH] = dot(wv, wg_ref[2 * HEADS + h]).astype(BF16)

    if not latent:
        xt_scr[8:16, :] = jnp.zeros((8, D), F32)
        xt_scr[16:16 + n_tok, :] = x_ref[...].astype(F32)
        xt_scr[16 + n_tok:24 + n_tok, :] = jnp.zeros((8, D), F32)
    else:
        for g in range(GRID_W // BF16_ROWS):
            xg = x_ref[g * BF16_ROWS:(g + 1) * BF16_ROWS].reshape(BF16_ROWS * ML_GROUP_COLS, D)
            yg = jnp.dot(perm_ref[...], xg, preferred_element_type=F32)
            for w in range(ML_GROUP_COLS):
                dst = 16 + w * GRID_W + g * BF16_ROWS
                xt_scr[dst:dst + BF16_ROWS, :] = yg[w * BF16_ROWS:(w + 1) * BF16_ROWS]
        last = ML_GROUP_COLS - 1
        prev = jnp.concatenate([xp_ref[0].astype(F32)[last:last + 1], xp_ref[1].astype(F32)[last:last + 1]], axis=0)
        xt_scr[14:16, :] = jnp.where(tile > 0, prev, 0.0)
        xt_scr[16 + n_tok:17 + n_tok, :] = jnp.where(tile < n_tiles - 1, xn_ref[0].astype(F32)[0:1], 0.0)

    if latent:
        for ck in range(n_chunks):
            gate_vectors(g_scr[slot_old, ck * L:(ck + 1) * L, :], ck)

    cw = cw_ref[...]
    for ck in range(n_chunks):
        r0 = ck * L
        xt = xt_scr[16 + r0:16 + r0 + L, :]
        xc = cb_ref[...] + cw[0:1] * xt_scr[14 + r0:14 + r0 + L, :] + cw[1:2] * xt_scr[15 + r0:15 + r0 + L, :]
        xc = xc + cw[2:3] * xt + cw[3:4] * xt_scr[17 + r0:17 + r0 + L, :]
        xc = _silu(xc)
        xc_ref[r0:r0 + L, :] = xc.astype(BF16)
        gates = jnp.zeros((L, 256), F32) + bg_ref[...]
        for h in range(HEADS):
            lo, hi = h * DH, (h + 1) * DH
            xc_h = xc[:, lo:hi].astype(BF16)
            qkg = jnp.dot(xc_h, wqk_scr[h], preferred_element_type=F32)
            vg = jnp.dot(xt[:, lo:hi].astype(BF16), wv_scr[h], preferred_element_type=F32)
            gates = gates + qkg[:, 2 * DH:3 * DH] + vg[:, DH:2 * DH]
            q_ref[r0:r0 + L, lo:hi] = qkg[:, 0:DH].astype(BF16)
            k_ref[r0:r0 + L, lo:hi] = qkg[:, DH:2 * DH].astype(BF16)
            kt = lax.dot_general(wkt_scr[h], xc_h, (((1,), (1,)), ((), ())), preferred_element_type=F32)
            kt_ref[lo:hi, r0:r0 + L] = kt.astype(BF16)
            v_ref[r0:r0 + L, lo:hi] = vg[:, 0:DH].astype(BF16)
        if latent:
            g_scr[slot_new, r0:r0 + L, :] = gates
        else:
            gate_vectors(gates, ck)


def _ml_rec_kernel(*refs, reverse, n_tok, mode):
    L = ML_CHUNK
    n_chunks = n_tok // L
    lane0 = ML_REV_LANE if reverse else 0
    if mode == "ctx":
        (q_ref, k_ref, kt_ref, v_ref, gv_ref, gr_ref, c_out, n_out, m_out, c_scr, n_scr, m_scr) = refs
    elif mode == "fwd":
        (q_ref, k_ref, kt_ref, v_ref, gv_ref, gr_ref, c0_ref, n0_ref, m0_ref, o_ref,
         c_scr, n_scr, m_scr, h_scr) = refs
    else:
        (q_ref, k_ref, kt_ref, v_ref, gv_ref, gr_ref, c0_ref, n0_ref, m0_ref, hf_ref, xc_ref, ng_ref, sk_ref,
         perm_ref, o_ref, c_scr, n_scr, m_scr, h_scr) = refs

    @pl.when(pl.program_id(1) == 0)
    def _():
        if mode == "ctx":
            c_scr[...] = jnp.zeros_like(c_scr)
            n_scr[...] = jnp.zeros_like(n_scr)
            m_scr[...] = jnp.zeros_like(m_scr)
        else:
            c_scr[...] = c0_ref[...]
            n_scr[...] = n0_ref[...]
            m_scr[...] = m0_ref[...]

    if mode != "ctx":
        row_id = lax.broadcasted_iota(jnp.int32, (L, L), 0)
        col_id = lax.broadcasted_iota(jnp.int32, (L, L), 1)
        keep = (col_id >= row_id) if reverse else (col_id <= row_id)

    def stage_b(ck):
        r0 = ck * L
        m_prev = m_scr[0:1, :]
        edge = 0 if reverse else L - 1
        b_tot = gv_ref[0, r0 + edge:r0 + edge + 1, :]
        gmax = gv_ref[1, r0 + edge:r0 + edge + 1, :]
        m_new = jnp.maximum(b_tot + m_prev, b_tot + gmax)
        dec = jnp.exp(b_tot + m_prev - m_new)
        ws = jnp.exp(b_tot + gv_ref[2, r0:r0 + L, :] - m_new)
        if mode != "ctx":
            cum = gv_ref[0, r0:r0 + L, :]
            inter = cum + m_prev
            m_t = jnp.maximum(inter, cum + gv_ref[1, r0:r0 + L, :])
            c_all = cum - m_t
            si_all = jnp.exp(inter - m_t)
            fl_all = jnp.exp(-m_t)
            g_row = gr_ref[ck]
        for h in range(HEADS):
            lo, hi = h * DH, (h + 1) * DH
            ln = lane0 + h
            q = q_ref[r0:r0 + L, lo:hi]
            kb = k_ref[r0:r0 + L, lo:hi]
            kt = kt_ref[lo:hi, r0:r0 + L]
            v = v_ref[r0:r0 + L, lo:hi]
            dec_h = dec[:, ln:ln + 1]
            c_old = c_scr[h]
            n_old = n_scr[h]
            if mode != "ctx":
                si_c = si_all[:, ln:ln + 1]
                s = jnp.dot(q, kt, preferred_element_type=F32)
                arg = jnp.where(keep, c_all[:, ln:ln + 1] + g_row[h:h + 1, :], -jnp.inf)
                p = s * jnp.exp(arg)
                num = jnp.dot(p.astype(BF16), v, preferred_element_type=F32)
                num = num + si_c * jnp.dot(q, c_old.astype(BF16), preferred_element_type=F32)
                qn = q.astype(F32) * n_old
                den = jnp.sum(p[:, 0:128] + p[:, 128:256], axis=1, keepdims=True)
                den = den + si_c * jnp.sum(qn[:, 0:128] + qn[:, 128:256], axis=1, keepdims=True)
                h_scr[r0:r0 + L, lo:hi] = num / jnp.maximum(jnp.abs(den), fl_all[:, ln:ln + 1])
            ws_row = jnp.exp((b_tot[:, ln:ln + 1] - m_new[:, ln:ln + 1]) + gr_ref[ck][h:h + 1, :]).astype(BF16)
            kw_t = kt * ws_row
            upd = jnp.dot(kw_t, v, preferred_element_type=F32)
            n_upd = jnp.sum(kb.astype(F32) * ws[:, ln:ln + 1], axis=0, keepdims=True)
            c_scr[h] = dec_h * c_old + upd
            n_scr[h] = dec_h * n_old + n_upd
        m_scr[...] = jnp.broadcast_to(m_new, m_scr.shape)

    for ci in range(n_chunks):
        stage_b((n_chunks - 1 - ci) if reverse else ci)

    a_rows = min(ML_A_ROWS, n_tok)
    if mode == "ctx":
        c_out[...] = c_scr[...]
        n_out[...] = n_scr[...]
        m_out[...] = m_scr[...]
    elif mode == "fwd":
        o_ref[...] = h_scr[...].astype(o_ref.dtype)
    else:
        for sb in range(n_tok // a_rows):
            r0 = sb * a_rows
            ht = h_scr[r0:r0 + a_rows, :] + hf_ref[r0:r0 + a_rows, :].astype(F32)
            for h in range(HEADS):
                lo, hi = h * DH, (h + 1) * DH
                hh = ht[:, lo:hi]
                mu = jnp.mean(hh, axis=-1, keepdims=True)
                var = jnp.mean(jnp.square(hh - mu), axis=-1, keepdims=True)
                hn = (hh - mu) * lax.rsqrt(var + EPS)
                xc = xc_ref[r0:r0 + a_rows, lo:hi].astype(F32)
                h_scr[r0:r0 + a_rows, lo:hi] = hn * ng_ref[:, lo:hi] + sk_ref[:, lo:hi] * xc
        for g in range(GRID_W // BF16_ROWS):
            zg = jnp.concatenate(
                [h_scr[w * GRID_W + g * BF16_ROWS:w * GRID_W + (g + 1) * BF16_ROWS, :] for w in range(ML_GROUP_COLS)],
                axis=0).astype(BF16)
            og = jnp.dot(perm_ref[...], zg, preferred_element_type=F32).astype(o_ref.dtype)
            o_ref[g * BF16_ROWS:(g + 1) * BF16_ROWS] = og.reshape(BF16_ROWS, ML_GROUP_COLS, D)


def _ml_weight_specs():
    return [
        _const_spec((4, D)),
        _const_spec((1, D)),
        _const_spec((D, 8)),
        _const_spec((D, 8)),
        _const_spec((D, 8)),
        _const_spec((3 * HEADS, DH, 256)),
        _const_spec((1, 256)),
    ]


def _ml_prep_call(mlx, weights, *, latent, name):
    scratch = [
        pltpu.VMEM(((ML_TILE if latent else CTX) + 32, D), F32),
        pltpu.VMEM((HEADS, DH, 3 * DH), BF16),
        pltpu.VMEM((HEADS, DH, 2 * DH), BF16),
        pltpu.VMEM((HEADS, DH, DH), BF16),
    ]
    if latent:
        n_tiles, n_tok, seq = GRID_W // ML_GROUP_COLS, ML_TILE, SEQ
        n_flat = NB * n_tiles
        grid = (n_flat + 1,)
        sem = ("arbitrary",)

        def at(lag):
            def split(step):
                flat = jnp.clip(step - lag, 0, n_flat - 1)
                return flat // n_tiles, flat % n_tiles
            return split

        new, old = at(0), at(1)
        in_specs = [
            pl.BlockSpec((None, GRID_W, ML_GROUP_COLS, D), lambda g: (new(g)[0], 0, new(g)[1], 0)),
            pl.BlockSpec((None, 2, ML_GROUP_COLS, D),
                         lambda g: (new(g)[0], GRID_W // 2 - 1, jnp.maximum(new(g)[1] - 1, 0), 0)),
            pl.BlockSpec((None, 1, ML_GROUP_COLS, D),
                         lambda g: (new(g)[0], 0, jnp.minimum(new(g)[1] + 1, n_tiles - 1), 0)),
            _const_spec((BF16_ROWS * ML_GROUP_COLS, BF16_ROWS * ML_GROUP_COLS)),
        ]
        args = [mlx, mlx, mlx, _swap_perm(BF16_ROWS, ML_GROUP_COLS)]
        scratch.append(pltpu.VMEM((2, n_tok, 256), F32))
        tok_spec = pl.BlockSpec((None, n_tok, D), lambda g: (new(g)[0], new(g)[1], 0))
        kt_spec = pl.BlockSpec((None, D, n_tok), lambda g: (new(g)[0], 0, new(g)[1]))
        gv_spec = pl.BlockSpec((None, 3, n_tok, 128), lambda g: (old(g)[0], 0, old(g)[1], 0))
        row_spec = pl.BlockSpec((None, n_tok // ML_CHUNK, 8, ML_CHUNK), lambda g: (old(g)[0], old(g)[1], 0, 0))
    else:
        n_tiles, n_tok, seq = 1, CTX, CTX
        grid = (NB, 1)
        sem = ("arbitrary", "arbitrary")
        in_specs = [pl.BlockSpec((None, CTX, D), lambda b, j: (b, 0, 0))]
        args = [mlx]
        tok_spec = pl.BlockSpec((None, n_tok, D), lambda b, j: (b, 0, 0))
        kt_spec = pl.BlockSpec((None, D, n_tok), lambda b, j: (b, 0, 0))
        gv_spec = pl.BlockSpec((None, 3, n_tok, 128), lambda b, j: (b, 0, 0, 0))
        row_spec = pl.BlockSpec((None, n_tok // ML_CHUNK, 8, ML_CHUNK), lambda b, j: (b, 0, 0, 0))
    tok_shape = jax.ShapeDtypeStruct((NB, seq, D), BF16)
    row_shape = jax.ShapeDtypeStruct((NB, seq // ML_CHUNK, 8, ML_CHUNK), F32)
    return pl.pallas_call(
        functools.partial(_ml_prep_kernel, n_tiles=n_tiles, n_tok=n_tok, latent=latent),
        grid=grid,
        in_specs=in_specs + _ml_weight_specs(),
        out_specs=[tok_spec, tok_spec, kt_spec, tok_spec, tok_spec, gv_spec, row_spec, row_spec],
        out_shape=[tok_shape, tok_shape, jax.ShapeDtypeStruct((NB, D, seq), BF16), tok_shape, tok_shape,
                   jax.ShapeDtypeStruct((NB, 3, seq, 128), F32), row_shape, row_shape],
        scratch_shapes=scratch,
        compiler_params=_params(sem),
        name=name,
    )(*args, *weights)


_STATE_SHAPES = [
    jax.ShapeDtypeStruct((NB, HEADS, DH, DH), F32),
    jax.ShapeDtypeStruct((NB, HEADS, 1, DH), F32),
    jax.ShapeDtypeStruct((NB, 8, 128), F32),
]


def _ml_rec_call(q, k, kt, v, gv, g_rows, *, reverse, state=None, merge_with=None, name):
    seq = q.shape[1]
    n_tok = min(ML_TILE, seq)
    n_tiles = seq // n_tok
    n_chunks = n_tok // ML_CHUNK
    tile = (lambda j: n_tiles - 1 - j) if reverse else (lambda j: j)
    tok_spec = pl.BlockSpec((None, n_tok, D), lambda b, j: (b, tile(j), 0))
    state_specs = [
        pl.BlockSpec((None, HEADS, DH, DH), lambda b, j: (b, 0, 0, 0)),
        pl.BlockSpec((None, HEADS, 1, DH), lambda b, j: (b, 0, 0, 0)),
        pl.BlockSpec((None, 8, 128), lambda b, j: (b, 0, 0)),
    ]
    in_specs = [
        tok_spec, tok_spec, pl.BlockSpec((None, D, n_tok), lambda b, j: (b, 0, tile(j))), tok_spec,
        pl.BlockSpec((None, 3, n_tok, 128), lambda b, j: (b, 0, tile(j), 0)),
        pl.BlockSpec((None, n_chunks, 8, ML_CHUNK), lambda b, j: (b, tile(j), 0, 0)),
    ]
    args = [q, k, kt, v, gv, g_rows]
    scratch = [
        pltpu.VMEM((HEADS, DH, DH), F32),
        pltpu.VMEM((HEADS, 1, DH), F32),
        pltpu.VMEM((8, 128), F32),
    ]
    if state is None:
        mode = "ctx"
        out_specs, out_shape = state_specs, _STATE_SHAPES
    else:
        in_specs += state_specs
        args += list(state)
        scratch.append(pltpu.VMEM((n_tok, D), F32))
        if merge_with is None:
            mode = "fwd"
            out_specs, out_shape = tok_spec, jax.ShapeDtypeStruct((NB, seq, D), BF16)
        else:
            mode = "rev"
            h_fwd, xc, norm_g, skip = merge_with
            in_specs += [tok_spec, tok_spec, _const_spec((1, D)), _const_spec((1, D)),
                         _const_spec((BF16_ROWS * ML_GROUP_COLS, BF16_ROWS * ML_GROUP_COLS))]
            args += [h_fwd, xc, norm_g, skip, _swap_perm(BF16_ROWS, ML_GROUP_COLS)]
            out_specs = pl.BlockSpec((None, GRID_W, ML_GROUP_COLS, D), lambda b, j: (b, 0, tile(j), 0))
            out_shape = jax.ShapeDtypeStruct((NB, GRID_W, GRID_W, D), BF16)
    return pl.pallas_call(
        functools.partial(_ml_rec_kernel, reverse=reverse, n_tok=n_tok, mode=mode),
        grid=(NB, n_tiles),
        in_specs=in_specs,
        out_specs=out_specs,
        out_shape=out_shape,
        scratch_shapes=scratch,
        compiler_params=_params(("arbitrary", "arbitrary")),
        name=name,
    )(*args)


def _final_kernel(x_ref, hrg_ref, grg_ref, hml_ref, smlo_ref, sgr_ref, sgm_ref, g1_ref, sh2_ref, sc2_ref, g2_ref,
                  n2_ref, nf_ref, wbr_ref, wbm_ref, wo_ref, wfi_ref, wfo_ref, o_ref, act_scr):
    dot = functools.partial(jnp.dot, preferred_element_type=F32)
    y_rg = (hrg_ref[...].astype(F32) * grg_ref[...].astype(F32)).astype(BF16)
    y_ml = (hml_ref[...].astype(F32) * smlo_ref[...].astype(F32)).astype(BF16)
    mix = sgr_ref[...].astype(F32) * dot(y_rg, wbr_ref[...])
    mix = mix + sgm_ref[...].astype(F32) * dot(y_ml, wbm_ref[...])
    x1 = x_ref[...] + g1_ref[...] * dot(mix.astype(BF16), wo_ref[...])
    ms = jnp.mean(x1 * x1, axis=-1, keepdims=True)
    hn = x1 * lax.rsqrt(ms + EPS) * n2_ref[...]
    hb = (hn * (1.0 + sc2_ref[...]) + sh2_ref[...]).astype(BF16)
    step = 256
    for c in range(D_FF // step):
        gate = dot(hb, wfi_ref[:, c * step:(c + 1) * step])
        up = dot(hb, wfi_ref[:, D_FF + c * step:D_FF + (c + 1) * step])
        act_scr[:, c * step:(c + 1) * step] = (_silu(gate) * up).astype(BF16)
    x2 = x1 + g2_ref[...] * dot(act_scr[...], wfo_ref[...])
    ms2 = jnp.mean(x2 * x2, axis=-1, keepdims=True)
    o_ref[...] = x2 * lax.rsqrt(ms2 + EPS) * nf_ref[...]


def _final_call(x, h_rg, grg, h_ml, smlo, sgr, sgm, mod3, norm2_g, final_g, wbr, wbm, wo, wfi, wfo):
    rows = FINAL_ROWS
    row_spec = pl.BlockSpec((None, rows, D), lambda b, i: (b, i, 0))
    mod_spec = lambda g: pl.BlockSpec((None, 1, D), lambda b, i: (b, 0, g))
    return pl.pallas_call(
        _final_kernel,
        grid=(NB, SEQ // rows),
        in_specs=[
            row_spec, row_spec, row_spec, row_spec, row_spec, row_spec, row_spec,
            mod_spec(2), mod_spec(3), mod_spec(4), mod_spec(5),
            _const_spec((1, D)), _const_spec((1, D)),
            _const_spec((D, D)), _const_spec((D, D)), _const_spec((D, D)),
            _const_spec((D, 2 * D_FF)), _const_spec((D_FF, D)),
        ],
        out_specs=row_spec,
        out_shape=jax.ShapeDtypeStruct((NB, SEQ, D), F32),
        scratch_shapes=[pltpu.VMEM((rows, D_FF), BF16)],
        compiler_params=_params(("arbitrary", "arbitrary")),
        name="final",
    )(x, h_rg, grg, h_ml, smlo, sgr, sgm, mod3, mod3, mod3, mod3, norm2_g, final_g, wbr, wbm, wo, wfi, wfo)


def _pair_blockdiag(w):
    w = w.reshape(8, 2, 64, 64)
    z = jnp.zeros((8, 64, 64), w.dtype)
    top = jnp.concatenate([w[:, 0], z], axis=2)
    bot = jnp.concatenate([z, w[:, 1]], axis=2)
    return jnp.concatenate([top, bot], axis=1)


def _rg_weights(wa, ba, wx, bx):
    w = jnp.concatenate([_pair_blockdiag(wa), _pair_blockdiag(wx)], axis=2).astype(BF16)
    bias = 0.5 * jnp.concatenate([ba.reshape(8, 1, 128), bx.reshape(8, 1, 128)], axis=2)
    b_hi = bias.astype(BF16)
    b_lo = (bias - b_hi.astype(F32)).astype(BF16)
    return (jnp.concatenate([w, b_hi, b_lo, jnp.zeros((8, 126, 256), BF16)], axis=1),)


def _block_diagonals(w):
    rows = [jnp.pad(w[:, i, :], ((0, 0), (3 - i, 1 + i))) for i in range(4)]
    return jnp.stack(rows, axis=1).reshape(D, 8)


def _gate_weights(wi, bi, wf, bf):
    def lanes(x):
        gap = [(0, 0)] * (x.ndim - 2)
        fwd = jnp.pad(x[0], gap + [(0, ML_REV_LANE - HEADS)])
        rev = jnp.pad(x[1], gap + [(0, 128 - ML_REV_LANE - HEADS)])
        return jnp.concatenate([fwd, rev], axis=-1)
    w = jnp.concatenate([lanes(wi), lanes(wf)], axis=-1)
    b = jnp.concatenate([lanes(bi), lanes(bf)], axis=-1).reshape(1, 256)
    return w.reshape(3 * HEADS, DH, 256).astype(BF16), b


def kernel(x, c, ctx, c_ctx, w_mod, b_mod, norm1_g, norm2_g, w_in, rg_conv_w, rg_conv_b, rg_wa, rg_ba, rg_wx,
           rg_bx, rg_lambda, ml_conv_w, ml_conv_b, ml_wq, ml_wk, ml_wv, ml_wi, ml_bi, ml_wf, ml_bf,
           ml_norm_g, ml_skip, w_branch_rg, w_branch_ml, w_out, w_ffn_in, w_ffn_out, final_norm_g):
    mod = _mod_call(c, c_ctx, w_mod[0], b_mod[0])
    mod3 = mod.reshape(2 * NB, 1, 6 * D)
    w_in_bf = w_in[0].astype(BF16)
    norm1 = norm1_g[0].reshape(1, D)

    rgx, grg, mlx, smlo, sgr, sgm = _proj_call(x, mod3, norm1, w_in_bf, ctx=False)
    rgx_c, mlx_c = _proj_call(ctx, mod3, norm1, w_in_bf, ctx=True)

    rg_cw = rg_conv_w[0]
    rg_cb = rg_conv_b[0].reshape(1, D)
    zero_h = jnp.zeros((NB, D), F32)
    rg_w = [_rg_weights(rg_wa[0, d], rg_ba[0, d], rg_wx[0, d], rg_bx[0, d]) + (rg_lambda[0, d].reshape(1, D),)
            for d in range(2)]
    _, h0_f = _rg_call(rgx_c, zero_h, rg_cw, rg_cb, *rg_w[0], reverse=False, mode="ctx", name="rg_ctx_fwd")
    _, h0_r = _rg_call(rgx_c, zero_h, rg_cw, rg_cb, *rg_w[1], reverse=True, mode="ctx", name="rg_ctx_rev")
    h_f, rg_xh, _ = _rg_call(rgx, h0_f, rg_cw, rg_cb, *rg_w[0], reverse=False, mode="fwd", name="rg_fwd")
    h_rg, _ = _rg_call(rg_xh, h0_r, rg_cw, rg_cb, *rg_w[1], reverse=True, mode="rev", h_fwd=h_f, name="rg_rev")

    ml_cw = ml_conv_w[0]
    ml_cb = ml_conv_b[0].reshape(1, D)
    ml_qkv = (_block_diagonals(ml_wq[0]), _block_diagonals(ml_wk[0]), _block_diagonals(ml_wv[0]))
    ml_w = (ml_cw, ml_cb) + ml_qkv + _gate_weights(ml_wi[0], ml_bi[0], ml_wf[0], ml_bf[0])
    q_c, k_c, kt_c, v_c, _, gv_c, grf_c, grr_c = _ml_prep_call(mlx_c, ml_w, latent=False, name="ml_prep_ctx")
    st_f = _ml_rec_call(q_c, k_c, kt_c, v_c, gv_c, grf_c, reverse=False, name="ml_ctx_fwd")
    st_r = _ml_rec_call(q_c, k_c, kt_c, v_c, gv_c, grr_c, reverse=True, name="ml_ctx_rev")
    q_l, k_l, kt_l, v_l, xc_l, gv_l, grf_l, grr_l = _ml_prep_call(
        mlx.reshape(NB, GRID_W, GRID_W, D), ml_w, latent=True, name="ml_prep")
    hm_f = _ml_rec_call(q_l, k_l, kt_l, v_l, gv_l, grf_l, reverse=False, state=st_f, name="ml_fwd")
    h_ml = _ml_rec_call(q_l, k_l, kt_l, v_l, gv_l, grr_l, reverse=True, state=st_r,
                        merge_with=(hm_f, xc_l, ml_norm_g[0].reshape(1, D), ml_skip[0].reshape(1, D)), name="ml_rev")

    return _final_call(
        x, h_rg, grg, h_ml.reshape(NB, SEQ, D), smlo, sgr, sgm, mod3,
        norm2_g[0].reshape(1, D), final_norm_g.reshape(1, D),
        w_branch_rg[0].astype(BF16), w_branch_ml[0].astype(BF16), w_out[0].astype(BF16),
        w_ffn_in[0].astype(BF16), w_ffn_out[0].astype(BF16))
```

```python
import functools

import jax
import jax.numpy as jnp
from jax import lax
from jax.experimental import pallas as pl
from jax.experimental.pallas import tpu as pltpu

F32 = jnp.float32
BF16 = jnp.bfloat16

D = 1024
NB = 8
SEQ = 4096
GRID_W = 64
CTX = 256
EPS = 1e-6
RG_C = 8.0
HEADS = 4
DH = D // HEADS
D_FF = 2816
N_IN = 6 * D
LOG2E = 1.4426950408889634

VMEM_LIMIT = 60 * 1024 * 1024
BF16_ROWS = 16

PROJ_ROWS = 1024
FINAL_ROWS = 512
RG_STEPS = 128
RG_SUB = 256
RG_SUB_STEPS = RG_SUB // NB
ML_CHUNK = 256
ML_GROUP_COLS = 16
ML_TILE = ML_GROUP_COLS * GRID_W
ML_A_ROWS = 512

assert ML_CHUNK == DH


def _sigmoid(x):
    return 0.5 * (jnp.tanh(0.5 * x) + 1.0)


def _silu(x):
    return x * _sigmoid(x)


def _softplus(x):
    return jnp.maximum(x, 0.0) + jnp.log(1.0 + jnp.exp(-jnp.abs(x)))


def _log_sigmoid(x):
    return jnp.minimum(x, 0.0) - jnp.log(1.0 + jnp.exp(-jnp.abs(x)))


def _split3(x):
    hi = x.astype(BF16)
    r1 = x - hi.astype(F32)
    mid = r1.astype(BF16)
    lo = (r1 - mid.astype(F32)).astype(BF16)
    return hi, mid, lo


def _params(sem):
    return pltpu.CompilerParams(dimension_semantics=sem, vmem_limit_bytes=VMEM_LIMIT)


def _const_spec(shape):
    nd = len(shape)
    return pl.BlockSpec(shape, lambda *_: (0,) * nd, pipeline_mode=pl.Buffered(1))


def _swap_perm(a, b):
    n = a * b
    out_row = jnp.arange(n)
    src = (out_row % a) * b + out_row // a
    return (src[:, None] == jnp.arange(n)[None, :]).astype(BF16)


def _mod_kernel(c_ref, cc_ref, w_ref, b_ref, o_ref):
    s = jnp.concatenate([c_ref[...], jnp.broadcast_to(cc_ref[...], (NB, D))], axis=0)
    s = _silu(s)
    s_hi, s_mid, _ = _split3(s)
    w_hi, w_mid, _ = _split3(w_ref[...])
    dot = functools.partial(jnp.dot, preferred_element_type=F32)
    o_ref[...] = dot(s_hi, w_hi) + dot(s_mid, w_hi) + dot(s_hi, w_mid) + b_ref[...]


def _mod_call(c, c_ctx, w_mod, b_mod):
    return pl.pallas_call(
        _mod_kernel,
        grid=(6,),
        in_specs=[
            pl.BlockSpec((NB, D), lambda g: (0, 0)),
            pl.BlockSpec((1, D), lambda g: (0, 0)),
            pl.BlockSpec((D, D), lambda g: (0, g)),
            pl.BlockSpec((1, D), lambda g: (0, g)),
        ],
        out_specs=pl.BlockSpec((2 * NB, D), lambda g: (0, g)),
        out_shape=jax.ShapeDtypeStruct((2 * NB, 6 * D), F32),
        compiler_params=_params(("arbitrary",)),
        name="mod",
    )(c, c_ctx.reshape(1, D), w_mod, b_mod.reshape(1, 6 * D))


def _gelu_tanh(x):
    return jax.nn.gelu(x, approximate=True)


def _identity(x):
    return x


_PROJ_FULL = ((0, _identity), (1, _gelu_tanh), (2, _identity), (3, _sigmoid), (4, _sigmoid), (5, _sigmoid))
_PROJ_CTX = ((0, _identity), (2, _identity))


def _proj_kernel(x_ref, sh_ref, sc_ref, g_ref, w_ref, *o_refs, groups):
    x = x_ref[...]
    ms = jnp.mean(x * x, axis=-1, keepdims=True)
    y = x * lax.rsqrt(ms + EPS) * g_ref[...]
    u = (y * (1.0 + sc_ref[...]) + sh_ref[...]).astype(BF16)
    for o_ref, (g, act) in zip(o_refs, groups):
        p = jnp.dot(u, w_ref[:, g * D:(g + 1) * D], preferred_element_type=F32)
        o_ref[...] = act(p).astype(o_ref.dtype)


def _proj_call(x, mod3, norm_g, w_in_bf, *, ctx):
    L = x.shape[1]
    rows = min(PROJ_ROWS, L)
    groups = _PROJ_CTX if ctx else _PROJ_FULL
    mod_row = (lambda b: NB) if ctx else (lambda b: b)
    row_spec = pl.BlockSpec((None, rows, D), lambda b, i: (b, i, 0))
    return pl.pallas_call(
        functools.partial(_proj_kernel, groups=groups),
        grid=(NB, L // rows),
        in_specs=[
            row_spec,
            pl.BlockSpec((None, 1, D), lambda b, i: (mod_row(b), 0, 0)),
            pl.BlockSpec((None, 1, D), lambda b, i: (mod_row(b), 0, 1)),
            _const_spec((1, D)),
            _const_spec((D, N_IN)),
        ],
        out_specs=[row_spec] * len(groups),
        out_shape=[jax.ShapeDtypeStruct((NB, L, D), BF16)] * len(groups),
        compiler_params=_params(("arbitrary", "arbitrary")),
        name="proj_ctx" if ctx else "proj",
    )(x, mod3, mod3, norm_g, w_in_bf)


def _rg_kernel(*refs, reverse, n_tiles, mode):
    if mode == "rev":
        (xh_ref, h0_ref, wbd_ref, lam_ref, hf_ref, pout_ref, o_ref, hlast_ref, a_scr, b_scr, h_scr) = refs
    elif mode == "fwd":
        (x_ref, xp_ref, xn_ref, h0_ref, cw_ref, cb_ref, wbd_ref, lam_ref, pin_ref, phalo_ref,
         o_ref, xh_out, hlast_ref, xe_scr, a_scr, b_scr, h_scr) = refs
    else:
        (x_ref, xp_ref, xn_ref, h0_ref, cw_ref, cb_ref, wbd_ref, lam_ref, pin_ref, phalo_ref,
         o_ref, hlast_ref, xe_scr, a_scr, b_scr, h_scr) = refs
    step = pl.program_id(0)
    order = jnp.minimum(step, n_tiles - 1)
    tile = (n_tiles - 1 - order) if reverse else order
    new = step % 2
    old = 1 - new
    rows = RG_STEPS * NB
    halo = 2 * NB

    @pl.when(step == 0)
    def _():
        h_scr[...] = h0_ref[...]
        a_scr[1] = jnp.ones((rows, D), F32)
        b_scr[1] = jnp.zeros((rows, D), F32)

    if mode != "rev":
        def halo_rows(ref):
            xb = jnp.concatenate([ref[b] for b in range(NB)], axis=0)
            return jnp.dot(phalo_ref[...], xb, preferred_element_type=F32)

        xe_scr[0:halo, :] = jnp.where(tile > 0, halo_rows(xp_ref)[(BF16_ROWS - 2) * NB:BF16_ROWS * NB], 0.0)
        xe_scr[halo + rows:2 * halo + rows, :] = jnp.where(tile < n_tiles - 1, halo_rows(xn_ref)[0:halo], 0.0)
        for s in range(rows // RG_SUB):
            t0 = s * RG_SUB_STEPS
            xb = jnp.concatenate([x_ref[b, t0:t0 + RG_SUB_STEPS, :] for b in range(NB)], axis=0)
            xe_scr[halo + s * RG_SUB:halo + (s + 1) * RG_SUB, :] = jnp.dot(
                pin_ref[...], xb, preferred_element_type=F32)
        cw = 0.5 * cw_ref[...]
        cb = 0.5 * cb_ref[...]

    lam2 = (-0.5 * RG_C * LOG2E) * _softplus(-lam_ref[...])

    ones = jnp.ones((RG_SUB, 128), BF16)

    for s in range(rows // RG_SUB):
        r0 = s * RG_SUB
        if mode == "rev":
            xh = xh_ref[r0:r0 + RG_SUB, :].astype(F32)
        else:
            xh = cb + cw[0:1] * xe_scr[r0:r0 + RG_SUB, :]
            xh = xh + cw[1:2] * xe_scr[r0 + NB:r0 + NB + RG_SUB, :]
            xh = xh + cw[2:3] * xe_scr[r0 + 2 * NB:r0 + 2 * NB + RG_SUB, :]
            xh = xh + cw[3:4] * xe_scr[r0 + 3 * NB:r0 + 3 * NB + RG_SUB, :]
        if mode == "fwd":
            xh_out[r0:r0 + RG_SUB, :] = xh.astype(BF16)
        for p in range(D // 128):
            lo, hi = p * 128, (p + 1) * 128
            xhp = xh[:, lo:hi]
            lhs = jnp.concatenate([xhp.astype(BF16), ones], axis=1)
            pre = jnp.dot(lhs, wbd_ref[p], preferred_element_type=F32)
            t_r = jnp.tanh(pre[:, 0:128])
            t_i = jnp.tanh(pre[:, 128:256])
            a = jnp.exp2(lam2[:, lo:hi] * (t_r + 1.0))
            y = (1.0 - a) * (1.0 + a)
            gain = jnp.where(y > 0.0, y * lax.rsqrt(y), 0.0)
            a_scr[new, r0:r0 + RG_SUB, lo:hi] = a
            b_scr[new, r0:r0 + RG_SUB, lo:hi] = gain * ((t_i + 1.0) * xhp)

    h = h_scr[...]
    for t in range(RG_STEPS):
        r0 = ((RG_STEPS - 1 - t) if reverse else t) * NB
        h = a_scr[old, r0:r0 + NB, :] * h + b_scr[old, r0:r0 + NB, :]
        b_scr[old, r0:r0 + NB, :] = h
    h_scr[...] = h
    hlast_ref[...] = h

    if mode == "rev":
        for s in range(rows // RG_SUB):
            r0 = s * RG_SUB
            t0 = s * RG_SUB_STEPS
            hs = (b_scr[old, r0:r0 + RG_SUB, :] + hf_ref[r0:r0 + RG_SUB, :].astype(F32)).astype(BF16)
            hb = jnp.dot(pout_ref[...], hs, preferred_element_type=F32).astype(o_ref.dtype)
            for b in range(NB):
                o_ref[b, t0:t0 + RG_SUB_STEPS, :] = hb[b * RG_SUB_STEPS:(b + 1) * RG_SUB_STEPS]
    else:
        o_ref[...] = b_scr[old].astype(o_ref.dtype)


def _rg_call(src, h0, cw, cb, wbd, lam, *, reverse, mode, h_fwd=None, name):
    L = src.shape[0] // NB if mode == "rev" else src.shape[1]
    rows = RG_STEPS * NB
    n_tiles = L // RG_STEPS
    per = RG_STEPS // BF16_ROWS
    n_halo = L // BF16_ROWS
    def tile_at(lag):
        def tile(i):
            order = jnp.clip(i - lag, 0, n_tiles - 1)
            return (n_tiles - 1 - order) if reverse else order
        return tile

    tile, tile_old = tile_at(0), tile_at(1)
    bm_spec = pl.BlockSpec((NB, RG_STEPS, D), lambda i: (0, tile(i), 0))
    tm_spec = pl.BlockSpec((rows, D), lambda i: (tile(i), 0))
    tm_old_spec = pl.BlockSpec((rows, D), lambda i: (tile_old(i), 0))
    tm_shape = jax.ShapeDtypeStruct((L * NB, D), BF16)
    state_spec = pl.BlockSpec((NB, D), lambda i: (0, 0))
    state_shape = jax.ShapeDtypeStruct((NB, D), F32)
    scratch = [pltpu.VMEM((2, rows, D), F32), pltpu.VMEM((2, rows, D), F32), pltpu.VMEM((NB, D), F32)]
    if mode == "rev":
        in_specs = [tm_spec, _const_spec((NB, D)), _const_spec((D // 128, 256, 256)), _const_spec((1, D)),
                    tm_old_spec, _const_spec((RG_SUB, RG_SUB))]
        args = [src, h0, wbd, lam, h_fwd, _swap_perm(RG_SUB_STEPS, NB)]
        out_specs = [pl.BlockSpec((NB, RG_STEPS, D), lambda i: (0, tile_old(i), 0)), state_spec]
        out_shape = [jax.ShapeDtypeStruct((NB, L, D), BF16), state_shape]
    else:
        in_specs = [
            bm_spec,
            pl.BlockSpec((NB, BF16_ROWS, D), lambda i: (0, jnp.maximum(tile(i) * per - 1, 0), 0)),
            pl.BlockSpec((NB, BF16_ROWS, D), lambda i: (0, jnp.minimum((tile(i) + 1) * per, n_halo - 1), 0)),
            _const_spec((NB, D)),
            _const_spec((4, D)),
            _const_spec((1, D)),
            _const_spec((D // 128, 256, 256)),
            _const_spec((1, D)),
            _const_spec((RG_SUB, RG_SUB)),
            _const_spec((NB * BF16_ROWS, NB * BF16_ROWS)),
        ]
        args = [src, src, src, h0, cw, cb, wbd, lam, _swap_perm(NB, RG_SUB_STEPS), _swap_perm(NB, BF16_ROWS)]
        out_specs = [tm_old_spec] + ([tm_spec] if mode == "fwd" else []) + [state_spec]
        out_shape = [tm_shape] + ([tm_shape] if mode == "fwd" else []) + [state_shape]
        scratch = [pltpu.VMEM((rows + 4 * NB, D), F32)] + scratch
    return pl.pallas_call(
        functools.partial(_rg_kernel, reverse=reverse, n_tiles=n_tiles, mode=mode),
        grid=(n_tiles + 1,),
        in_specs=in_specs,
        out_specs=out_specs,
        out_shape=out_shape,
        scratch_shapes=scratch,
        compiler_params=_params(("arbitrary",)),
        name=name,
    )(*args)


ML_REV_LANE = 8


def _scan_lanes_both(x, op, fill):
    n = x.shape[1]
    lane = lax.broadcasted_iota(jnp.int32, x.shape, 1)
    is_prefix = lax.broadcasted_iota(jnp.int32, x.shape, 0) < ML_REV_LANE
    sh = 1
    while sh < n:
        before = jnp.where(lane >= sh, pltpu.roll(x, sh, axis=1), fill)
        after = jnp.where(lane < n - sh, pltpu.roll(x, n - sh, axis=1), fill)
        x = op(x, jnp.where(is_prefix, before, after))
        sh *= 2
    return x


def _ml_prep_kernel(*refs, n_tiles, n_tok, latent):
    L = ML_CHUNK
    n_chunks = n_tok // L
    if latent:
        (x_ref, xp_ref, xn_ref, perm_ref, cw_ref, cb_ref, dq_ref, dk_ref, dv_ref, wg_ref, bg_ref,
         q_ref, k_ref, kt_ref, v_ref, xc_ref, gv_ref, grf_ref, grr_ref,
         xt_scr, wqk_scr, wv_scr, wkt_scr, g_scr) = refs
        step = pl.program_id(0)
        tile = jnp.minimum(step, NB * n_tiles - 1) % n_tiles
        first = step == 0
        slot_new = step % 2
        slot_old = 1 - slot_new

        @pl.when(first)
        def _():
            g_scr[1] = jnp.zeros(g_scr.shape[1:], F32)
    else:
        (x_ref, cw_ref, cb_ref, dq_ref, dk_ref, dv_ref, wg_ref, bg_ref,
         q_ref, k_ref, kt_ref, v_ref, xc_ref, gv_ref, grf_ref, grr_ref, xt_scr, wqk_scr, wv_scr, wkt_scr) = refs
        first = pl.program_id(0) == 0

    src_t = lax.broadcasted_iota(jnp.int32, (L, L), 0)
    dst_t = lax.broadcasted_iota(jnp.int32, (L, L), 1)
    tri_fwd = (src_t <= dst_t).astype(BF16)
    tri_rev = (src_t >= dst_t).astype(BF16)

    def gate_vectors(gates, ck):
        r0 = ck * L
        li_rows = gates[:, 0:128].T[0:2 * ML_REV_LANE]
        lf_rows = _log_sigmoid(gates[:, 128:256].T[0:2 * ML_REV_LANE])
        split = jnp.concatenate(_split3(lf_rows), axis=0)
        to_t = jnp.dot(split, tri_fwd, preferred_element_type=F32)
        from_t = jnp.dot(split, tri_rev, preferred_element_type=F32)
        cum_rows = jnp.concatenate([
            to_t[0:8] + to_t[BF16_ROWS:BF16_ROWS + 8] + to_t[2 * BF16_ROWS:2 * BF16_ROWS + 8],
            from_t[8:16] + from_t[BF16_ROWS + 8:2 * BF16_ROWS] + from_t[2 * BF16_ROWS + 8:3 * BF16_ROWS]], axis=0)
        g_rows = li_rows - cum_rows
        grf_ref[ck] = g_rows[0:ML_REV_LANE]
        grr_ref[ck] = g_rows[ML_REV_LANE:2 * ML_REV_LANE]
        pm_rows = _scan_lanes_both(g_rows, jnp.maximum, -jnp.inf)
        pad = jnp.zeros((128 - 2 * ML_REV_LANE, L), F32)
        gv_ref[0, r0:r0 + L, :] = jnp.concatenate([cum_rows, pad], axis=0).T
        gv_ref[1, r0:r0 + L, :] = jnp.concatenate([pm_rows, pad], axis=0).T
        gv_ref[2, r0:r0 + L, :] = jnp.concatenate([g_rows, pad], axis=0).T

    @pl.when(first)
    def _():
        diff = lax.broadcasted_iota(jnp.int32, (DH, DH), 1) - lax.broadcasted_iota(jnp.int32, (DH, DH), 0)

        def dense(d_ref, h):
            diag = d_ref[h * DH:(h + 1) * DH, :]
            out = jnp.zeros((DH, DH), F32)
            for d in range(-3, 4):
                out = jnp.where(diff == d, diag[:, 3 + d:4 + d], out)
            return out

        dot = functools.partial(jnp.dot, preferred_element_type=F32)
        for h in range(HEADS):
            wq = dense(dq_ref, h).astype(BF16)
            wk_f32 = dense(dk_ref, h) * (DH ** -0.5)
            wk = wk_f32.astype(BF16)
            wv = dense(dv_ref, h).astype(BF16)
            wkt_scr[h] = wk_f32.T.astype(BF16)
            wqk_scr[h, :, 0:DH] = wq
            wqk_scr[h, :, DH:2 * DH] = wk
            wqk_scr[h, :, 2 * DH:3 * DH] = (dot(wq, wg_ref[h]) + dot(wk, wg_ref[HEADS + h])).astype(BF16)
            wv_scr[h, :, 0:DH] = wv
            wv_scr[h, :, DH:2 * DH] = dot(wv, wg_ref[2 * HEADS + h]).astype(BF16)

    if not latent:
        xt_scr[8:16, :] = jnp.zeros((8, D), F32)
        xt_scr[16:16 + n_tok, :] = x_ref[...].astype(F32)
        xt_scr[16 + n_tok:24 + n_tok, :] = jnp.zeros((8, D), F32)
    else:
        for g in range(GRID_W // BF16_ROWS):
            xg = x_ref[g * BF16_ROWS:(g + 1) * BF16_ROWS].reshape(BF16_ROWS * ML_GROUP_COLS, D)
            yg = jnp.dot(perm_ref[...], xg, preferred_element_type=F32)
            for w in range(ML_GROUP_COLS):
                dst = 16 + w * GRID_W + g * BF16_ROWS
                xt_scr[dst:dst + BF16_ROWS, :] = yg[w * BF16_ROWS:(w + 1) * BF16_ROWS]
        last = ML_GROUP_COLS - 1
        prev = jnp.concatenate([xp_ref[0].astype(F32)[last:last + 1], xp_ref[1].astype(F32)[last:last + 1]], axis=0)
        xt_scr[14:16, :] = jnp.where(tile > 0, prev, 0.0)
        xt_scr[16 + n_tok:17 + n_tok, :] = jnp.where(tile < n_tiles - 1, xn_ref[0].astype(F32)[0:1], 0.0)

    if latent:
        for ck in range(n_chunks):
            gate_vectors(g_scr[slot_old, ck * L:(ck + 1) * L, :], ck)

    cw = cw_ref[...]
    for ck in range(n_chunks):
        r0 = ck * L
        xt = xt_scr[16 + r0:16 + r0 + L, :]
        xc = cb_ref[...] + cw[0:1] * xt_scr[14 + r0:14 + r0 + L, :] + cw[1:2] * xt_scr[15 + r0:15 + r0 + L, :]
        xc = xc + cw[2:3] * xt + cw[3:4] * xt_scr[17 + r0:17 + r0 + L, :]
        xc = _silu(xc)
        xc_ref[r0:r0 + L, :] = xc.astype(BF16)
        gates = jnp.zeros((L, 256), F32) + bg_ref[...]
        for h in range(HEADS):
            lo, hi = h * DH, (h + 1) * DH
            xc_h = xc[:, lo:hi].astype(BF16)
            qkg = jnp.dot(xc_h, wqk_scr[h], preferred_element_type=F32)
            vg = jnp.dot(xt[:, lo:hi].astype(BF16), wv_scr[h], preferred_element_type=F32)
            gates = gates + qkg[:, 2 * DH:3 * DH] + vg[:, DH:2 * DH]
            q_ref[r0:r0 + L, lo:hi] = qkg[:, 0:DH].astype(BF16)
            k_ref[r0:r0 + L, lo:hi] = qkg[:, DH:2 * DH].astype(BF16)
            kt = lax.dot_general(wkt_scr[h], xc_h, (((1,), (1,)), ((), ())), preferred_element_type=F32)
            kt_ref[lo:hi, r0:r0 + L] = kt.astype(BF16)
            v_ref[r0:r0 + L, lo:hi] = vg[:, 0:DH].astype(BF16)
        if latent:
            g_scr[slot_new, r0:r0 + L, :] = gates
        else:
            gate_vectors(gates, ck)


def _ml_rec_kernel(*refs, reverse, n_tok, mode):
    L = ML_CHUNK
    n_chunks = n_tok // L
    lane0 = ML_REV_LANE if reverse else 0
    if mode == "ctx":
        (q_ref, k_ref, kt_ref, v_ref, gv_ref, gr_ref, c_out, n_out, m_out, c_scr, n_scr, m_scr) = refs
    elif mode == "fwd":
        (q_ref, k_ref, kt_ref, v_ref, gv_ref, gr_ref, c0_ref, n0_ref, m0_ref, o_ref,
         c_scr, n_scr, m_scr, h_scr) = refs
    else:
        (q_ref, k_ref, kt_ref, v_ref, gv_ref, gr_ref, c0_ref, n0_ref, m0_ref, hf_ref, xc_ref, ng_ref, sk_ref,
         perm_ref, o_ref, c_scr, n_scr, m_scr, h_scr) = refs

    @pl.when(pl.program_id(1) == 0)
    def _():
        if mode == "ctx":
            c_scr[...] = jnp.zeros_like(c_scr)
            n_scr[...] = jnp.zeros_like(n_scr)
            m_scr[...] = jnp.zeros_like(m_scr)
        else:
            c_scr[...] = c0_ref[...]
            n_scr[...] = n0_ref[...]
            m_scr[...] = m0_ref[...]

    if mode != "ctx":
        row_id = lax.broadcasted_iota(jnp.int32, (L, L), 0)
        col_id = lax.broadcasted_iota(jnp.int32, (L, L), 1)
        keep = (col_id >= row_id) if reverse else (col_id <= row_id)

    def stage_b(ck):
        r0 = ck * L
        m_prev = m_scr[0:1, :]
        edge = 0 if reverse else L - 1
        b_tot = gv_ref[0, r0 + edge:r0 + edge + 1, :]
        gmax = gv_ref[1, r0 + edge:r0 + edge + 1, :]
        m_new = jnp.maximum(b_tot + m_prev, b_tot + gmax)
        dec = jnp.exp(b_tot + m_prev - m_new)
        ws = jnp.exp(b_tot + gv_ref[2, r0:r0 + L, :] - m_new)
        if mode != "ctx":
            cum = gv_ref[0, r0:r0 + L, :]
            inter = cum + m_prev
            m_t = jnp.maximum(inter, cum + gv_ref[1, r0:r0 + L, :])
            c_all = cum - m_t
            si_all = jnp.exp(inter - m_t)
            fl_all = jnp.exp(-m_t)
            g_row = gr_ref[ck]
        for h in range(HEADS):
            lo, hi = h * DH, (h + 1) * DH
            ln = lane0 + h
            q = q_ref[r0:r0 + L, lo:hi]
            kb = k_ref[r0:r0 + L, lo:hi]
            kt = kt_ref[lo:hi, r0:r0 + L]
            v = v_ref[r0:r0 + L, lo:hi]
            dec_h = dec[:, ln:ln + 1]
            c_old = c_scr[h]
            n_old = n_scr[h]
            if mode != "ctx":
                si_c = si_all[:, ln:ln + 1]
                s = jnp.dot(q, kt, preferred_element_type=F32)
                arg = jnp.where(keep, c_all[:, ln:ln + 1] + g_row[h:h + 1, :], -jnp.inf)
                p = s * jnp.exp(arg)
                num = jnp.dot(p.astype(BF16), v, preferred_element_type=F32)
                num = num + si_c * jnp.dot(q, c_old.astype(BF16), preferred_element_type=F32)
                qn = q.astype(F32) * n_old
                den = jnp.sum(p[:, 0:128] + p[:, 128:256], axis=1, keepdims=True)
                den = den + si_c * jnp.sum(qn[:, 0:128] + qn[:, 128:256], axis=1, keepdims=True)
                h_scr[r0:r0 + L, lo:hi] = num / jnp.maximum(jnp.abs(den), fl_all[:, ln:ln + 1])
            ws_row = jnp.exp((b_tot[:, ln:ln + 1] - m_new[:, ln:ln + 1]) + gr_ref[ck][h:h + 1, :]).astype(BF16)
            kw_t = kt * ws_row
            upd = jnp.dot(kw_t, v, preferred_element_type=F32)
            n_upd = jnp.sum(kb.astype(F32) * ws[:, ln:ln + 1], axis=0, keepdims=True)
            c_scr[h] = dec_h * c_old + upd
            n_scr[h] = dec_h * n_old + n_upd
        m_scr[...] = jnp.broadcast_to(m_new, m_scr.shape)

    for ci in range(n_chunks):
        stage_b((n_chunks - 1 - ci) if reverse else ci)

    a_rows = min(ML_A_ROWS, n_tok)
    if mode == "ctx":
        c_out[...] = c_scr[...]
        n_out[...] = n_scr[...]
        m_out[...] = m_scr[...]
    elif mode == "fwd":
        o_ref[...] = h_scr[...].astype(o_ref.dtype)
    else:
        for sb in range(n_tok // a_rows):
            r0 = sb * a_rows
            ht = h_scr[r0:r0 + a_rows, :] + hf_ref[r0:r0 + a_rows, :].astype(F32)
            for h in range(HEADS):
                lo, hi = h * DH, (h + 1) * DH
                hh = ht[:, lo:hi]
                mu = jnp.mean(hh, axis=-1, keepdims=True)
                var = jnp.mean(jnp.square(hh - mu), axis=-1, keepdims=True)
                hn = (hh - mu) * lax.rsqrt(var + EPS)
                xc = xc_ref[r0:r0 + a_rows, lo:hi].astype(F32)
                h_scr[r0:r0 + a_rows, lo:hi] = hn * ng_ref[:, lo:hi] + sk_ref[:, lo:hi] * xc
        for g in range(GRID_W // BF16_ROWS):
            zg = jnp.concatenate(
                [h_scr[w * GRID_W + g * BF16_ROWS:w * GRID_W + (g + 1) * BF16_ROWS, :] for w in range(ML_GROUP_COLS)],
                axis=0).astype(BF16)
            og = jnp.dot(perm_ref[...], zg, preferred_element_type=F32).astype(o_ref.dtype)
            o_ref[g * BF16_ROWS:(g + 1) * BF16_ROWS] = og.reshape(BF16_ROWS, ML_GROUP_COLS, D)


def _ml_weight_specs():
    return [
        _const_spec((4, D)),
        _const_spec((1, D)),
        _const_spec((D, 8)),
        _const_spec((D, 8)),
        _const_spec((D, 8)),
        _const_spec((3 * HEADS, DH, 256)),
        _const_spec((1, 256)),
    ]


def _ml_prep_call(mlx, weights, *, latent, name):
    scratch = [
        pltpu.VMEM(((ML_TILE if latent else CTX) + 32, D), F32),
        pltpu.VMEM((HEADS, DH, 3 * DH), BF16),
        pltpu.VMEM((HEADS, DH, 2 * DH), BF16),
        pltpu.VMEM((HEADS, DH, DH), BF16),
    ]
    if latent:
        n_tiles, n_tok, seq = GRID_W // ML_GROUP_COLS, ML_TILE, SEQ
        n_flat = NB * n_tiles
        grid = (n_flat + 1,)
        sem = ("arbitrary",)

        def at(lag):
            def split(step):
                flat = jnp.clip(step - lag, 0, n_flat - 1)
                return flat // n_tiles, flat % n_tiles
            return split

        new, old = at(0), at(1)
        in_specs = [
            pl.BlockSpec((None, GRID_W, ML_GROUP_COLS, D), lambda g: (new(g)[0], 0, new(g)[1], 0)),
            pl.BlockSpec((None, 2, ML_GROUP_COLS, D),
                         lambda g: (new(g)[0], GRID_W // 2 - 1, jnp.maximum(new(g)[1] - 1, 0), 0)),
            pl.BlockSpec((None, 1, ML_GROUP_COLS, D),
                         lambda g: (new(g)[0], 0, jnp.minimum(new(g)[1] + 1, n_tiles - 1), 0)),
            _const_spec((BF16_ROWS * ML_GROUP_COLS, BF16_ROWS * ML_GROUP_COLS)),
        ]
        args = [mlx, mlx, mlx, _swap_perm(BF16_ROWS, ML_GROUP_COLS)]
        scratch.append(pltpu.VMEM((2, n_tok, 256), F32))
        tok_spec = pl.BlockSpec((None, n_tok, D), lambda g: (new(g)[0], new(g)[1], 0))
        kt_spec = pl.BlockSpec((None, D, n_tok), lambda g: (new(g)[0], 0, new(g)[1]))
        gv_spec = pl.BlockSpec((None, 3, n_tok, 128), lambda g: (old(g)[0], 0, old(g)[1], 0))
        row_spec = pl.BlockSpec((None, n_tok // ML_CHUNK, 8, ML_CHUNK), lambda g: (old(g)[0], old(g)[1], 0, 0))
    else:
        n_tiles, n_tok, seq = 1, CTX, CTX
        grid = (NB, 1)
        sem = ("arbitrary", "arbitrary")
        in_specs = [pl.BlockSpec((None, CTX, D), lambda b, j: (b, 0, 0))]
        args = [mlx]
        tok_spec = pl.BlockSpec((None, n_tok, D), lambda b, j: (b, 0, 0))
        kt_spec = pl.BlockSpec((None, D, n_tok), lambda b, j: (b, 0, 0))
        gv_spec = pl.BlockSpec((None, 3, n_tok, 128), lambda b, j: (b, 0, 0, 0))
        row_spec = pl.BlockSpec((None, n_tok // ML_CHUNK, 8, ML_CHUNK), lambda b, j: (b, 0, 0, 0))
    tok_shape = jax.ShapeDtypeStruct((NB, seq, D), BF16)
    row_shape = jax.ShapeDtypeStruct((NB, seq // ML_CHUNK, 8, ML_CHUNK), F32)
    return pl.pallas_call(
        functools.partial(_ml_prep_kernel, n_tiles=n_tiles, n_tok=n_tok, latent=latent),
        grid=grid,
        in_specs=in_specs + _ml_weight_specs(),
        out_specs=[tok_spec, tok_spec, kt_spec, tok_spec, tok_spec, gv_spec, row_spec, row_spec],
        out_shape=[tok_shape, tok_shape, jax.ShapeDtypeStruct((NB, D, seq), BF16), tok_shape, tok_shape,
                   jax.ShapeDtypeStruct((NB, 3, seq, 128), F32), row_shape, row_shape],
        scratch_shapes=scratch,
        compiler_params=_params(sem),
        name=name,
    )(*args, *weights)


_STATE_SHAPES = [
    jax.ShapeDtypeStruct((NB, HEADS, DH, DH), F32),
    jax.ShapeDtypeStruct((NB, HEADS, 1, DH), F32),
    jax.ShapeDtypeStruct((NB, 8, 128), F32),
]


def _ml_rec_call(q, k, kt, v, gv, g_rows, *, reverse, state=None, merge_with=None, name):
    seq = q.shape[1]
    n_tok = min(ML_TILE, seq)
    n_tiles = seq // n_tok
    n_chunks = n_tok // ML_CHUNK
    tile = (lambda j: n_tiles - 1 - j) if reverse else (lambda j: j)
    tok_spec = pl.BlockSpec((None, n_tok, D), lambda b, j: (b, tile(j), 0))
    state_specs = [
        pl.BlockSpec((None, HEADS, DH, DH), lambda b, j: (b, 0, 0, 0)),
        pl.BlockSpec((None, HEADS, 1, DH), lambda b, j: (b, 0, 0, 0)),
        pl.BlockSpec((None, 8, 128), lambda b, j: (b, 0, 0)),
    ]
    in_specs = [
        tok_spec, tok_spec, pl.BlockSpec((None, D, n_tok), lambda b, j: (b, 0, tile(j))), tok_spec,
        pl.BlockSpec((None, 3, n_tok, 128), lambda b, j: (b, 0, tile(j), 0)),
        pl.BlockSpec((None, n_chunks, 8, ML_CHUNK), lambda b, j: (b, tile(j), 0, 0)),
    ]
    args = [q, k, kt, v, gv, g_rows]
    scratch = [
        pltpu.VMEM((HEADS, DH, DH), F32),
        pltpu.VMEM((HEADS, 1, DH), F32),
        pltpu.VMEM((8, 128), F32),
    ]
    if state is None:
        mode = "ctx"
        out_specs, out_shape = state_specs, _STATE_SHAPES
    else:
        in_specs += state_specs
        args += list(state)
        scratch.append(pltpu.VMEM((n_tok, D), F32))
        if merge_with is None:
            mode = "fwd"
            out_specs, out_shape = tok_spec, jax.ShapeDtypeStruct((NB, seq, D), BF16)
        else:
            mode = "rev"
            h_fwd, xc, norm_g, skip = merge_with
            in_specs += [tok_spec, tok_spec, _const_spec((1, D)), _const_spec((1, D)),
                         _const_spec((BF16_ROWS * ML_GROUP_COLS, BF16_ROWS * ML_GROUP_COLS))]
            args += [h_fwd, xc, norm_g, skip, _swap_perm(BF16_ROWS, ML_GROUP_COLS)]
            out_specs = pl.BlockSpec((None, GRID_W, ML_GROUP_COLS, D), lambda b, j: (b, 0, tile(j), 0))
            out_shape = jax.ShapeDtypeStruct((NB, GRID_W, GRID_W, D), BF16)
    return pl.pallas_call(
        functools.partial(_ml_rec_kernel, reverse=reverse, n_tok=n_tok, mode=mode),
        grid=(NB, n_tiles),
        in_specs=in_specs,
        out_specs=out_specs,
        out_shape=out_shape,
        scratch_shapes=scratch,
        compiler_params=_params(("arbitrary", "arbitrary")),
        name=name,
    )(*args)


def _final_kernel(x_ref, hrg_ref, grg_ref, hml_ref, smlo_ref, sgr_ref, sgm_ref, g1_ref, sh2_ref, sc2_ref, g2_ref,
                  n2_ref, nf_ref, wbr_ref, wbm_ref, wo_ref, wfi_ref, wfo_ref, o_ref, act_scr):
    dot = functools.partial(jnp.dot, preferred_element_type=F32)
    y_rg = (hrg_ref[...].astype(F32) * grg_ref[...].astype(F32)).astype(BF16)
    y_ml = (hml_ref[...].astype(F32) * smlo_ref[...].astype(F32)).astype(BF16)
    mix = sgr_ref[...].astype(F32) * dot(y_rg, wbr_ref[...])
    mix = mix + sgm_ref[...].astype(F32) * dot(y_ml, wbm_ref[...])
    x1 = x_ref[...] + g1_ref[...] * dot(mix.astype(BF16), wo_ref[...])
    ms = jnp.mean(x1 * x1, axis=-1, keepdims=True)
    hn = x1 * lax.rsqrt(ms + EPS) * n2_ref[...]
    hb = (hn * (1.0 + sc2_ref[...]) + sh2_ref[...]).astype(BF16)
    step = 256
    for c in range(D_FF // step):
        gate = dot(hb, wfi_ref[:, c * step:(c + 1) * step])
        up = dot(hb, wfi_ref[:, D_FF + c * step:D_FF + (c + 1) * step])
        act_scr[:, c * step:(c + 1) * step] = (_silu(gate) * up).astype(BF16)
    x2 = x1 + g2_ref[...] * dot(act_scr[...], wfo_ref[...])
    ms2 = jnp.mean(x2 * x2, axis=-1, keepdims=True)
    o_ref[...] = x2 * lax.rsqrt(ms2 + EPS) * nf_ref[...]


def _final_call(x, h_rg, grg, h_ml, smlo, sgr, sgm, mod3, norm2_g, final_g, wbr, wbm, wo, wfi, wfo):
    rows = FINAL_ROWS
    row_spec = pl.BlockSpec((None, rows, D), lambda b, i: (b, i, 0))
    mod_spec = lambda g: pl.BlockSpec((None, 1, D), lambda b, i: (b, 0, g))
    return pl.pallas_call(
        _final_kernel,
        grid=(NB, SEQ // rows),
        in_specs=[
            row_spec, row_spec, row_spec, row_spec, row_spec, row_spec, row_spec,
            mod_spec(2), mod_spec(3), mod_spec(4), mod_spec(5),
            _const_spec((1, D)), _const_spec((1, D)),
            _const_spec((D, D)), _const_spec((D, D)), _const_spec((D, D)),
            _const_spec((D, 2 * D_FF)), _const_spec((D_FF, D)),
        ],
        out_specs=row_spec,
        out_shape=jax.ShapeDtypeStruct((NB, SEQ, D), F32),
        scratch_shapes=[pltpu.VMEM((rows, D_FF), BF16)],
        compiler_params=_params(("arbitrary", "arbitrary")),
        name="final",
    )(x, h_rg, grg, h_ml, smlo, sgr, sgm, mod3, mod3, mod3, mod3, norm2_g, final_g, wbr, wbm, wo, wfi, wfo)


def _pair_blockdiag(w):
    w = w.reshape(8, 2, 64, 64)
    z = jnp.zeros((8, 64, 64), w.dtype)
    top = jnp.concatenate([w[:, 0], z], axis=2)
    bot = jnp.concatenate([z, w[:, 1]], axis=2)
    return jnp.concatenate([top, bot], axis=1)


def _rg_weights(wa, ba, wx, bx):
    w = jnp.concatenate([_pair_blockdiag(wa), _pair_blockdiag(wx)], axis=2).astype(BF16)
    bias = 0.5 * jnp.concatenate([ba.reshape(8, 1, 128), bx.reshape(8, 1, 128)], axis=2)
    b_hi = bias.astype(BF16)
    b_lo = (bias - b_hi.astype(F32)).astype(BF16)
    return (jnp.concatenate([w, b_hi, b_lo, jnp.zeros((8, 126, 256), BF16)], axis=1),)


def _block_diagonals(w):
    rows = [jnp.pad(w[:, i, :], ((0, 0), (3 - i, 1 + i))) for i in range(4)]
    return jnp.stack(rows, axis=1).reshape(D, 8)


def _gate_weights(wi, bi, wf, bf):
    def lanes(x):
        gap = [(0, 0)] * (x.ndim - 2)
        fwd = jnp.pad(x[0], gap + [(0, ML_REV_LANE - HEADS)])
        rev = jnp.pad(x[1], gap + [(0, 128 - ML_REV_LANE - HEADS)])
        return jnp.concatenate([fwd, rev], axis=-1)
    w = jnp.concatenate([lanes(wi), lanes(wf)], axis=-1)
    b = jnp.concatenate([lanes(bi), lanes(bf)], axis=-1).reshape(1, 256)
    return w.reshape(3 * HEADS, DH, 256).astype(BF16), b


def kernel(x, c, ctx, c_ctx, w_mod, b_mod, norm1_g, norm2_g, w_in, rg_conv_w, rg_conv_b, rg_wa, rg_ba, rg_wx,
           rg_bx, rg_lambda, ml_conv_w, ml_conv_b, ml_wq, ml_wk, ml_wv, ml_wi, ml_bi, ml_wf, ml_bf,
           ml_norm_g, ml_skip, w_branch_rg, w_branch_ml, w_out, w_ffn_in, w_ffn_out, final_norm_g):
    mod = _mod_call(c, c_ctx, w_mod[0], b_mod[0])
    mod3 = mod.reshape(2 * NB, 1, 6 * D)
    w_in_bf = w_in[0].astype(BF16)
    norm1 = norm1_g[0].reshape(1, D)

    rgx, grg, mlx, smlo, sgr, sgm = _proj_call(x, mod3, norm1, w_in_bf, ctx=False)
    rgx_c, mlx_c = _proj_call(ctx, mod3, norm1, w_in_bf, ctx=True)

    rg_cw = rg_conv_w[0]
    rg_cb = rg_conv_b[0].reshape(1, D)
    zero_h = jnp.zeros((NB, D), F32)
    rg_w = [_rg_weights(rg_wa[0, d], rg_ba[0, d], rg_wx[0, d], rg_bx[0, d]) + (rg_lambda[0, d].reshape(1, D),)
            for d in range(2)]
    _, h0_f = _rg_call(rgx_c, zero_h, rg_cw, rg_cb, *rg_w[0], reverse=False, mode="ctx", name="rg_ctx_fwd")
    _, h0_r = _rg_call(rgx_c, zero_h, rg_cw, rg_cb, *rg_w[1], reverse=True, mode="ctx", name="rg_ctx_rev")
    h_f, rg_xh, _ = _rg_call(rgx, h0_f, rg_cw, rg_cb, *rg_w[0], reverse=False, mode="fwd", name="rg_fwd")
    h_rg, _ = _rg_call(rg_xh, h0_r, rg_cw, rg_cb, *rg_w[1], reverse=True, mode="rev", h_fwd=h_f, name="rg_rev")

    ml_cw = ml_conv_w[0]
    ml_cb = ml_conv_b[0].reshape(1, D)
    ml_qkv = (_block_diagonals(ml_wq[0]), _block_diagonals(ml_wk[0]), _block_diagonals(ml_wv[0]))
    ml_w = (ml_cw, ml_cb) + ml_qkv + _gate_weights(ml_wi[0], ml_bi[0], ml_wf[0], ml_bf[0])
    q_c, k_c, kt_c, v_c, _, gv_c, grf_c, grr_c = _ml_prep_call(mlx_c, ml_w, latent=False, name="ml_prep_ctx")
    st_f = _ml_rec_call(q_c, k_c, kt_c, v_c, gv_c, grf_c, reverse=False, name="ml_ctx_fwd")
    st_r = _ml_rec_call(q_c, k_c, kt_c, v_c, gv_c, grr_c, reverse=True, name="ml_ctx_rev")
    q_l, k_l, kt_l, v_l, xc_l, gv_l, grf_l, grr_l = _ml_prep_call(
        mlx.reshape(NB, GRID_W, GRID_W, D), ml_w, latent=True, name="ml_prep")
    hm_f = _ml_rec_call(q_l, k_l, kt_l, v_l, gv_l, grf_l, reverse=False, state=st_f, name="ml_fwd")
    h_ml = _ml_rec_call(q_l, k_l, kt_l, v_l, gv_l, grr_l, reverse=True, state=st_r,
                        merge_with=(hm_f, xc_l, ml_norm_g[0].reshape(1, D), ml_skip[0].reshape(1, D)), name="ml_rev")

    return _final_call(
        x, h_rg, grg, h_ml.reshape(NB, SEQ, D), smlo, sgr, sgm, mod3,
        norm2_g[0].reshape(1, D), final_norm_g.reshape(1, D),
        w_branch_rg[0].astype(BF16), w_branch_ml[0].astype(BF16), w_out[0].astype(BF16),
        w_ffn_in[0].astype(BF16), w_ffn_out[0].astype(BF16))
```

```python
import functools

import jax
import jax.numpy as jnp
from jax import lax
from jax.experimental import pallas as pl
from jax.experimental.pallas import tpu as pltpu

F32 = jnp.float32
BF16 = jnp.bfloat16

D = 1024
NB = 8
SEQ = 4096
GRID_W = 64
CTX = 256
EPS = 1e-6
RG_C = 8.0
HEADS = 4
DH = D // HEADS
D_FF = 2816
N_IN = 6 * D
LOG2E = 1.4426950408889634

VMEM_LIMIT = 60 * 1024 * 1024
BF16_ROWS = 16

PROJ_ROWS = 1024
FINAL_ROWS = 512
RG_STEPS = 128
RG_SUB = 256
RG_SUB_STEPS = RG_SUB // NB
ML_CHUNK = 256
ML_GROUP_COLS = 16
ML_TILE = ML_GROUP_COLS * GRID_W
ML_A_ROWS = 512

assert ML_CHUNK == DH


def _sigmoid(x):
    return 0.5 * (jnp.tanh(0.5 * x) + 1.0)


def _silu(x):
    return x * _sigmoid(x)


def _softplus(x):
    return jnp.maximum(x, 0.0) + jnp.log(1.0 + jnp.exp(-jnp.abs(x)))


def _log_sigmoid(x):
    return jnp.minimum(x, 0.0) - jnp.log(1.0 + jnp.exp(-jnp.abs(x)))


def _split3(x):
    hi = x.astype(BF16)
    r1 = x - hi.astype(F32)
    mid = r1.astype(BF16)
    lo = (r1 - mid.astype(F32)).astype(BF16)
    return hi, mid, lo


def _params(sem):
    return pltpu.CompilerParams(dimension_semantics=sem, vmem_limit_bytes=VMEM_LIMIT)


def _const_spec(shape):
    nd = len(shape)
    return pl.BlockSpec(shape, lambda *_: (0,) * nd, pipeline_mode=pl.Buffered(1))


def _swap_perm(a, b):
    n = a * b
    out_row = jnp.arange(n)
    src = (out_row % a) * b + out_row // a
    return (src[:, None] == jnp.arange(n)[None, :]).astype(BF16)


def _mod_kernel(c_ref, cc_ref, w_ref, b_ref, o_ref):
    s = jnp.concatenate([c_ref[...], jnp.broadcast_to(cc_ref[...], (NB, D))], axis=0)
    s = _silu(s)
    s_hi, s_mid, _ = _split3(s)
    w_hi, w_mid, _ = _split3(w_ref[...])
    dot = functools.partial(jnp.dot, preferred_element_type=F32)
    o_ref[...] = dot(s_hi, w_hi) + dot(s_mid, w_hi) + dot(s_hi, w_mid) + b_ref[...]


def _mod_call(c, c_ctx, w_mod, b_mod):
    return pl.pallas_call(
        _mod_kernel,
        grid=(6,),
        in_specs=[
            pl.BlockSpec((NB, D), lambda g: (0, 0)),
            pl.BlockSpec((1, D), lambda g: (0, 0)),
            pl.BlockSpec((D, D), lambda g: (0, g)),
            pl.BlockSpec((1, D), lambda g: (0, g)),
        ],
        out_specs=pl.BlockSpec((2 * NB, D), lambda g: (0, g)),
        out_shape=jax.ShapeDtypeStruct((2 * NB, 6 * D), F32),
        compiler_params=_params(("arbitrary",)),
        name="mod",
    )(c, c_ctx.reshape(1, D), w_mod, b_mod.reshape(1, 6 * D))


def _gelu_tanh(x):
    return jax.nn.gelu(x, approximate=True)


def _identity(x):
    return x


_PROJ_FULL = ((0, _identity), (1, _gelu_tanh), (2, _identity), (3, _sigmoid), (4, _sigmoid), (5, _sigmoid))
_PROJ_CTX = ((0, _identity), (2, _identity))


def _proj_kernel(x_ref, sh_ref, sc_ref, g_ref, w_ref, *o_refs, groups):
    x = x_ref[...]
    ms = jnp.mean(x * x, axis=-1, keepdims=True)
    y = x * lax.rsqrt(ms + EPS) * g_ref[...]
    u = (y * (1.0 + sc_ref[...]) + sh_ref[...]).astype(BF16)
    for o_ref, (g, act) in zip(o_refs, groups):
        p = jnp.dot(u, w_ref[:, g * D:(g + 1) * D], preferred_element_type=F32)
        o_ref[...] = act(p).astype(o_ref.dtype)


def _proj_call(x, mod3, norm_g, w_in_bf, *, ctx):
    L = x.shape[1]
    rows = min(PROJ_ROWS, L)
    groups = _PROJ_CTX if ctx else _PROJ_FULL
    mod_row = (lambda b: NB) if ctx else (lambda b: b)
    row_spec = pl.BlockSpec((None, rows, D), lambda b, i: (b, i, 0))
    return pl.pallas_call(
        functools.partial(_proj_kernel, groups=groups),
        grid=(NB, L // rows),
        in_specs=[
            row_spec,
            pl.BlockSpec((None, 1, D), lambda b, i: (mod_row(b), 0, 0)),
            pl.BlockSpec((None, 1, D), lambda b, i: (mod_row(b), 0, 1)),
            _const_spec((1, D)),
            _const_spec((D, N_IN)),
        ],
        out_specs=[row_spec] * len(groups),
        out_shape=[jax.ShapeDtypeStruct((NB, L, D), BF16)] * len(groups),
        compiler_params=_params(("arbitrary", "arbitrary")),
        name="proj_ctx" if ctx else "proj",
    )(x, mod3, mod3, norm_g, w_in_bf)


def _rg_kernel(*refs, reverse, n_tiles, mode):
    if mode == "rev":
        (xh_ref, h0_ref, wbd_ref, lam_ref, hf_ref, pout_ref, o_ref, hlast_ref, a_scr, b_scr, h_scr) = refs
    elif mode == "fwd":
        (x_ref, xp_ref, xn_ref, h0_ref, cw_ref, cb_ref, wbd_ref, lam_ref, pin_ref, phalo_ref,
         o_ref, xh_out, hlast_ref, xe_scr, a_scr, b_scr, h_scr) = refs
    else:
        (x_ref, xp_ref, xn_ref, h0_ref, cw_ref, cb_ref, wbd_ref, lam_ref, pin_ref, phalo_ref,
         o_ref, hlast_ref, xe_scr, a_scr, b_scr, h_scr) = refs
    step = pl.program_id(0)
    order = jnp.minimum(step, n_tiles - 1)
    tile = (n_tiles - 1 - order) if reverse else order
    new = step % 2
    old = 1 - new
    rows = RG_STEPS * NB
    halo = 2 * NB

    @pl.when(step == 0)
    def _():
        h_scr[...] = h0_ref[...]
        a_scr[1] = jnp.ones((rows, D), F32)
        b_scr[1] = jnp.zeros((rows, D), F32)

    if mode != "rev":
        def halo_rows(ref):
            xb = jnp.concatenate([ref[b] for b in range(NB)], axis=0)
            return jnp.dot(phalo_ref[...], xb, preferred_element_type=F32)

        xe_scr[0:halo, :] = jnp.where(tile > 0, halo_rows(xp_ref)[(BF16_ROWS - 2) * NB:BF16_ROWS * NB], 0.0)
        xe_scr[halo + rows:2 * halo + rows, :] = jnp.where(tile < n_tiles - 1, halo_rows(xn_ref)[0:halo], 0.0)
        for s in range(rows // RG_SUB):
            t0 = s * RG_SUB_STEPS
            xb = jnp.concatenate([x_ref[b, t0:t0 + RG_SUB_STEPS, :] for b in range(NB)], axis=0)
            xe_scr[halo + s * RG_SUB:halo + (s + 1) * RG_SUB, :] = jnp.dot(
                pin_ref[...], xb, preferred_element_type=F32)
        cw = 0.5 * cw_ref[...]
        cb = 0.5 * cb_ref[...]

    lam2 = (-0.5 * RG_C * LOG2E) * _softplus(-lam_ref[...])

    ones = jnp.ones((RG_SUB, 128), BF16)

    for s in range(rows // RG_SUB):
        r0 = s * RG_SUB
        if mode == "rev":
            xh = xh_ref[r0:r0 + RG_SUB, :].astype(F32)
        else:
            xh = cb + cw[0:1] * xe_scr[r0:r0 + RG_SUB, :]
            xh = xh + cw[1:2] * xe_scr[r0 + NB:r0 + NB + RG_SUB, :]
            xh = xh + cw[2:3] * xe_scr[r0 + 2 * NB:r0 + 2 * NB + RG_SUB, :]
            xh = xh + cw[3:4] * xe_scr[r0 + 3 * NB:r0 + 3 * NB + RG_SUB, :]
        if mode == "fwd":
            xh_out[r0:r0 + RG_SUB, :] = xh.astype(BF16)
        for p in range(D // 128):
            lo, hi = p * 128, (p + 1) * 128
            xhp = xh[:, lo:hi]
            lhs = jnp.concatenate([xhp.astype(BF16), ones], axis=1)
            pre = jnp.dot(lhs, wbd_ref[p], preferred_element_type=F32)
            t_r = jnp.tanh(pre[:, 0:128])
            t_i = jnp.tanh(pre[:, 128:256])
            a = jnp.exp2(lam2[:, lo:hi] * (t_r + 1.0))
            y = (1.0 - a) * (1.0 + a)
            gain = jnp.where(y > 0.0, y * lax.rsqrt(y), 0.0)
            a_scr[new, r0:r0 + RG_SUB, lo:hi] = a
            b_scr[new, r0:r0 + RG_SUB, lo:hi] = gain * ((t_i + 1.0) * xhp)

    h = h_scr[...]
    for t in range(RG_STEPS):
        r0 = ((RG_STEPS - 1 - t) if reverse else t) * NB
        h = a_scr[old, r0:r0 + NB, :] * h + b_scr[old, r0:r0 + NB, :]
        b_scr[old, r0:r0 + NB, :] = h
    h_scr[...] = h
    hlast_ref[...] = h

    if mode == "rev":
        for s in range(rows // RG_SUB):
            r0 = s * RG_SUB
            t0 = s * RG_SUB_STEPS
            hs = (b_scr[old, r0:r0 + RG_SUB, :] + hf_ref[r0:r0 + RG_SUB, :].astype(F32)).astype(BF16)
            hb = jnp.dot(pout_ref[...], hs, preferred_element_type=F32).astype(o_ref.dtype)
            for b in range(NB):
                o_ref[b, t0:t0 + RG_SUB_STEPS, :] = hb[b * RG_SUB_STEPS:(b + 1) * RG_SUB_STEPS]
    else:
        o_ref[...] = b_scr[old].astype(o_ref.dtype)


def _rg_call(src, h0, cw, cb, wbd, lam, *, reverse, mode, h_fwd=None, name):
    L = src.shape[0] // NB if mode == "rev" else src.shape[1]
    rows = RG_STEPS * NB
    n_tiles = L // RG_STEPS
    per = RG_STEPS // BF16_ROWS
    n_halo = L // BF16_ROWS
    def tile_at(lag):
        def tile(i):
            order = jnp.clip(i - lag, 0, n_tiles - 1)
            return (n_tiles - 1 - order) if reverse else order
        return tile

    tile, tile_old = tile_at(0), tile_at(1)
    bm_spec = pl.BlockSpec((NB, RG_STEPS, D), lambda i: (0, tile(i), 0))
    tm_spec = pl.BlockSpec((rows, D), lambda i: (tile(i), 0))
    tm_old_spec = pl.BlockSpec((rows, D), lambda i: (tile_old(i), 0))
    tm_shape = jax.ShapeDtypeStruct((L * NB, D), BF16)
    state_spec = pl.BlockSpec((NB, D), lambda i: (0, 0))
    state_shape = jax.ShapeDtypeStruct((NB, D), F32)
    scratch = [pltpu.VMEM((2, rows, D), F32), pltpu.VMEM((2, rows, D), F32), pltpu.VMEM((NB, D), F32)]
    if mode == "rev":
        in_specs = [tm_spec, _const_spec((NB, D)), _const_spec((D // 128, 256, 256)), _const_spec((1, D)),
                    tm_old_spec, _const_spec((RG_SUB, RG_SUB))]
        args = [src, h0, wbd, lam, h_fwd, _swap_perm(RG_SUB_STEPS, NB)]
        out_specs = [pl.BlockSpec((NB, RG_STEPS, D), lambda i: (0, tile_old(i), 0)), state_spec]
        out_shape = [jax.ShapeDtypeStruct((NB, L, D), BF16), state_shape]
    else:
        in_specs = [
            bm_spec,
            pl.BlockSpec((NB, BF16_ROWS, D), lambda i: (0, jnp.maximum(tile(i) * per - 1, 0), 0)),
            pl.BlockSpec((NB, BF16_ROWS, D), lambda i: (0, jnp.minimum((tile(i) + 1) * per, n_halo - 1), 0)),
            _const_spec((NB, D)),
            _const_spec((4, D)),
            _const_spec((1, D)),
            _const_spec((D // 128, 256, 256)),
            _const_spec((1, D)),
            _const_spec((RG_SUB, RG_SUB)),
            _const_spec((NB * BF16_ROWS, NB * BF16_ROWS)),
        ]
        args = [src, src, src, h0, cw, cb, wbd, lam, _swap_perm(NB, RG_SUB_STEPS), _swap_perm(NB, BF16_ROWS)]
        out_specs = [tm_old_spec] + ([tm_spec] if mode == "fwd" else []) + [state_spec]
        out_shape = [tm_shape] + ([tm_shape] if mode == "fwd" else []) + [state_shape]
        scratch = [pltpu.VMEM((rows + 4 * NB, D), F32)] + scratch
    return pl.pallas_call(
        functools.partial(_rg_kernel, reverse=reverse, n_tiles=n_tiles, mode=mode),
        grid=(n_tiles + 1,),
        in_specs=in_specs,
        out_specs=out_specs,
        out_shape=out_shape,
        scratch_shapes=scratch,
        compiler_params=_params(("arbitrary",)),
        name=name,
    )(*args)


ML_REV_LANE = 8


def _scan_lanes_both(x, op, fill):
    n = x.shape[1]
    lane = lax.broadcasted_iota(jnp.int32, x.shape, 1)
    is_prefix = lax.broadcasted_iota(jnp.int32, x.shape, 0) < ML_REV_LANE
    sh = 1
    while sh < n:
        before = jnp.where(lane >= sh, pltpu.roll(x, sh, axis=1), fill)
        after = jnp.where(lane < n - sh, pltpu.roll(x, n - sh, axis=1), fill)
        x = op(x, jnp.where(is_prefix, before, after))
        sh *= 2
    return x


def _ml_prep_kernel(*refs, n_tiles, n_tok, latent):
    L = ML_CHUNK
    n_chunks = n_tok // L
    if latent:
        (x_ref, xp_ref, xn_ref, perm_ref, cw_ref, cb_ref, dq_ref, dk_ref, dv_ref, wg_ref, bg_ref,
         q_ref, kt_ref, v_ref, xc_ref, gv_ref, grf_ref, grr_ref,
         xt_scr, wqg_scr, wv_scr, wkt_scr, g_scr) = refs
        step = pl.program_id(0)
        tile = jnp.minimum(step, NB * n_tiles - 1) % n_tiles
        first = step == 0
        slot_new = step % 2
        slot_old = 1 - slot_new

        @pl.when(first)
        def _():
            g_scr[1] = jnp.zeros(g_scr.shape[1:], F32)
    else:
        (x_ref, cw_ref, cb_ref, dq_ref, dk_ref, dv_ref, wg_ref, bg_ref,
         q_ref, kt_ref, v_ref, xc_ref, gv_ref, grf_ref, grr_ref, xt_scr, wqg_scr, wv_scr, wkt_scr) = refs
        first = pl.program_id(0) == 0

    src_t = lax.broadcasted_iota(jnp.int32, (L, L), 0)
    dst_t = lax.broadcasted_iota(jnp.int32, (L, L), 1)
    tri_fwd = (src_t <= dst_t).astype(BF16)
    tri_rev = (src_t >= dst_t).astype(BF16)

    def gate_vectors(gates, ck):
        r0 = ck * L
        li_rows = gates[:, 0:128].T[0:2 * ML_REV_LANE]
        lf_rows = _log_sigmoid(gates[:, 128:256].T[0:2 * ML_REV_LANE])
        split = jnp.concatenate(_split3(lf_rows), axis=0)
        to_t = jnp.dot(split, tri_fwd, preferred_element_type=F32)
        from_t = jnp.dot(split, tri_rev, preferred_element_type=F32)
        cum_rows = jnp.concatenate([
            to_t[0:8] + to_t[BF16_ROWS:BF16_ROWS + 8] + to_t[2 * BF16_ROWS:2 * BF16_ROWS + 8],
            from_t[8:16] + from_t[BF16_ROWS + 8:2 * BF16_ROWS] + from_t[2 * BF16_ROWS + 8:3 * BF16_ROWS]], axis=0)
        g_rows = li_rows - cum_rows
        grf_ref[ck] = g_rows[0:ML_REV_LANE]
        grr_ref[ck] = g_rows[ML_REV_LANE:2 * ML_REV_LANE]
        pm_rows = _scan_lanes_both(g_rows, jnp.maximum, -jnp.inf)
        pad = jnp.zeros((128 - 2 * ML_REV_LANE, L), F32)
        gv_ref[0, r0:r0 + L, :] = jnp.concatenate([cum_rows, pad], axis=0).T
        gv_ref[1, r0:r0 + L, :] = jnp.concatenate([pm_rows, pad], axis=0).T

    @pl.when(first)
    def _():
        diff = lax.broadcasted_iota(jnp.int32, (DH, DH), 1) - lax.broadcasted_iota(jnp.int32, (DH, DH), 0)

        def dense(d_ref, h):
            diag = d_ref[h * DH:(h + 1) * DH, :]
            out = jnp.zeros((DH, DH), F32)
            for d in range(-3, 4):
                out = jnp.where(diff == d, diag[:, 3 + d:4 + d], out)
            return out

        dot = functools.partial(jnp.dot, preferred_element_type=F32)
        for h in range(HEADS):
            wq = dense(dq_ref, h).astype(BF16)
            wk_f32 = dense(dk_ref, h) * (DH ** -0.5)
            wk = wk_f32.astype(BF16)
            wv = dense(dv_ref, h).astype(BF16)
            wkt_scr[h] = wk_f32.T.astype(BF16)
            wqg_scr[h, :, 0:DH] = wq
            wqg_scr[h, :, DH:2 * DH] = (dot(wq, wg_ref[h]) + dot(wk, wg_ref[HEADS + h])).astype(BF16)
            wv_scr[h, :, 0:DH] = wv
            wv_scr[h, :, DH:2 * DH] = dot(wv, wg_ref[2 * HEADS + h]).astype(BF16)

    if not latent:
        xt_scr[8:16, :] = jnp.zeros((8, D), F32)
        xt_scr[16:16 + n_tok, :] = x_ref[...].astype(F32)
        xt_scr[16 + n_tok:24 + n_tok, :] = jnp.zeros((8, D), F32)
    else:
        for g in range(GRID_W // BF16_ROWS):
            xg = x_ref[g * BF16_ROWS:(g + 1) * BF16_ROWS].reshape(BF16_ROWS * ML_GROUP_COLS, D)
            yg = jnp.dot(perm_ref[...], xg, preferred_element_type=F32)
            for w in range(ML_GROUP_COLS):
                dst = 16 + w * GRID_W + g * BF16_ROWS
                xt_scr[dst:dst + BF16_ROWS, :] = yg[w * BF16_ROWS:(w + 1) * BF16_ROWS]
        last = ML_GROUP_COLS - 1
        prev = jnp.concatenate([xp_ref[0].astype(F32)[last:last + 1], xp_ref[1].astype(F32)[last:last + 1]], axis=0)
        xt_scr[14:16, :] = jnp.where(tile > 0, prev, 0.0)
        xt_scr[16 + n_tok:17 + n_tok, :] = jnp.where(tile < n_tiles - 1, xn_ref[0].astype(F32)[0:1], 0.0)

    if latent:
        for ck in range(n_chunks):
            gate_vectors(g_scr[slot_old, ck * L:(ck + 1) * L, :], ck)

    cw = cw_ref[...]
    for ck in range(n_chunks):
        r0 = ck * L
        xt = xt_scr[16 + r0:16 + r0 + L, :]
        xc = cb_ref[...] + cw[0:1] * xt_scr[14 + r0:14 + r0 + L, :] + cw[1:2] * xt_scr[15 + r0:15 + r0 + L, :]
        xc = xc + cw[2:3] * xt + cw[3:4] * xt_scr[17 + r0:17 + r0 + L, :]
        xc = _silu(xc)
        xc_ref[r0:r0 + L, :] = xc.astype(BF16)
        gates = jnp.zeros((L, 256), F32) + bg_ref[...]
        for h in range(HEADS):
            lo, hi = h * DH, (h + 1) * DH
            xc_h = xc[:, lo:hi].astype(BF16)
            qg = jnp.dot(xc_h, wqg_scr[h], preferred_element_type=F32)
            vg = jnp.dot(xt[:, lo:hi].astype(BF16), wv_scr[h], preferred_element_type=F32)
            gates = gates + qg[:, DH:2 * DH] + vg[:, DH:2 * DH]
            q_ref[r0:r0 + L, lo:hi] = qg[:, 0:DH].astype(BF16)
            kt = lax.dot_general(wkt_scr[h], xc_h, (((1,), (1,)), ((), ())), preferred_element_type=F32)
            kt_ref[lo:hi, r0:r0 + L] = kt.astype(BF16)
            v_ref[r0:r0 + L, lo:hi] = vg[:, 0:DH].astype(BF16)
        if latent:
            g_scr[slot_new, r0:r0 + L, :] = gates
        else:
            gate_vectors(gates, ck)


def _ml_rec_kernel(*refs, reverse, n_tok, mode):
    L = ML_CHUNK
    n_chunks = n_tok // L
    lane0 = ML_REV_LANE if reverse else 0
    if mode == "ctx":
        (q_ref, kt_ref, v_ref, gv_ref, gr_ref, c_out, m_out, c_scr, m_scr) = refs
    elif mode == "fwd":
        (q_ref, kt_ref, v_ref, gv_ref, gr_ref, c0_ref, m0_ref, o_ref, c_scr, m_scr, h_scr) = refs
    else:
        (q_ref, kt_ref, v_ref, gv_ref, gr_ref, c0_ref, m0_ref, hf_ref, xc_ref, ng_ref, sk_ref,
         perm_ref, o_ref, c_scr, m_scr, h_scr) = refs

    @pl.when(pl.program_id(1) == 0)
    def _():
        if mode == "ctx":
            c_scr[...] = jnp.zeros_like(c_scr)
            m_scr[...] = jnp.zeros_like(m_scr)
        else:
            c_scr[...] = c0_ref[...]
            m_scr[...] = m0_ref[...]

    ones_lanes = jnp.ones((L, 128), BF16)
    if mode != "ctx":
        row_id = lax.broadcasted_iota(jnp.int32, (L, L), 0)
        col_id = lax.broadcasted_iota(jnp.int32, (L, L), 1)
        keep = (col_id >= row_id) if reverse else (col_id <= row_id)

    def stage_b(ck):
        r0 = ck * L
        m_prev = m_scr[0:1, :]
        edge = 0 if reverse else L - 1
        b_tot = gv_ref[0, r0 + edge:r0 + edge + 1, :]
        gmax = gv_ref[1, r0 + edge:r0 + edge + 1, :]
        m_new = jnp.maximum(b_tot + m_prev, b_tot + gmax)
        dec = jnp.exp(b_tot + m_prev - m_new)
        if mode != "ctx":
            cum = gv_ref[0, r0:r0 + L, :]
            inter = cum + m_prev
            m_t = jnp.maximum(inter, cum + gv_ref[1, r0:r0 + L, :])
            c_all = cum - m_t
            si_all = jnp.exp(inter - m_t)
            fl_all = jnp.exp(-m_t)
            g_row = gr_ref[ck]
        for h in range(HEADS):
            lo, hi = h * DH, (h + 1) * DH
            ln = lane0 + h
            q = q_ref[r0:r0 + L, lo:hi]
            kt = kt_ref[lo:hi, r0:r0 + L]
            v = jnp.concatenate([v_ref[r0:r0 + L, lo:hi], ones_lanes], axis=1)
            dec_h = dec[:, ln:ln + 1]
            c_old = c_scr[h]
            if mode != "ctx":
                si_c = si_all[:, ln:ln + 1]
                s = jnp.dot(q, kt, preferred_element_type=F32)
                arg = jnp.where(keep, c_all[:, ln:ln + 1] + g_row[h:h + 1, :], -jnp.inf)
                p = (s * jnp.exp(arg)).astype(BF16)
                both = jnp.dot(p, v, preferred_element_type=F32)
                both = both + si_c * jnp.dot(q, c_old.astype(BF16), preferred_element_type=F32)
                scale = 1.0 / jnp.maximum(jnp.abs(both[:, DH:DH + 128]), fl_all[:, ln:ln + 1])
                h_scr[r0:r0 + L, lo:hi] = both[:, 0:DH] * jnp.concatenate([scale, scale], axis=1)
            ws_row = jnp.exp((b_tot[:, ln:ln + 1] - m_new[:, ln:ln + 1]) + gr_ref[ck][h:h + 1, :]).astype(BF16)
            kw_t = kt * ws_row
            c_scr[h] = dec_h * c_old + jnp.dot(kw_t, v, preferred_element_type=F32)
        m_scr[...] = jnp.broadcast_to(m_new, m_scr.shape)

    for ci in range(n_chunks):
        stage_b((n_chunks - 1 - ci) if reverse else ci)

    a_rows = min(ML_A_ROWS, n_tok)
    if mode == "ctx":
        c_out[...] = c_scr[...]
        m_out[...] = m_scr[...]
    elif mode == "fwd":
        o_ref[...] = h_scr[...].astype(o_ref.dtype)
    else:
        for sb in range(n_tok // a_rows):
            r0 = sb * a_rows
            ht = h_scr[r0:r0 + a_rows, :] + hf_ref[r0:r0 + a_rows, :].astype(F32)
            for h in range(HEADS):
                lo, hi = h * DH, (h + 1) * DH
                hh = ht[:, lo:hi]
                mu = jnp.mean(hh, axis=-1, keepdims=True)
                var = jnp.mean(jnp.square(hh - mu), axis=-1, keepdims=True)
                hn = (hh - mu) * lax.rsqrt(var + EPS)
                xc = xc_ref[r0:r0 + a_rows, lo:hi].astype(F32)
                h_scr[r0:r0 + a_rows, lo:hi] = hn * ng_ref[:, lo:hi] + sk_ref[:, lo:hi] * xc
        for g in range(GRID_W // BF16_ROWS):
            zg = jnp.concatenate(
                [h_scr[w * GRID_W + g * BF16_ROWS:w * GRID_W + (g + 1) * BF16_ROWS, :] for w in range(ML_GROUP_COLS)],
                axis=0).astype(BF16)
            og = jnp.dot(perm_ref[...], zg, preferred_element_type=F32).astype(o_ref.dtype)
            o_ref[g * BF16_ROWS:(g + 1) * BF16_ROWS] = og.reshape(BF16_ROWS, ML_GROUP_COLS, D)


def _ml_weight_specs():
    return [
        _const_spec((4, D)),
        _const_spec((1, D)),
        _const_spec((D, 8)),
        _const_spec((D, 8)),
        _const_spec((D, 8)),
        _const_spec((3 * HEADS, DH, 256)),
        _const_spec((1, 256)),
    ]


def _ml_prep_call(mlx, weights, *, latent, name):
    scratch = [
        pltpu.VMEM(((ML_TILE if latent else CTX) + 32, D), F32),
        pltpu.VMEM((HEADS, DH, 2 * DH), BF16),
        pltpu.VMEM((HEADS, DH, 2 * DH), BF16),
        pltpu.VMEM((HEADS, DH, DH), BF16),
    ]
    if latent:
        n_tiles, n_tok, seq = GRID_W // ML_GROUP_COLS, ML_TILE, SEQ
        n_flat = NB * n_tiles
        grid = (n_flat + 1,)
        sem = ("arbitrary",)

        def at(lag):
            def split(step):
                flat = jnp.clip(step - lag, 0, n_flat - 1)
                return flat // n_tiles, flat % n_tiles
            return split

        new, old = at(0), at(1)
        in_specs = [
            pl.BlockSpec((None, GRID_W, ML_GROUP_COLS, D), lambda g: (new(g)[0], 0, new(g)[1], 0)),
            pl.BlockSpec((None, 2, ML_GROUP_COLS, D),
                         lambda g: (new(g)[0], GRID_W // 2 - 1, jnp.maximum(new(g)[1] - 1, 0), 0)),
            pl.BlockSpec((None, 1, ML_GROUP_COLS, D),
                         lambda g: (new(g)[0], 0, jnp.minimum(new(g)[1] + 1, n_tiles - 1), 0)),
            _const_spec((BF16_ROWS * ML_GROUP_COLS, BF16_ROWS * ML_GROUP_COLS)),
        ]
        args = [mlx, mlx, mlx, _swap_perm(BF16_ROWS, ML_GROUP_COLS)]
        scratch.append(pltpu.VMEM((2, n_tok, 256), F32))
        tok_spec = pl.BlockSpec((None, n_tok, D), lambda g: (new(g)[0], new(g)[1], 0))
        kt_spec = pl.BlockSpec((None, D, n_tok), lambda g: (new(g)[0], 0, new(g)[1]))
        gv_spec = pl.BlockSpec((None, 2, n_tok, 128), lambda g: (old(g)[0], 0, old(g)[1], 0))
        row_spec = pl.BlockSpec((None, n_tok // ML_CHUNK, 8, ML_CHUNK), lambda g: (old(g)[0], old(g)[1], 0, 0))
    else:
        n_tiles, n_tok, seq = 1, CTX, CTX
        grid = (NB, 1)
        sem = ("arbitrary", "arbitrary")
        in_specs = [pl.BlockSpec((None, CTX, D), lambda b, j: (b, 0, 0))]
        args = [mlx]
        tok_spec = pl.BlockSpec((None, n_tok, D), lambda b, j: (b, 0, 0))
        kt_spec = pl.BlockSpec((None, D, n_tok), lambda b, j: (b, 0, 0))
        gv_spec = pl.BlockSpec((None, 2, n_tok, 128), lambda b, j: (b, 0, 0, 0))
        row_spec = pl.BlockSpec((None, n_tok // ML_CHUNK, 8, ML_CHUNK), lambda b, j: (b, 0, 0, 0))
    tok_shape = jax.ShapeDtypeStruct((NB, seq, D), BF16)
    row_shape = jax.ShapeDtypeStruct((NB, seq // ML_CHUNK, 8, ML_CHUNK), F32)
    return pl.pallas_call(
        functools.partial(_ml_prep_kernel, n_tiles=n_tiles, n_tok=n_tok, latent=latent),
        grid=grid,
        in_specs=in_specs + _ml_weight_specs(),
        out_specs=[tok_spec, kt_spec, tok_spec, tok_spec, gv_spec, row_spec, row_spec],
        out_shape=[tok_shape, jax.ShapeDtypeStruct((NB, D, seq), BF16), tok_shape, tok_shape,
                   jax.ShapeDtypeStruct((NB, 2, seq, 128), F32), row_shape, row_shape],
        scratch_shapes=scratch,
        compiler_params=_params(sem),
        name=name,
    )(*args, *weights)


_STATE_SHAPES = [
    jax.ShapeDtypeStruct((NB, HEADS, DH, DH + 128), F32),
    jax.ShapeDtypeStruct((NB, 8, 128), F32),
]


def _ml_rec_call(q, kt, v, gv, g_rows, *, reverse, state=None, merge_with=None, name):
    seq = q.shape[1]
    n_tok = min(ML_TILE, seq)
    n_tiles = seq // n_tok
    n_chunks = n_tok // ML_CHUNK
    tile = (lambda j: n_tiles - 1 - j) if reverse else (lambda j: j)
    tok_spec = pl.BlockSpec((None, n_tok, D), lambda b, j: (b, tile(j), 0))
    state_specs = [
        pl.BlockSpec((None, HEADS, DH, DH + 128), lambda b, j: (b, 0, 0, 0)),
        pl.BlockSpec((None, 8, 128), lambda b, j: (b, 0, 0)),
    ]
    in_specs = [
        tok_spec, pl.BlockSpec((None, D, n_tok), lambda b, j: (b, 0, tile(j))), tok_spec,
        pl.BlockSpec((None, 2, n_tok, 128), lambda b, j: (b, 0, tile(j), 0)),
        pl.BlockSpec((None, n_chunks, 8, ML_CHUNK), lambda b, j: (b, tile(j), 0, 0)),
    ]
    args = [q, kt, v, gv, g_rows]
    scratch = [
        pltpu.VMEM((HEADS, DH, DH + 128), F32),
        pltpu.VMEM((8, 128), F32),
    ]
    if state is None:
        mode = "ctx"
        out_specs, out_shape = state_specs, _STATE_SHAPES
    else:
        in_specs += state_specs
        args += list(state)
        scratch.append(pltpu.VMEM((n_tok, D), F32))
        if merge_with is None:
            mode = "fwd"
            out_specs, out_shape = tok_spec, jax.ShapeDtypeStruct((NB, seq, D), BF16)
        else:
            mode = "rev"
            h_fwd, xc, norm_g, skip = merge_with
            in_specs += [tok_spec, tok_spec, _const_spec((1, D)), _const_spec((1, D)),
                         _const_spec((BF16_ROWS * ML_GROUP_COLS, BF16_ROWS * ML_GROUP_COLS))]
            args += [h_fwd, xc, norm_g, skip, _swap_perm(BF16_ROWS, ML_GROUP_COLS)]
            out_specs = pl.BlockSpec((None, GRID_W, ML_GROUP_COLS, D), lambda b, j: (b, 0, tile(j), 0))
            out_shape = jax.ShapeDtypeStruct((NB, GRID_W, GRID_W, D), BF16)
    return pl.pallas_call(
        functools.partial(_ml_rec_kernel, reverse=reverse, n_tok=n_tok, mode=mode),
        grid=(NB, n_tiles),
        in_specs=in_specs,
        out_specs=out_specs,
        out_shape=out_shape,
        scratch_shapes=scratch,
        compiler_params=_params(("arbitrary", "arbitrary")),
        name=name,
    )(*args)


def _final_kernel(x_ref, hrg_ref, grg_ref, hml_ref, smlo_ref, sgr_ref, sgm_ref, g1_ref, sh2_ref, sc2_ref, g2_ref,
                  n2_ref, nf_ref, wbr_ref, wbm_ref, wo_ref, wfi_ref, wfo_ref, o_ref, act_scr):
    dot = functools.partial(jnp.dot, preferred_element_type=F32)
    y_rg = (hrg_ref[...].astype(F32) * grg_ref[...].astype(F32)).astype(BF16)
    y_ml = (hml_ref[...].astype(F32) * smlo_ref[...].astype(F32)).astype(BF16)
    mix = sgr_ref[...].astype(F32) * dot(y_rg, wbr_ref[...])
    mix = mix + sgm_ref[...].astype(F32) * dot(y_ml, wbm_ref[...])
    x1 = x_ref[...] + g1_ref[...] * dot(mix.astype(BF16), wo_ref[...])
    ms = jnp.mean(x1 * x1, axis=-1, keepdims=True)
    hn = x1 * lax.rsqrt(ms + EPS) * n2_ref[...]
    hb = (hn * (1.0 + sc2_ref[...]) + sh2_ref[...]).astype(BF16)
    step = 256
    for c in range(D_FF // step):
        gate = dot(hb, wfi_ref[:, c * step:(c + 1) * step])
        up = dot(hb, wfi_ref[:, D_FF + c * step:D_FF + (c + 1) * step])
        act_scr[:, c * step:(c + 1) * step] = (_silu(gate) * up).astype(BF16)
    x2 = x1 + g2_ref[...] * dot(act_scr[...], wfo_ref[...])
    ms2 = jnp.mean(x2 * x2, axis=-1, keepdims=True)
    o_ref[...] = x2 * lax.rsqrt(ms2 + EPS) * nf_ref[...]


def _final_call(x, h_rg, grg, h_ml, smlo, sgr, sgm, mod3, norm2_g, final_g, wbr, wbm, wo, wfi, wfo):
    rows = FINAL_ROWS
    row_spec = pl.BlockSpec((None, rows, D), lambda b, i: (b, i, 0))
    mod_spec = lambda g: pl.BlockSpec((None, 1, D), lambda b, i: (b, 0, g))
    return pl.pallas_call(
        _final_kernel,
        grid=(NB, SEQ // rows),
        in_specs=[
            row_spec, row_spec, row_spec, row_spec, row_spec, row_spec, row_spec,
            mod_spec(2), mod_spec(3), mod_spec(4), mod_spec(5),
            _const_spec((1, D)), _const_spec((1, D)),
            _const_spec((D, D)), _const_spec((D, D)), _const_spec((D, D)),
            _const_spec((D, 2 * D_FF)), _const_spec((D_FF, D)),
        ],
        out_specs=row_spec,
        out_shape=jax.ShapeDtypeStruct((NB, SEQ, D), F32),
        scratch_shapes=[pltpu.VMEM((rows, D_FF), BF16)],
        compiler_params=_params(("arbitrary", "arbitrary")),
        name="final",
    )(x, h_rg, grg, h_ml, smlo, sgr, sgm, mod3, mod3, mod3, mod3, norm2_g, final_g, wbr, wbm, wo, wfi, wfo)


def _pair_blockdiag(w):
    w = w.reshape(8, 2, 64, 64)
    z = jnp.zeros((8, 64, 64), w.dtype)
    top = jnp.concatenate([w[:, 0], z], axis=2)
    bot = jnp.concatenate([z, w[:, 1]], axis=2)
    return jnp.concatenate([top, bot], axis=1)


def _rg_weights(wa, ba, wx, bx):
    w = jnp.concatenate([_pair_blockdiag(wa), _pair_blockdiag(wx)], axis=2).astype(BF16)
    bias = 0.5 * jnp.concatenate([ba.reshape(8, 1, 128), bx.reshape(8, 1, 128)], axis=2)
    b_hi = bias.astype(BF16)
    b_lo = (bias - b_hi.astype(F32)).astype(BF16)
    return (jnp.concatenate([w, b_hi, b_lo, jnp.zeros((8, 126, 256), BF16)], axis=1),)


def _block_diagonals(w):
    rows = [jnp.pad(w[:, i, :], ((0, 0), (3 - i, 1 + i))) for i in range(4)]
    return jnp.stack(rows, axis=1).reshape(D, 8)


def _gate_weights(wi, bi, wf, bf):
    def lanes(x):
        gap = [(0, 0)] * (x.ndim - 2)
        fwd = jnp.pad(x[0], gap + [(0, ML_REV_LANE - HEADS)])
        rev = jnp.pad(x[1], gap + [(0, 128 - ML_REV_LANE - HEADS)])
        return jnp.concatenate([fwd, rev], axis=-1)
    w = jnp.concatenate([lanes(wi), lanes(wf)], axis=-1)
    b = jnp.concatenate([lanes(bi), lanes(bf)], axis=-1).reshape(1, 256)
    return w.reshape(3 * HEADS, DH, 256).astype(BF16), b


def kernel(x, c, ctx, c_ctx, w_mod, b_mod, norm1_g, norm2_g, w_in, rg_conv_w, rg_conv_b, rg_wa, rg_ba, rg_wx,
           rg_bx, rg_lambda, ml_conv_w, ml_conv_b, ml_wq, ml_wk, ml_wv, ml_wi, ml_bi, ml_wf, ml_bf,
           ml_norm_g, ml_skip, w_branch_rg, w_branch_ml, w_out, w_ffn_in, w_ffn_out, final_norm_g):
    mod = _mod_call(c, c_ctx, w_mod[0], b_mod[0])
    mod3 = mod.reshape(2 * NB, 1, 6 * D)
    w_in_bf = w_in[0].astype(BF16)
    norm1 = norm1_g[0].reshape(1, D)

    rgx, grg, mlx, smlo, sgr, sgm = _proj_call(x, mod3, norm1, w_in_bf, ctx=False)
    rgx_c, mlx_c = _proj_call(ctx, mod3, norm1, w_in_bf, ctx=True)

    rg_cw = rg_conv_w[0]
    rg_cb = rg_conv_b[0].reshape(1, D)
    zero_h = jnp.zeros((NB, D), F32)
    rg_w = [_rg_weights(rg_wa[0, d], rg_ba[0, d], rg_wx[0, d], rg_bx[0, d]) + (rg_lambda[0, d].reshape(1, D),)
            for d in range(2)]
    _, h0_f = _rg_call(rgx_c, zero_h, rg_cw, rg_cb, *rg_w[0], reverse=False, mode="ctx", name="rg_ctx_fwd")
    _, h0_r = _rg_call(rgx_c, zero_h, rg_cw, rg_cb, *rg_w[1], reverse=True, mode="ctx", name="rg_ctx_rev")
    h_f, rg_xh, _ = _rg_call(rgx, h0_f, rg_cw, rg_cb, *rg_w[0], reverse=False, mode="fwd", name="rg_fwd")
    h_rg, _ = _rg_call(rg_xh, h0_r, rg_cw, rg_cb, *rg_w[1], reverse=True, mode="rev", h_fwd=h_f, name="rg_rev")

    ml_cw = ml_conv_w[0]
    ml_cb = ml_conv_b[0].reshape(1, D)
    ml_qkv = (_block_diagonals(ml_wq[0]), _block_diagonals(ml_wk[0]), _block_diagonals(ml_wv[0]))
    ml_w = (ml_cw, ml_cb) + ml_qkv + _gate_weights(ml_wi[0], ml_bi[0], ml_wf[0], ml_bf[0])
    q_c, kt_c, v_c, _, gv_c, grf_c, grr_c = _ml_prep_call(mlx_c, ml_w, latent=False, name="ml_prep_ctx")
    st_f = _ml_rec_call(q_c, kt_c, v_c, gv_c, grf_c, reverse=False, name="ml_ctx_fwd")
    st_r = _ml_rec_call(q_c, kt_c, v_c, gv_c, grr_c, reverse=True, name="ml_ctx_rev")
    q_l, kt_l, v_l, xc_l, gv_l, grf_l, grr_l = _ml_prep_call(
        mlx.reshape(NB, GRID_W, GRID_W, D), ml_w, latent=True, name="ml_prep")
    hm_f = _ml_rec_call(q_l, kt_l, v_l, gv_l, grf_l, reverse=False, state=st_f, name="ml_fwd")
    h_ml = _ml_rec_call(q_l, kt_l, v_l, gv_l, grr_l, reverse=True, state=st_r,
                        merge_with=(hm_f, xc_l, ml_norm_g[0].reshape(1, D), ml_skip[0].reshape(1, D)), name="ml_rev")

    return _final_call(
        x, h_rg, grg, h_ml.reshape(NB, SEQ, D), smlo, sgr, sgm, mod3,
        norm2_g[0].reshape(1, D), final_norm_g.reshape(1, D),
        w_branch_rg[0].astype(BF16), w_branch_ml[0].astype(BF16), w_out[0].astype(BF16),
        w_ffn_in[0].astype(BF16), w_ffn_out[0].astype(BF16))
```

```python
import functools

import jax
import jax.numpy as jnp
from jax import lax
from jax.experimental import pallas as pl
from jax.experimental.pallas import tpu as pltpu

F32 = jnp.float32
BF16 = jnp.bfloat16

D = 1024
NB = 8
SEQ = 4096
GRID_W = 64
CTX = 256
EPS = 1e-6
RG_C = 8.0
HEADS = 4
DH = D // HEADS
D_FF = 2816
N_IN = 6 * D
LOG2E = 1.4426950408889634

VMEM_LIMIT = 60 * 1024 * 1024
BF16_ROWS = 16

PROJ_ROWS = 1024
FINAL_ROWS = 512
RG_STEPS = 128
RG_SUB = 256
RG_SUB_STEPS = RG_SUB // NB
RG_COEF = 128
ML_CHUNK = 256
ML_GROUP_COLS = 16
ML_TILE = ML_GROUP_COLS * GRID_W
ML_A_ROWS = 512

assert ML_CHUNK == DH


def _sigmoid(x):
    return 0.5 * (jnp.tanh(0.5 * x) + 1.0)


def _silu(x):
    half = 0.5 * x
    return half * (jnp.tanh(half) + 1.0)


def _softplus(x):
    return jnp.maximum(x, 0.0) + jnp.log(1.0 + jnp.exp(-jnp.abs(x)))


def _log_sigmoid(x):
    return jnp.minimum(x, 0.0) - jnp.log(1.0 + jnp.exp(-jnp.abs(x)))


def _split3(x):
    hi = x.astype(BF16)
    r1 = x - hi.astype(F32)
    mid = r1.astype(BF16)
    lo = (r1 - mid.astype(F32)).astype(BF16)
    return hi, mid, lo


def _params(sem):
    return pltpu.CompilerParams(dimension_semantics=sem, vmem_limit_bytes=VMEM_LIMIT)


def _const_spec(shape):
    nd = len(shape)
    return pl.BlockSpec(shape, lambda *_: (0,) * nd, pipeline_mode=pl.Buffered(1))


def _swap_perm(a, b):
    n = a * b
    out_row = jnp.arange(n)
    src = (out_row % a) * b + out_row // a
    return (src[:, None] == jnp.arange(n)[None, :]).astype(BF16)


def _mod_kernel(c_ref, cc_ref, w_ref, b_ref, o_ref):
    s = jnp.concatenate([c_ref[...], jnp.broadcast_to(cc_ref[...], (NB, D))], axis=0)
    s = _silu(s)
    s_hi, s_mid, _ = _split3(s)
    w_hi, w_mid, _ = _split3(w_ref[...])
    dot = functools.partial(jnp.dot, preferred_element_type=F32)
    o_ref[...] = dot(s_hi, w_hi) + dot(s_mid, w_hi) + dot(s_hi, w_mid) + b_ref[...]


def _mod_call(c, c_ctx, w_mod, b_mod):
    return pl.pallas_call(
        _mod_kernel,
        grid=(6,),
        in_specs=[
            pl.BlockSpec((NB, D), lambda g: (0, 0)),
            pl.BlockSpec((1, D), lambda g: (0, 0)),
            pl.BlockSpec((D, D), lambda g: (0, g)),
            pl.BlockSpec((1, D), lambda g: (0, g)),
        ],
        out_specs=pl.BlockSpec((2 * NB, D), lambda g: (0, g)),
        out_shape=jax.ShapeDtypeStruct((2 * NB, 6 * D), F32),
        compiler_params=_params(("arbitrary",)),
        name="mod",
    )(c, c_ctx.reshape(1, D), w_mod, b_mod.reshape(1, 6 * D))


def _gelu_tanh(x):
    return jax.nn.gelu(x, approximate=True)


def _identity(x):
    return x


_PROJ_FULL = ((0, _identity), (1, _gelu_tanh), (2, _identity), (3, _sigmoid), (4, _sigmoid), (5, _sigmoid))
_PROJ_CTX = ((0, _identity), (2, _identity))


def _proj_kernel(x_ref, sh_ref, sc_ref, g_ref, w_ref, *o_refs, groups):
    x = x_ref[...]
    ms = jnp.mean(x * x, axis=-1, keepdims=True)
    y = x * lax.rsqrt(ms + EPS) * g_ref[...]
    u = (y * (1.0 + sc_ref[...]) + sh_ref[...]).astype(BF16)
    for o_ref, (g, act) in zip(o_refs, groups):
        p = jnp.dot(u, w_ref[:, g * D:(g + 1) * D], preferred_element_type=F32)
        o_ref[...] = act(p).astype(o_ref.dtype)


def _proj_call(x, mod3, norm_g, w_in_bf, *, ctx):
    L = x.shape[1]
    rows = min(PROJ_ROWS, L)
    groups = _PROJ_CTX if ctx else _PROJ_FULL
    mod_row = (lambda b: NB) if ctx else (lambda b: b)
    row_spec = pl.BlockSpec((None, rows, D), lambda b, i: (b, i, 0))
    return pl.pallas_call(
        functools.partial(_proj_kernel, groups=groups),
        grid=(NB, L // rows),
        in_specs=[
            row_spec,
            pl.BlockSpec((None, 1, D), lambda b, i: (mod_row(b), 0, 0)),
            pl.BlockSpec((None, 1, D), lambda b, i: (mod_row(b), 0, 1)),
            _const_spec((1, D)),
            _const_spec((D, N_IN)),
        ],
        out_specs=[row_spec] * len(groups),
        out_shape=[jax.ShapeDtypeStruct((NB, L, D), BF16)] * len(groups),
        compiler_params=_params(("arbitrary", "arbitrary")),
        name="proj_ctx" if ctx else "proj",
    )(x, mod3, mod3, norm_g, w_in_bf)


def _rg_kernel(*refs, reverse, n_tiles, mode):
    if mode == "rev":
        (xh_ref, h0_ref, wbd_ref, lam_ref, hf_ref, pout_ref, o_ref, hlast_ref, a_scr, b_scr, h_scr) = refs
    elif mode == "fwd":
        (x_ref, xp_ref, xn_ref, h0_ref, cw_ref, cb_ref, wbd_ref, lam_ref, pin_ref, phalo_ref,
         o_ref, xh_out, hlast_ref, xe_scr, a_scr, b_scr, h_scr) = refs
    else:
        (x_ref, xp_ref, xn_ref, h0_ref, cw_ref, cb_ref, wbd_ref, lam_ref, pin_ref, phalo_ref,
         o_ref, hlast_ref, xe_scr, a_scr, b_scr, h_scr) = refs
    step = pl.program_id(0)
    order = jnp.minimum(step, n_tiles - 1)
    tile = (n_tiles - 1 - order) if reverse else order
    new = step % 2
    old = 1 - new
    rows = RG_STEPS * NB
    halo = 2 * NB

    @pl.when(step == 0)
    def _():
        h_scr[...] = h0_ref[...]
        a_scr[1] = jnp.ones((rows, D), F32)
        b_scr[1] = jnp.zeros((rows, D), F32)

    if mode != "rev":
        def halo_rows(ref):
            xb = jnp.concatenate([ref[b] for b in range(NB)], axis=0)
            return jnp.dot(phalo_ref[...], xb, preferred_element_type=F32)

        xe_scr[0:halo, :] = jnp.where(tile > 0, halo_rows(xp_ref)[(BF16_ROWS - 2) * NB:BF16_ROWS * NB], 0.0)
        xe_scr[halo + rows:2 * halo + rows, :] = jnp.where(tile < n_tiles - 1, halo_rows(xn_ref)[0:halo], 0.0)
        for s in range(rows // RG_SUB):
            t0 = s * RG_SUB_STEPS
            xb = jnp.concatenate([x_ref[b, t0:t0 + RG_SUB_STEPS, :] for b in range(NB)], axis=0)
            xe_scr[halo + s * RG_SUB:halo + (s + 1) * RG_SUB, :] = jnp.dot(
                pin_ref[...], xb, preferred_element_type=F32)
        cw = 0.5 * cw_ref[...]
        cb = 0.5 * cb_ref[...]

    lam2 = (-0.5 * RG_C * LOG2E) * _softplus(-lam_ref[...])

    ones = jnp.ones((RG_COEF, 128), BF16)

    for s in range(rows // RG_COEF):
        r0 = s * RG_COEF
        if mode == "rev":
            xh = xh_ref[r0:r0 + RG_COEF, :].astype(F32)
        else:
            xh = cb + cw[0:1] * xe_scr[r0:r0 + RG_COEF, :]
            xh = xh + cw[1:2] * xe_scr[r0 + NB:r0 + NB + RG_COEF, :]
            xh = xh + cw[2:3] * xe_scr[r0 + 2 * NB:r0 + 2 * NB + RG_COEF, :]
            xh = xh + cw[3:4] * xe_scr[r0 + 3 * NB:r0 + 3 * NB + RG_COEF, :]
        if mode == "fwd":
            xh_out[r0:r0 + RG_COEF, :] = xh.astype(BF16)
        for p in range(D // 128):
            lo, hi = p * 128, (p + 1) * 128
            xhp = xh[:, lo:hi]
            lhs = jnp.concatenate([xhp.astype(BF16), ones], axis=1)
            pre = jnp.dot(lhs, wbd_ref[p], preferred_element_type=F32)
            t_r = jnp.tanh(pre[:, 0:128])
            t_i = jnp.tanh(pre[:, 128:256])
            a = jnp.exp2(lam2[:, lo:hi] * (t_r + 1.0))
            y = (1.0 - a) * (1.0 + a)
            gain = jnp.where(y > 0.0, y * lax.rsqrt(y), 0.0)
            a_scr[new, r0:r0 + RG_COEF, lo:hi] = a
            b_scr[new, r0:r0 + RG_COEF, lo:hi] = gain * ((t_i + 1.0) * xhp)

    h = h_scr[...]
    for t in range(RG_STEPS):
        r0 = ((RG_STEPS - 1 - t) if reverse else t) * NB
        h = a_scr[old, r0:r0 + NB, :] * h + b_scr[old, r0:r0 + NB, :]
        b_scr[old, r0:r0 + NB, :] = h
    h_scr[...] = h
    hlast_ref[...] = h

    if mode == "rev":
        for s in range(rows // RG_SUB):
            r0 = s * RG_SUB
            t0 = s * RG_SUB_STEPS
            hs = (b_scr[old, r0:r0 + RG_SUB, :] + hf_ref[r0:r0 + RG_SUB, :].astype(F32)).astype(BF16)
            hb = jnp.dot(pout_ref[...], hs, preferred_element_type=F32).astype(o_ref.dtype)
            for b in range(NB):
                o_ref[b, t0:t0 + RG_SUB_STEPS, :] = hb[b * RG_SUB_STEPS:(b + 1) * RG_SUB_STEPS]
    else:
        o_ref[...] = b_scr[old].astype(o_ref.dtype)


def _rg_call(src, h0, cw, cb, wbd, lam, *, reverse, mode, h_fwd=None, name):
    L = src.shape[0] // NB if mode == "rev" else src.shape[1]
    rows = RG_STEPS * NB
    n_tiles = L // RG_STEPS
    per = RG_STEPS // BF16_ROWS
    n_halo = L // BF16_ROWS
    def tile_at(lag):
        def tile(i):
            order = jnp.clip(i - lag, 0, n_tiles - 1)
            return (n_tiles - 1 - order) if reverse else order
        return tile

    tile, tile_old = tile_at(0), tile_at(1)
    bm_spec = pl.BlockSpec((NB, RG_STEPS, D), lambda i: (0, tile(i), 0))
    tm_spec = pl.BlockSpec((rows, D), lambda i: (tile(i), 0))
    tm_old_spec = pl.BlockSpec((rows, D), lambda i: (tile_old(i), 0))
    tm_shape = jax.ShapeDtypeStruct((L * NB, D), BF16)
    state_spec = pl.BlockSpec((NB, D), lambda i: (0, 0))
    state_shape = jax.ShapeDtypeStruct((NB, D), F32)
    scratch = [pltpu.VMEM((2, rows, D), F32), pltpu.VMEM((2, rows, D), F32), pltpu.VMEM((NB, D), F32)]
    if mode == "rev":
        in_specs = [tm_spec, _const_spec((NB, D)), _const_spec((D // 128, 256, 256)), _const_spec((1, D)),
                    tm_old_spec, _const_spec((RG_SUB, RG_SUB))]
        args = [src, h0, wbd, lam, h_fwd, _swap_perm(RG_SUB_STEPS, NB)]
        out_specs = [pl.BlockSpec((NB, RG_STEPS, D), lambda i: (0, tile_old(i), 0)), state_spec]
        out_shape = [jax.ShapeDtypeStruct((NB, L, D), BF16), state_shape]
    else:
        in_specs = [
            bm_spec,
            pl.BlockSpec((NB, BF16_ROWS, D), lambda i: (0, jnp.maximum(tile(i) * per - 1, 0), 0)),
            pl.BlockSpec((NB, BF16_ROWS, D), lambda i: (0, jnp.minimum((tile(i) + 1) * per, n_halo - 1), 0)),
            _const_spec((NB, D)),
            _const_spec((4, D)),
            _const_spec((1, D)),
            _const_spec((D // 128, 256, 256)),
            _const_spec((1, D)),
            _const_spec((RG_SUB, RG_SUB)),
            _const_spec((NB * BF16_ROWS, NB * BF16_ROWS)),
        ]
        args = [src, src, src, h0, cw, cb, wbd, lam, _swap_perm(NB, RG_SUB_STEPS), _swap_perm(NB, BF16_ROWS)]
        out_specs = [tm_old_spec] + ([tm_spec] if mode == "fwd" else []) + [state_spec]
        out_shape = [tm_shape] + ([tm_shape] if mode == "fwd" else []) + [state_shape]
        scratch = [pltpu.VMEM((rows + 4 * NB, D), F32)] + scratch
    return pl.pallas_call(
        functools.partial(_rg_kernel, reverse=reverse, n_tiles=n_tiles, mode=mode),
        grid=(n_tiles + 1,),
        in_specs=in_specs,
        out_specs=out_specs,
        out_shape=out_shape,
        scratch_shapes=scratch,
        compiler_params=_params(("arbitrary",)),
        name=name,
    )(*args)


ML_REV_LANE = 8


def _scan_lanes_both(x, op, fill):
    n = x.shape[1]
    lane = lax.broadcasted_iota(jnp.int32, x.shape, 1)
    is_prefix = lax.broadcasted_iota(jnp.int32, x.shape, 0) < ML_REV_LANE
    sh = 1
    while sh < n:
        before = jnp.where(lane >= sh, pltpu.roll(x, sh, axis=1), fill)
        after = jnp.where(lane < n - sh, pltpu.roll(x, n - sh, axis=1), fill)
        x = op(x, jnp.where(is_prefix, before, after))
        sh *= 2
    return x


def _ml_prep_kernel(*refs, n_tiles, n_tok, latent):
    L = ML_CHUNK
    n_chunks = n_tok // L
    if latent:
        (x_ref, xp_ref, xn_ref, perm_ref, cw_ref, cb_ref, dq_ref, dk_ref, dv_ref, wg_ref, bg_ref,
         q_ref, kt_ref, v_ref, xc_ref, gv_ref, grf_ref, grr_ref,
         xt_scr, wqg_scr, wv_scr, wkt_scr, g_scr) = refs
        step = pl.program_id(0)
        tile = jnp.minimum(step, NB * n_tiles - 1) % n_tiles
        first = step == 0
        slot_new = step % 2
        slot_old = 1 - slot_new

        @pl.when(first)
        def _():
            g_scr[1] = jnp.zeros(g_scr.shape[1:], F32)
    else:
        (x_ref, cw_ref, cb_ref, dq_ref, dk_ref, dv_ref, wg_ref, bg_ref,
         q_ref, kt_ref, v_ref, xc_ref, gv_ref, grf_ref, grr_ref, xt_scr, wqg_scr, wv_scr, wkt_scr) = refs
        first = pl.program_id(0) == 0

    src_t = lax.broadcasted_iota(jnp.int32, (L, L), 0)
    dst_t = lax.broadcasted_iota(jnp.int32, (L, L), 1)
    tri_fwd = (src_t <= dst_t).astype(BF16)
    tri_rev = (src_t >= dst_t).astype(BF16)

    def gate_vectors(gates, ck):
        r0 = ck * L
        li_rows = gates[:, 0:128].T[0:2 * ML_REV_LANE]
        lf_rows = _log_sigmoid(gates[:, 128:256].T[0:2 * ML_REV_LANE])
        split = jnp.concatenate(_split3(lf_rows), axis=0)
        to_t = jnp.dot(split, tri_fwd, preferred_element_type=F32)
        from_t = jnp.dot(split, tri_rev, preferred_element_type=F32)
        cum_rows = jnp.concatenate([
            to_t[0:8] + to_t[BF16_ROWS:BF16_ROWS + 8] + to_t[2 * BF16_ROWS:2 * BF16_ROWS + 8],
            from_t[8:16] + from_t[BF16_ROWS + 8:2 * BF16_ROWS] + from_t[2 * BF16_ROWS + 8:3 * BF16_ROWS]], axis=0)
        g_rows = li_rows - cum_rows
        grf_ref[ck] = g_rows[0:ML_REV_LANE]
        grr_ref[ck] = g_rows[ML_REV_LANE:2 * ML_REV_LANE]
        pm_rows = _scan_lanes_both(g_rows, jnp.maximum, -jnp.inf)
        pad = jnp.zeros((128 - 2 * ML_REV_LANE, L), F32)
        gv_ref[0, r0:r0 + L, :] = jnp.concatenate([cum_rows, pad], axis=0).T
        gv_ref[1, r0:r0 + L, :] = jnp.concatenate([pm_rows, pad], axis=0).T

    @pl.when(first)
    def _():
        diff = lax.broadcasted_iota(jnp.int32, (DH, DH), 1) - lax.broadcasted_iota(jnp.int32, (DH, DH), 0)

        def dense(d_ref, h):
            diag = d_ref[h * DH:(h + 1) * DH, :]
            out = jnp.zeros((DH, DH), F32)
            for d in range(-3, 4):
                out = jnp.where(diff == d, diag[:, 3 + d:4 + d], out)
            return out

        dot = functools.partial(jnp.dot, preferred_element_type=F32)
        for h in range(HEADS):
            wq = dense(dq_ref, h).astype(BF16)
            wk_f32 = dense(dk_ref, h) * (DH ** -0.5)
            wk = wk_f32.astype(BF16)
            wv = dense(dv_ref, h).astype(BF16)
            wkt_scr[h] = wk_f32.T.astype(BF16)
            wqg_scr[h, :, 0:DH] = wq
            wqg_scr[h, :, DH:2 * DH] = (dot(wq, wg_ref[h]) + dot(wk, wg_ref[HEADS + h])).astype(BF16)
            wv_scr[h, :, 0:DH] = wv
            wv_scr[h, :, DH:2 * DH] = dot(wv, wg_ref[2 * HEADS + h]).astype(BF16)

    if not latent:
        xt_scr[8:16, :] = jnp.zeros((8, D), F32)
        xt_scr[16:16 + n_tok, :] = x_ref[...].astype(F32)
        xt_scr[16 + n_tok:24 + n_tok, :] = jnp.zeros((8, D), F32)
    else:
        for g in range(GRID_W // BF16_ROWS):
            xg = x_ref[g * BF16_ROWS:(g + 1) * BF16_ROWS].reshape(BF16_ROWS * ML_GROUP_COLS, D)
            yg = jnp.dot(perm_ref[...], xg, preferred_element_type=F32)
            for w in range(ML_GROUP_COLS):
                dst = 16 + w * GRID_W + g * BF16_ROWS
                xt_scr[dst:dst + BF16_ROWS, :] = yg[w * BF16_ROWS:(w + 1) * BF16_ROWS]
        last = ML_GROUP_COLS - 1
        prev = jnp.concatenate([xp_ref[0].astype(F32)[last:last + 1], xp_ref[1].astype(F32)[last:last + 1]], axis=0)
        xt_scr[14:16, :] = jnp.where(tile > 0, prev, 0.0)
        xt_scr[16 + n_tok:17 + n_tok, :] = jnp.where(tile < n_tiles - 1, xn_ref[0].astype(F32)[0:1], 0.0)

    if latent:
        for ck in range(n_chunks):
            gate_vectors(g_scr[slot_old, ck * L:(ck + 1) * L, :], ck)

    cw = cw_ref[...]
    for ck in range(n_chunks):
        r0 = ck * L
        xt = xt_scr[16 + r0:16 + r0 + L, :]
        xc = cb_ref[...] + cw[0:1] * xt_scr[14 + r0:14 + r0 + L, :] + cw[1:2] * xt_scr[15 + r0:15 + r0 + L, :]
        xc = xc + cw[2:3] * xt + cw[3:4] * xt_scr[17 + r0:17 + r0 + L, :]
        xc = _silu(xc)
        xc_ref[r0:r0 + L, :] = xc.astype(BF16)
        gates = jnp.zeros((L, 256), F32) + bg_ref[...]
        for h in range(HEADS):
            lo, hi = h * DH, (h + 1) * DH
            xc_h = xc[:, lo:hi].astype(BF16)
            qg = jnp.dot(xc_h, wqg_scr[h], preferred_element_type=F32)
            vg = jnp.dot(xt[:, lo:hi].astype(BF16), wv_scr[h], preferred_element_type=F32)
            gates = gates + qg[:, DH:2 * DH] + vg[:, DH:2 * DH]
            q_ref[r0:r0 + L, lo:hi] = qg[:, 0:DH].astype(BF16)
            kt = lax.dot_general(wkt_scr[h], xc_h, (((1,), (1,)), ((), ())), preferred_element_type=F32)
            kt_ref[lo:hi, r0:r0 + L] = kt.astype(BF16)
            v_ref[r0:r0 + L, lo:hi] = vg[:, 0:DH].astype(BF16)
        if latent:
            g_scr[slot_new, r0:r0 + L, :] = gates
        else:
            gate_vectors(gates, ck)


def _ml_rec_kernel(*refs, reverse, n_tok, mode):
    L = ML_CHUNK
    n_chunks = n_tok // L
    lane0 = ML_REV_LANE if reverse else 0
    if mode == "ctx":
        (q_ref, kt_ref, v_ref, gv_ref, gr_ref, c_out, m_out, c_scr, m_scr) = refs
    elif mode == "fwd":
        (q_ref, kt_ref, v_ref, gv_ref, gr_ref, c0_ref, m0_ref, o_ref, c_scr, m_scr, h_scr) = refs
    else:
        (q_ref, kt_ref, v_ref, gv_ref, gr_ref, c0_ref, m0_ref, hf_ref, xc_ref, ng_ref, sk_ref,
         perm_ref, o_ref, c_scr, m_scr, h_scr) = refs

    @pl.when(pl.program_id(1) == 0)
    def _():
        if mode == "ctx":
            c_scr[...] = jnp.zeros_like(c_scr)
            m_scr[...] = jnp.zeros_like(m_scr)
        else:
            c_scr[...] = c0_ref[...]
            m_scr[...] = m0_ref[...]

    ones_lanes = jnp.ones((L, 128), BF16)
    if mode != "ctx":
        row_id = lax.broadcasted_iota(jnp.int32, (L, L), 0)
        col_id = lax.broadcasted_iota(jnp.int32, (L, L), 1)
        keep = (col_id >= row_id) if reverse else (col_id <= row_id)

    def stage_b(ck):
        r0 = ck * L
        m_prev = m_scr[0:1, :]
        edge = 0 if reverse else L - 1
        b_tot = gv_ref[0, r0 + edge:r0 + edge + 1, :]
        gmax = gv_ref[1, r0 + edge:r0 + edge + 1, :]
        m_new = jnp.maximum(b_tot + m_prev, b_tot + gmax)
        dec = jnp.exp(b_tot + m_prev - m_new)
        if mode != "ctx":
            cum = gv_ref[0, r0:r0 + L, :]
            inter = cum + m_prev
            m_t = jnp.maximum(inter, cum + gv_ref[1, r0:r0 + L, :])
            c_all = cum - m_t
            si_all = jnp.exp(inter - m_t)
            fl_all = jnp.exp(-m_t)
            g_row = gr_ref[ck]
        for h in range(HEADS):
            lo, hi = h * DH, (h + 1) * DH
            ln = lane0 + h
            q = q_ref[r0:r0 + L, lo:hi]
            kt = kt_ref[lo:hi, r0:r0 + L]
            v = jnp.concatenate([v_ref[r0:r0 + L, lo:hi], ones_lanes], axis=1)
            dec_h = dec[:, ln:ln + 1]
            c_old = c_scr[h]
            if mode != "ctx":
                si_c = si_all[:, ln:ln + 1]
                s = jnp.dot(q, kt, preferred_element_type=F32)
                arg = jnp.where(keep, c_all[:, ln:ln + 1] + g_row[h:h + 1, :], -jnp.inf)
                p = (s * jnp.exp(arg)).astype(BF16)
                both = jnp.dot(p, v, preferred_element_type=F32)
                both = both + si_c * jnp.dot(q, c_old.astype(BF16), preferred_element_type=F32)
                scale = 1.0 / jnp.maximum(jnp.abs(both[:, DH:DH + 128]), fl_all[:, ln:ln + 1])
                h_scr[r0:r0 + L, lo:hi] = both[:, 0:DH] * jnp.concatenate([scale, scale], axis=1)
            ws_row = jnp.exp((b_tot[:, ln:ln + 1] - m_new[:, ln:ln + 1]) + gr_ref[ck][h:h + 1, :]).astype(BF16)
            kw_t = kt * ws_row
            c_scr[h] = dec_h * c_old + jnp.dot(kw_t, v, preferred_element_type=F32)
        m_scr[...] = jnp.broadcast_to(m_new, m_scr.shape)

    for ci in range(n_chunks):
        stage_b((n_chunks - 1 - ci) if reverse else ci)

    a_rows = min(ML_A_ROWS, n_tok)
    if mode == "ctx":
        c_out[...] = c_scr[...]
        m_out[...] = m_scr[...]
    elif mode == "fwd":
        o_ref[...] = h_scr[...].astype(o_ref.dtype)
    else:
        for sb in range(n_tok // a_rows):
            r0 = sb * a_rows
            ht = h_scr[r0:r0 + a_rows, :] + hf_ref[r0:r0 + a_rows, :].astype(F32)
            for h in range(HEADS):
                lo, hi = h * DH, (h + 1) * DH
                hh = ht[:, lo:hi]
                mu = jnp.mean(hh, axis=-1, keepdims=True)
                var = jnp.mean(jnp.square(hh - mu), axis=-1, keepdims=True)
                hn = (hh - mu) * lax.rsqrt(var + EPS)
                xc = xc_ref[r0:r0 + a_rows, lo:hi].astype(F32)
                h_scr[r0:r0 + a_rows, lo:hi] = hn * ng_ref[:, lo:hi] + sk_ref[:, lo:hi] * xc
        for g in range(GRID_W // BF16_ROWS):
            zg = jnp.concatenate(
                [h_scr[w * GRID_W + g * BF16_ROWS:w * GRID_W + (g + 1) * BF16_ROWS, :] for w in range(ML_GROUP_COLS)],
                axis=0).astype(BF16)
            og = jnp.dot(perm_ref[...], zg, preferred_element_type=F32).astype(o_ref.dtype)
            o_ref[g * BF16_ROWS:(g + 1) * BF16_ROWS] = og.reshape(BF16_ROWS, ML_GROUP_COLS, D)


def _ml_weight_specs():
    return [
        _const_spec((4, D)),
        _const_spec((1, D)),
        _const_spec((D, 8)),
        _const_spec((D, 8)),
        _const_spec((D, 8)),
        _const_spec((3 * HEADS, DH, 256)),
        _const_spec((1, 256)),
    ]


def _ml_prep_call(mlx, weights, *, latent, name):
    scratch = [
        pltpu.VMEM(((ML_TILE if latent else CTX) + 32, D), F32),
        pltpu.VMEM((HEADS, DH, 2 * DH), BF16),
        pltpu.VMEM((HEADS, DH, 2 * DH), BF16),
        pltpu.VMEM((HEADS, DH, DH), BF16),
    ]
    if latent:
        n_tiles, n_tok, seq = GRID_W // ML_GROUP_COLS, ML_TILE, SEQ
        n_flat = NB * n_tiles
        grid = (n_flat + 1,)
        sem = ("arbitrary",)

        def at(lag):
            def split(step):
                flat = jnp.clip(step - lag, 0, n_flat - 1)
                return flat // n_tiles, flat % n_tiles
            return split

        new, old = at(0), at(1)
        in_specs = [
            pl.BlockSpec((None, GRID_W, ML_GROUP_COLS, D), lambda g: (new(g)[0], 0, new(g)[1], 0)),
            pl.BlockSpec((None, 2, ML_GROUP_COLS, D),
                         lambda g: (new(g)[0], GRID_W // 2 - 1, jnp.maximum(new(g)[1] - 1, 0), 0)),
            pl.BlockSpec((None, 1, ML_GROUP_COLS, D),
                         lambda g: (new(g)[0], 0, jnp.minimum(new(g)[1] + 1, n_tiles - 1), 0)),
            _const_spec((BF16_ROWS * ML_GROUP_COLS, BF16_ROWS * ML_GROUP_COLS)),
        ]
        args = [mlx, mlx, mlx, _swap_perm(BF16_ROWS, ML_GROUP_COLS)]
        scratch.append(pltpu.VMEM((2, n_tok, 256), F32))
        tok_spec = pl.BlockSpec((None, n_tok, D), lambda g: (new(g)[0], new(g)[1], 0))
        kt_spec = pl.BlockSpec((None, D, n_tok), lambda g: (new(g)[0], 0, new(g)[1]))
        gv_spec = pl.BlockSpec((None, 2, n_tok, 128), lambda g: (old(g)[0], 0, old(g)[1], 0))
        row_spec = pl.BlockSpec((None, n_tok // ML_CHUNK, 8, ML_CHUNK), lambda g: (old(g)[0], old(g)[1], 0, 0))
    else:
        n_tiles, n_tok, seq = 1, CTX, CTX
        grid = (NB, 1)
        sem = ("arbitrary", "arbitrary")
        in_specs = [pl.BlockSpec((None, CTX, D), lambda b, j: (b, 0, 0))]
        args = [mlx]
        tok_spec = pl.BlockSpec((None, n_tok, D), lambda b, j: (b, 0, 0))
        kt_spec = pl.BlockSpec((None, D, n_tok), lambda b, j: (b, 0, 0))
        gv_spec = pl.BlockSpec((None, 2, n_tok, 128), lambda b, j: (b, 0, 0, 0))
        row_spec = pl.BlockSpec((None, n_tok // ML_CHUNK, 8, ML_CHUNK), lambda b, j: (b, 0, 0, 0))
    tok_shape = jax.ShapeDtypeStruct((NB, seq, D), BF16)
    row_shape = jax.ShapeDtypeStruct((NB, seq // ML_CHUNK, 8, ML_CHUNK), F32)
    return pl.pallas_call(
        functools.partial(_ml_prep_kernel, n_tiles=n_tiles, n_tok=n_tok, latent=latent),
        grid=grid,
        in_specs=in_specs + _ml_weight_specs(),
        out_specs=[tok_spec, kt_spec, tok_spec, tok_spec, gv_spec, row_spec, row_spec],
        out_shape=[tok_shape, jax.ShapeDtypeStruct((NB, D, seq), BF16), tok_shape, tok_shape,
                   jax.ShapeDtypeStruct((NB, 2, seq, 128), F32), row_shape, row_shape],
        scratch_shapes=scratch,
        compiler_params=_params(sem),
        name=name,
    )(*args, *weights)


_STATE_SHAPES = [
    jax.ShapeDtypeStruct((NB, HEADS, DH, DH + 128), F32),
    jax.ShapeDtypeStruct((NB, 8, 128), F32),
]


def _ml_rec_call(q, kt, v, gv, g_rows, *, reverse, state=None, merge_with=None, name):
    seq = q.shape[1]
    n_tok = min(ML_TILE, seq)
    n_tiles = seq // n_tok
    n_chunks = n_tok // ML_CHUNK
    tile = (lambda j: n_tiles - 1 - j) if reverse else (lambda j: j)
    tok_spec = pl.BlockSpec((None, n_tok, D), lambda b, j: (b, tile(j), 0))
    state_specs = [
        pl.BlockSpec((None, HEADS, DH, DH + 128), lambda b, j: (b, 0, 0, 0)),
        pl.BlockSpec((None, 8, 128), lambda b, j: (b, 0, 0)),
    ]
    in_specs = [
        tok_spec, pl.BlockSpec((None, D, n_tok), lambda b, j: (b, 0, tile(j))), tok_spec,
        pl.BlockSpec((None, 2, n_tok, 128), lambda b, j: (b, 0, tile(j), 0)),
        pl.BlockSpec((None, n_chunks, 8, ML_CHUNK), lambda b, j: (b, tile(j), 0, 0)),
    ]
    args = [q, kt, v, gv, g_rows]
    scratch = [
        pltpu.VMEM((HEADS, DH, DH + 128), F32),
        pltpu.VMEM((8, 128), F32),
    ]
    if state is None:
        mode = "ctx"
        out_specs, out_shape = state_specs, _STATE_SHAPES
    else:
        in_specs += state_specs
        args += list(state)
        scratch.append(pltpu.VMEM((n_tok, D), F32))
        if merge_with is None:
            mode = "fwd"
            out_specs, out_shape = tok_spec, jax.ShapeDtypeStruct((NB, seq, D), BF16)
        else:
            mode = "rev"
            h_fwd, xc, norm_g, skip = merge_with
            in_specs += [tok_spec, tok_spec, _const_spec((1, D)), _const_spec((1, D)),
                         _const_spec((BF16_ROWS * ML_GROUP_COLS, BF16_ROWS * ML_GROUP_COLS))]
            args += [h_fwd, xc, norm_g, skip, _swap_perm(BF16_ROWS, ML_GROUP_COLS)]
            out_specs = pl.BlockSpec((None, GRID_W, ML_GROUP_COLS, D), lambda b, j: (b, 0, tile(j), 0))
            out_shape = jax.ShapeDtypeStruct((NB, GRID_W, GRID_W, D), BF16)
    return pl.pallas_call(
        functools.partial(_ml_rec_kernel, reverse=reverse, n_tok=n_tok, mode=mode),
        grid=(NB, n_tiles),
        in_specs=in_specs,
        out_specs=out_specs,
        out_shape=out_shape,
        scratch_shapes=scratch,
        compiler_params=_params(("arbitrary", "arbitrary")),
        name=name,
    )(*args)


def _final_kernel(x_ref, hrg_ref, grg_ref, hml_ref, smlo_ref, sgr_ref, sgm_ref, g1_ref, sh2_ref, sc2_ref, g2_ref,
                  n2_ref, nf_ref, wbr_ref, wbm_ref, wo_ref, wfi_ref, wfo_ref, o_ref, act_scr):
    dot = functools.partial(jnp.dot, preferred_element_type=F32)
    y_rg = (hrg_ref[...].astype(F32) * grg_ref[...].astype(F32)).astype(BF16)
    y_ml = (hml_ref[...].astype(F32) * smlo_ref[...].astype(F32)).astype(BF16)
    mix = sgr_ref[...].astype(F32) * dot(y_rg, wbr_ref[...])
    mix = mix + sgm_ref[...].astype(F32) * dot(y_ml, wbm_ref[...])
    x1 = x_ref[...] + g1_ref[...] * dot(mix.astype(BF16), wo_ref[...])
    ms = jnp.mean(x1 * x1, axis=-1, keepdims=True)
    hn = x1 * lax.rsqrt(ms + EPS) * n2_ref[...]
    hb = (hn * (1.0 + sc2_ref[...]) + sh2_ref[...]).astype(BF16)
    step = 256
    for c in range(D_FF // step):
        gate = dot(hb, wfi_ref[:, c * step:(c + 1) * step])
        up = dot(hb, wfi_ref[:, D_FF + c * step:D_FF + (c + 1) * step])
        act_scr[:, c * step:(c + 1) * step] = (_silu(gate) * up).astype(BF16)
    x2 = x1 + g2_ref[...] * dot(act_scr[...], wfo_ref[...])
    ms2 = jnp.mean(x2 * x2, axis=-1, keepdims=True)
    o_ref[...] = x2 * lax.rsqrt(ms2 + EPS) * nf_ref[...]


def _final_call(x, h_rg, grg, h_ml, smlo, sgr, sgm, mod3, norm2_g, final_g, wbr, wbm, wo, wfi, wfo):
    rows = FINAL_ROWS
    row_spec = pl.BlockSpec((None, rows, D), lambda b, i: (b, i, 0))
    mod_spec = lambda g: pl.BlockSpec((None, 1, D), lambda b, i: (b, 0, g))
    return pl.pallas_call(
        _final_kernel,
        grid=(NB, SEQ // rows),
        in_specs=[
            row_spec, row_spec, row_spec, row_spec, row_spec, row_spec, row_spec,
            mod_spec(2), mod_spec(3), mod_spec(4), mod_spec(5),
            _const_spec((1, D)), _const_spec((1, D)),
            _const_spec((D, D)), _const_spec((D, D)), _const_spec((D, D)),
            _const_spec((D, 2 * D_FF)), _const_spec((D_FF, D)),
        ],
        out_specs=row_spec,
        out_shape=jax.ShapeDtypeStruct((NB, SEQ, D), F32),
        scratch_shapes=[pltpu.VMEM((rows, D_FF), BF16)],
        compiler_params=_params(("arbitrary", "arbitrary")),
        name="final",
    )(x, h_rg, grg, h_ml, smlo, sgr, sgm, mod3, mod3, mod3, mod3, norm2_g, final_g, wbr, wbm, wo, wfi, wfo)


def _pair_blockdiag(w):
    w = w.reshape(8, 2, 64, 64)
    z = jnp.zeros((8, 64, 64), w.dtype)
    top = jnp.concatenate([w[:, 0], z], axis=2)
    bot = jnp.concatenate([z, w[:, 1]], axis=2)
    return jnp.concatenate([top, bot], axis=1)


def _rg_weights(wa, ba, wx, bx):
    w = jnp.concatenate([_pair_blockdiag(wa), _pair_blockdiag(wx)], axis=2).astype(BF16)
    bias = 0.5 * jnp.concatenate([ba.reshape(8, 1, 128), bx.reshape(8, 1, 128)], axis=2)
    b_hi = bias.astype(BF16)
    b_lo = (bias - b_hi.astype(F32)).astype(BF16)
    return (jnp.concatenate([w, b_hi, b_lo, jnp.zeros((8, 126, 256), BF16)], axis=1),)


def _block_diagonals(w):
    rows = [jnp.pad(w[:, i, :], ((0, 0), (3 - i, 1 + i))) for i in range(4)]
    return jnp.stack(rows, axis=1).reshape(D, 8)


def _gate_weights(wi, bi, wf, bf):
    def lanes(x):
        gap = [(0, 0)] * (x.ndim - 2)
        fwd = jnp.pad(x[0], gap + [(0, ML_REV_LANE - HEADS)])
        rev = jnp.pad(x[1], gap + [(0, 128 - ML_REV_LANE - HEADS)])
        return jnp.concatenate([fwd, rev], axis=-1)
    w = jnp.concatenate([lanes(wi), lanes(wf)], axis=-1)
    b = jnp.concatenate([lanes(bi), lanes(bf)], axis=-1).reshape(1, 256)
    return w.reshape(3 * HEADS, DH, 256).astype(BF16), b


def kernel(x, c, ctx, c_ctx, w_mod, b_mod, norm1_g, norm2_g, w_in, rg_conv_w, rg_conv_b, rg_wa, rg_ba, rg_wx,
           rg_bx, rg_lambda, ml_conv_w, ml_conv_b, ml_wq, ml_wk, ml_wv, ml_wi, ml_bi, ml_wf, ml_bf,
           ml_norm_g, ml_skip, w_branch_rg, w_branch_ml, w_out, w_ffn_in, w_ffn_out, final_norm_g):
    mod = _mod_call(c, c_ctx, w_mod[0], b_mod[0])
    mod3 = mod.reshape(2 * NB, 1, 6 * D)
    w_in_bf = w_in[0].astype(BF16)
    norm1 = norm1_g[0].reshape(1, D)

    rgx, grg, mlx, smlo, sgr, sgm = _proj_call(x, mod3, norm1, w_in_bf, ctx=False)
    rgx_c, mlx_c = _proj_call(ctx, mod3, norm1, w_in_bf, ctx=True)

    rg_cw = rg_conv_w[0]
    rg_cb = rg_conv_b[0].reshape(1, D)
    zero_h = jnp.zeros((NB, D), F32)
    rg_w = [_rg_weights(rg_wa[0, d], rg_ba[0, d], rg_wx[0, d], rg_bx[0, d]) + (rg_lambda[0, d].reshape(1, D),)
            for d in range(2)]
    _, h0_f = _rg_call(rgx_c, zero_h, rg_cw, rg_cb, *rg_w[0], reverse=False, mode="ctx", name="rg_ctx_fwd")
    _, h0_r = _rg_call(rgx_c, zero_h, rg_cw, rg_cb, *rg_w[1], reverse=True, mode="ctx", name="rg_ctx_rev")
    h_f, rg_xh, _ = _rg_call(rgx, h0_f, rg_cw, rg_cb, *rg_w[0], reverse=False, mode="fwd", name="rg_fwd")
    h_rg, _ = _rg_call(rg_xh, h0_r, rg_cw, rg_cb, *rg_w[1], reverse=True, mode="rev", h_fwd=h_f, name="rg_rev")

    ml_cw = ml_conv_w[0]
    ml_cb = ml_conv_b[0].reshape(1, D)
    ml_qkv = (_block_diagonals(ml_wq[0]), _block_diagonals(ml_wk[0]), _block_diagonals(ml_wv[0]))
    ml_w = (ml_cw, ml_cb) + ml_qkv + _gate_weights(ml_wi[0], ml_bi[0], ml_wf[0], ml_bf[0])
    q_c, kt_c, v_c, _, gv_c, grf_c, grr_c = _ml_prep_call(mlx_c, ml_w, latent=False, name="ml_prep_ctx")
    st_f = _ml_rec_call(q_c, kt_c, v_c, gv_c, grf_c, reverse=False, name="ml_ctx_fwd")
    st_r = _ml_rec_call(q_c, kt_c, v_c, gv_c, grr_c, reverse=True, name="ml_ctx_rev")
    q_l, kt_l, v_l, xc_l, gv_l, grf_l, grr_l = _ml_prep_call(
        mlx.reshape(NB, GRID_W, GRID_W, D), ml_w, latent=True, name="ml_prep")
    hm_f = _ml_rec_call(q_l, kt_l, v_l, gv_l, grf_l, reverse=False, state=st_f, name="ml_fwd")
    h_ml = _ml_rec_call(q_l, kt_l, v_l, gv_l, grr_l, reverse=True, state=st_r,
                        merge_with=(hm_f, xc_l, ml_norm_g[0].reshape(1, D), ml_skip[0].reshape(1, D)), name="ml_rev")

    return _final_call(
        x, h_rg, grg, h_ml.reshape(NB, SEQ, D), smlo, sgr, sgm, mod3,
        norm2_g[0].reshape(1, D), final_norm_g.reshape(1, D),
        w_branch_rg[0].astype(BF16), w_branch_ml[0].astype(BF16), w_out[0].astype(BF16),
        w_ffn_in[0].astype(BF16), w_ffn_out[0].astype(BF16))
```

```python
import functools

import jax
import jax.numpy as jnp
from jax import lax
from jax.experimental import pallas as pl
from jax.experimental.pallas import tpu as pltpu

F32 = jnp.float32
BF16 = jnp.bfloat16

D = 1024
NB = 8
SEQ = 4096
GRID_W = 64
CTX = 256
EPS = 1e-6
RG_C = 8.0
HEADS = 4
DH = D // HEADS
D_FF = 2816
N_IN = 6 * D
LOG2E = 1.4426950408889634

VMEM_LIMIT = 60 * 1024 * 1024
BF16_ROWS = 16

PROJ_ROWS = 1024
FINAL_ROWS = 512
RG_STEPS = 128
RG_SUB = 256
RG_SUB_STEPS = RG_SUB // NB
RG_COEF = 128
ML_CHUNK = 256
ML_GROUP_COLS = 16
ML_TILE = ML_GROUP_COLS * GRID_W
ML_A_ROWS = 512

assert ML_CHUNK == DH


def _sigmoid(x):
    return 0.5 * (jnp.tanh(0.5 * x) + 1.0)


def _silu(x):
    half = 0.5 * x
    return half * (jnp.tanh(half) + 1.0)


def _softplus(x):
    return jnp.maximum(x, 0.0) + jnp.log(1.0 + jnp.exp(-jnp.abs(x)))


def _log_sigmoid(x):
    return jnp.minimum(x, 0.0) - jnp.log(1.0 + jnp.exp(-jnp.abs(x)))


def _split3(x):
    hi = x.astype(BF16)
    r1 = x - hi.astype(F32)
    mid = r1.astype(BF16)
    lo = (r1 - mid.astype(F32)).astype(BF16)
    return hi, mid, lo


def _params(sem):
    return pltpu.CompilerParams(dimension_semantics=sem, vmem_limit_bytes=VMEM_LIMIT)


def _const_spec(shape):
    nd = len(shape)
    return pl.BlockSpec(shape, lambda *_: (0,) * nd, pipeline_mode=pl.Buffered(1))


def _swap_perm(a, b):
    n = a * b
    out_row = jnp.arange(n)
    src = (out_row % a) * b + out_row // a
    return (src[:, None] == jnp.arange(n)[None, :]).astype(BF16)


def _mod_kernel(c_ref, cc_ref, w_ref, b_ref, o_ref):
    s = jnp.concatenate([c_ref[...], jnp.broadcast_to(cc_ref[...], (NB, D))], axis=0)
    s = _silu(s)
    s_hi, s_mid, _ = _split3(s)
    w_hi, w_mid, _ = _split3(w_ref[...])
    dot = functools.partial(jnp.dot, preferred_element_type=F32)
    o_ref[...] = dot(s_hi, w_hi) + dot(s_mid, w_hi) + dot(s_hi, w_mid) + b_ref[...]


def _mod_call(c, c_ctx, w_mod, b_mod):
    return pl.pallas_call(
        _mod_kernel,
        grid=(6,),
        in_specs=[
            pl.BlockSpec((NB, D), lambda g: (0, 0)),
            pl.BlockSpec((1, D), lambda g: (0, 0)),
            pl.BlockSpec((D, D), lambda g: (0, g)),
            pl.BlockSpec((1, D), lambda g: (0, g)),
        ],
        out_specs=pl.BlockSpec((2 * NB, D), lambda g: (0, g)),
        out_shape=jax.ShapeDtypeStruct((2 * NB, 6 * D), F32),
        compiler_params=_params(("arbitrary",)),
        name="mod",
    )(c, c_ctx.reshape(1, D), w_mod, b_mod.reshape(1, 6 * D))


def _gelu_tanh(x):
    return jax.nn.gelu(x, approximate=True)


def _identity(x):
    return x


_PROJ_FULL = ((0, _identity), (1, _gelu_tanh), (2, _identity), (3, _sigmoid), (4, _sigmoid), (5, _sigmoid))
_PROJ_CTX = ((0, _identity), (2, _identity))


def _proj_kernel(x_ref, sh_ref, sc_ref, g_ref, *refs, acts):
    w_refs, o_refs = refs[:len(acts)], refs[len(acts):]
    x = x_ref[...]
    ms = jnp.mean(x * x, axis=-1, keepdims=True)
    y = x * lax.rsqrt(ms + EPS) * g_ref[...]
    u = (y * (1.0 + sc_ref[...]) + sh_ref[...]).astype(BF16)
    for w_ref, o_ref, act in zip(w_refs, o_refs, acts):
        p = jnp.dot(u, w_ref[...], preferred_element_type=F32)
        o_ref[...] = act(p).astype(o_ref.dtype)


def _proj_call(x, mod3, norm_g, w_in_bf, *, ctx):
    L = x.shape[1]
    rows = min(PROJ_ROWS, L)
    groups = _PROJ_CTX if ctx else _PROJ_FULL
    mod_row = (lambda b: NB) if ctx else (lambda b: b)
    row_spec = pl.BlockSpec((None, rows, D), lambda b, i: (b, i, 0))

    def group_spec(g):
        return pl.BlockSpec((D, D), lambda b, i: (0, g), pipeline_mode=pl.Buffered(1))

    return pl.pallas_call(
        functools.partial(_proj_kernel, acts=tuple(act for _, act in groups)),
        grid=(NB, L // rows),
        in_specs=[
            row_spec,
            pl.BlockSpec((None, 1, D), lambda b, i: (mod_row(b), 0, 0)),
            pl.BlockSpec((None, 1, D), lambda b, i: (mod_row(b), 0, 1)),
            _const_spec((1, D)),
            *[group_spec(g) for g, _ in groups],
        ],
        out_specs=[row_spec] * len(groups),
        out_shape=[jax.ShapeDtypeStruct((NB, L, D), BF16)] * len(groups),
        compiler_params=_params(("arbitrary", "arbitrary")),
        name="proj_ctx" if ctx else "proj",
    )(x, mod3, mod3, norm_g, *[w_in_bf] * len(groups))


def _rg_kernel(*refs, reverse, n_tiles, mode):
    if mode == "rev":
        (xh_ref, h0_ref, wbd_ref, lam_ref, hf_ref, pout_ref, o_ref, hlast_ref, a_scr, b_scr, h_scr) = refs
    elif mode == "fwd":
        (x_ref, xp_ref, xn_ref, h0_ref, cw_ref, cb_ref, wbd_ref, lam_ref, pin_ref, phalo_ref,
         o_ref, xh_out, hlast_ref, xe_scr, a_scr, b_scr, h_scr) = refs
    else:
        (x_ref, xp_ref, xn_ref, h0_ref, cw_ref, cb_ref, wbd_ref, lam_ref, pin_ref, phalo_ref,
         o_ref, hlast_ref, xe_scr, a_scr, b_scr, h_scr) = refs
    step = pl.program_id(0)
    order = jnp.minimum(step, n_tiles - 1)
    tile = (n_tiles - 1 - order) if reverse else order
    new = step % 2
    old = 1 - new
    rows = RG_STEPS * NB
    halo = 2 * NB

    @pl.when(step == 0)
    def _():
        h_scr[...] = h0_ref[...]
        a_scr[1] = jnp.ones((rows, D), F32)
        b_scr[1] = jnp.zeros((rows, D), F32)

    def coef_stage():
        if mode != "rev":
            def halo_rows(ref):
                xb = jnp.concatenate([ref[b] for b in range(NB)], axis=0)
                return jnp.dot(phalo_ref[...], xb, preferred_element_type=F32)

            xe_scr[0:halo, :] = jnp.where(tile > 0, halo_rows(xp_ref)[(BF16_ROWS - 2) * NB:BF16_ROWS * NB], 0.0)
            xe_scr[halo + rows:2 * halo + rows, :] = jnp.where(tile < n_tiles - 1, halo_rows(xn_ref)[0:halo], 0.0)
            for s in range(rows // RG_SUB):
                t0 = s * RG_SUB_STEPS
                xb = jnp.concatenate([x_ref[b, t0:t0 + RG_SUB_STEPS, :] for b in range(NB)], axis=0)
                xe_scr[halo + s * RG_SUB:halo + (s + 1) * RG_SUB, :] = jnp.dot(
                    pin_ref[...], xb, preferred_element_type=F32)
            cw = 0.5 * cw_ref[...]
            cb = 0.5 * cb_ref[...]

        lam2 = (-0.5 * RG_C * LOG2E) * _softplus(-lam_ref[...])

        ones = jnp.ones((RG_COEF, 128), BF16)

        for s in range(rows // RG_COEF):
            r0 = s * RG_COEF
            if mode == "rev":
                xh = xh_ref[r0:r0 + RG_COEF, :].astype(F32)
            else:
                xh = cb + cw[0:1] * xe_scr[r0:r0 + RG_COEF, :]
                xh = xh + cw[1:2] * xe_scr[r0 + NB:r0 + NB + RG_COEF, :]
                xh = xh + cw[2:3] * xe_scr[r0 + 2 * NB:r0 + 2 * NB + RG_COEF, :]
                xh = xh + cw[3:4] * xe_scr[r0 + 3 * NB:r0 + 3 * NB + RG_COEF, :]
            if mode == "fwd":
                xh_out[r0:r0 + RG_COEF, :] = xh.astype(BF16)
            for p in range(D // 128):
                lo, hi = p * 128, (p + 1) * 128
                xhp = xh[:, lo:hi]
                lhs = jnp.concatenate([xhp.astype(BF16), ones], axis=1)
                pre = jnp.dot(lhs, wbd_ref[p], preferred_element_type=F32)
                t_r = jnp.tanh(pre[:, 0:128])
                t_i = jnp.tanh(pre[:, 128:256])
                a = jnp.exp2(lam2[:, lo:hi] * (t_r + 1.0))
                y = (1.0 - a) * (1.0 + a)
                gain = jnp.where(y > 0.0, y * lax.rsqrt(y), 0.0)
                a_scr[new, r0:r0 + RG_COEF, lo:hi] = a
                b_scr[new, r0:r0 + RG_COEF, lo:hi] = gain * ((t_i + 1.0) * xhp)

    if mode == "ctx":
        pl.when(step < n_tiles)(coef_stage)
    else:
        coef_stage()

    h = h_scr[...]
    for t in range(RG_STEPS):
        r0 = ((RG_STEPS - 1 - t) if reverse else t) * NB
        h = a_scr[old, r0:r0 + NB, :] * h + b_scr[old, r0:r0 + NB, :]
        b_scr[old, r0:r0 + NB, :] = h
    h_scr[...] = h
    hlast_ref[...] = h

    if mode == "rev":
        for s in range(rows // RG_SUB):
            r0 = s * RG_SUB
            t0 = s * RG_SUB_STEPS
            hs = (b_scr[old, r0:r0 + RG_SUB, :] + hf_ref[r0:r0 + RG_SUB, :].astype(F32)).astype(BF16)
            hb = jnp.dot(pout_ref[...], hs, preferred_element_type=F32).astype(o_ref.dtype)
            for b in range(NB):
                o_ref[b, t0:t0 + RG_SUB_STEPS, :] = hb[b * RG_SUB_STEPS:(b + 1) * RG_SUB_STEPS]
    else:
        o_ref[...] = b_scr[old].astype(o_ref.dtype)


def _rg_call(src, h0, cw, cb, wbd, lam, *, reverse, mode, h_fwd=None, name):
    L = src.shape[0] // NB if mode == "rev" else src.shape[1]
    rows = RG_STEPS * NB
    n_tiles = L // RG_STEPS
    per = RG_STEPS // BF16_ROWS
    n_halo = L // BF16_ROWS
    def tile_at(lag):
        def tile(i):
            order = jnp.clip(i - lag, 0, n_tiles - 1)
            return (n_tiles - 1 - order) if reverse else order
        return tile

    tile, tile_old = tile_at(0), tile_at(1)
    bm_spec = pl.BlockSpec((NB, RG_STEPS, D), lambda i: (0, tile(i), 0))
    tm_spec = pl.BlockSpec((rows, D), lambda i: (tile(i), 0))
    tm_old_spec = pl.BlockSpec((rows, D), lambda i: (tile_old(i), 0))
    tm_shape = jax.ShapeDtypeStruct((L * NB, D), BF16)
    state_spec = pl.BlockSpec((NB, D), lambda i: (0, 0))
    state_shape = jax.ShapeDtypeStruct((NB, D), F32)
    scratch = [pltpu.VMEM((2, rows, D), F32), pltpu.VMEM((2, rows, D), F32), pltpu.VMEM((NB, D), F32)]
    if mode == "rev":
        in_specs = [tm_spec, _const_spec((NB, D)), _const_spec((D // 128, 256, 256)), _const_spec((1, D)),
                    tm_old_spec, _const_spec((RG_SUB, RG_SUB))]
        args = [src, h0, wbd, lam, h_fwd, _swap_perm(RG_SUB_STEPS, NB)]
        out_specs = [pl.BlockSpec((NB, RG_STEPS, D), lambda i: (0, tile_old(i), 0)), state_spec]
        out_shape = [jax.ShapeDtypeStruct((NB, L, D), BF16), state_shape]
    else:
        in_specs = [
            bm_spec,
            pl.BlockSpec((NB, BF16_ROWS, D), lambda i: (0, jnp.maximum(tile(i) * per - 1, 0), 0)),
            pl.BlockSpec((NB, BF16_ROWS, D), lambda i: (0, jnp.minimum((tile(i) + 1) * per, n_halo - 1), 0)),
            _const_spec((NB, D)),
            _const_spec((4, D)),
            _const_spec((1, D)),
            _const_spec((D // 128, 256, 256)),
            _const_spec((1, D)),
            _const_spec((RG_SUB, RG_SUB)),
            _const_spec((NB * BF16_ROWS, NB * BF16_ROWS)),
        ]
        args = [src, src, src, h0, cw, cb, wbd, lam, _swap_perm(NB, RG_SUB_STEPS), _swap_perm(NB, BF16_ROWS)]
        out_specs = [tm_old_spec] + ([tm_spec] if mode == "fwd" else []) + [state_spec]
        out_shape = [tm_shape] + ([tm_shape] if mode == "fwd" else []) + [state_shape]
        scratch = [pltpu.VMEM((rows + 4 * NB, D), F32)] + scratch
    return pl.pallas_call(
        functools.partial(_rg_kernel, reverse=reverse, n_tiles=n_tiles, mode=mode),
        grid=(n_tiles + 1,),
        in_specs=in_specs,
        out_specs=out_specs,
        out_shape=out_shape,
        scratch_shapes=scratch,
        compiler_params=_params(("arbitrary",)),
        name=name,
    )(*args)


ML_REV_LANE = 8


def _scan_lanes_both(x, op, fill):
    n = x.shape[1]
    lane = lax.broadcasted_iota(jnp.int32, x.shape, 1)
    is_prefix = lax.broadcasted_iota(jnp.int32, x.shape, 0) < ML_REV_LANE
    sh = 1
    while sh < n:
        before = jnp.where(lane >= sh, pltpu.roll(x, sh, axis=1), fill)
        after = jnp.where(lane < n - sh, pltpu.roll(x, n - sh, axis=1), fill)
        x = op(x, jnp.where(is_prefix, before, after))
        sh *= 2
    return x


def _ml_prep_kernel(*refs, n_tiles, n_tok, latent):
    L = ML_CHUNK
    n_chunks = n_tok // L
    if latent:
        (x_ref, xp_ref, xn_ref, perm_ref, cw_ref, cb_ref, dq_ref, dk_ref, dv_ref, wg_ref, bg_ref,
         q_ref, kt_ref, v_ref, xc_ref, gv_ref, grf_ref, grr_ref,
         xt_scr, wqg_scr, wv_scr, wkt_scr, g_scr) = refs
    else:
        (x_ref, cw_ref, cb_ref, dq_ref, dk_ref, dv_ref, wg_ref, bg_ref,
         q_ref, kt_ref, v_ref, xc_ref, gv_ref, grf_ref, grr_ref,
         xt_scr, wqg_scr, wv_scr, wkt_scr, g_scr) = refs
    step = pl.program_id(0)
    tile = jnp.minimum(step, NB * n_tiles - 1) % n_tiles
    first = step == 0
    slot_new = step % 2
    slot_old = 1 - slot_new

    @pl.when(first)
    def _():
        g_scr[1] = jnp.zeros(g_scr.shape[1:], F32)

    src_t = lax.broadcasted_iota(jnp.int32, (L, L), 0)
    dst_t = lax.broadcasted_iota(jnp.int32, (L, L), 1)
    tri_fwd = (src_t <= dst_t).astype(BF16)
    tri_rev = (src_t >= dst_t).astype(BF16)

    def gate_vectors(gates, ck):
        r0 = ck * L
        li_rows = gates[:, 0:128].T[0:2 * ML_REV_LANE]
        lf_rows = _log_sigmoid(gates[:, 128:256].T[0:2 * ML_REV_LANE])
        split = jnp.concatenate(_split3(lf_rows), axis=0)
        to_t = jnp.dot(split, tri_fwd, preferred_element_type=F32)
        from_t = jnp.dot(split, tri_rev, preferred_element_type=F32)
        cum_rows = jnp.concatenate([
            to_t[0:8] + to_t[BF16_ROWS:BF16_ROWS + 8] + to_t[2 * BF16_ROWS:2 * BF16_ROWS + 8],
            from_t[8:16] + from_t[BF16_ROWS + 8:2 * BF16_ROWS] + from_t[2 * BF16_ROWS + 8:3 * BF16_ROWS]], axis=0)
        g_rows = li_rows - cum_rows
        grf_ref[ck] = g_rows[0:ML_REV_LANE]
        grr_ref[ck] = g_rows[ML_REV_LANE:2 * ML_REV_LANE]
        pm_rows = _scan_lanes_both(g_rows, jnp.maximum, -jnp.inf)
        pad = jnp.zeros((128 - 2 * ML_REV_LANE, L), F32)
        gv_ref[0, r0:r0 + L, :] = jnp.concatenate([cum_rows, pad], axis=0).T
        gv_ref[1, r0:r0 + L, :] = jnp.concatenate([pm_rows, pad], axis=0).T

    @pl.when(first)
    def _():
        diff = lax.broadcasted_iota(jnp.int32, (DH, DH), 1) - lax.broadcasted_iota(jnp.int32, (DH, DH), 0)

        def dense(d_ref, h):
            diag = d_ref[h * DH:(h + 1) * DH, :]
            out = jnp.zeros((DH, DH), F32)
            for d in range(-3, 4):
                out = jnp.where(diff == d, diag[:, 3 + d:4 + d], out)
            return out

        dot = functools.partial(jnp.dot, preferred_element_type=F32)
        for h in range(HEADS):
            wq = dense(dq_ref, h).astype(BF16)
            wk_f32 = dense(dk_ref, h) * (DH ** -0.5)
            wk = wk_f32.astype(BF16)
            wv = dense(dv_ref, h).astype(BF16)
            wkt_scr[h] = wk_f32.T.astype(BF16)
            wqg_scr[h, :, 0:DH] = wq
            wqg_scr[h, :, DH:2 * DH] = (dot(wq, wg_ref[h]) + dot(wk, wg_ref[HEADS + h])).astype(BF16)
            wv_scr[h, :, 0:DH] = wv
            wv_scr[h, :, DH:2 * DH] = dot(wv, wg_ref[2 * HEADS + h]).astype(BF16)

    if not latent:
        xt_scr[8:16, :] = jnp.zeros((8, D), F32)
        xt_scr[16:16 + n_tok, :] = x_ref[...].astype(F32)
        xt_scr[16 + n_tok:24 + n_tok, :] = jnp.zeros((8, D), F32)
    else:
        for g in range(GRID_W // BF16_ROWS):
            xg = x_ref[g * BF16_ROWS:(g + 1) * BF16_ROWS].reshape(BF16_ROWS * ML_GROUP_COLS, D)
            yg = jnp.dot(perm_ref[...], xg, preferred_element_type=F32)
            for w in range(ML_GROUP_COLS):
                dst = 16 + w * GRID_W + g * BF16_ROWS
                xt_scr[dst:dst + BF16_ROWS, :] = yg[w * BF16_ROWS:(w + 1) * BF16_ROWS]
        last = ML_GROUP_COLS - 1
        prev = jnp.concatenate([xp_ref[0].astype(F32)[last:last + 1], xp_ref[1].astype(F32)[last:last + 1]], axis=0)
        xt_scr[14:16, :] = jnp.where(tile > 0, prev, 0.0)
        xt_scr[16 + n_tok:17 + n_tok, :] = jnp.where(tile < n_tiles - 1, xn_ref[0].astype(F32)[0:1], 0.0)

    for ck in range(n_chunks):
        gate_vectors(g_scr[slot_old, ck * L:(ck + 1) * L, :], ck)

    cw = cw_ref[...]
    for ck in range(n_chunks):
        r0 = ck * L
        xt = xt_scr[16 + r0:16 + r0 + L, :]
        xc = cb_ref[...] + cw[0:1] * xt_scr[14 + r0:14 + r0 + L, :] + cw[1:2] * xt_scr[15 + r0:15 + r0 + L, :]
        xc = xc + cw[2:3] * xt + cw[3:4] * xt_scr[17 + r0:17 + r0 + L, :]
        xc = _silu(xc)
        xc_ref[r0:r0 + L, :] = xc.astype(BF16)
        gates = jnp.zeros((L, 256), F32) + bg_ref[...]
        for h in range(HEADS):
            lo, hi = h * DH, (h + 1) * DH
            xc_h = xc[:, lo:hi].astype(BF16)
            qg = jnp.dot(xc_h, wqg_scr[h], preferred_element_type=F32)
            vg = jnp.dot(xt[:, lo:hi].astype(BF16), wv_scr[h], preferred_element_type=F32)
            gates = gates + qg[:, DH:2 * DH] + vg[:, DH:2 * DH]
            q_ref[r0:r0 + L, lo:hi] = qg[:, 0:DH].astype(BF16)
            kt = lax.dot_general(wkt_scr[h], xc_h, (((1,), (1,)), ((), ())), preferred_element_type=F32)
            kt_ref[lo:hi, r0:r0 + L] = kt.astype(BF16)
            v_ref[r0:r0 + L, lo:hi] = vg[:, 0:DH].astype(BF16)
        g_scr[slot_new, r0:r0 + L, :] = gates


def _ml_rec_kernel(*refs, reverse, n_tok, mode):
    L = ML_CHUNK
    n_chunks = n_tok // L
    lane0 = ML_REV_LANE if reverse else 0
    if mode == "ctx":
        (q_ref, kt_ref, v_ref, gv_ref, gr_ref, c_out, m_out, c_scr, m_scr) = refs
    elif mode == "fwd":
        (q_ref, kt_ref, v_ref, gv_ref, gr_ref, c0_ref, m0_ref, o_ref, c_scr, m_scr, h_scr) = refs
    else:
        (q_ref, kt_ref, v_ref, gv_ref, gr_ref, c0_ref, m0_ref, hf_ref, xc_ref, ng_ref, sk_ref,
         perm_ref, o_ref, c_scr, m_scr, h_scr) = refs

    @pl.when(pl.program_id(1) == 0)
    def _():
        if mode == "ctx":
            c_scr[...] = jnp.zeros_like(c_scr)
            m_scr[...] = jnp.zeros_like(m_scr)
        else:
            c_scr[...] = c0_ref[...]
            m_scr[...] = m0_ref[...]

    ones_lanes = jnp.ones((L, 128), BF16)
    if mode != "ctx":
        row_id = lax.broadcasted_iota(jnp.int32, (L, L), 0)
        col_id = lax.broadcasted_iota(jnp.int32, (L, L), 1)
        keep = (col_id >= row_id) if reverse else (col_id <= row_id)

    def stage_b(ck):
        r0 = ck * L
        m_prev = m_scr[0:1, :]
        edge = 0 if reverse else L - 1
        b_tot = gv_ref[0, r0 + edge:r0 + edge + 1, :]
        gmax = gv_ref[1, r0 + edge:r0 + edge + 1, :]
        m_new = jnp.maximum(b_tot + m_prev, b_tot + gmax)
        dec = jnp.exp(b_tot + m_prev - m_new)
        if mode != "ctx":
            cum = gv_ref[0, r0:r0 + L, :]
            inter = cum + m_prev
            m_t = jnp.maximum(inter, cum + gv_ref[1, r0:r0 + L, :])
            c_all = cum - m_t
            si_all = jnp.exp(inter - m_t)
            fl_all = jnp.exp(-m_t)
            g_row = gr_ref[ck]
        for h in range(HEADS):
            lo, hi = h * DH, (h + 1) * DH
            ln = lane0 + h
            q = q_ref[r0:r0 + L, lo:hi]
            kt = kt_ref[lo:hi, r0:r0 + L]
            v = jnp.concatenate([v_ref[r0:r0 + L, lo:hi], ones_lanes], axis=1)
            dec_h = dec[:, ln:ln + 1]
            c_old = c_scr[h]
            if mode != "ctx":
                si_c = si_all[:, ln:ln + 1]
                s = jnp.dot(q, kt, preferred_element_type=F32)
                arg = jnp.where(keep, c_all[:, ln:ln + 1] + g_row[h:h + 1, :], -jnp.inf)
                p = (s * jnp.exp(arg)).astype(BF16)
                both = jnp.dot(p, v, preferred_element_type=F32)
                both = both + si_c * jnp.dot(q, c_old.astype(BF16), preferred_element_type=F32)
                scale = 1.0 / jnp.maximum(jnp.abs(both[:, DH:DH + 128]), fl_all[:, ln:ln + 1])
                h_scr[r0:r0 + L, lo:hi] = both[:, 0:DH] * jnp.concatenate([scale, scale], axis=1)
            ws_row = jnp.exp((b_tot[:, ln:ln + 1] - m_new[:, ln:ln + 1]) + gr_ref[ck][h:h + 1, :]).astype(BF16)
            kw_t = kt * ws_row
            c_scr[h] = dec_h * c_old + jnp.dot(kw_t, v, preferred_element_type=F32)
        m_scr[...] = jnp.broadcast_to(m_new, m_scr.shape)

    for ci in range(n_chunks):
        stage_b((n_chunks - 1 - ci) if reverse else ci)

    a_rows = min(ML_A_ROWS, n_tok)
    if mode == "ctx":
        c_out[...] = c_scr[...]
        m_out[...] = m_scr[...]
    elif mode == "fwd":
        o_ref[...] = h_scr[...].astype(o_ref.dtype)
    else:
        for sb in range(n_tok // a_rows):
            r0 = sb * a_rows
            ht = h_scr[r0:r0 + a_rows, :] + hf_ref[r0:r0 + a_rows, :].astype(F32)
            for h in range(HEADS):
                lo, hi = h * DH, (h + 1) * DH
                hh = ht[:, lo:hi]
                mu = jnp.mean(hh, axis=-1, keepdims=True)
                var = jnp.mean(jnp.square(hh - mu), axis=-1, keepdims=True)
                hn = (hh - mu) * lax.rsqrt(var + EPS)
                xc = xc_ref[r0:r0 + a_rows, lo:hi].astype(F32)
                h_scr[r0:r0 + a_rows, lo:hi] = hn * ng_ref[:, lo:hi] + sk_ref[:, lo:hi] * xc
        for g in range(GRID_W // BF16_ROWS):
            zg = jnp.concatenate(
                [h_scr[w * GRID_W + g * BF16_ROWS:w * GRID_W + (g + 1) * BF16_ROWS, :] for w in range(ML_GROUP_COLS)],
                axis=0).astype(BF16)
            og = jnp.dot(perm_ref[...], zg, preferred_element_type=F32).astype(o_ref.dtype)
            o_ref[g * BF16_ROWS:(g + 1) * BF16_ROWS] = og.reshape(BF16_ROWS, ML_GROUP_COLS, D)


def _ml_weight_specs():
    return [
        _const_spec((4, D)),
        _const_spec((1, D)),
        _const_spec((D, 8)),
        _const_spec((D, 8)),
        _const_spec((D, 8)),
        _const_spec((3 * HEADS, DH, 256)),
        _const_spec((1, 256)),
    ]


def _ml_prep_call(mlx, weights, *, latent, name):
    scratch = [
        pltpu.VMEM(((ML_TILE if latent else CTX) + 32, D), F32),
        pltpu.VMEM((HEADS, DH, 2 * DH), BF16),
        pltpu.VMEM((HEADS, DH, 2 * DH), BF16),
        pltpu.VMEM((HEADS, DH, DH), BF16),
    ]
    n_tiles, n_tok, seq = (GRID_W // ML_GROUP_COLS, ML_TILE, SEQ) if latent else (1, CTX, CTX)
    n_flat = NB * n_tiles

    def at(lag):
        def split(step):
            flat = jnp.clip(step - lag, 0, n_flat - 1)
            return flat // n_tiles, flat % n_tiles
        return split

    new, old = at(0), at(1)
    if latent:
        in_specs = [
            pl.BlockSpec((None, GRID_W, ML_GROUP_COLS, D), lambda g: (new(g)[0], 0, new(g)[1], 0)),
            pl.BlockSpec((None, 2, ML_GROUP_COLS, D),
                         lambda g: (new(g)[0], GRID_W // 2 - 1, jnp.maximum(new(g)[1] - 1, 0), 0)),
            pl.BlockSpec((None, 1, ML_GROUP_COLS, D),
                         lambda g: (new(g)[0], 0, jnp.minimum(new(g)[1] + 1, n_tiles - 1), 0)),
            _const_spec((BF16_ROWS * ML_GROUP_COLS, BF16_ROWS * ML_GROUP_COLS)),
        ]
        args = [mlx, mlx, mlx, _swap_perm(BF16_ROWS, ML_GROUP_COLS)]
    else:
        in_specs = [pl.BlockSpec((None, CTX, D), lambda g: (new(g)[0], 0, 0))]
        args = [mlx]
    scratch.append(pltpu.VMEM((2, n_tok, 256), F32))
    tok_spec = pl.BlockSpec((None, n_tok, D), lambda g: (new(g)[0], new(g)[1], 0))
    kt_spec = pl.BlockSpec((None, D, n_tok), lambda g: (new(g)[0], 0, new(g)[1]))
    gv_spec = pl.BlockSpec((None, 2, n_tok, 128), lambda g: (old(g)[0], 0, old(g)[1], 0))
    row_spec = pl.BlockSpec((None, n_tok // ML_CHUNK, 8, ML_CHUNK), lambda g: (old(g)[0], old(g)[1], 0, 0))
    tok_shape = jax.ShapeDtypeStruct((NB, seq, D), BF16)
    row_shape = jax.ShapeDtypeStruct((NB, seq // ML_CHUNK, 8, ML_CHUNK), F32)
    return pl.pallas_call(
        functools.partial(_ml_prep_kernel, n_tiles=n_tiles, n_tok=n_tok, latent=latent),
        grid=(n_flat + 1,),
        in_specs=in_specs + _ml_weight_specs(),
        out_specs=[tok_spec, kt_spec, tok_spec, tok_spec, gv_spec, row_spec, row_spec],
        out_shape=[tok_shape, jax.ShapeDtypeStruct((NB, D, seq), BF16), tok_shape, tok_shape,
                   jax.ShapeDtypeStruct((NB, 2, seq, 128), F32), row_shape, row_shape],
        scratch_shapes=scratch,
        compiler_params=_params(("arbitrary",)),
        name=name,
    )(*args, *weights)


_STATE_SHAPES = [
    jax.ShapeDtypeStruct((NB, HEADS, DH, DH + 128), F32),
    jax.ShapeDtypeStruct((NB, 8, 128), F32),
]


def _ml_rec_call(q, kt, v, gv, g_rows, *, reverse, state=None, merge_with=None, name):
    seq = q.shape[1]
    n_tok = min(ML_TILE, seq)
    n_tiles = seq // n_tok
    n_chunks = n_tok // ML_CHUNK
    tile = (lambda j: n_tiles - 1 - j) if reverse else (lambda j: j)
    tok_spec = pl.BlockSpec((None, n_tok, D), lambda b, j: (b, tile(j), 0))
    state_specs = [
        pl.BlockSpec((None, HEADS, DH, DH + 128), lambda b, j: (b, 0, 0, 0)),
        pl.BlockSpec((None, 8, 128), lambda b, j: (b, 0, 0)),
    ]
    in_specs = [
        tok_spec, pl.BlockSpec((None, D, n_tok), lambda b, j: (b, 0, tile(j))), tok_spec,
        pl.BlockSpec((None, 2, n_tok, 128), lambda b, j: (b, 0, tile(j), 0)),
        pl.BlockSpec((None, n_chunks, 8, ML_CHUNK), lambda b, j: (b, tile(j), 0, 0)),
    ]
    args = [q, kt, v, gv, g_rows]
    scratch = [
        pltpu.VMEM((HEADS, DH, DH + 128), F32),
        pltpu.VMEM((8, 128), F32),
    ]
    if state is None:
        mode = "ctx"
        out_specs, out_shape = state_specs, _STATE_SHAPES
    else:
        in_specs += state_specs
        args += list(state)
        scratch.append(pltpu.VMEM((n_tok, D), F32))
        if merge_with is None:
            mode = "fwd"
            out_specs, out_shape = tok_spec, jax.ShapeDtypeStruct((NB, seq, D), BF16)
        else:
            mode = "rev"
            h_fwd, xc, norm_g, skip = merge_with
            in_specs += [tok_spec, tok_spec, _const_spec((1, D)), _const_spec((1, D)),
                         _const_spec((BF16_ROWS * ML_GROUP_COLS, BF16_ROWS * ML_GROUP_COLS))]
            args += [h_fwd, xc, norm_g, skip, _swap_perm(BF16_ROWS, ML_GROUP_COLS)]
            out_specs = pl.BlockSpec((None, GRID_W, ML_GROUP_COLS, D), lambda b, j: (b, 0, tile(j), 0))
            out_shape = jax.ShapeDtypeStruct((NB, GRID_W, GRID_W, D), BF16)
    return pl.pallas_call(
        functools.partial(_ml_rec_kernel, reverse=reverse, n_tok=n_tok, mode=mode),
        grid=(NB, n_tiles),
        in_specs=in_specs,
        out_specs=out_specs,
        out_shape=out_shape,
        scratch_shapes=scratch,
        compiler_params=_params(("arbitrary", "arbitrary")),
        name=name,
    )(*args)


def _final_kernel(x_ref, hrg_ref, grg_ref, hml_ref, smlo_ref, sgr_ref, sgm_ref, g1_ref, sh2_ref, sc2_ref, g2_ref,
                  n2_ref, nf_ref, wbr_ref, wbm_ref, wo_ref, wfi_ref, wfo_ref, o_ref, act_scr):
    dot = functools.partial(jnp.dot, preferred_element_type=F32)
    y_rg = (hrg_ref[...].astype(F32) * grg_ref[...].astype(F32)).astype(BF16)
    y_ml = (hml_ref[...].astype(F32) * smlo_ref[...].astype(F32)).astype(BF16)
    mix = sgr_ref[...].astype(F32) * dot(y_rg, wbr_ref[...])
    mix = mix + sgm_ref[...].astype(F32) * dot(y_ml, wbm_ref[...])
    x1 = x_ref[...] + g1_ref[...] * dot(mix.astype(BF16), wo_ref[...])
    ms = jnp.mean(x1 * x1, axis=-1, keepdims=True)
    hn = x1 * lax.rsqrt(ms + EPS) * n2_ref[...]
    hb = (hn * (1.0 + sc2_ref[...]) + sh2_ref[...]).astype(BF16)
    step = 256
    for c in range(D_FF // step):
        gate = dot(hb, wfi_ref[:, c * step:(c + 1) * step])
        up = dot(hb, wfi_ref[:, D_FF + c * step:D_FF + (c + 1) * step])
        act_scr[:, c * step:(c + 1) * step] = (_silu(gate) * up).astype(BF16)
    x2 = x1 + g2_ref[...] * dot(act_scr[...], wfo_ref[...])
    ms2 = jnp.mean(x2 * x2, axis=-1, keepdims=True)
    o_ref[...] = x2 * lax.rsqrt(ms2 + EPS) * nf_ref[...]


def _final_call(x, h_rg, grg, h_ml, smlo, sgr, sgm, mod3, norm2_g, final_g, wbr, wbm, wo, wfi, wfo):
    rows = FINAL_ROWS
    row_spec = pl.BlockSpec((None, rows, D), lambda b, i: (b, i, 0))
    mod_spec = lambda g: pl.BlockSpec((None, 1, D), lambda b, i: (b, 0, g))
    return pl.pallas_call(
        _final_kernel,
        grid=(NB, SEQ // rows),
        in_specs=[
            row_spec, row_spec, row_spec, row_spec, row_spec, row_spec, row_spec,
            mod_spec(2), mod_spec(3), mod_spec(4), mod_spec(5),
            _const_spec((1, D)), _const_spec((1, D)),
            _const_spec((D, D)), _const_spec((D, D)), _const_spec((D, D)),
            _const_spec((D, 2 * D_FF)), _const_spec((D_FF, D)),
        ],
        out_specs=row_spec,
        out_shape=jax.ShapeDtypeStruct((NB, SEQ, D), F32),
        scratch_shapes=[pltpu.VMEM((rows, D_FF), BF16)],
        compiler_params=_params(("arbitrary", "arbitrary")),
        name="final",
    )(x, h_rg, grg, h_ml, smlo, sgr, sgm, mod3, mod3, mod3, mod3, norm2_g, final_g, wbr, wbm, wo, wfi, wfo)


def _pair_blockdiag(w):
    w = w.reshape(8, 2, 64, 64)
    z = jnp.zeros((8, 64, 64), w.dtype)
    top = jnp.concatenate([w[:, 0], z], axis=2)
    bot = jnp.concatenate([z, w[:, 1]], axis=2)
    return jnp.concatenate([top, bot], axis=1)


def _rg_weights(wa, ba, wx, bx):
    w = jnp.concatenate([_pair_blockdiag(wa), _pair_blockdiag(wx)], axis=2).astype(BF16)
    bias = 0.5 * jnp.concatenate([ba.reshape(8, 1, 128), bx.reshape(8, 1, 128)], axis=2)
    b_hi = bias.astype(BF16)
    b_lo = (bias - b_hi.astype(F32)).astype(BF16)
    return (jnp.concatenate([w, b_hi, b_lo, jnp.zeros((8, 126, 256), BF16)], axis=1),)


def _block_diagonals(w):
    rows = [jnp.pad(w[:, i, :], ((0, 0), (3 - i, 1 + i))) for i in range(4)]
    return jnp.stack(rows, axis=1).reshape(D, 8)


def _gate_weights(wi, bi, wf, bf):
    def lanes(x):
        gap = [(0, 0)] * (x.ndim - 2)
        fwd = jnp.pad(x[0], gap + [(0, ML_REV_LANE - HEADS)])
        rev = jnp.pad(x[1], gap + [(0, 128 - ML_REV_LANE - HEADS)])
        return jnp.concatenate([fwd, rev], axis=-1)
    w = jnp.concatenate([lanes(wi), lanes(wf)], axis=-1)
    b = jnp.concatenate([lanes(bi), lanes(bf)], axis=-1).reshape(1, 256)
    return w.reshape(3 * HEADS, DH, 256).astype(BF16), b


def kernel(x, c, ctx, c_ctx, w_mod, b_mod, norm1_g, norm2_g, w_in, rg_conv_w, rg_conv_b, rg_wa, rg_ba, rg_wx,
           rg_bx, rg_lambda, ml_conv_w, ml_conv_b, ml_wq, ml_wk, ml_wv, ml_wi, ml_bi, ml_wf, ml_bf,
           ml_norm_g, ml_skip, w_branch_rg, w_branch_ml, w_out, w_ffn_in, w_ffn_out, final_norm_g):
    mod = _mod_call(c, c_ctx, w_mod[0], b_mod[0])
    mod3 = mod.reshape(2 * NB, 1, 6 * D)
    w_in_bf = w_in[0].astype(BF16)
    norm1 = norm1_g[0].reshape(1, D)

    rgx, grg, mlx, smlo, sgr, sgm = _proj_call(x, mod3, norm1, w_in_bf, ctx=False)
    rgx_c, mlx_c = _proj_call(ctx, mod3, norm1, w_in_bf, ctx=True)

    rg_cw = rg_conv_w[0]
    rg_cb = rg_conv_b[0].reshape(1, D)
    zero_h = jnp.zeros((NB, D), F32)
    rg_w = [_rg_weights(rg_wa[0, d], rg_ba[0, d], rg_wx[0, d], rg_bx[0, d]) + (rg_lambda[0, d].reshape(1, D),)
            for d in range(2)]
    _, h0_f = _rg_call(rgx_c, zero_h, rg_cw, rg_cb, *rg_w[0], reverse=False, mode="ctx", name="rg_ctx_fwd")
    _, h0_r = _rg_call(rgx_c, zero_h, rg_cw, rg_cb, *rg_w[1], reverse=True, mode="ctx", name="rg_ctx_rev")
    h_f, rg_xh, _ = _rg_call(rgx, h0_f, rg_cw, rg_cb, *rg_w[0], reverse=False, mode="fwd", name="rg_fwd")
    h_rg, _ = _rg_call(rg_xh, h0_r, rg_cw, rg_cb, *rg_w[1], reverse=True, mode="rev", h_fwd=h_f, name="rg_rev")

    ml_cw = ml_conv_w[0]
    ml_cb = ml_conv_b[0].reshape(1, D)
    ml_qkv = (_block_diagonals(ml_wq[0]), _block_diagonals(ml_wk[0]), _block_diagonals(ml_wv[0]))
    ml_w = (ml_cw, ml_cb) + ml_qkv + _gate_weights(ml_wi[0], ml_bi[0], ml_wf[0], ml_bf[0])
    q_c, kt_c, v_c, _, gv_c, grf_c, grr_c = _ml_prep_call(mlx_c, ml_w, latent=False, name="ml_prep_ctx")
    st_f = _ml_rec_call(q_c, kt_c, v_c, gv_c, grf_c, reverse=False, name="ml_ctx_fwd")
    st_r = _ml_rec_call(q_c, kt_c, v_c, gv_c, grr_c, reverse=True, name="ml_ctx_rev")
    q_l, kt_l, v_l, xc_l, gv_l, grf_l, grr_l = _ml_prep_call(
        mlx.reshape(NB, GRID_W, GRID_W, D), ml_w, latent=True, name="ml_prep")
    hm_f = _ml_rec_call(q_l, kt_l, v_l, gv_l, grf_l, reverse=False, state=st_f, name="ml_fwd")
    h_ml = _ml_rec_call(q_l, kt_l, v_l, gv_l, grr_l, reverse=True, state=st_r,
                        merge_with=(hm_f, xc_l, ml_norm_g[0].reshape(1, D), ml_skip[0].reshape(1, D)), name="ml_rev")

    return _final_call(
        x, h_rg, grg, h_ml.reshape(NB, SEQ, D), smlo, sgr, sgm, mod3,
        norm2_g[0].reshape(1, D), final_norm_g.reshape(1, D),
        w_branch_rg[0].astype(BF16), w_branch_ml[0].astype(BF16), w_out[0].astype(BF16),
        w_ffn_in[0].astype(BF16), w_ffn_out[0].astype(BF16))
```

```python
import functools

import jax
import jax.numpy as jnp
from jax import lax
from jax.experimental import pallas as pl
from jax.experimental.pallas import tpu as pltpu

F32 = jnp.float32
BF16 = jnp.bfloat16

D = 1024
NB = 8
SEQ = 4096
GRID_W = 64
CTX = 256
EPS = 1e-6
RG_C = 8.0
HEADS = 4
DH = D // HEADS
D_FF = 2816
N_IN = 6 * D
LOG2E = 1.4426950408889634

VMEM_LIMIT = 60 * 1024 * 1024
BF16_ROWS = 16

PROJ_ROWS = 1024
FINAL_ROWS = 512
RG_STEPS = 128
RG_SUB = 256
RG_SUB_STEPS = RG_SUB // NB
RG_COEF = 128
ML_CHUNK = 256
ML_GROUP_COLS = 16
ML_TILE = ML_GROUP_COLS * GRID_W
ML_A_ROWS = 512

assert ML_CHUNK == DH


def _sigmoid(x):
    return 0.5 * (jnp.tanh(0.5 * x) + 1.0)


def _silu(x):
    half = 0.5 * x
    return half * (jnp.tanh(half) + 1.0)


def _softplus(x):
    return jnp.maximum(x, 0.0) + jnp.log(1.0 + jnp.exp(-jnp.abs(x)))


def _log_sigmoid(x):
    return jnp.minimum(x, 0.0) - jnp.log(1.0 + jnp.exp(-jnp.abs(x)))


def _split3(x):
    hi = x.astype(BF16)
    r1 = x - hi.astype(F32)
    mid = r1.astype(BF16)
    lo = (r1 - mid.astype(F32)).astype(BF16)
    return hi, mid, lo


def _params(sem):
    return pltpu.CompilerParams(dimension_semantics=sem, vmem_limit_bytes=VMEM_LIMIT)


def _const_spec(shape):
    nd = len(shape)
    return pl.BlockSpec(shape, lambda *_: (0,) * nd, pipeline_mode=pl.Buffered(1))


def _swap_perm(a, b):
    n = a * b
    out_row = jnp.arange(n)
    src = (out_row % a) * b + out_row // a
    return (src[:, None] == jnp.arange(n)[None, :]).astype(BF16)


def _mod_kernel(c_ref, cc_ref, w_ref, b_ref, o_ref):
    s = jnp.concatenate([c_ref[...], jnp.broadcast_to(cc_ref[...], (NB, D))], axis=0)
    s = _silu(s)
    s_hi, s_mid, _ = _split3(s)
    w_hi, w_mid, _ = _split3(w_ref[...])
    dot = functools.partial(jnp.dot, preferred_element_type=F32)
    o_ref[...] = dot(s_hi, w_hi) + dot(s_mid, w_hi) + dot(s_hi, w_mid) + b_ref[...]


def _mod_call(c, c_ctx, w_mod, b_mod):
    return pl.pallas_call(
        _mod_kernel,
        grid=(6,),
        in_specs=[
            pl.BlockSpec((NB, D), lambda g: (0, 0)),
            pl.BlockSpec((1, D), lambda g: (0, 0)),
            pl.BlockSpec((D, D), lambda g: (0, g)),
            pl.BlockSpec((1, D), lambda g: (0, g)),
        ],
        out_specs=pl.BlockSpec((2 * NB, D), lambda g: (0, g)),
        out_shape=jax.ShapeDtypeStruct((2 * NB, 6 * D), F32),
        compiler_params=_params(("arbitrary",)),
        name="mod",
    )(c, c_ctx.reshape(1, D), w_mod, b_mod.reshape(1, 6 * D))


def _gelu_tanh(x):
    return jax.nn.gelu(x, approximate=True)


def _identity(x):
    return x


_PROJ_FULL = ((0, _identity), (1, _gelu_tanh), (2, _identity), (3, _sigmoid), (4, _sigmoid), (5, _sigmoid))
_PROJ_CTX = ((0, _identity), (2, _identity))


def _proj_kernel(x_ref, sh_ref, sc_ref, g_ref, *refs, acts):
    w_refs, o_refs = refs[:len(acts)], refs[len(acts):]
    x = x_ref[...]
    ms = jnp.mean(x * x, axis=-1, keepdims=True)
    y = x * lax.rsqrt(ms + EPS) * g_ref[...]
    u = (y * (1.0 + sc_ref[...]) + sh_ref[...]).astype(BF16)
    for w_ref, o_ref, act in zip(w_refs, o_refs, acts):
        p = jnp.dot(u, w_ref[...], preferred_element_type=F32)
        o_ref[...] = act(p).astype(o_ref.dtype)


def _proj_call(x, mod3, norm_g, w_in_bf, *, ctx):
    L = x.shape[1]
    rows = min(PROJ_ROWS, L)
    groups = _PROJ_CTX if ctx else _PROJ_FULL
    mod_row = (lambda b: NB) if ctx else (lambda b: b)
    row_spec = pl.BlockSpec((None, rows, D), lambda b, i: (b, i, 0))

    def group_spec(g):
        return pl.BlockSpec((D, D), lambda b, i: (0, g), pipeline_mode=pl.Buffered(1))

    return pl.pallas_call(
        functools.partial(_proj_kernel, acts=tuple(act for _, act in groups)),
        grid=(NB, L // rows),
        in_specs=[
            row_spec,
            pl.BlockSpec((None, 1, D), lambda b, i: (mod_row(b), 0, 0)),
            pl.BlockSpec((None, 1, D), lambda b, i: (mod_row(b), 0, 1)),
            _const_spec((1, D)),
            *[group_spec(g) for g, _ in groups],
        ],
        out_specs=[row_spec] * len(groups),
        out_shape=[jax.ShapeDtypeStruct((NB, L, D), BF16)] * len(groups),
        compiler_params=_params(("arbitrary", "arbitrary")),
        name="proj_ctx" if ctx else "proj",
    )(x, mod3, mod3, norm_g, *[w_in_bf] * len(groups))


def _rg_kernel(*refs, reverse, n_tiles, mode):
    if mode == "rev":
        (xh_ref, h0_ref, wbd_ref, lam_ref, hf_ref, pout_ref, o_ref, hlast_ref, a_scr, b_scr, h_scr) = refs
    elif mode == "fwd":
        (x_ref, xp_ref, xn_ref, h0_ref, cw_ref, cb_ref, wbd_ref, lam_ref, pin_ref, phalo_ref,
         o_ref, xh_out, hlast_ref, xe_scr, a_scr, b_scr, h_scr) = refs
    else:
        (x_ref, xp_ref, xn_ref, h0_ref, cw_ref, cb_ref, wbd_ref, lam_ref, pin_ref, phalo_ref,
         o_ref, hlast_ref, xe_scr, a_scr, b_scr, h_scr) = refs
    step = pl.program_id(0)
    order = jnp.minimum(step, n_tiles - 1)
    tile = (n_tiles - 1 - order) if reverse else order
    new = step % 2
    old = 1 - new
    rows = RG_STEPS * NB
    halo = 2 * NB

    @pl.when(step == 0)
    def _():
        h_scr[...] = h0_ref[...]
        a_scr[1] = jnp.ones((rows, D), F32)
        b_scr[1] = jnp.zeros((rows, D), F32)

    def coef_stage():
        if mode != "rev":
            def halo_rows(ref):
                xb = jnp.concatenate([ref[b] for b in range(NB)], axis=0)
                return jnp.dot(phalo_ref[...], xb, preferred_element_type=F32)

            xe_scr[0:halo, :] = jnp.where(tile > 0, halo_rows(xp_ref)[(BF16_ROWS - 2) * NB:BF16_ROWS * NB], 0.0)
            xe_scr[halo + rows:2 * halo + rows, :] = jnp.where(tile < n_tiles - 1, halo_rows(xn_ref)[0:halo], 0.0)
            for s in range(rows // RG_SUB):
                t0 = s * RG_SUB_STEPS
                xb = jnp.concatenate([x_ref[b, t0:t0 + RG_SUB_STEPS, :] for b in range(NB)], axis=0)
                xe_scr[halo + s * RG_SUB:halo + (s + 1) * RG_SUB, :] = jnp.dot(
                    pin_ref[...], xb, preferred_element_type=F32)
            cw = 0.5 * cw_ref[...]
            cb = 0.5 * cb_ref[...]

        lam2 = (-0.5 * RG_C * LOG2E) * _softplus(-lam_ref[...])

        ones = jnp.ones((RG_COEF, 128), BF16)

        for s in range(rows // RG_COEF):
            r0 = s * RG_COEF
            if mode == "rev":
                xh = xh_ref[r0:r0 + RG_COEF, :].astype(F32)
            else:
                xh = cb + cw[0:1] * xe_scr[r0:r0 + RG_COEF, :]
                xh = xh + cw[1:2] * xe_scr[r0 + NB:r0 + NB + RG_COEF, :]
                xh = xh + cw[2:3] * xe_scr[r0 + 2 * NB:r0 + 2 * NB + RG_COEF, :]
                xh = xh + cw[3:4] * xe_scr[r0 + 3 * NB:r0 + 3 * NB + RG_COEF, :]
            if mode == "fwd":
                xh_out[r0:r0 + RG_COEF, :] = xh.astype(BF16)
            for p in range(D // 128):
                lo, hi = p * 128, (p + 1) * 128
                xhp = xh[:, lo:hi]
                lhs = jnp.concatenate([xhp.astype(BF16), ones], axis=1)
                pre = jnp.dot(lhs, wbd_ref[p], preferred_element_type=F32)
                t_r = jnp.tanh(pre[:, 0:128])
                t_i = jnp.tanh(pre[:, 128:256])
                a = jnp.exp2(lam2[:, lo:hi] * (t_r + 1.0))
                y = (1.0 - a) * (1.0 + a)
                gain = jnp.where(y > 0.0, y * lax.rsqrt(y), 0.0)
                a_scr[new, r0:r0 + RG_COEF, lo:hi] = a
                b_scr[new, r0:r0 + RG_COEF, lo:hi] = gain * ((t_i + 1.0) * xhp)

    if mode == "rev":
        coef_stage()
    else:
        pl.when(step < n_tiles)(coef_stage)

    h = h_scr[...]
    for t in range(RG_STEPS):
        r0 = ((RG_STEPS - 1 - t) if reverse else t) * NB
        h = a_scr[old, r0:r0 + NB, :] * h + b_scr[old, r0:r0 + NB, :]
        b_scr[old, r0:r0 + NB, :] = h
    h_scr[...] = h
    hlast_ref[...] = h

    if mode == "rev":
        for s in range(rows // RG_SUB):
            r0 = s * RG_SUB
            t0 = s * RG_SUB_STEPS
            hs = (b_scr[old, r0:r0 + RG_SUB, :] + hf_ref[r0:r0 + RG_SUB, :].astype(F32)).astype(BF16)
            hb = jnp.dot(pout_ref[...], hs, preferred_element_type=F32).astype(o_ref.dtype)
            for b in range(NB):
                o_ref[b, t0:t0 + RG_SUB_STEPS, :] = hb[b * RG_SUB_STEPS:(b + 1) * RG_SUB_STEPS]
    else:
        o_ref[...] = b_scr[old].astype(o_ref.dtype)


def _rg_call(src, h0, cw, cb, wbd, lam, *, reverse, mode, h_fwd=None, name):
    L = src.shape[0] // NB if mode == "rev" else src.shape[1]
    rows = RG_STEPS * NB
    n_tiles = L // RG_STEPS
    per = RG_STEPS // BF16_ROWS
    n_halo = L // BF16_ROWS
    def tile_at(lag):
        def tile(i):
            order = jnp.clip(i - lag, 0, n_tiles - 1)
            return (n_tiles - 1 - order) if reverse else order
        return tile

    tile, tile_old = tile_at(0), tile_at(1)
    bm_spec = pl.BlockSpec((NB, RG_STEPS, D), lambda i: (0, tile(i), 0))
    tm_spec = pl.BlockSpec((rows, D), lambda i: (tile(i), 0))
    tm_old_spec = pl.BlockSpec((rows, D), lambda i: (tile_old(i), 0))
    tm_shape = jax.ShapeDtypeStruct((L * NB, D), BF16)
    state_spec = pl.BlockSpec((NB, D), lambda i: (0, 0))
    state_shape = jax.ShapeDtypeStruct((NB, D), F32)
    scratch = [pltpu.VMEM((2, rows, D), F32), pltpu.VMEM((2, rows, D), F32), pltpu.VMEM((NB, D), F32)]
    if mode == "rev":
        in_specs = [tm_spec, _const_spec((NB, D)), _const_spec((D // 128, 256, 256)), _const_spec((1, D)),
                    tm_old_spec, _const_spec((RG_SUB, RG_SUB))]
        args = [src, h0, wbd, lam, h_fwd, _swap_perm(RG_SUB_STEPS, NB)]
        out_specs = [pl.BlockSpec((NB, RG_STEPS, D), lambda i: (0, tile_old(i), 0)), state_spec]
        out_shape = [jax.ShapeDtypeStruct((NB, L, D), BF16), state_shape]
    else:
        in_specs = [
            bm_spec,
            pl.BlockSpec((NB, BF16_ROWS, D), lambda i: (0, jnp.maximum(tile(i) * per - 1, 0), 0)),
            pl.BlockSpec((NB, BF16_ROWS, D), lambda i: (0, jnp.minimum((tile(i) + 1) * per, n_halo - 1), 0)),
            _const_spec((NB, D)),
            _const_spec((4, D)),
            _const_spec((1, D)),
            _const_spec((D // 128, 256, 256)),
            _const_spec((1, D)),
            _const_spec((RG_SUB, RG_SUB)),
            _const_spec((NB * BF16_ROWS, NB * BF16_ROWS)),
        ]
        args = [src, src, src, h0, cw, cb, wbd, lam, _swap_perm(NB, RG_SUB_STEPS), _swap_perm(NB, BF16_ROWS)]
        out_specs = [tm_old_spec] + ([tm_spec] if mode == "fwd" else []) + [state_spec]
        out_shape = [tm_shape] + ([tm_shape] if mode == "fwd" else []) + [state_shape]
        scratch = [pltpu.VMEM((rows + 4 * NB, D), F32)] + scratch
    return pl.pallas_call(
        functools.partial(_rg_kernel, reverse=reverse, n_tiles=n_tiles, mode=mode),
        grid=(n_tiles + 1,),
        in_specs=in_specs,
        out_specs=out_specs,
        out_shape=out_shape,
        scratch_shapes=scratch,
        compiler_params=_params(("arbitrary",)),
        name=name,
    )(*args)


ML_REV_LANE = 8


def _scan_lanes_both(x, op, fill):
    n = x.shape[1]
    lane = lax.broadcasted_iota(jnp.int32, x.shape, 1)
    is_prefix = lax.broadcasted_iota(jnp.int32, x.shape, 0) < ML_REV_LANE
    sh = 1
    while sh < n:
        before = jnp.where(lane >= sh, pltpu.roll(x, sh, axis=1), fill)
        after = jnp.where(lane < n - sh, pltpu.roll(x, n - sh, axis=1), fill)
        x = op(x, jnp.where(is_prefix, before, after))
        sh *= 2
    return x


def _ml_prep_kernel(*refs, n_tiles, n_tok, latent):
    L = ML_CHUNK
    n_chunks = n_tok // L
    if latent:
        (x_ref, xp_ref, xn_ref, perm_ref, cw_ref, cb_ref, dq_ref, dk_ref, dv_ref, wg_ref, bg_ref,
         q_ref, kt_ref, v_ref, xc_ref, gv_ref, grf_ref, grr_ref,
         xt_scr, wqg_scr, wv_scr, wkt_scr, g_scr) = refs
    else:
        (x_ref, cw_ref, cb_ref, dq_ref, dk_ref, dv_ref, wg_ref, bg_ref,
         q_ref, kt_ref, v_ref, xc_ref, gv_ref, grf_ref, grr_ref,
         xt_scr, wqg_scr, wv_scr, wkt_scr, g_scr) = refs
    step = pl.program_id(0)
    tile = jnp.minimum(step, NB * n_tiles - 1) % n_tiles
    first = step == 0
    slot_new = step % 2
    slot_old = 1 - slot_new

    @pl.when(first)
    def _():
        g_scr[1] = jnp.zeros(g_scr.shape[1:], F32)

    src_t = lax.broadcasted_iota(jnp.int32, (L, L), 0)
    dst_t = lax.broadcasted_iota(jnp.int32, (L, L), 1)
    tri_fwd = (src_t <= dst_t).astype(BF16)
    tri_rev = (src_t >= dst_t).astype(BF16)

    def gate_vectors(gates, ck):
        r0 = ck * L
        li_rows = gates[:, 0:128].T[0:2 * ML_REV_LANE]
        lf_rows = _log_sigmoid(gates[:, 128:256].T[0:2 * ML_REV_LANE])
        split = jnp.concatenate(_split3(lf_rows), axis=0)
        to_t = jnp.dot(split, tri_fwd, preferred_element_type=F32)
        from_t = jnp.dot(split, tri_rev, preferred_element_type=F32)
        cum_rows = jnp.concatenate([
            to_t[0:8] + to_t[BF16_ROWS:BF16_ROWS + 8] + to_t[2 * BF16_ROWS:2 * BF16_ROWS + 8],
            from_t[8:16] + from_t[BF16_ROWS + 8:2 * BF16_ROWS] + from_t[2 * BF16_ROWS + 8:3 * BF16_ROWS]], axis=0)
        g_rows = li_rows - cum_rows
        grf_ref[ck] = g_rows[0:ML_REV_LANE]
        grr_ref[ck] = g_rows[ML_REV_LANE:2 * ML_REV_LANE]
        pm_rows = _scan_lanes_both(g_rows, jnp.maximum, -jnp.inf)
        pad = jnp.zeros((128 - 2 * ML_REV_LANE, L), F32)
        gv_ref[0, r0:r0 + L, :] = jnp.concatenate([cum_rows, pad], axis=0).T
        gv_ref[1, r0:r0 + L, :] = jnp.concatenate([pm_rows, pad], axis=0).T

    @pl.when(first)
    def _():
        diff = lax.broadcasted_iota(jnp.int32, (DH, DH), 1) - lax.broadcasted_iota(jnp.int32, (DH, DH), 0)

        def dense(d_ref, h):
            diag = d_ref[h * DH:(h + 1) * DH, :]
            out = jnp.zeros((DH, DH), F32)
            for d in range(-3, 4):
                out = jnp.where(diff == d, diag[:, 3 + d:4 + d], out)
            return out

        dot = functools.partial(jnp.dot, preferred_element_type=F32)
        for h in range(HEADS):
            wq = dense(dq_ref, h).astype(BF16)
            wk_f32 = dense(dk_ref, h) * (DH ** -0.5)
            wk = wk_f32.astype(BF16)
            wv = dense(dv_ref, h).astype(BF16)
            wkt_scr[h] = wk_f32.T.astype(BF16)
            wqg_scr[h, :, 0:DH] = wq
            wqg_scr[h, :, DH:2 * DH] = (dot(wq, wg_ref[h]) + dot(wk, wg_ref[HEADS + h])).astype(BF16)
            wv_scr[h, :, 0:DH] = wv
            wv_scr[h, :, DH:2 * DH] = dot(wv, wg_ref[2 * HEADS + h]).astype(BF16)

    if not latent:
        xt_scr[8:16, :] = jnp.zeros((8, D), F32)
        xt_scr[16:16 + n_tok, :] = x_ref[...].astype(F32)
        xt_scr[16 + n_tok:24 + n_tok, :] = jnp.zeros((8, D), F32)
    else:
        for g in range(GRID_W // BF16_ROWS):
            xg = x_ref[g * BF16_ROWS:(g + 1) * BF16_ROWS].reshape(BF16_ROWS * ML_GROUP_COLS, D)
            yg = jnp.dot(perm_ref[...], xg, preferred_element_type=F32)
            for w in range(ML_GROUP_COLS):
                dst = 16 + w * GRID_W + g * BF16_ROWS
                xt_scr[dst:dst + BF16_ROWS, :] = yg[w * BF16_ROWS:(w + 1) * BF16_ROWS]
        last = ML_GROUP_COLS - 1
        prev = jnp.concatenate([xp_ref[0].astype(F32)[last:last + 1], xp_ref[1].astype(F32)[last:last + 1]], axis=0)
        xt_scr[14:16, :] = jnp.where(tile > 0, prev, 0.0)
        xt_scr[16 + n_tok:17 + n_tok, :] = jnp.where(tile < n_tiles - 1, xn_ref[0].astype(F32)[0:1], 0.0)

    for ck in range(n_chunks):
        gate_vectors(g_scr[slot_old, ck * L:(ck + 1) * L, :], ck)

    cw = cw_ref[...]
    for ck in range(n_chunks):
        r0 = ck * L
        xt = xt_scr[16 + r0:16 + r0 + L, :]
        xc = cb_ref[...] + cw[0:1] * xt_scr[14 + r0:14 + r0 + L, :] + cw[1:2] * xt_scr[15 + r0:15 + r0 + L, :]
        xc = xc + cw[2:3] * xt + cw[3:4] * xt_scr[17 + r0:17 + r0 + L, :]
        xc = _silu(xc)
        xc_ref[r0:r0 + L, :] = xc.astype(BF16)
        gates = jnp.zeros((L, 256), F32) + bg_ref[...]
        for h in range(HEADS):
            lo, hi = h * DH, (h + 1) * DH
            xc_h = xc[:, lo:hi].astype(BF16)
            qg = jnp.dot(xc_h, wqg_scr[h], preferred_element_type=F32)
            vg = jnp.dot(xt[:, lo:hi].astype(BF16), wv_scr[h], preferred_element_type=F32)
            gates = gates + qg[:, DH:2 * DH] + vg[:, DH:2 * DH]
            q_ref[r0:r0 + L, lo:hi] = qg[:, 0:DH].astype(BF16)
            kt = lax.dot_general(wkt_scr[h], xc_h, (((1,), (1,)), ((), ())), preferred_element_type=F32)
            kt_ref[lo:hi, r0:r0 + L] = kt.astype(BF16)
            v_ref[r0:r0 + L, lo:hi] = vg[:, 0:DH].astype(BF16)
        g_scr[slot_new, r0:r0 + L, :] = gates


def _ml_rec_kernel(*refs, reverse, n_tok, mode):
    L = ML_CHUNK
    n_chunks = n_tok // L
    lane0 = ML_REV_LANE if reverse else 0
    if mode == "ctx":
        (q_ref, kt_ref, v_ref, gv_ref, gr_ref, c_out, m_out, c_scr, m_scr) = refs
    elif mode == "fwd":
        (q_ref, kt_ref, v_ref, gv_ref, gr_ref, c0_ref, m0_ref, o_ref, c_scr, m_scr, h_scr) = refs
    else:
        (q_ref, kt_ref, v_ref, gv_ref, gr_ref, c0_ref, m0_ref, hf_ref, xc_ref, ng_ref, sk_ref,
         perm_ref, o_ref, c_scr, m_scr, h_scr) = refs

    @pl.when(pl.program_id(1) == 0)
    def _():
        if mode == "ctx":
            c_scr[...] = jnp.zeros_like(c_scr)
            m_scr[...] = jnp.zeros_like(m_scr)
        else:
            c_scr[...] = c0_ref[...]
            m_scr[...] = m0_ref[...]

    ones_lanes = jnp.ones((L, 128), BF16)
    if mode != "ctx":
        row_id = lax.broadcasted_iota(jnp.int32, (L, L), 0)
        col_id = lax.broadcasted_iota(jnp.int32, (L, L), 1)
        keep = (col_id >= row_id) if reverse else (col_id <= row_id)

    def stage_b(ck):
        r0 = ck * L
        m_prev = m_scr[0:1, :]
        edge = 0 if reverse else L - 1
        b_tot = gv_ref[0, r0 + edge:r0 + edge + 1, :]
        gmax = gv_ref[1, r0 + edge:r0 + edge + 1, :]
        m_new = jnp.maximum(b_tot + m_prev, b_tot + gmax)
        dec = jnp.exp(b_tot + m_prev - m_new)
        if mode != "ctx":
            cum = gv_ref[0, r0:r0 + L, :]
            inter = cum + m_prev
            m_t = jnp.maximum(inter, cum + gv_ref[1, r0:r0 + L, :])
            c_all = cum - m_t
            si_all = jnp.exp(inter - m_t)
            fl_all = jnp.exp(-m_t)
            g_row = gr_ref[ck]
        for h in range(HEADS):
            lo, hi = h * DH, (h + 1) * DH
            ln = lane0 + h
            q = q_ref[r0:r0 + L, lo:hi]
            kt = kt_ref[lo:hi, r0:r0 + L]
            v = jnp.concatenate([v_ref[r0:r0 + L, lo:hi], ones_lanes], axis=1)
            dec_h = dec[:, ln:ln + 1]
            c_old = c_scr[h]
            if mode != "ctx":
                si_c = si_all[:, ln:ln + 1]
                s = jnp.dot(q, kt, preferred_element_type=F32)
                arg = jnp.where(keep, c_all[:, ln:ln + 1] + g_row[h:h + 1, :], -jnp.inf)
                p = (s * jnp.exp(arg)).astype(BF16)
                both = jnp.dot(p, v, preferred_element_type=F32)
                both = both + si_c * jnp.dot(q, c_old.astype(BF16), preferred_element_type=F32)
                scale = 1.0 / jnp.maximum(jnp.abs(both[:, DH:DH + 128]), fl_all[:, ln:ln + 1])
                h_scr[r0:r0 + L, lo:hi] = both[:, 0:DH] * jnp.concatenate([scale, scale], axis=1)
            ws_row = jnp.exp((b_tot[:, ln:ln + 1] - m_new[:, ln:ln + 1]) + gr_ref[ck][h:h + 1, :]).astype(BF16)
            kw_t = kt * ws_row
            c_scr[h] = dec_h * c_old + jnp.dot(kw_t, v, preferred_element_type=F32)
        m_scr[...] = jnp.broadcast_to(m_new, m_scr.shape)

    for ci in range(n_chunks):
        stage_b((n_chunks - 1 - ci) if reverse else ci)

    a_rows = min(ML_A_ROWS, n_tok)
    if mode == "ctx":
        c_out[...] = c_scr[...]
        m_out[...] = m_scr[...]
    elif mode == "fwd":
        o_ref[...] = h_scr[...].astype(o_ref.dtype)
    else:
        for sb in range(n_tok // a_rows):
            r0 = sb * a_rows
            ht = h_scr[r0:r0 + a_rows, :] + hf_ref[r0:r0 + a_rows, :].astype(F32)
            for h in range(HEADS):
                lo, hi = h * DH, (h + 1) * DH
                hh = ht[:, lo:hi]
                mu = jnp.mean(hh, axis=-1, keepdims=True)
                var = jnp.mean(jnp.square(hh - mu), axis=-1, keepdims=True)
                hn = (hh - mu) * lax.rsqrt(var + EPS)
                xc = xc_ref[r0:r0 + a_rows, lo:hi].astype(F32)
                h_scr[r0:r0 + a_rows, lo:hi] = hn * ng_ref[:, lo:hi] + sk_ref[:, lo:hi] * xc
        for g in range(GRID_W // BF16_ROWS):
            zg = jnp.concatenate(
                [h_scr[w * GRID_W + g * BF16_ROWS:w * GRID_W + (g + 1) * BF16_ROWS, :] for w in range(ML_GROUP_COLS)],
                axis=0).astype(BF16)
            og = jnp.dot(perm_ref[...], zg, preferred_element_type=F32).astype(o_ref.dtype)
            o_ref[g * BF16_ROWS:(g + 1) * BF16_ROWS] = og.reshape(BF16_ROWS, ML_GROUP_COLS, D)


def _ml_weight_specs():
    return [
        _const_spec((4, D)),
        _const_spec((1, D)),
        _const_spec((D, 8)),
        _const_spec((D, 8)),
        _const_spec((D, 8)),
        _const_spec((3 * HEADS, DH, 256)),
        _const_spec((1, 256)),
    ]


def _ml_prep_call(mlx, weights, *, latent, name):
    scratch = [
        pltpu.VMEM(((ML_TILE if latent else CTX) + 32, D), F32),
        pltpu.VMEM((HEADS, DH, 2 * DH), BF16),
        pltpu.VMEM((HEADS, DH, 2 * DH), BF16),
        pltpu.VMEM((HEADS, DH, DH), BF16),
    ]
    n_tiles, n_tok, seq = (GRID_W // ML_GROUP_COLS, ML_TILE, SEQ) if latent else (1, CTX, CTX)
    n_flat = NB * n_tiles

    def at(lag):
        def split(step):
            flat = jnp.clip(step - lag, 0, n_flat - 1)
            return flat // n_tiles, flat % n_tiles
        return split

    new, old = at(0), at(1)
    if latent:
        in_specs = [
            pl.BlockSpec((None, GRID_W, ML_GROUP_COLS, D), lambda g: (new(g)[0], 0, new(g)[1], 0)),
            pl.BlockSpec((None, 2, ML_GROUP_COLS, D),
                         lambda g: (new(g)[0], GRID_W // 2 - 1, jnp.maximum(new(g)[1] - 1, 0), 0)),
            pl.BlockSpec((None, 1, ML_GROUP_COLS, D),
                         lambda g: (new(g)[0], 0, jnp.minimum(new(g)[1] + 1, n_tiles - 1), 0)),
            _const_spec((BF16_ROWS * ML_GROUP_COLS, BF16_ROWS * ML_GROUP_COLS)),
        ]
        args = [mlx, mlx, mlx, _swap_perm(BF16_ROWS, ML_GROUP_COLS)]
    else:
        in_specs = [pl.BlockSpec((None, CTX, D), lambda g: (new(g)[0], 0, 0))]
        args = [mlx]
    scratch.append(pltpu.VMEM((2, n_tok, 256), F32))
    tok_spec = pl.BlockSpec((None, n_tok, D), lambda g: (new(g)[0], new(g)[1], 0))
    kt_spec = pl.BlockSpec((None, D, n_tok), lambda g: (new(g)[0], 0, new(g)[1]))
    gv_spec = pl.BlockSpec((None, 2, n_tok, 128), lambda g: (old(g)[0], 0, old(g)[1], 0))
    row_spec = pl.BlockSpec((None, n_tok // ML_CHUNK, 8, ML_CHUNK), lambda g: (old(g)[0], old(g)[1], 0, 0))
    tok_shape = jax.ShapeDtypeStruct((NB, seq, D), BF16)
    row_shape = jax.ShapeDtypeStruct((NB, seq // ML_CHUNK, 8, ML_CHUNK), F32)
    return pl.pallas_call(
        functools.partial(_ml_prep_kernel, n_tiles=n_tiles, n_tok=n_tok, latent=latent),
        grid=(n_flat + 1,),
        in_specs=in_specs + _ml_weight_specs(),
        out_specs=[tok_spec, kt_spec, tok_spec, tok_spec, gv_spec, row_spec, row_spec],
        out_shape=[tok_shape, jax.ShapeDtypeStruct((NB, D, seq), BF16), tok_shape, tok_shape,
                   jax.ShapeDtypeStruct((NB, 2, seq, 128), F32), row_shape, row_shape],
        scratch_shapes=scratch,
        compiler_params=_params(("arbitrary",)),
        name=name,
    )(*args, *weights)


_STATE_SHAPES = [
    jax.ShapeDtypeStruct((NB, HEADS, DH, DH + 128), F32),
    jax.ShapeDtypeStruct((NB, 8, 128), F32),
]


def _ml_rec_call(q, kt, v, gv, g_rows, *, reverse, state=None, merge_with=None, name):
    seq = q.shape[1]
    n_tok = min(ML_TILE, seq)
    n_tiles = seq // n_tok
    n_chunks = n_tok // ML_CHUNK
    tile = (lambda j: n_tiles - 1 - j) if reverse else (lambda j: j)
    tok_spec = pl.BlockSpec((None, n_tok, D), lambda b, j: (b, tile(j), 0))
    state_specs = [
        pl.BlockSpec((None, HEADS, DH, DH + 128), lambda b, j: (b, 0, 0, 0)),
        pl.BlockSpec((None, 8, 128), lambda b, j: (b, 0, 0)),
    ]
    in_specs = [
        tok_spec, pl.BlockSpec((None, D, n_tok), lambda b, j: (b, 0, tile(j))), tok_spec,
        pl.BlockSpec((None, 2, n_tok, 128), lambda b, j: (b, 0, tile(j), 0)),
        pl.BlockSpec((None, n_chunks, 8, ML_CHUNK), lambda b, j: (b, tile(j), 0, 0)),
    ]
    args = [q, kt, v, gv, g_rows]
    scratch = [
        pltpu.VMEM((HEADS, DH, DH + 128), F32),
        pltpu.VMEM((8, 128), F32),
    ]
    if state is None:
        mode = "ctx"
        out_specs, out_shape = state_specs, _STATE_SHAPES
    else:
        in_specs += state_specs
        args += list(state)
        scratch.append(pltpu.VMEM((n_tok, D), F32))
        if merge_with is None:
            mode = "fwd"
            out_specs, out_shape = tok_spec, jax.ShapeDtypeStruct((NB, seq, D), BF16)
        else:
            mode = "rev"
            h_fwd, xc, norm_g, skip = merge_with
            in_specs += [tok_spec, tok_spec, _const_spec((1, D)), _const_spec((1, D)),
                         _const_spec((BF16_ROWS * ML_GROUP_COLS, BF16_ROWS * ML_GROUP_COLS))]
            args += [h_fwd, xc, norm_g, skip, _swap_perm(BF16_ROWS, ML_GROUP_COLS)]
            out_specs = pl.BlockSpec((None, GRID_W, ML_GROUP_COLS, D), lambda b, j: (b, 0, tile(j), 0))
            out_shape = jax.ShapeDtypeStruct((NB, GRID_W, GRID_W, D), BF16)
    return pl.pallas_call(
        functools.partial(_ml_rec_kernel, reverse=reverse, n_tok=n_tok, mode=mode),
        grid=(NB, n_tiles),
        in_specs=in_specs,
        out_specs=out_specs,
        out_shape=out_shape,
        scratch_shapes=scratch,
        compiler_params=_params(("arbitrary", "arbitrary")),
        name=name,
    )(*args)


def _final_kernel(x_ref, hrg_ref, grg_ref, hml_ref, smlo_ref, sgr_ref, sgm_ref, g1_ref, sh2_ref, sc2_ref, g2_ref,
                  n2_ref, nf_ref, wbr_ref, wbm_ref, wo_ref, wfi_ref, wfo_ref, o_ref, act_scr):
    dot = functools.partial(jnp.dot, preferred_element_type=F32)
    y_rg = (hrg_ref[...].astype(F32) * grg_ref[...].astype(F32)).astype(BF16)
    y_ml = (hml_ref[...].astype(F32) * smlo_ref[...].astype(F32)).astype(BF16)
    mix = sgr_ref[...].astype(F32) * dot(y_rg, wbr_ref[...])
    mix = mix + sgm_ref[...].astype(F32) * dot(y_ml, wbm_ref[...])
    x1 = x_ref[...] + g1_ref[...] * dot(mix.astype(BF16), wo_ref[...])
    ms = jnp.mean(x1 * x1, axis=-1, keepdims=True)
    hn = x1 * lax.rsqrt(ms + EPS) * n2_ref[...]
    hb = (hn * (1.0 + sc2_ref[...]) + sh2_ref[...]).astype(BF16)
    step = 256
    for c in range(D_FF // step):
        gate = dot(hb, wfi_ref[:, c * step:(c + 1) * step])
        up = dot(hb, wfi_ref[:, D_FF + c * step:D_FF + (c + 1) * step])
        act_scr[:, c * step:(c + 1) * step] = (_silu(gate) * up).astype(BF16)
    x2 = x1 + g2_ref[...] * dot(act_scr[...], wfo_ref[...])
    ms2 = jnp.mean(x2 * x2, axis=-1, keepdims=True)
    o_ref[...] = x2 * lax.rsqrt(ms2 + EPS) * nf_ref[...]


def _final_call(x, h_rg, grg, h_ml, smlo, sgr, sgm, mod3, norm2_g, final_g, wbr, wbm, wo, wfi, wfo):
    rows = FINAL_ROWS
    row_spec = pl.BlockSpec((None, rows, D), lambda b, i: (b, i, 0))
    mod_spec = lambda g: pl.BlockSpec((None, 1, D), lambda b, i: (b, 0, g))
    return pl.pallas_call(
        _final_kernel,
        grid=(NB, SEQ // rows),
        in_specs=[
            row_spec, row_spec, row_spec, row_spec, row_spec, row_spec, row_spec,
            mod_spec(2), mod_spec(3), mod_spec(4), mod_spec(5),
            _const_spec((1, D)), _const_spec((1, D)),
            _const_spec((D, D)), _const_spec((D, D)), _const_spec((D, D)),
            _const_spec((D, 2 * D_FF)), _const_spec((D_FF, D)),
        ],
        out_specs=row_spec,
        out_shape=jax.ShapeDtypeStruct((NB, SEQ, D), F32),
        scratch_shapes=[pltpu.VMEM((rows, D_FF), BF16)],
        compiler_params=_params(("arbitrary", "arbitrary")),
        name="final",
    )(x, h_rg, grg, h_ml, smlo, sgr, sgm, mod3, mod3, mod3, mod3, norm2_g, final_g, wbr, wbm, wo, wfi, wfo)


def _pair_blockdiag(w):
    w = w.reshape(8, 2, 64, 64)
    z = jnp.zeros((8, 64, 64), w.dtype)
    top = jnp.concatenate([w[:, 0], z], axis=2)
    bot = jnp.concatenate([z, w[:, 1]], axis=2)
    return jnp.concatenate([top, bot], axis=1)


def _rg_weights(wa, ba, wx, bx):
    w = jnp.concatenate([_pair_blockdiag(wa), _pair_blockdiag(wx)], axis=2).astype(BF16)
    bias = 0.5 * jnp.concatenate([ba.reshape(8, 1, 128), bx.reshape(8, 1, 128)], axis=2)
    b_hi = bias.astype(BF16)
    b_lo = (bias - b_hi.astype(F32)).astype(BF16)
    return (jnp.concatenate([w, b_hi, b_lo, jnp.zeros((8, 126, 256), BF16)], axis=1),)


def _block_diagonals(w):
    rows = [jnp.pad(w[:, i, :], ((0, 0), (3 - i, 1 + i))) for i in range(4)]
    return jnp.stack(rows, axis=1).reshape(D, 8)


def _gate_weights(wi, bi, wf, bf):
    def lanes(x):
        gap = [(0, 0)] * (x.ndim - 2)
        fwd = jnp.pad(x[0], gap + [(0, ML_REV_LANE - HEADS)])
        rev = jnp.pad(x[1], gap + [(0, 128 - ML_REV_LANE - HEADS)])
        return jnp.concatenate([fwd, rev], axis=-1)
    w = jnp.concatenate([lanes(wi), lanes(wf)], axis=-1)
    b = jnp.concatenate([lanes(bi), lanes(bf)], axis=-1).reshape(1, 256)
    return w.reshape(3 * HEADS, DH, 256).astype(BF16), b


def kernel(x, c, ctx, c_ctx, w_mod, b_mod, norm1_g, norm2_g, w_in, rg_conv_w, rg_conv_b, rg_wa, rg_ba, rg_wx,
           rg_bx, rg_lambda, ml_conv_w, ml_conv_b, ml_wq, ml_wk, ml_wv, ml_wi, ml_bi, ml_wf, ml_bf,
           ml_norm_g, ml_skip, w_branch_rg, w_branch_ml, w_out, w_ffn_in, w_ffn_out, final_norm_g):
    mod = _mod_call(c, c_ctx, w_mod[0], b_mod[0])
    mod3 = mod.reshape(2 * NB, 1, 6 * D)
    w_in_bf = w_in[0].astype(BF16)
    norm1 = norm1_g[0].reshape(1, D)

    rgx, grg, mlx, smlo, sgr, sgm = _proj_call(x, mod3, norm1, w_in_bf, ctx=False)
    rgx_c, mlx_c = _proj_call(ctx, mod3, norm1, w_in_bf, ctx=True)

    rg_cw = rg_conv_w[0]
    rg_cb = rg_conv_b[0].reshape(1, D)
    zero_h = jnp.zeros((NB, D), F32)
    rg_w = [_rg_weights(rg_wa[0, d], rg_ba[0, d], rg_wx[0, d], rg_bx[0, d]) + (rg_lambda[0, d].reshape(1, D),)
            for d in range(2)]
    _, h0_f = _rg_call(rgx_c, zero_h, rg_cw, rg_cb, *rg_w[0], reverse=False, mode="ctx", name="rg_ctx_fwd")
    _, h0_r = _rg_call(rgx_c, zero_h, rg_cw, rg_cb, *rg_w[1], reverse=True, mode="ctx", name="rg_ctx_rev")
    h_f, rg_xh, _ = _rg_call(rgx, h0_f, rg_cw, rg_cb, *rg_w[0], reverse=False, mode="fwd", name="rg_fwd")
    h_rg, _ = _rg_call(rg_xh, h0_r, rg_cw, rg_cb, *rg_w[1], reverse=True, mode="rev", h_fwd=h_f, name="rg_rev")

    ml_cw = ml_conv_w[0]
    ml_cb = ml_conv_b[0].reshape(1, D)
    ml_qkv = (_block_diagonals(ml_wq[0]), _block_diagonals(ml_wk[0]), _block_diagonals(ml_wv[0]))
    ml_w = (ml_cw, ml_cb) + ml_qkv + _gate_weights(ml_wi[0], ml_bi[0], ml_wf[0], ml_bf[0])
    q_c, kt_c, v_c, _, gv_c, grf_c, grr_c = _ml_prep_call(mlx_c, ml_w, latent=False, name="ml_prep_ctx")
    st_f = _ml_rec_call(q_c, kt_c, v_c, gv_c, grf_c, reverse=False, name="ml_ctx_fwd")
    st_r = _ml_rec_call(q_c, kt_c, v_c, gv_c, grr_c, reverse=True, name="ml_ctx_rev")
    q_l, kt_l, v_l, xc_l, gv_l, grf_l, grr_l = _ml_prep_call(
        mlx.reshape(NB, GRID_W, GRID_W, D), ml_w, latent=True, name="ml_prep")
    hm_f = _ml_rec_call(q_l, kt_l, v_l, gv_l, grf_l, reverse=False, state=st_f, name="ml_fwd")
    h_ml = _ml_rec_call(q_l, kt_l, v_l, gv_l, grr_l, reverse=True, state=st_r,
                        merge_with=(hm_f, xc_l, ml_norm_g[0].reshape(1, D), ml_skip[0].reshape(1, D)), name="ml_rev")

    return _final_call(
        x, h_rg, grg, h_ml.reshape(NB, SEQ, D), smlo, sgr, sgm, mod3,
        norm2_g[0].reshape(1, D), final_norm_g.reshape(1, D),
        w_branch_rg[0].astype(BF16), w_branch_ml[0].astype(BF16), w_out[0].astype(BF16),
        w_ffn_in[0].astype(BF16), w_ffn_out[0].astype(BF16))
```

```python
import functools

import jax
import jax.numpy as jnp
from jax import lax
from jax.experimental import pallas as pl
from jax.experimental.pallas import tpu as pltpu

F32 = jnp.float32
BF16 = jnp.bfloat16

D = 1024
NB = 8
SEQ = 4096
GRID_W = 64
CTX = 256
EPS = 1e-6
RG_C = 8.0
HEADS = 4
DH = D // HEADS
D_FF = 2816
N_IN = 6 * D
LOG2E = 1.4426950408889634
F32_TINY = 1.1754944e-38

VMEM_LIMIT = 60 * 1024 * 1024
BF16_ROWS = 16

PROJ_ROWS = 1024
FINAL_ROWS = 512
RG_STEPS = 128
RG_SUB = 256
RG_SUB_STEPS = RG_SUB // NB
RG_COEF = 128
ML_CHUNK = 256
ML_GROUP_COLS = 16
ML_TILE = ML_GROUP_COLS * GRID_W
ML_A_ROWS = 512

assert ML_CHUNK == DH


def _sigmoid(x):
    return 0.5 * (jnp.tanh(0.5 * x) + 1.0)


def _silu(x):
    half = 0.5 * x
    return half * (jnp.tanh(half) + 1.0)


def _softplus(x):
    return jnp.maximum(x, 0.0) + jnp.log(1.0 + jnp.exp(-jnp.abs(x)))


def _log_sigmoid(x):
    return jnp.minimum(x, 0.0) - jnp.log(1.0 + jnp.exp(-jnp.abs(x)))


def _split3(x):
    hi = x.astype(BF16)
    r1 = x - hi.astype(F32)
    mid = r1.astype(BF16)
    lo = (r1 - mid.astype(F32)).astype(BF16)
    return hi, mid, lo


def _params(sem):
    return pltpu.CompilerParams(dimension_semantics=sem, vmem_limit_bytes=VMEM_LIMIT)


def _const_spec(shape):
    nd = len(shape)
    return pl.BlockSpec(shape, lambda *_: (0,) * nd, pipeline_mode=pl.Buffered(1))


def _swap_perm(a, b):
    n = a * b
    out_row = jnp.arange(n)
    src = (out_row % a) * b + out_row // a
    return (src[:, None] == jnp.arange(n)[None, :]).astype(BF16)


def _mod_kernel(c_ref, cc_ref, w_ref, b_ref, o_ref):
    s = jnp.concatenate([c_ref[...], jnp.broadcast_to(cc_ref[...], (NB, D))], axis=0)
    s = _silu(s)
    s_hi, s_mid, _ = _split3(s)
    w_hi, w_mid, _ = _split3(w_ref[...])
    dot = functools.partial(jnp.dot, preferred_element_type=F32)
    o_ref[...] = dot(s_hi, w_hi) + dot(s_mid, w_hi) + dot(s_hi, w_mid) + b_ref[...]


def _mod_call(c, c_ctx, w_mod, b_mod):
    return pl.pallas_call(
        _mod_kernel,
        grid=(6,),
        in_specs=[
            pl.BlockSpec((NB, D), lambda g: (0, 0)),
            pl.BlockSpec((1, D), lambda g: (0, 0)),
            pl.BlockSpec((D, D), lambda g: (0, g)),
            pl.BlockSpec((1, D), lambda g: (0, g)),
        ],
        out_specs=pl.BlockSpec((2 * NB, D), lambda g: (0, g)),
        out_shape=jax.ShapeDtypeStruct((2 * NB, 6 * D), F32),
        compiler_params=_params(("arbitrary",)),
        name="mod",
    )(c, c_ctx.reshape(1, D), w_mod, b_mod.reshape(1, 6 * D))


def _gelu_tanh(x):
    return jax.nn.gelu(x, approximate=True)


def _identity(x):
    return x


_PROJ_FULL = ((0, _identity), (1, _gelu_tanh), (2, _identity), (3, _sigmoid), (4, _sigmoid), (5, _sigmoid))
_PROJ_CTX = ((0, _identity), (2, _identity))


def _proj_kernel(x_ref, sh_ref, sc_ref, g_ref, *refs, acts):
    w_refs, o_refs = refs[:len(acts)], refs[len(acts):]
    x = x_ref[...]
    ms = jnp.mean(x * x, axis=-1, keepdims=True)
    y = x * lax.rsqrt(ms + EPS) * g_ref[...]
    u = (y * (1.0 + sc_ref[...]) + sh_ref[...]).astype(BF16)
    for w_ref, o_ref, act in zip(w_refs, o_refs, acts):
        p = jnp.dot(u, w_ref[...], preferred_element_type=F32)
        o_ref[...] = act(p).astype(o_ref.dtype)


def _proj_call(x, mod3, norm_g, w_in_bf, *, ctx):
    L = x.shape[1]
    rows = min(PROJ_ROWS, L)
    groups = _PROJ_CTX if ctx else _PROJ_FULL
    mod_row = (lambda b: NB) if ctx else (lambda b: b)
    row_spec = pl.BlockSpec((None, rows, D), lambda b, i: (b, i, 0))

    def group_spec(g):
        return pl.BlockSpec((D, D), lambda b, i: (0, g), pipeline_mode=pl.Buffered(1))

    return pl.pallas_call(
        functools.partial(_proj_kernel, acts=tuple(act for _, act in groups)),
        grid=(NB, L // rows),
        in_specs=[
            row_spec,
            pl.BlockSpec((None, 1, D), lambda b, i: (mod_row(b), 0, 0)),
            pl.BlockSpec((None, 1, D), lambda b, i: (mod_row(b), 0, 1)),
            _const_spec((1, D)),
            *[group_spec(g) for g, _ in groups],
        ],
        out_specs=[row_spec] * len(groups),
        out_shape=[jax.ShapeDtypeStruct((NB, L, D), BF16)] * len(groups),
        compiler_params=_params(("arbitrary", "arbitrary")),
        name="proj_ctx" if ctx else "proj",
    )(x, mod3, mod3, norm_g, *[w_in_bf] * len(groups))


def _rg_kernel(*refs, reverse, n_tiles, mode):
    if mode == "rev":
        (xh_ref, h0_ref, wbd_ref, lam_ref, hf_ref, pout_ref, o_ref, hlast_ref, a_scr, b_scr, h_scr) = refs
    elif mode == "fwd":
        (x_ref, xp_ref, xn_ref, h0_ref, cw_ref, cb_ref, wbd_ref, lam_ref, pin_ref, phalo_ref,
         o_ref, xh_out, hlast_ref, xe_scr, a_scr, b_scr, h_scr) = refs
    else:
        (x_ref, xp_ref, xn_ref, h0_ref, cw_ref, cb_ref, wbd_ref, lam_ref, pin_ref, phalo_ref,
         o_ref, hlast_ref, xe_scr, a_scr, b_scr, h_scr) = refs
    step = pl.program_id(0)
    order = jnp.minimum(step, n_tiles - 1)
    tile = (n_tiles - 1 - order) if reverse else order
    new = step % 2
    old = 1 - new
    rows = RG_STEPS * NB
    halo = 2 * NB

    @pl.when(step == 0)
    def _():
        h_scr[...] = h0_ref[...]
        a_scr[1] = jnp.ones((rows, D), F32)
        b_scr[1] = jnp.zeros((rows, D), F32)

    def coef_stage():
        if mode != "rev":
            def halo_rows(ref):
                xb = jnp.concatenate([ref[b] for b in range(NB)], axis=0)
                return jnp.dot(phalo_ref[...], xb, preferred_element_type=F32)

            xe_scr[0:halo, :] = jnp.where(tile > 0, halo_rows(xp_ref)[(BF16_ROWS - 2) * NB:BF16_ROWS * NB], 0.0)
            xe_scr[halo + rows:2 * halo + rows, :] = jnp.where(tile < n_tiles - 1, halo_rows(xn_ref)[0:halo], 0.0)
            for s in range(rows // RG_SUB):
                t0 = s * RG_SUB_STEPS
                xb = jnp.concatenate([x_ref[b, t0:t0 + RG_SUB_STEPS, :] for b in range(NB)], axis=0)
                xe_scr[halo + s * RG_SUB:halo + (s + 1) * RG_SUB, :] = jnp.dot(
                    pin_ref[...], xb, preferred_element_type=F32)
            cw = 0.5 * cw_ref[...]
            cb = 0.5 * cb_ref[...]

        lam2 = (-0.5 * RG_C * LOG2E) * _softplus(-lam_ref[...])

        ones = jnp.ones((RG_COEF, 128), BF16)

        for s in range(rows // RG_COEF):
            r0 = s * RG_COEF
            if mode == "rev":
                xh = xh_ref[r0:r0 + RG_COEF, :].astype(F32)
            else:
                xh = cb + cw[0:1] * xe_scr[r0:r0 + RG_COEF, :]
                xh = xh + cw[1:2] * xe_scr[r0 + NB:r0 + NB + RG_COEF, :]
                xh = xh + cw[2:3] * xe_scr[r0 + 2 * NB:r0 + 2 * NB + RG_COEF, :]
                xh = xh + cw[3:4] * xe_scr[r0 + 3 * NB:r0 + 3 * NB + RG_COEF, :]
            if mode == "fwd":
                xh_out[r0:r0 + RG_COEF, :] = xh.astype(BF16)
            for p in range(D // 128):
                lo, hi = p * 128, (p + 1) * 128
                xhp = xh[:, lo:hi]
                lhs = jnp.concatenate([xhp.astype(BF16), ones], axis=1)
                pre = jnp.dot(lhs, wbd_ref[p], preferred_element_type=F32)
                t_r = jnp.tanh(pre[:, 0:128])
                t_i = jnp.tanh(pre[:, 128:256])
                a = jnp.exp2(lam2[:, lo:hi] * (t_r + 1.0))
                y = jnp.maximum(1.0 - a * a, F32_TINY)
                gain = y * lax.rsqrt(y)
                a_scr[new, r0:r0 + RG_COEF, lo:hi] = a
                b_scr[new, r0:r0 + RG_COEF, lo:hi] = gain * ((t_i + 1.0) * xhp)

    if mode == "rev":
        coef_stage()
    else:
        pl.when(step < n_tiles)(coef_stage)

    h = h_scr[...]
    for t in range(RG_STEPS):
        r0 = ((RG_STEPS - 1 - t) if reverse else t) * NB
        h = a_scr[old, r0:r0 + NB, :] * h + b_scr[old, r0:r0 + NB, :]
        b_scr[old, r0:r0 + NB, :] = h
    h_scr[...] = h
    hlast_ref[...] = h

    if mode == "rev":
        for s in range(rows // RG_SUB):
            r0 = s * RG_SUB
            t0 = s * RG_SUB_STEPS
            hs = (b_scr[old, r0:r0 + RG_SUB, :] + hf_ref[r0:r0 + RG_SUB, :].astype(F32)).astype(BF16)
            hb = jnp.dot(pout_ref[...], hs, preferred_element_type=F32).astype(o_ref.dtype)
            for b in range(NB):
                o_ref[b, t0:t0 + RG_SUB_STEPS, :] = hb[b * RG_SUB_STEPS:(b + 1) * RG_SUB_STEPS]
    else:
        o_ref[...] = b_scr[old].astype(o_ref.dtype)


def _rg_call(src, h0, cw, cb, wbd, lam, *, reverse, mode, h_fwd=None, name):
    L = src.shape[0] // NB if mode == "rev" else src.shape[1]
    rows = RG_STEPS * NB
    n_tiles = L // RG_STEPS
    per = RG_STEPS // BF16_ROWS
    n_halo = L // BF16_ROWS
    def tile_at(lag):
        def tile(i):
            order = jnp.clip(i - lag, 0, n_tiles - 1)
            return (n_tiles - 1 - order) if reverse else order
        return tile

    tile, tile_old = tile_at(0), tile_at(1)
    bm_spec = pl.BlockSpec((NB, RG_STEPS, D), lambda i: (0, tile(i), 0))
    tm_spec = pl.BlockSpec((rows, D), lambda i: (tile(i), 0))
    tm_old_spec = pl.BlockSpec((rows, D), lambda i: (tile_old(i), 0))
    tm_shape = jax.ShapeDtypeStruct((L * NB, D), BF16)
    state_spec = pl.BlockSpec((NB, D), lambda i: (0, 0))
    state_shape = jax.ShapeDtypeStruct((NB, D), F32)
    scratch = [pltpu.VMEM((2, rows, D), F32), pltpu.VMEM((2, rows, D), F32), pltpu.VMEM((NB, D), F32)]
    if mode == "rev":
        in_specs = [tm_spec, _const_spec((NB, D)), _const_spec((D // 128, 256, 256)), _const_spec((1, D)),
                    tm_old_spec, _const_spec((RG_SUB, RG_SUB))]
        args = [src, h0, wbd, lam, h_fwd, _swap_perm(RG_SUB_STEPS, NB)]
        out_specs = [pl.BlockSpec((NB, RG_STEPS, D), lambda i: (0, tile_old(i), 0)), state_spec]
        out_shape = [jax.ShapeDtypeStruct((NB, L, D), BF16), state_shape]
    else:
        in_specs = [
            bm_spec,
            pl.BlockSpec((NB, BF16_ROWS, D), lambda i: (0, jnp.maximum(tile(i) * per - 1, 0), 0)),
            pl.BlockSpec((NB, BF16_ROWS, D), lambda i: (0, jnp.minimum((tile(i) + 1) * per, n_halo - 1), 0)),
            _const_spec((NB, D)),
            _const_spec((4, D)),
            _const_spec((1, D)),
            _const_spec((D // 128, 256, 256)),
            _const_spec((1, D)),
            _const_spec((RG_SUB, RG_SUB)),
            _const_spec((NB * BF16_ROWS, NB * BF16_ROWS)),
        ]
        args = [src, src, src, h0, cw, cb, wbd, lam, _swap_perm(NB, RG_SUB_STEPS), _swap_perm(NB, BF16_ROWS)]
        out_specs = [tm_old_spec] + ([tm_spec] if mode == "fwd" else []) + [state_spec]
        out_shape = [tm_shape] + ([tm_shape] if mode == "fwd" else []) + [state_shape]
        scratch = [pltpu.VMEM((rows + 4 * NB, D), F32)] + scratch
    return pl.pallas_call(
        functools.partial(_rg_kernel, reverse=reverse, n_tiles=n_tiles, mode=mode),
        grid=(n_tiles + 1,),
        in_specs=in_specs,
        out_specs=out_specs,
        out_shape=out_shape,
        scratch_shapes=scratch,
        compiler_params=_params(("arbitrary",)),
        name=name,
    )(*args)


ML_REV_LANE = 8


def _scan_lanes_both(x, op, fill):
    n = x.shape[1]
    lane = lax.broadcasted_iota(jnp.int32, x.shape, 1)
    is_prefix = lax.broadcasted_iota(jnp.int32, x.shape, 0) < ML_REV_LANE
    sh = 1
    while sh < n:
        before = jnp.where(lane >= sh, pltpu.roll(x, sh, axis=1), fill)
        after = jnp.where(lane < n - sh, pltpu.roll(x, n - sh, axis=1), fill)
        x = op(x, jnp.where(is_prefix, before, after))
        sh *= 2
    return x


def _ml_prep_kernel(*refs, n_tiles, n_tok, latent):
    L = ML_CHUNK
    n_chunks = n_tok // L
    if latent:
        (x_ref, xp_ref, xn_ref, perm_ref, cw_ref, cb_ref, dq_ref, dk_ref, dv_ref, wg_ref, bg_ref,
         q_ref, kt_ref, v_ref, xc_ref, gv_ref, grf_ref, grr_ref,
         xt_scr, wqg_scr, wv_scr, wkt_scr, g_scr) = refs
    else:
        (x_ref, cw_ref, cb_ref, dq_ref, dk_ref, dv_ref, wg_ref, bg_ref,
         q_ref, kt_ref, v_ref, xc_ref, gv_ref, grf_ref, grr_ref,
         xt_scr, wqg_scr, wv_scr, wkt_scr, g_scr) = refs
    step = pl.program_id(0)
    tile = jnp.minimum(step, NB * n_tiles - 1) % n_tiles
    first = step == 0
    slot_new = step % 2
    slot_old = 1 - slot_new

    @pl.when(first)
    def _():
        g_scr[1] = jnp.zeros(g_scr.shape[1:], F32)

    src_t = lax.broadcasted_iota(jnp.int32, (L, L), 0)
    dst_t = lax.broadcasted_iota(jnp.int32, (L, L), 1)
    tri_fwd = (src_t <= dst_t).astype(BF16)
    tri_rev = (src_t >= dst_t).astype(BF16)

    def gate_vectors(gates, ck):
        r0 = ck * L
        li_rows = gates[:, 0:128].T[0:2 * ML_REV_LANE]
        lf_rows = _log_sigmoid(gates[:, 128:256].T[0:2 * ML_REV_LANE])
        split = jnp.concatenate(_split3(lf_rows), axis=0)
        to_t = jnp.dot(split, tri_fwd, preferred_element_type=F32)
        from_t = jnp.dot(split, tri_rev, preferred_element_type=F32)
        cum_rows = jnp.concatenate([
            to_t[0:8] + to_t[BF16_ROWS:BF16_ROWS + 8] + to_t[2 * BF16_ROWS:2 * BF16_ROWS + 8],
            from_t[8:16] + from_t[BF16_ROWS + 8:2 * BF16_ROWS] + from_t[2 * BF16_ROWS + 8:3 * BF16_ROWS]], axis=0)
        g_rows = li_rows - cum_rows
        grf_ref[ck] = g_rows[0:ML_REV_LANE]
        grr_ref[ck] = g_rows[ML_REV_LANE:2 * ML_REV_LANE]
        pm_rows = _scan_lanes_both(g_rows, jnp.maximum, -jnp.inf)
        pad = jnp.zeros((128 - 2 * ML_REV_LANE, L), F32)
        gv_ref[0, r0:r0 + L, :] = jnp.concatenate([cum_rows, pad], axis=0).T
        gv_ref[1, r0:r0 + L, :] = jnp.concatenate([pm_rows, pad], axis=0).T

    @pl.when(first)
    def _():
        diff = lax.broadcasted_iota(jnp.int32, (DH, DH), 1) - lax.broadcasted_iota(jnp.int32, (DH, DH), 0)

        def dense(d_ref, h):
            diag = d_ref[h * DH:(h + 1) * DH, :]
            out = jnp.zeros((DH, DH), F32)
            for d in range(-3, 4):
                out = jnp.where(diff == d, diag[:, 3 + d:4 + d], out)
            return out

        dot = functools.partial(jnp.dot, preferred_element_type=F32)
        for h in range(HEADS):
            wq = dense(dq_ref, h).astype(BF16)
            wk_f32 = dense(dk_ref, h) * (DH ** -0.5)
            wk = wk_f32.astype(BF16)
            wv = dense(dv_ref, h).astype(BF16)
            wkt_scr[h] = wk_f32.T.astype(BF16)
            wqg_scr[h, :, 0:DH] = wq
            wqg_scr[h, :, DH:2 * DH] = (dot(wq, wg_ref[h]) + dot(wk, wg_ref[HEADS + h])).astype(BF16)
            wv_scr[h, :, 0:DH] = wv
            wv_scr[h, :, DH:2 * DH] = dot(wv, wg_ref[2 * HEADS + h]).astype(BF16)

    if not latent:
        xt_scr[8:16, :] = jnp.zeros((8, D), F32)
        xt_scr[16:16 + n_tok, :] = x_ref[...].astype(F32)
        xt_scr[16 + n_tok:24 + n_tok, :] = jnp.zeros((8, D), F32)
    else:
        for g in range(GRID_W // BF16_ROWS):
            xg = x_ref[g * BF16_ROWS:(g + 1) * BF16_ROWS].reshape(BF16_ROWS * ML_GROUP_COLS, D)
            yg = jnp.dot(perm_ref[...], xg, preferred_element_type=F32)
            for w in range(ML_GROUP_COLS):
                dst = 16 + w * GRID_W + g * BF16_ROWS
                xt_scr[dst:dst + BF16_ROWS, :] = yg[w * BF16_ROWS:(w + 1) * BF16_ROWS]
        last = ML_GROUP_COLS - 1
        prev = jnp.concatenate([xp_ref[0].astype(F32)[last:last + 1], xp_ref[1].astype(F32)[last:last + 1]], axis=0)
        xt_scr[14:16, :] = jnp.where(tile > 0, prev, 0.0)
        xt_scr[16 + n_tok:17 + n_tok, :] = jnp.where(tile < n_tiles - 1, xn_ref[0].astype(F32)[0:1], 0.0)

    for ck in range(n_chunks):
        gate_vectors(g_scr[slot_old, ck * L:(ck + 1) * L, :], ck)

    cw = cw_ref[...]
    for ck in range(n_chunks):
        r0 = ck * L
        xt = xt_scr[16 + r0:16 + r0 + L, :]
        xc = cb_ref[...] + cw[0:1] * xt_scr[14 + r0:14 + r0 + L, :] + cw[1:2] * xt_scr[15 + r0:15 + r0 + L, :]
        xc = xc + cw[2:3] * xt + cw[3:4] * xt_scr[17 + r0:17 + r0 + L, :]
        xc = _silu(xc)
        xc_ref[r0:r0 + L, :] = xc.astype(BF16)
        gates = jnp.zeros((L, 256), F32) + bg_ref[...]
        for h in range(HEADS):
            lo, hi = h * DH, (h + 1) * DH
            xc_h = xc[:, lo:hi].astype(BF16)
            qg = jnp.dot(xc_h, wqg_scr[h], preferred_element_type=F32)
            vg = jnp.dot(xt[:, lo:hi].astype(BF16), wv_scr[h], preferred_element_type=F32)
            gates = gates + qg[:, DH:2 * DH] + vg[:, DH:2 * DH]
            q_ref[r0:r0 + L, lo:hi] = qg[:, 0:DH].astype(BF16)
            kt = lax.dot_general(wkt_scr[h], xc_h, (((1,), (1,)), ((), ())), preferred_element_type=F32)
            kt_ref[lo:hi, r0:r0 + L] = kt.astype(BF16)
            v_ref[r0:r0 + L, lo:hi] = vg[:, 0:DH].astype(BF16)
        g_scr[slot_new, r0:r0 + L, :] = gates


def _ml_rec_kernel(*refs, reverse, n_tok, mode):
    L = ML_CHUNK
    n_chunks = n_tok // L
    lane0 = ML_REV_LANE if reverse else 0
    if mode == "ctx":
        (q_ref, kt_ref, v_ref, gv_ref, gr_ref, c_out, m_out, c_scr, m_scr) = refs
    elif mode == "fwd":
        (q_ref, kt_ref, v_ref, gv_ref, gr_ref, c0_ref, m0_ref, o_ref, c_scr, m_scr, h_scr) = refs
    else:
        (q_ref, kt_ref, v_ref, gv_ref, gr_ref, c0_ref, m0_ref, hf_ref, xc_ref, ng_ref, sk_ref,
         perm_ref, o_ref, c_scr, m_scr, h_scr) = refs

    @pl.when(pl.program_id(1) == 0)
    def _():
        if mode == "ctx":
            c_scr[...] = jnp.zeros_like(c_scr)
            m_scr[...] = jnp.zeros_like(m_scr)
        else:
            c_scr[...] = c0_ref[...]
            m_scr[...] = m0_ref[...]

    ones_lanes = jnp.ones((L, 128), BF16)
    if mode != "ctx":
        row_id = lax.broadcasted_iota(jnp.int32, (L, L), 0)
        col_id = lax.broadcasted_iota(jnp.int32, (L, L), 1)
        keep = (col_id >= row_id) if reverse else (col_id <= row_id)

    def stage_b(ck):
        r0 = ck * L
        m_prev = m_scr[0:1, :]
        edge = 0 if reverse else L - 1
        b_tot = gv_ref[0, r0 + edge:r0 + edge + 1, :]
        gmax = gv_ref[1, r0 + edge:r0 + edge + 1, :]
        m_new = jnp.maximum(b_tot + m_prev, b_tot + gmax)
        dec = jnp.exp(b_tot + m_prev - m_new)
        if mode != "ctx":
            cum = gv_ref[0, r0:r0 + L, :]
            inter = cum + m_prev
            m_t = jnp.maximum(inter, cum + gv_ref[1, r0:r0 + L, :])
            c_all = cum - m_t
            si_all = jnp.exp(inter - m_t)
            fl_all = jnp.exp(-m_t)
            g_row = gr_ref[ck]
        for h in range(HEADS):
            lo, hi = h * DH, (h + 1) * DH
            ln = lane0 + h
            q = q_ref[r0:r0 + L, lo:hi]
            kt = kt_ref[lo:hi, r0:r0 + L]
            v = jnp.concatenate([v_ref[r0:r0 + L, lo:hi], ones_lanes], axis=1)
            dec_h = dec[:, ln:ln + 1]
            c_old = c_scr[h]
            if mode != "ctx":
                si_c = si_all[:, ln:ln + 1]
                s = jnp.dot(q, kt, preferred_element_type=F32)
                arg = jnp.where(keep, c_all[:, ln:ln + 1] + g_row[h:h + 1, :], -jnp.inf)
                p = (s * jnp.exp(arg)).astype(BF16)
                both = jnp.dot(p, v, preferred_element_type=F32)
                both = both + si_c * jnp.dot(q, c_old.astype(BF16), preferred_element_type=F32)
                scale = 1.0 / jnp.maximum(jnp.abs(both[:, DH:DH + 128]), fl_all[:, ln:ln + 1])
                h_scr[r0:r0 + L, lo:hi] = both[:, 0:DH] * jnp.concatenate([scale, scale], axis=1)
            ws_row = jnp.exp((b_tot[:, ln:ln + 1] - m_new[:, ln:ln + 1]) + gr_ref[ck][h:h + 1, :]).astype(BF16)
            kw_t = kt * ws_row
            c_scr[h] = dec_h * c_old + jnp.dot(kw_t, v, preferred_element_type=F32)
        m_scr[...] = jnp.broadcast_to(m_new, m_scr.shape)

    for ci in range(n_chunks):
        stage_b((n_chunks - 1 - ci) if reverse else ci)

    a_rows = min(ML_A_ROWS, n_tok)
    if mode == "ctx":
        c_out[...] = c_scr[...]
        m_out[...] = m_scr[...]
    elif mode == "fwd":
        o_ref[...] = h_scr[...].astype(o_ref.dtype)
    else:
        for sb in range(n_tok // a_rows):
            r0 = sb * a_rows
            ht = h_scr[r0:r0 + a_rows, :] + hf_ref[r0:r0 + a_rows, :].astype(F32)
            for h in range(HEADS):
                lo, hi = h * DH, (h + 1) * DH
                hh = ht[:, lo:hi]
                mu = jnp.mean(hh, axis=-1, keepdims=True)
                var = jnp.mean(jnp.square(hh - mu), axis=-1, keepdims=True)
                hn = (hh - mu) * lax.rsqrt(var + EPS)
                xc = xc_ref[r0:r0 + a_rows, lo:hi].astype(F32)
                h_scr[r0:r0 + a_rows, lo:hi] = hn * ng_ref[:, lo:hi] + sk_ref[:, lo:hi] * xc
        for g in range(GRID_W // BF16_ROWS):
            zg = jnp.concatenate(
                [h_scr[w * GRID_W + g * BF16_ROWS:w * GRID_W + (g + 1) * BF16_ROWS, :] for w in range(ML_GROUP_COLS)],
                axis=0).astype(BF16)
            og = jnp.dot(perm_ref[...], zg, preferred_element_type=F32).astype(o_ref.dtype)
            o_ref[g * BF16_ROWS:(g + 1) * BF16_ROWS] = og.reshape(BF16_ROWS, ML_GROUP_COLS, D)


def _ml_weight_specs():
    return [
        _const_spec((4, D)),
        _const_spec((1, D)),
        _const_spec((D, 8)),
        _const_spec((D, 8)),
        _const_spec((D, 8)),
        _const_spec((3 * HEADS, DH, 256)),
        _const_spec((1, 256)),
    ]


def _ml_prep_call(mlx, weights, *, latent, name):
    scratch = [
        pltpu.VMEM(((ML_TILE if latent else CTX) + 32, D), F32),
        pltpu.VMEM((HEADS, DH, 2 * DH), BF16),
        pltpu.VMEM((HEADS, DH, 2 * DH), BF16),
        pltpu.VMEM((HEADS, DH, DH), BF16),
    ]
    n_tiles, n_tok, seq = (GRID_W // ML_GROUP_COLS, ML_TILE, SEQ) if latent else (1, CTX, CTX)
    n_flat = NB * n_tiles

    def at(lag):
        def split(step):
            flat = jnp.clip(step - lag, 0, n_flat - 1)
            return flat // n_tiles, flat % n_tiles
        return split

    new, old = at(0), at(1)
    if latent:
        in_specs = [
            pl.BlockSpec((None, GRID_W, ML_GROUP_COLS, D), lambda g: (new(g)[0], 0, new(g)[1], 0)),
            pl.BlockSpec((None, 2, ML_GROUP_COLS, D),
                         lambda g: (new(g)[0], GRID_W // 2 - 1, jnp.maximum(new(g)[1] - 1, 0), 0)),
            pl.BlockSpec((None, 1, ML_GROUP_COLS, D),
                         lambda g: (new(g)[0], 0, jnp.minimum(new(g)[1] + 1, n_tiles - 1), 0)),
            _const_spec((BF16_ROWS * ML_GROUP_COLS, BF16_ROWS * ML_GROUP_COLS)),
        ]
        args = [mlx, mlx, mlx, _swap_perm(BF16_ROWS, ML_GROUP_COLS)]
    else:
        in_specs = [pl.BlockSpec((None, CTX, D), lambda g: (new(g)[0], 0, 0))]
        args = [mlx]
    scratch.append(pltpu.VMEM((2, n_tok, 256), F32))
    tok_spec = pl.BlockSpec((None, n_tok, D), lambda g: (new(g)[0], new(g)[1], 0))
    kt_spec = pl.BlockSpec((None, D, n_tok), lambda g: (new(g)[0], 0, new(g)[1]))
    gv_spec = pl.BlockSpec((None, 2, n_tok, 128), lambda g: (old(g)[0], 0, old(g)[1], 0))
    row_spec = pl.BlockSpec((None, n_tok // ML_CHUNK, 8, ML_CHUNK), lambda g: (old(g)[0], old(g)[1], 0, 0))
    tok_shape = jax.ShapeDtypeStruct((NB, seq, D), BF16)
    row_shape = jax.ShapeDtypeStruct((NB, seq // ML_CHUNK, 8, ML_CHUNK), F32)
    return pl.pallas_call(
        functools.partial(_ml_prep_kernel, n_tiles=n_tiles, n_tok=n_tok, latent=latent),
        grid=(n_flat + 1,),
        in_specs=in_specs + _ml_weight_specs(),
        out_specs=[tok_spec, kt_spec, tok_spec, tok_spec, gv_spec, row_spec, row_spec],
        out_shape=[tok_shape, jax.ShapeDtypeStruct((NB, D, seq), BF16), tok_shape, tok_shape,
                   jax.ShapeDtypeStruct((NB, 2, seq, 128), F32), row_shape, row_shape],
        scratch_shapes=scratch,
        compiler_params=_params(("arbitrary",)),
        name=name,
    )(*args, *weights)


_STATE_SHAPES = [
    jax.ShapeDtypeStruct((NB, HEADS, DH, DH + 128), F32),
    jax.ShapeDtypeStruct((NB, 8, 128), F32),
]


def _ml_rec_call(q, kt, v, gv, g_rows, *, reverse, state=None, merge_with=None, name):
    seq = q.shape[1]
    n_tok = min(ML_TILE, seq)
    n_tiles = seq // n_tok
    n_chunks = n_tok // ML_CHUNK
    tile = (lambda j: n_tiles - 1 - j) if reverse else (lambda j: j)
    tok_spec = pl.BlockSpec((None, n_tok, D), lambda b, j: (b, tile(j), 0))
    state_specs = [
        pl.BlockSpec((None, HEADS, DH, DH + 128), lambda b, j: (b, 0, 0, 0)),
        pl.BlockSpec((None, 8, 128), lambda b, j: (b, 0, 0)),
    ]
    in_specs = [
        tok_spec, pl.BlockSpec((None, D, n_tok), lambda b, j: (b, 0, tile(j))), tok_spec,
        pl.BlockSpec((None, 2, n_tok, 128), lambda b, j: (b, 0, tile(j), 0)),
        pl.BlockSpec((None, n_chunks, 8, ML_CHUNK), lambda b, j: (b, tile(j), 0, 0)),
    ]
    args = [q, kt, v, gv, g_rows]
    scratch = [
        pltpu.VMEM((HEADS, DH, DH + 128), F32),
        pltpu.VMEM((8, 128), F32),
    ]
    if state is None:
        mode = "ctx"
        out_specs, out_shape = state_specs, _STATE_SHAPES
    else:
        in_specs += state_specs
        args += list(state)
        scratch.append(pltpu.VMEM((n_tok, D), F32))
        if merge_with is None:
            mode = "fwd"
            out_specs, out_shape = tok_spec, jax.ShapeDtypeStruct((NB, seq, D), BF16)
        else:
            mode = "rev"
            h_fwd, xc, norm_g, skip = merge_with
            in_specs += [tok_spec, tok_spec, _const_spec((1, D)), _const_spec((1, D)),
                         _const_spec((BF16_ROWS * ML_GROUP_COLS, BF16_ROWS * ML_GROUP_COLS))]
            args += [h_fwd, xc, norm_g, skip, _swap_perm(BF16_ROWS, ML_GROUP_COLS)]
            out_specs = pl.BlockSpec((None, GRID_W, ML_GROUP_COLS, D), lambda b, j: (b, 0, tile(j), 0))
            out_shape = jax.ShapeDtypeStruct((NB, GRID_W, GRID_W, D), BF16)
    return pl.pallas_call(
        functools.partial(_ml_rec_kernel, reverse=reverse, n_tok=n_tok, mode=mode),
        grid=(NB, n_tiles),
        in_specs=in_specs,
        out_specs=out_specs,
        out_shape=out_shape,
        scratch_shapes=scratch,
        compiler_params=_params(("arbitrary", "arbitrary")),
        name=name,
    )(*args)


def _final_kernel(x_ref, hrg_ref, grg_ref, hml_ref, smlo_ref, sgr_ref, sgm_ref, g1_ref, sh2_ref, sc2_ref, g2_ref,
                  n2_ref, nf_ref, wbr_ref, wbm_ref, wo_ref, wfi_ref, wfo_ref, o_ref, act_scr):
    dot = functools.partial(jnp.dot, preferred_element_type=F32)
    y_rg = (hrg_ref[...].astype(F32) * grg_ref[...].astype(F32)).astype(BF16)
    y_ml = (hml_ref[...].astype(F32) * smlo_ref[...].astype(F32)).astype(BF16)
    mix = sgr_ref[...].astype(F32) * dot(y_rg, wbr_ref[...])
    mix = mix + sgm_ref[...].astype(F32) * dot(y_ml, wbm_ref[...])
    x1 = x_ref[...] + g1_ref[...] * dot(mix.astype(BF16), wo_ref[...])
    ms = jnp.mean(x1 * x1, axis=-1, keepdims=True)
    hn = x1 * lax.rsqrt(ms + EPS) * n2_ref[...]
    hb = (hn * (1.0 + sc2_ref[...]) + sh2_ref[...]).astype(BF16)
    step = 256
    for c in range(D_FF // step):
        gate = dot(hb, wfi_ref[:, c * step:(c + 1) * step])
        up = dot(hb, wfi_ref[:, D_FF + c * step:D_FF + (c + 1) * step])
        act_scr[:, c * step:(c + 1) * step] = (_silu(gate) * up).astype(BF16)
    x2 = x1 + g2_ref[...] * dot(act_scr[...], wfo_ref[...])
    ms2 = jnp.mean(x2 * x2, axis=-1, keepdims=True)
    o_ref[...] = x2 * lax.rsqrt(ms2 + EPS) * nf_ref[...]


def _final_call(x, h_rg, grg, h_ml, smlo, sgr, sgm, mod3, norm2_g, final_g, wbr, wbm, wo, wfi, wfo):
    rows = FINAL_ROWS
    row_spec = pl.BlockSpec((None, rows, D), lambda b, i: (b, i, 0))
    mod_spec = lambda g: pl.BlockSpec((None, 1, D), lambda b, i: (b, 0, g))
    return pl.pallas_call(
        _final_kernel,
        grid=(NB, SEQ // rows),
        in_specs=[
            row_spec, row_spec, row_spec, row_spec, row_spec, row_spec, row_spec,
            mod_spec(2), mod_spec(3), mod_spec(4), mod_spec(5),
            _const_spec((1, D)), _const_spec((1, D)),
            _const_spec((D, D)), _const_spec((D, D)), _const_spec((D, D)),
            _const_spec((D, 2 * D_FF)), _const_spec((D_FF, D)),
        ],
        out_specs=row_spec,
        out_shape=jax.ShapeDtypeStruct((NB, SEQ, D), F32),
        scratch_shapes=[pltpu.VMEM((rows, D_FF), BF16)],
        compiler_params=_params(("arbitrary", "arbitrary")),
        name="final",
    )(x, h_rg, grg, h_ml, smlo, sgr, sgm, mod3, mod3, mod3, mod3, norm2_g, final_g, wbr, wbm, wo, wfi, wfo)


def _pair_blockdiag(w):
    w = w.reshape(8, 2, 64, 64)
    z = jnp.zeros((8, 64, 64), w.dtype)
    top = jnp.concatenate([w[:, 0], z], axis=2)
    bot = jnp.concatenate([z, w[:, 1]], axis=2)
    return jnp.concatenate([top, bot], axis=1)


def _rg_weights(wa, ba, wx, bx):
    w = jnp.concatenate([_pair_blockdiag(wa), _pair_blockdiag(wx)], axis=2).astype(BF16)
    bias = 0.5 * jnp.concatenate([ba.reshape(8, 1, 128), bx.reshape(8, 1, 128)], axis=2)
    b_hi = bias.astype(BF16)
    b_lo = (bias - b_hi.astype(F32)).astype(BF16)
    return (jnp.concatenate([w, b_hi, b_lo, jnp.zeros((8, 126, 256), BF16)], axis=1),)


def _block_diagonals(w):
    rows = [jnp.pad(w[:, i, :], ((0, 0), (3 - i, 1 + i))) for i in range(4)]
    return jnp.stack(rows, axis=1).reshape(D, 8)


def _gate_weights(wi, bi, wf, bf):
    def lanes(x):
        gap = [(0, 0)] * (x.ndim - 2)
        fwd = jnp.pad(x[0], gap + [(0, ML_REV_LANE - HEADS)])
        rev = jnp.pad(x[1], gap + [(0, 128 - ML_REV_LANE - HEADS)])
        return jnp.concatenate([fwd, rev], axis=-1)
    w = jnp.concatenate([lanes(wi), lanes(wf)], axis=-1)
    b = jnp.concatenate([lanes(bi), lanes(bf)], axis=-1).reshape(1, 256)
    return w.reshape(3 * HEADS, DH, 256).astype(BF16), b


def kernel(x, c, ctx, c_ctx, w_mod, b_mod, norm1_g, norm2_g, w_in, rg_conv_w, rg_conv_b, rg_wa, rg_ba, rg_wx,
           rg_bx, rg_lambda, ml_conv_w, ml_conv_b, ml_wq, ml_wk, ml_wv, ml_wi, ml_bi, ml_wf, ml_bf,
           ml_norm_g, ml_skip, w_branch_rg, w_branch_ml, w_out, w_ffn_in, w_ffn_out, final_norm_g):
    mod = _mod_call(c, c_ctx, w_mod[0], b_mod[0])
    mod3 = mod.reshape(2 * NB, 1, 6 * D)
    w_in_bf = w_in[0].astype(BF16)
    norm1 = norm1_g[0].reshape(1, D)

    rgx, grg, mlx, smlo, sgr, sgm = _proj_call(x, mod3, norm1, w_in_bf, ctx=False)
    rgx_c, mlx_c = _proj_call(ctx, mod3, norm1, w_in_bf, ctx=True)

    rg_cw = rg_conv_w[0]
    rg_cb = rg_conv_b[0].reshape(1, D)
    zero_h = jnp.zeros((NB, D), F32)
    rg_w = [_rg_weights(rg_wa[0, d], rg_ba[0, d], rg_wx[0, d], rg_bx[0, d]) + (rg_lambda[0, d].reshape(1, D),)
            for d in range(2)]
    _, h0_f = _rg_call(rgx_c, zero_h, rg_cw, rg_cb, *rg_w[0], reverse=False, mode="ctx", name="rg_ctx_fwd")
    _, h0_r = _rg_call(rgx_c, zero_h, rg_cw, rg_cb, *rg_w[1], reverse=True, mode="ctx", name="rg_ctx_rev")
    h_f, rg_xh, _ = _rg_call(rgx, h0_f, rg_cw, rg_cb, *rg_w[0], reverse=False, mode="fwd", name="rg_fwd")
    h_rg, _ = _rg_call(rg_xh, h0_r, rg_cw, rg_cb, *rg_w[1], reverse=True, mode="rev", h_fwd=h_f, name="rg_rev")

    ml_cw = ml_conv_w[0]
    ml_cb = ml_conv_b[0].reshape(1, D)
    ml_qkv = (_block_diagonals(ml_wq[0]), _block_diagonals(ml_wk[0]), _block_diagonals(ml_wv[0]))
    ml_w = (ml_cw, ml_cb) + ml_qkv + _gate_weights(ml_wi[0], ml_bi[0], ml_wf[0], ml_bf[0])
    q_c, kt_c, v_c, _, gv_c, grf_c, grr_c = _ml_prep_call(mlx_c, ml_w, latent=False, name="ml_prep_ctx")
    st_f = _ml_rec_call(q_c, kt_c, v_c, gv_c, grf_c, reverse=False, name="ml_ctx_fwd")
    st_r = _ml_rec_call(q_c, kt_c, v_c, gv_c, grr_c, reverse=True, name="ml_ctx_rev")
    q_l, kt_l, v_l, xc_l, gv_l, grf_l, grr_l = _ml_prep_call(
        mlx.reshape(NB, GRID_W, GRID_W, D), ml_w, latent=True, name="ml_prep")
    hm_f = _ml_rec_call(q_l, kt_l, v_l, gv_l, grf_l, reverse=False, state=st_f, name="ml_fwd")
    h_ml = _ml_rec_call(q_l, kt_l, v_l, gv_l, grr_l, reverse=True, state=st_r,
                        merge_with=(hm_f, xc_l, ml_norm_g[0].reshape(1, D), ml_skip[0].reshape(1, D)), name="ml_rev")

    return _final_call(
        x, h_rg, grg, h_ml.reshape(NB, SEQ, D), smlo, sgr, sgm, mod3,
        norm2_g[0].reshape(1, D), final_norm_g.reshape(1, D),
        w_branch_rg[0].astype(BF16), w_branch_ml[0].astype(BF16), w_out[0].astype(BF16),
        w_ffn_in[0].astype(BF16), w_ffn_out[0].astype(BF16))
```

```python
import functools

import jax
import jax.numpy as jnp
from jax import lax
from jax.experimental import pallas as pl
from jax.experimental.pallas import tpu as pltpu

F32 = jnp.float32
BF16 = jnp.bfloat16

D = 1024
NB = 8
SEQ = 4096
GRID_W = 64
CTX = 256
EPS = 1e-6
RG_C = 8.0
HEADS = 4
DH = D // HEADS
D_FF = 2816
N_IN = 6 * D
LOG2E = 1.4426950408889634
F32_TINY = 1.1754944e-38

VMEM_LIMIT = 60 * 1024 * 1024
BF16_ROWS = 16

PROJ_ROWS = 1024
FINAL_ROWS = 512
RG_STEPS = 128
RG_SUB = 256
RG_SUB_STEPS = RG_SUB // NB
RG_COEF = 128
ML_CHUNK = 256
ML_GROUP_COLS = 16
ML_TILE = ML_GROUP_COLS * GRID_W
ML_A_ROWS = 512

assert ML_CHUNK == DH


def _sigmoid(x):
    return 0.5 * (jnp.tanh(0.5 * x) + 1.0)


def _silu(x):
    half = 0.5 * x
    return half * (jnp.tanh(half) + 1.0)


def _softplus(x):
    return jnp.maximum(x, 0.0) + jnp.log(1.0 + jnp.exp(-jnp.abs(x)))


def _log_sigmoid(x):
    return jnp.minimum(x, 0.0) - jnp.log(1.0 + jnp.exp(-jnp.abs(x)))


def _split3(x):
    hi = x.astype(BF16)
    r1 = x - hi.astype(F32)
    mid = r1.astype(BF16)
    lo = (r1 - mid.astype(F32)).astype(BF16)
    return hi, mid, lo


def _params(sem):
    return pltpu.CompilerParams(dimension_semantics=sem, vmem_limit_bytes=VMEM_LIMIT)


def _const_spec(shape):
    nd = len(shape)
    return pl.BlockSpec(shape, lambda *_: (0,) * nd, pipeline_mode=pl.Buffered(1))


def _swap_perm(a, b):
    n = a * b
    out_row = jnp.arange(n)
    src = (out_row % a) * b + out_row // a
    return (src[:, None] == jnp.arange(n)[None, :]).astype(BF16)


def _mod_kernel(c_ref, cc_ref, w_ref, b_ref, o_ref):
    s = jnp.concatenate([c_ref[...], jnp.broadcast_to(cc_ref[...], (NB, D))], axis=0)
    s = _silu(s)
    s_hi, s_mid, _ = _split3(s)
    w_hi, w_mid, _ = _split3(w_ref[...])
    dot = functools.partial(jnp.dot, preferred_element_type=F32)
    o_ref[...] = dot(s_hi, w_hi) + dot(s_mid, w_hi) + dot(s_hi, w_mid) + b_ref[...]


def _mod_call(c, c_ctx, w_mod, b_mod):
    return pl.pallas_call(
        _mod_kernel,
        grid=(6,),
        in_specs=[
            pl.BlockSpec((NB, D), lambda g: (0, 0)),
            pl.BlockSpec((1, D), lambda g: (0, 0)),
            pl.BlockSpec((D, D), lambda g: (0, g)),
            pl.BlockSpec((1, D), lambda g: (0, g)),
        ],
        out_specs=pl.BlockSpec((2 * NB, D), lambda g: (0, g)),
        out_shape=jax.ShapeDtypeStruct((2 * NB, 6 * D), F32),
        compiler_params=_params(("arbitrary",)),
        name="mod",
    )(c, c_ctx.reshape(1, D), w_mod, b_mod.reshape(1, 6 * D))


def _gelu_tanh(x):
    return jax.nn.gelu(x, approximate=True)


def _identity(x):
    return x


_PROJ_FULL = ((0, _identity), (1, _gelu_tanh), (2, _identity), (3, _sigmoid), (4, _sigmoid), (5, _sigmoid))
_PROJ_CTX = ((0, _identity), (2, _identity))


def _proj_kernel(x_ref, sh_ref, sc_ref, g_ref, w_ref, *refs, acts):
    o_refs, w_scr = refs[:len(acts)], refs[len(acts)]
    step = pl.program_id(0)

    @pl.when(step < len(acts))
    def _():
        w_scr[step] = w_ref[...].astype(BF16)

    @pl.when(step >= len(acts))
    def _():
        x = x_ref[...]
        ms = jnp.mean(x * x, axis=-1, keepdims=True)
        y = x * lax.rsqrt(ms + EPS) * g_ref[...]
        u = (y * (1.0 + sc_ref[...]) + sh_ref[...]).astype(BF16)
        for k, (o_ref, act) in enumerate(zip(o_refs, acts)):
            p = jnp.dot(u, w_scr[k], preferred_element_type=F32)
            o_ref[...] = act(p).astype(o_ref.dtype)


def _proj_call(x, mod3, norm_g, w_in, *, ctx):
    L = x.shape[1]
    rows = min(PROJ_ROWS, L)
    n_row_tiles = L // rows
    groups = _PROJ_CTX if ctx else _PROJ_FULL
    n_groups = len(groups)
    group_stride = groups[1][0] - groups[0][0]
    assert all(g == k * group_stride for k, (g, _) in enumerate(groups))
    mod_row = (lambda b: NB) if ctx else (lambda b: b)

    def tile(step):
        t = jnp.maximum(step - n_groups, 0)
        return t // n_row_tiles, t % n_row_tiles

    row_spec = pl.BlockSpec((None, rows, D), lambda s: (*tile(s), 0))
    return pl.pallas_call(
        functools.partial(_proj_kernel, acts=tuple(act for _, act in groups)),
        grid=(n_groups + NB * n_row_tiles,),
        in_specs=[
            row_spec,
            pl.BlockSpec((None, 1, D), lambda s: (mod_row(tile(s)[0]), 0, 0)),
            pl.BlockSpec((None, 1, D), lambda s: (mod_row(tile(s)[0]), 0, 1)),
            _const_spec((1, D)),
            pl.BlockSpec((D, D), lambda s: (0, group_stride * jnp.minimum(s, n_groups - 1))),
        ],
        out_specs=[row_spec] * n_groups,
        out_shape=[jax.ShapeDtypeStruct((NB, L, D), BF16)] * n_groups,
        scratch_shapes=[pltpu.VMEM((n_groups, D, D), BF16)],
        compiler_params=_params(("arbitrary",)),
        name="proj_ctx" if ctx else "proj",
    )(x, mod3, mod3, norm_g, w_in)


def _rg_kernel(*refs, reverse, n_tiles, mode):
    if mode == "rev":
        (xh_ref, h0_ref, wbd_ref, lam_ref, hf_ref, pout_ref, o_ref, hlast_ref, a_scr, b_scr, h_scr) = refs
    elif mode == "fwd":
        (x_ref, xp_ref, xn_ref, h0_ref, cw_ref, cb_ref, wbd_ref, lam_ref, pin_ref, phalo_ref,
         o_ref, xh_out, hlast_ref, xe_scr, a_scr, b_scr, h_scr) = refs
    else:
        (x_ref, xp_ref, xn_ref, h0_ref, cw_ref, cb_ref, wbd_ref, lam_ref, pin_ref, phalo_ref,
         o_ref, hlast_ref, xe_scr, a_scr, b_scr, h_scr) = refs
    step = pl.program_id(0)
    order = jnp.minimum(step, n_tiles - 1)
    tile = (n_tiles - 1 - order) if reverse else order
    new = step % 2
    old = 1 - new
    rows = RG_STEPS * NB
    halo = 2 * NB

    @pl.when(step == 0)
    def _():
        h_scr[...] = h0_ref[...]
        a_scr[1] = jnp.ones((rows, D), F32)
        b_scr[1] = jnp.zeros((rows, D), F32)

    def coef_stage():
        if mode != "rev":
            def halo_rows(ref):
                xb = jnp.concatenate([ref[b] for b in range(NB)], axis=0)
                return jnp.dot(phalo_ref[...], xb, preferred_element_type=F32)

            xe_scr[0:halo, :] = jnp.where(tile > 0, halo_rows(xp_ref)[(BF16_ROWS - 2) * NB:BF16_ROWS * NB], 0.0)
            xe_scr[halo + rows:2 * halo + rows, :] = jnp.where(tile < n_tiles - 1, halo_rows(xn_ref)[0:halo], 0.0)
            for s in range(rows // RG_SUB):
                t0 = s * RG_SUB_STEPS
                xb = jnp.concatenate([x_ref[b, t0:t0 + RG_SUB_STEPS, :] for b in range(NB)], axis=0)
                xe_scr[halo + s * RG_SUB:halo + (s + 1) * RG_SUB, :] = jnp.dot(
                    pin_ref[...], xb, preferred_element_type=F32)
            cw = 0.5 * cw_ref[...]
            cb = 0.5 * cb_ref[...]

        lam2 = (-0.5 * RG_C * LOG2E) * _softplus(-lam_ref[...])

        ones = jnp.ones((RG_COEF, 128), BF16)

        for s in range(rows // RG_COEF):
            r0 = s * RG_COEF
            if mode == "rev":
                xh = xh_ref[r0:r0 + RG_COEF, :].astype(F32)
            else:
                xh = cb + cw[0:1] * xe_scr[r0:r0 + RG_COEF, :]
                xh = xh + cw[1:2] * xe_scr[r0 + NB:r0 + NB + RG_COEF, :]
                xh = xh + cw[2:3] * xe_scr[r0 + 2 * NB:r0 + 2 * NB + RG_COEF, :]
                xh = xh + cw[3:4] * xe_scr[r0 + 3 * NB:r0 + 3 * NB + RG_COEF, :]
            if mode == "fwd":
                xh_out[r0:r0 + RG_COEF, :] = xh.astype(BF16)
            for p in range(D // 128):
                lo, hi = p * 128, (p + 1) * 128
                xhp = xh[:, lo:hi]
                lhs = jnp.concatenate([xhp.astype(BF16), ones], axis=1)
                pre = jnp.dot(lhs, wbd_ref[p], preferred_element_type=F32)
                t_r = jnp.tanh(pre[:, 0:128])
                t_i = jnp.tanh(pre[:, 128:256])
                a = jnp.exp2(lam2[:, lo:hi] * (t_r + 1.0))
                y = jnp.maximum(1.0 - a * a, F32_TINY)
                gain = y * lax.rsqrt(y)
                a_scr[new, r0:r0 + RG_COEF, lo:hi] = a
                b_scr[new, r0:r0 + RG_COEF, lo:hi] = gain * ((t_i + 1.0) * xhp)

    if mode == "rev":
        coef_stage()
    else:
        pl.when(step < n_tiles)(coef_stage)

    h = h_scr[...]
    for t in range(RG_STEPS):
        r0 = ((RG_STEPS - 1 - t) if reverse else t) * NB
        h = a_scr[old, r0:r0 + NB, :] * h + b_scr[old, r0:r0 + NB, :]
        b_scr[old, r0:r0 + NB, :] = h
    h_scr[...] = h
    hlast_ref[...] = h

    if mode == "rev":
        for s in range(rows // RG_SUB):
            r0 = s * RG_SUB
            t0 = s * RG_SUB_STEPS
            hs = (b_scr[old, r0:r0 + RG_SUB, :] + hf_ref[r0:r0 + RG_SUB, :].astype(F32)).astype(BF16)
            hb = jnp.dot(pout_ref[...], hs, preferred_element_type=F32).astype(o_ref.dtype)
            for b in range(NB):
                o_ref[b, t0:t0 + RG_SUB_STEPS, :] = hb[b * RG_SUB_STEPS:(b + 1) * RG_SUB_STEPS]
    else:
        o_ref[...] = b_scr[old].astype(o_ref.dtype)


def _rg_call(src, h0, cw, cb, wbd, lam, *, reverse, mode, h_fwd=None, name):
    L = src.shape[0] // NB if mode == "rev" else src.shape[1]
    rows = RG_STEPS * NB
    n_tiles = L // RG_STEPS
    per = RG_STEPS // BF16_ROWS
    n_halo = L // BF16_ROWS
    def tile_at(lag):
        def tile(i):
            order = jnp.clip(i - lag, 0, n_tiles - 1)
            return (n_tiles - 1 - order) if reverse else order
        return tile

    tile, tile_old = tile_at(0), tile_at(1)
    bm_spec = pl.BlockSpec((NB, RG_STEPS, D), lambda i: (0, tile(i), 0))
    tm_spec = pl.BlockSpec((rows, D), lambda i: (tile(i), 0))
    tm_old_spec = pl.BlockSpec((rows, D), lambda i: (tile_old(i), 0))
    tm_shape = jax.ShapeDtypeStruct((L * NB, D), BF16)
    state_spec = pl.BlockSpec((NB, D), lambda i: (0, 0))
    state_shape = jax.ShapeDtypeStruct((NB, D), F32)
    scratch = [pltpu.VMEM((2, rows, D), F32), pltpu.VMEM((2, rows, D), F32), pltpu.VMEM((NB, D), F32)]
    if mode == "rev":
        in_specs = [tm_spec, _const_spec((NB, D)), _const_spec((D // 128, 256, 256)), _const_spec((1, D)),
                    tm_old_spec, _const_spec((RG_SUB, RG_SUB))]
        args = [src, h0, wbd, lam, h_fwd, _swap_perm(RG_SUB_STEPS, NB)]
        out_specs = [pl.BlockSpec((NB, RG_STEPS, D), lambda i: (0, tile_old(i), 0)), state_spec]
        out_shape = [jax.ShapeDtypeStruct((NB, L, D), BF16), state_shape]
    else:
        in_specs = [
            bm_spec,
            pl.BlockSpec((NB, BF16_ROWS, D), lambda i: (0, jnp.maximum(tile(i) * per - 1, 0), 0)),
            pl.BlockSpec((NB, BF16_ROWS, D), lambda i: (0, jnp.minimum((tile(i) + 1) * per, n_halo - 1), 0)),
            _const_spec((NB, D)),
            _const_spec((4, D)),
            _const_spec((1, D)),
            _const_spec((D // 128, 256, 256)),
            _const_spec((1, D)),
            _const_spec((RG_SUB, RG_SUB)),
            _const_spec((NB * BF16_ROWS, NB * BF16_ROWS)),
        ]
        args = [src, src, src, h0, cw, cb, wbd, lam, _swap_perm(NB, RG_SUB_STEPS), _swap_perm(NB, BF16_ROWS)]
        out_specs = [tm_old_spec] + ([tm_spec] if mode == "fwd" else []) + [state_spec]
        out_shape = [tm_shape] + ([tm_shape] if mode == "fwd" else []) + [state_shape]
        scratch = [pltpu.VMEM((rows + 4 * NB, D), F32)] + scratch
    return pl.pallas_call(
        functools.partial(_rg_kernel, reverse=reverse, n_tiles=n_tiles, mode=mode),
        grid=(n_tiles + 1,),
        in_specs=in_specs,
        out_specs=out_specs,
        out_shape=out_shape,
        scratch_shapes=scratch,
        compiler_params=_params(("arbitrary",)),
        name=name,
    )(*args)


ML_REV_LANE = 8


def _scan_lanes_both(x, op, fill):
    n = x.shape[1]
    lane = lax.broadcasted_iota(jnp.int32, x.shape, 1)
    is_prefix = lax.broadcasted_iota(jnp.int32, x.shape, 0) < ML_REV_LANE
    sh = 1
    while sh < n:
        before = jnp.where(lane >= sh, pltpu.roll(x, sh, axis=1), fill)
        after = jnp.where(lane < n - sh, pltpu.roll(x, n - sh, axis=1), fill)
        x = op(x, jnp.where(is_prefix, before, after))
        sh *= 2
    return x


def _ml_prep_kernel(*refs, n_tiles, n_tok, latent):
    L = ML_CHUNK
    n_chunks = n_tok // L
    if latent:
        (x_ref, xp_ref, xn_ref, perm_ref, cw_ref, cb_ref, dq_ref, dk_ref, dv_ref, wg_ref, bg_ref,
         q_ref, kt_ref, v_ref, xc_ref, gv_ref, grf_ref, grr_ref,
         xt_scr, wqg_scr, wv_scr, wkt_scr, g_scr) = refs
    else:
        (x_ref, cw_ref, cb_ref, dq_ref, dk_ref, dv_ref, wg_ref, bg_ref,
         q_ref, kt_ref, v_ref, xc_ref, gv_ref, grf_ref, grr_ref,
         xt_scr, wqg_scr, wv_scr, wkt_scr, g_scr) = refs
    step = pl.program_id(0)
    tile = jnp.minimum(step, NB * n_tiles - 1) % n_tiles
    first = step == 0
    slot_new = step % 2
    slot_old = 1 - slot_new

    @pl.when(first)
    def _():
        g_scr[1] = jnp.zeros(g_scr.shape[1:], F32)

    src_t = lax.broadcasted_iota(jnp.int32, (L, L), 0)
    dst_t = lax.broadcasted_iota(jnp.int32, (L, L), 1)
    tri_fwd = (src_t <= dst_t).astype(BF16)
    tri_rev = (src_t >= dst_t).astype(BF16)

    def gate_vectors(gates, ck):
        r0 = ck * L
        li_rows = gates[:, 0:128].T[0:2 * ML_REV_LANE]
        lf_rows = _log_sigmoid(gates[:, 128:256].T[0:2 * ML_REV_LANE])
        split = jnp.concatenate(_split3(lf_rows), axis=0)
        to_t = jnp.dot(split, tri_fwd, preferred_element_type=F32)
        from_t = jnp.dot(split, tri_rev, preferred_element_type=F32)
        cum_rows = jnp.concatenate([
            to_t[0:8] + to_t[BF16_ROWS:BF16_ROWS + 8] + to_t[2 * BF16_ROWS:2 * BF16_ROWS + 8],
            from_t[8:16] + from_t[BF16_ROWS + 8:2 * BF16_ROWS] + from_t[2 * BF16_ROWS + 8:3 * BF16_ROWS]], axis=0)
        g_rows = li_rows - cum_rows
        grf_ref[ck] = g_rows[0:ML_REV_LANE]
        grr_ref[ck] = g_rows[ML_REV_LANE:2 * ML_REV_LANE]
        pm_rows = _scan_lanes_both(g_rows, jnp.maximum, -jnp.inf)
        pad = jnp.zeros((128 - 2 * ML_REV_LANE, L), F32)
        gv_ref[0, r0:r0 + L, :] = jnp.concatenate([cum_rows, pad], axis=0).T
        gv_ref[1, r0:r0 + L, :] = jnp.concatenate([pm_rows, pad], axis=0).T

    @pl.when(first)
    def _():
        diff = lax.broadcasted_iota(jnp.int32, (DH, DH), 1) - lax.broadcasted_iota(jnp.int32, (DH, DH), 0)

        def dense(d_ref, h):
            diag = d_ref[h * DH:(h + 1) * DH, :]
            out = jnp.zeros((DH, DH), F32)
            for d in range(-3, 4):
                out = jnp.where(diff == d, diag[:, 3 + d:4 + d], out)
            return out

        dot = functools.partial(jnp.dot, preferred_element_type=F32)
        for h in range(HEADS):
            wq = dense(dq_ref, h).astype(BF16)
            wk_f32 = dense(dk_ref, h) * (DH ** -0.5)
            wk = wk_f32.astype(BF16)
            wv = dense(dv_ref, h).astype(BF16)
            wkt_scr[h] = wk_f32.T.astype(BF16)
            wqg_scr[h, :, 0:DH] = wq
            wqg_scr[h, :, DH:2 * DH] = (dot(wq, wg_ref[h]) + dot(wk, wg_ref[HEADS + h])).astype(BF16)
            wv_scr[h, :, 0:DH] = wv
            wv_scr[h, :, DH:2 * DH] = dot(wv, wg_ref[2 * HEADS + h]).astype(BF16)

    if not latent:
        xt_scr[8:16, :] = jnp.zeros((8, D), F32)
        xt_scr[16:16 + n_tok, :] = x_ref[...].astype(F32)
        xt_scr[16 + n_tok:24 + n_tok, :] = jnp.zeros((8, D), F32)
    else:
        for g in range(GRID_W // BF16_ROWS):
            xg = x_ref[g * BF16_ROWS:(g + 1) * BF16_ROWS].reshape(BF16_ROWS * ML_GROUP_COLS, D)
            yg = jnp.dot(perm_ref[...], xg, preferred_element_type=F32)
            for w in range(ML_GROUP_COLS):
                dst = 16 + w * GRID_W + g * BF16_ROWS
                xt_scr[dst:dst + BF16_ROWS, :] = yg[w * BF16_ROWS:(w + 1) * BF16_ROWS]
        last = ML_GROUP_COLS - 1
        prev = jnp.concatenate([xp_ref[0].astype(F32)[last:last + 1], xp_ref[1].astype(F32)[last:last + 1]], axis=0)
        xt_scr[14:16, :] = jnp.where(tile > 0, prev, 0.0)
        xt_scr[16 + n_tok:17 + n_tok, :] = jnp.where(tile < n_tiles - 1, xn_ref[0].astype(F32)[0:1], 0.0)

    for ck in range(n_chunks):
        gate_vectors(g_scr[slot_old, ck * L:(ck + 1) * L, :], ck)

    cw = cw_ref[...]
    for ck in range(n_chunks):
        r0 = ck * L
        xt = xt_scr[16 + r0:16 + r0 + L, :]
        xc = cb_ref[...] + cw[0:1] * xt_scr[14 + r0:14 + r0 + L, :] + cw[1:2] * xt_scr[15 + r0:15 + r0 + L, :]
        xc = xc + cw[2:3] * xt + cw[3:4] * xt_scr[17 + r0:17 + r0 + L, :]
        xc = _silu(xc)
        xc_ref[r0:r0 + L, :] = xc.astype(BF16)
        gates = jnp.zeros((L, 256), F32) + bg_ref[...]
        for h in range(HEADS):
            lo, hi = h * DH, (h + 1) * DH
            xc_h = xc[:, lo:hi].astype(BF16)
            qg = jnp.dot(xc_h, wqg_scr[h], preferred_element_type=F32)
            vg = jnp.dot(xt[:, lo:hi].astype(BF16), wv_scr[h], preferred_element_type=F32)
            gates = gates + qg[:, DH:2 * DH] + vg[:, DH:2 * DH]
            q_ref[r0:r0 + L, lo:hi] = qg[:, 0:DH].astype(BF16)
            kt = lax.dot_general(wkt_scr[h], xc_h, (((1,), (1,)), ((), ())), preferred_element_type=F32)
            kt_ref[lo:hi, r0:r0 + L] = kt.astype(BF16)
            v_ref[r0:r0 + L, lo:hi] = vg[:, 0:DH].astype(BF16)
        g_scr[slot_new, r0:r0 + L, :] = gates


def _ml_rec_kernel(*refs, reverse, n_tok, mode):
    L = ML_CHUNK
    n_chunks = n_tok // L
    lane0 = ML_REV_LANE if reverse else 0
    if mode == "ctx":
        (q_ref, kt_ref, v_ref, gv_ref, gr_ref, c_out, m_out, c_scr, m_scr) = refs
    elif mode == "fwd":
        (q_ref, kt_ref, v_ref, gv_ref, gr_ref, c0_ref, m0_ref, o_ref, c_scr, m_scr, h_scr) = refs
    else:
        (q_ref, kt_ref, v_ref, gv_ref, gr_ref, c0_ref, m0_ref, hf_ref, xc_ref, ng_ref, sk_ref,
         perm_ref, o_ref, c_scr, m_scr, h_scr) = refs

    @pl.when(pl.program_id(1) == 0)
    def _():
        if mode == "ctx":
            c_scr[...] = jnp.zeros_like(c_scr)
            m_scr[...] = jnp.zeros_like(m_scr)
        else:
            c_scr[...] = c0_ref[...]
            m_scr[...] = m0_ref[...]

    ones_lanes = jnp.ones((L, 128), BF16)
    if mode != "ctx":
        row_id = lax.broadcasted_iota(jnp.int32, (L, L), 0)
        col_id = lax.broadcasted_iota(jnp.int32, (L, L), 1)
        keep = (col_id >= row_id) if reverse else (col_id <= row_id)

    def stage_b(ck):
        r0 = ck * L
        m_prev = m_scr[0:1, :]
        edge = 0 if reverse else L - 1
        b_tot = gv_ref[0, r0 + edge:r0 + edge + 1, :]
        gmax = gv_ref[1, r0 + edge:r0 + edge + 1, :]
        m_new = jnp.maximum(b_tot + m_prev, b_tot + gmax)
        dec = jnp.exp(b_tot + m_prev - m_new)
        if mode != "ctx":
            cum = gv_ref[0, r0:r0 + L, :]
            inter = cum + m_prev
            m_t = jnp.maximum(inter, cum + gv_ref[1, r0:r0 + L, :])
            c_all = cum - m_t
            si_all = jnp.exp(inter - m_t)
            fl_all = jnp.exp(-m_t)
            g_row = gr_ref[ck]
        for h in range(HEADS):
            lo, hi = h * DH, (h + 1) * DH
            ln = lane0 + h
            q = q_ref[r0:r0 + L, lo:hi]
            kt = kt_ref[lo:hi, r0:r0 + L]
            v = jnp.concatenate([v_ref[r0:r0 + L, lo:hi], ones_lanes], axis=1)
            dec_h = dec[:, ln:ln + 1]
            c_old = c_scr[h]
            if mode != "ctx":
                si_c = si_all[:, ln:ln + 1]
                s = jnp.dot(q, kt, preferred_element_type=F32)
                arg = jnp.where(keep, c_all[:, ln:ln + 1] + g_row[h:h + 1, :], -jnp.inf)
                p = (s * jnp.exp(arg)).astype(BF16)
                both = jnp.dot(p, v, preferred_element_type=F32)
                both = both + si_c * jnp.dot(q, c_old.astype(BF16), preferred_element_type=F32)
                scale = 1.0 / jnp.maximum(jnp.abs(both[:, DH:DH + 128]), fl_all[:, ln:ln + 1])
                h_scr[r0:r0 + L, lo:hi] = both[:, 0:DH] * jnp.concatenate([scale, scale], axis=1)
            ws_row = jnp.exp((b_tot[:, ln:ln + 1] - m_new[:, ln:ln + 1]) + gr_ref[ck][h:h + 1, :]).astype(BF16)
            kw_t = kt * ws_row
            c_scr[h] = dec_h * c_old + jnp.dot(kw_t, v, preferred_element_type=F32)
        m_scr[...] = jnp.broadcast_to(m_new, m_scr.shape)

    for ci in range(n_chunks):
        stage_b((n_chunks - 1 - ci) if reverse else ci)

    a_rows = min(ML_A_ROWS, n_tok)
    if mode == "ctx":
        c_out[...] = c_scr[...]
        m_out[...] = m_scr[...]
    elif mode == "fwd":
        o_ref[...] = h_scr[...].astype(o_ref.dtype)
    else:
        for sb in range(n_tok // a_rows):
            r0 = sb * a_rows
            ht = h_scr[r0:r0 + a_rows, :] + hf_ref[r0:r0 + a_rows, :].astype(F32)
            for h in range(HEADS):
                lo, hi = h * DH, (h + 1) * DH
                hh = ht[:, lo:hi]
                mu = jnp.mean(hh, axis=-1, keepdims=True)
                var = jnp.mean(jnp.square(hh - mu), axis=-1, keepdims=True)
                hn = (hh - mu) * lax.rsqrt(var + EPS)
                xc = xc_ref[r0:r0 + a_rows, lo:hi].astype(F32)
                h_scr[r0:r0 + a_rows, lo:hi] = hn * ng_ref[:, lo:hi] + sk_ref[:, lo:hi] * xc
        for g in range(GRID_W // BF16_ROWS):
            zg = jnp.concatenate(
                [h_scr[w * GRID_W + g * BF16_ROWS:w * GRID_W + (g + 1) * BF16_ROWS, :] for w in range(ML_GROUP_COLS)],
                axis=0).astype(BF16)
            og = jnp.dot(perm_ref[...], zg, preferred_element_type=F32).astype(o_ref.dtype)
            o_ref[g * BF16_ROWS:(g + 1) * BF16_ROWS] = og.reshape(BF16_ROWS, ML_GROUP_COLS, D)


def _ml_weight_specs():
    return [
        _const_spec((4, D)),
        _const_spec((1, D)),
        _const_spec((D, 8)),
        _const_spec((D, 8)),
        _const_spec((D, 8)),
        _const_spec((3 * HEADS, DH, 256)),
        _const_spec((1, 256)),
    ]


def _ml_prep_call(mlx, weights, *, latent, name):
    scratch = [
        pltpu.VMEM(((ML_TILE if latent else CTX) + 32, D), F32),
        pltpu.VMEM((HEADS, DH, 2 * DH), BF16),
        pltpu.VMEM((HEADS, DH, 2 * DH), BF16),
        pltpu.VMEM((HEADS, DH, DH), BF16),
    ]
    n_tiles, n_tok, seq = (GRID_W // ML_GROUP_COLS, ML_TILE, SEQ) if latent else (1, CTX, CTX)
    n_flat = NB * n_tiles

    def at(lag):
        def split(step):
            flat = jnp.clip(step - lag, 0, n_flat - 1)
            return flat // n_tiles, flat % n_tiles
        return split

    new, old = at(0), at(1)
    if latent:
        in_specs = [
            pl.BlockSpec((None, GRID_W, ML_GROUP_COLS, D), lambda g: (new(g)[0], 0, new(g)[1], 0)),
            pl.BlockSpec((None, 2, ML_GROUP_COLS, D),
                         lambda g: (new(g)[0], GRID_W // 2 - 1, jnp.maximum(new(g)[1] - 1, 0), 0)),
            pl.BlockSpec((None, 1, ML_GROUP_COLS, D),
                         lambda g: (new(g)[0], 0, jnp.minimum(new(g)[1] + 1, n_tiles - 1), 0)),
            _const_spec((BF16_ROWS * ML_GROUP_COLS, BF16_ROWS * ML_GROUP_COLS)),
        ]
        args = [mlx, mlx, mlx, _swap_perm(BF16_ROWS, ML_GROUP_COLS)]
    else:
        in_specs = [pl.BlockSpec((None, CTX, D), lambda g: (new(g)[0], 0, 0))]
        args = [mlx]
    scratch.append(pltpu.VMEM((2, n_tok, 256), F32))
    tok_spec = pl.BlockSpec((None, n_tok, D), lambda g: (new(g)[0], new(g)[1], 0))
    kt_spec = pl.BlockSpec((None, D, n_tok), lambda g: (new(g)[0], 0, new(g)[1]))
    gv_spec = pl.BlockSpec((None, 2, n_tok, 128), lambda g: (old(g)[0], 0, old(g)[1], 0))
    row_spec = pl.BlockSpec((None, n_tok // ML_CHUNK, 8, ML_CHUNK), lambda g: (old(g)[0], old(g)[1], 0, 0))
    tok_shape = jax.ShapeDtypeStruct((NB, seq, D), BF16)
    row_shape = jax.ShapeDtypeStruct((NB, seq // ML_CHUNK, 8, ML_CHUNK), F32)
    return pl.pallas_call(
        functools.partial(_ml_prep_kernel, n_tiles=n_tiles, n_tok=n_tok, latent=latent),
        grid=(n_flat + 1,),
        in_specs=in_specs + _ml_weight_specs(),
        out_specs=[tok_spec, kt_spec, tok_spec, tok_spec, gv_spec, row_spec, row_spec],
        out_shape=[tok_shape, jax.ShapeDtypeStruct((NB, D, seq), BF16), tok_shape, tok_shape,
                   jax.ShapeDtypeStruct((NB, 2, seq, 128), F32), row_shape, row_shape],
        scratch_shapes=scratch,
        compiler_params=_params(("arbitrary",)),
        name=name,
    )(*args, *weights)


_STATE_SHAPES = [
    jax.ShapeDtypeStruct((NB, HEADS, DH, DH + 128), F32),
    jax.ShapeDtypeStruct((NB, 8, 128), F32),
]


def _ml_rec_call(q, kt, v, gv, g_rows, *, reverse, state=None, merge_with=None, name):
    seq = q.shape[1]
    n_tok = min(ML_TILE, seq)
    n_tiles = seq // n_tok
    n_chunks = n_tok // ML_CHUNK
    tile = (lambda j: n_tiles - 1 - j) if reverse else (lambda j: j)
    tok_spec = pl.BlockSpec((None, n_tok, D), lambda b, j: (b, tile(j), 0))
    state_specs = [
        pl.BlockSpec((None, HEADS, DH, DH + 128), lambda b, j: (b, 0, 0, 0)),
        pl.BlockSpec((None, 8, 128), lambda b, j: (b, 0, 0)),
    ]
    in_specs = [
        tok_spec, pl.BlockSpec((None, D, n_tok), lambda b, j: (b, 0, tile(j))), tok_spec,
        pl.BlockSpec((None, 2, n_tok, 128), lambda b, j: (b, 0, tile(j), 0)),
        pl.BlockSpec((None, n_chunks, 8, ML_CHUNK), lambda b, j: (b, tile(j), 0, 0)),
    ]
    args = [q, kt, v, gv, g_rows]
    scratch = [
        pltpu.VMEM((HEADS, DH, DH + 128), F32),
        pltpu.VMEM((8, 128), F32),
    ]
    if state is None:
        mode = "ctx"
        out_specs, out_shape = state_specs, _STATE_SHAPES
    else:
        in_specs += state_specs
        args += list(state)
        scratch.append(pltpu.VMEM((n_tok, D), F32))
        if merge_with is None:
            mode = "fwd"
            out_specs, out_shape = tok_spec, jax.ShapeDtypeStruct((NB, seq, D), BF16)
        else:
            mode = "rev"
            h_fwd, xc, norm_g, skip = merge_with
            in_specs += [tok_spec, tok_spec, _const_spec((1, D)), _const_spec((1, D)),
                         _const_spec((BF16_ROWS * ML_GROUP_COLS, BF16_ROWS * ML_GROUP_COLS))]
            args += [h_fwd, xc, norm_g, skip, _swap_perm(BF16_ROWS, ML_GROUP_COLS)]
            out_specs = pl.BlockSpec((None, GRID_W, ML_GROUP_COLS, D), lambda b, j: (b, 0, tile(j), 0))
            out_shape = jax.ShapeDtypeStruct((NB, GRID_W, GRID_W, D), BF16)
    return pl.pallas_call(
        functools.partial(_ml_rec_kernel, reverse=reverse, n_tok=n_tok, mode=mode),
        grid=(NB, n_tiles),
        in_specs=in_specs,
        out_specs=out_specs,
        out_shape=out_shape,
        scratch_shapes=scratch,
        compiler_params=_params(("arbitrary", "arbitrary")),
        name=name,
    )(*args)


def _final_kernel(x_ref, hrg_ref, grg_ref, hml_ref, smlo_ref, sgr_ref, sgm_ref, g1_ref, sh2_ref, sc2_ref, g2_ref,
                  n2_ref, nf_ref, wbr_ref, wbm_ref, wo_ref, wfi_ref, wfo_ref, o_ref, act_scr):
    dot = functools.partial(jnp.dot, preferred_element_type=F32)
    y_rg = (hrg_ref[...].astype(F32) * grg_ref[...].astype(F32)).astype(BF16)
    y_ml = (hml_ref[...].astype(F32) * smlo_ref[...].astype(F32)).astype(BF16)
    mix = sgr_ref[...].astype(F32) * dot(y_rg, wbr_ref[...])
    mix = mix + sgm_ref[...].astype(F32) * dot(y_ml, wbm_ref[...])
    x1 = x_ref[...] + g1_ref[...] * dot(mix.astype(BF16), wo_ref[...])
    ms = jnp.mean(x1 * x1, axis=-1, keepdims=True)
    hn = x1 * lax.rsqrt(ms + EPS) * n2_ref[...]
    hb = (hn * (1.0 + sc2_ref[...]) + sh2_ref[...]).astype(BF16)
    step = 256
    for c in range(D_FF // step):
        gate = dot(hb, wfi_ref[:, c * step:(c + 1) * step])
        up = dot(hb, wfi_ref[:, D_FF + c * step:D_FF + (c + 1) * step])
        act_scr[:, c * step:(c + 1) * step] = (_silu(gate) * up).astype(BF16)
    x2 = x1 + g2_ref[...] * dot(act_scr[...], wfo_ref[...])
    ms2 = jnp.mean(x2 * x2, axis=-1, keepdims=True)
    o_ref[...] = x2 * lax.rsqrt(ms2 + EPS) * nf_ref[...]


def _final_call(x, h_rg, grg, h_ml, smlo, sgr, sgm, mod3, norm2_g, final_g, wbr, wbm, wo, wfi, wfo):
    rows = FINAL_ROWS
    row_spec = pl.BlockSpec((None, rows, D), lambda b, i: (b, i, 0))
    mod_spec = lambda g: pl.BlockSpec((None, 1, D), lambda b, i: (b, 0, g))
    return pl.pallas_call(
        _final_kernel,
        grid=(NB, SEQ // rows),
        in_specs=[
            row_spec, row_spec, row_spec, row_spec, row_spec, row_spec, row_spec,
            mod_spec(2), mod_spec(3), mod_spec(4), mod_spec(5),
            _const_spec((1, D)), _const_spec((1, D)),
            _const_spec((D, D)), _const_spec((D, D)), _const_spec((D, D)),
            _const_spec((D, 2 * D_FF)), _const_spec((D_FF, D)),
        ],
        out_specs=row_spec,
        out_shape=jax.ShapeDtypeStruct((NB, SEQ, D), F32),
        scratch_shapes=[pltpu.VMEM((rows, D_FF), BF16)],
        compiler_params=_params(("arbitrary", "arbitrary")),
        name="final",
    )(x, h_rg, grg, h_ml, smlo, sgr, sgm, mod3, mod3, mod3, mod3, norm2_g, final_g, wbr, wbm, wo, wfi, wfo)


def _pair_blockdiag(w):
    w = w.reshape(8, 2, 64, 64)
    z = jnp.zeros((8, 64, 64), w.dtype)
    top = jnp.concatenate([w[:, 0], z], axis=2)
    bot = jnp.concatenate([z, w[:, 1]], axis=2)
    return jnp.concatenate([top, bot], axis=1)


def _rg_weights(wa, ba, wx, bx):
    w = jnp.concatenate([_pair_blockdiag(wa), _pair_blockdiag(wx)], axis=2).astype(BF16)
    bias = 0.5 * jnp.concatenate([ba.reshape(8, 1, 128), bx.reshape(8, 1, 128)], axis=2)
    b_hi = bias.astype(BF16)
    b_lo = (bias - b_hi.astype(F32)).astype(BF16)
    return (jnp.concatenate([w, b_hi, b_lo, jnp.zeros((8, 126, 256), BF16)], axis=1),)


def _block_diagonals(w):
    rows = [jnp.pad(w[:, i, :], ((0, 0), (3 - i, 1 + i))) for i in range(4)]
    return jnp.stack(rows, axis=1).reshape(D, 8)


def _gate_weights(wi, bi, wf, bf):
    def lanes(x):
        gap = [(0, 0)] * (x.ndim - 2)
        fwd = jnp.pad(x[0], gap + [(0, ML_REV_LANE - HEADS)])
        rev = jnp.pad(x[1], gap + [(0, 128 - ML_REV_LANE - HEADS)])
        return jnp.concatenate([fwd, rev], axis=-1)
    w = jnp.concatenate([lanes(wi), lanes(wf)], axis=-1)
    b = jnp.concatenate([lanes(bi), lanes(bf)], axis=-1).reshape(1, 256)
    return w.reshape(3 * HEADS, DH, 256).astype(BF16), b


def kernel(x, c, ctx, c_ctx, w_mod, b_mod, norm1_g, norm2_g, w_in, rg_conv_w, rg_conv_b, rg_wa, rg_ba, rg_wx,
           rg_bx, rg_lambda, ml_conv_w, ml_conv_b, ml_wq, ml_wk, ml_wv, ml_wi, ml_bi, ml_wf, ml_bf,
           ml_norm_g, ml_skip, w_branch_rg, w_branch_ml, w_out, w_ffn_in, w_ffn_out, final_norm_g):
    mod = _mod_call(c, c_ctx, w_mod[0], b_mod[0])
    mod3 = mod.reshape(2 * NB, 1, 6 * D)
    norm1 = norm1_g[0].reshape(1, D)

    rgx, grg, mlx, smlo, sgr, sgm = _proj_call(x, mod3, norm1, w_in[0], ctx=False)
    rgx_c, mlx_c = _proj_call(ctx, mod3, norm1, w_in[0], ctx=True)

    rg_cw = rg_conv_w[0]
    rg_cb = rg_conv_b[0].reshape(1, D)
    zero_h = jnp.zeros((NB, D), F32)
    rg_w = [_rg_weights(rg_wa[0, d], rg_ba[0, d], rg_wx[0, d], rg_bx[0, d]) + (rg_lambda[0, d].reshape(1, D),)
            for d in range(2)]
    _, h0_f = _rg_call(rgx_c, zero_h, rg_cw, rg_cb, *rg_w[0], reverse=False, mode="ctx", name="rg_ctx_fwd")
    _, h0_r = _rg_call(rgx_c, zero_h, rg_cw, rg_cb, *rg_w[1], reverse=True, mode="ctx", name="rg_ctx_rev")
    h_f, rg_xh, _ = _rg_call(rgx, h0_f, rg_cw, rg_cb, *rg_w[0], reverse=False, mode="fwd", name="rg_fwd")
    h_rg, _ = _rg_call(rg_xh, h0_r, rg_cw, rg_cb, *rg_w[1], reverse=True, mode="rev", h_fwd=h_f, name="rg_rev")

    ml_cw = ml_conv_w[0]
    ml_cb = ml_conv_b[0].reshape(1, D)
    ml_qkv = (_block_diagonals(ml_wq[0]), _block_diagonals(ml_wk[0]), _block_diagonals(ml_wv[0]))
    ml_w = (ml_cw, ml_cb) + ml_qkv + _gate_weights(ml_wi[0], ml_bi[0], ml_wf[0], ml_bf[0])
    q_c, kt_c, v_c, _, gv_c, grf_c, grr_c = _ml_prep_call(mlx_c, ml_w, latent=False, name="ml_prep_ctx")
    st_f = _ml_rec_call(q_c, kt_c, v_c, gv_c, grf_c, reverse=False, name="ml_ctx_fwd")
    st_r = _ml_rec_call(q_c, kt_c, v_c, gv_c, grr_c, reverse=True, name="ml_ctx_rev")
    q_l, kt_l, v_l, xc_l, gv_l, grf_l, grr_l = _ml_prep_call(
        mlx.reshape(NB, GRID_W, GRID_W, D), ml_w, latent=True, name="ml_prep")
    hm_f = _ml_rec_call(q_l, kt_l, v_l, gv_l, grf_l, reverse=False, state=st_f, name="ml_fwd")
    h_ml = _ml_rec_call(q_l, kt_l, v_l, gv_l, grr_l, reverse=True, state=st_r,
                        merge_with=(hm_f, xc_l, ml_norm_g[0].reshape(1, D), ml_skip[0].reshape(1, D)), name="ml_rev")

    return _final_call(
        x, h_rg, grg, h_ml.reshape(NB, SEQ, D), smlo, sgr, sgm, mod3,
        norm2_g[0].reshape(1, D), final_norm_g.reshape(1, D),
        w_branch_rg[0].astype(BF16), w_branch_ml[0].astype(BF16), w_out[0].astype(BF16),
        w_ffn_in[0].astype(BF16), w_ffn_out[0].astype(BF16))
```

```python
import functools

import jax
import jax.numpy as jnp
from jax import lax
from jax.experimental import pallas as pl
from jax.experimental.pallas import tpu as pltpu

F32 = jnp.float32
BF16 = jnp.bfloat16

D = 1024
NB = 8
SEQ = 4096
GRID_W = 64
CTX = 256
EPS = 1e-6
RG_C = 8.0
HEADS = 4
DH = D // HEADS
D_FF = 2816
N_IN = 6 * D
LOG2E = 1.4426950408889634
F32_TINY = 1.1754944e-38

VMEM_LIMIT = 60 * 1024 * 1024
BF16_ROWS = 16

PROJ_ROWS = 1024
FINAL_ROWS = 512
RG_STEPS = 128
RG_SUB = 256
RG_SUB_STEPS = RG_SUB // NB
RG_COEF = 128
ML_CHUNK = 256
ML_GROUP_COLS = 16
ML_TILE = ML_GROUP_COLS * GRID_W
ML_A_ROWS = 512

assert ML_CHUNK == DH


def _sigmoid(x):
    return 0.5 * (jnp.tanh(0.5 * x) + 1.0)


def _silu(x):
    half = 0.5 * x
    return half * (jnp.tanh(half) + 1.0)


def _softplus(x):
    return jnp.maximum(x, 0.0) + jnp.log(1.0 + jnp.exp(-jnp.abs(x)))


def _log_sigmoid(x):
    return jnp.minimum(x, 0.0) - jnp.log(1.0 + jnp.exp(-jnp.abs(x)))


def _split3(x):
    hi = x.astype(BF16)
    r1 = x - hi.astype(F32)
    mid = r1.astype(BF16)
    lo = (r1 - mid.astype(F32)).astype(BF16)
    return hi, mid, lo


def _params(sem):
    return pltpu.CompilerParams(dimension_semantics=sem, vmem_limit_bytes=VMEM_LIMIT)


def _const_spec(shape):
    nd = len(shape)
    return pl.BlockSpec(shape, lambda *_: (0,) * nd, pipeline_mode=pl.Buffered(1))


def _swap_perm(a, b):
    n = a * b
    out_row = jnp.arange(n)
    src = (out_row % a) * b + out_row // a
    return (src[:, None] == jnp.arange(n)[None, :]).astype(BF16)


def _mod_kernel(c_ref, cc_ref, w_ref, b_ref, o_ref):
    s = jnp.concatenate([c_ref[...], jnp.broadcast_to(cc_ref[...], (NB, D))], axis=0)
    s = _silu(s)
    s_hi, s_mid, _ = _split3(s)
    w_hi, w_mid, _ = _split3(w_ref[...])
    dot = functools.partial(jnp.dot, preferred_element_type=F32)
    o_ref[...] = dot(s_hi, w_hi) + dot(s_mid, w_hi) + dot(s_hi, w_mid) + b_ref[...]


def _mod_call(c, c_ctx, w_mod, b_mod):
    return pl.pallas_call(
        _mod_kernel,
        grid=(6,),
        in_specs=[
            pl.BlockSpec((NB, D), lambda g: (0, 0)),
            pl.BlockSpec((1, D), lambda g: (0, 0)),
            pl.BlockSpec((D, D), lambda g: (0, g)),
            pl.BlockSpec((1, D), lambda g: (0, g)),
        ],
        out_specs=pl.BlockSpec((2 * NB, D), lambda g: (0, g)),
        out_shape=jax.ShapeDtypeStruct((2 * NB, 6 * D), F32),
        compiler_params=_params(("arbitrary",)),
        name="mod",
    )(c, c_ctx.reshape(1, D), w_mod, b_mod.reshape(1, 6 * D))


def _gelu_tanh(x):
    return jax.nn.gelu(x, approximate=True)


def _identity(x):
    return x


_PROJ_FULL = ((0, _identity), (1, _gelu_tanh), (2, _identity), (3, _sigmoid), (4, _sigmoid), (5, _sigmoid))
_PROJ_CTX = ((0, _identity), (2, _identity))


def _proj_kernel(x_ref, sh_ref, sc_ref, g_ref, *refs, acts):
    w_refs, o_refs = refs[:len(acts)], refs[len(acts):]
    x = x_ref[...]
    ms = jnp.mean(x * x, axis=-1, keepdims=True)
    y = x * lax.rsqrt(ms + EPS) * g_ref[...]
    u = (y * (1.0 + sc_ref[...]) + sh_ref[...]).astype(BF16)
    for w_ref, o_ref, act in zip(w_refs, o_refs, acts):
        p = jnp.dot(u, w_ref[...], preferred_element_type=F32)
        o_ref[...] = act(p).astype(o_ref.dtype)


def _proj_call(x, mod3, norm_g, w_in_bf, *, ctx):
    L = x.shape[1]
    rows = min(PROJ_ROWS, L)
    groups = _PROJ_CTX if ctx else _PROJ_FULL
    mod_row = (lambda b: NB) if ctx else (lambda b: b)
    row_spec = pl.BlockSpec((None, rows, D), lambda b, i: (b, i, 0))

    def group_spec(g):
        return pl.BlockSpec((D, D), lambda b, i: (0, g), pipeline_mode=pl.Buffered(1))

    return pl.pallas_call(
        functools.partial(_proj_kernel, acts=tuple(act for _, act in groups)),
        grid=(NB, L // rows),
        in_specs=[
            row_spec,
            pl.BlockSpec((None, 1, D), lambda b, i: (mod_row(b), 0, 0)),
            pl.BlockSpec((None, 1, D), lambda b, i: (mod_row(b), 0, 1)),
            _const_spec((1, D)),
            *[group_spec(g) for g, _ in groups],
        ],
        out_specs=[row_spec] * len(groups),
        out_shape=[jax.ShapeDtypeStruct((NB, L, D), BF16)] * len(groups),
        compiler_params=_params(("arbitrary", "arbitrary")),
        name="proj_ctx" if ctx else "proj",
    )(x, mod3, mod3, norm_g, *[w_in_bf] * len(groups))


def _rg_kernel(*refs, reverse, n_tiles, mode):
    if mode == "rev":
        (xh_ref, h0_ref, wbd_ref, lam_ref, hf_ref, pout_ref, o_ref, hlast_ref, a_scr, b_scr, h_scr) = refs
    elif mode == "fwd":
        (x_ref, xp_ref, xn_ref, h0_ref, cw_ref, cb_ref, wbd_ref, lam_ref, pin_ref, phalo_ref,
         o_ref, xh_out, hlast_ref, xe_scr, a_scr, b_scr, h_scr) = refs
    else:
        (x_ref, xp_ref, xn_ref, h0_ref, cw_ref, cb_ref, wbd_ref, lam_ref, pin_ref, phalo_ref,
         o_ref, hlast_ref, xe_scr, a_scr, b_scr, h_scr) = refs
    step = pl.program_id(0)
    order = jnp.minimum(step, n_tiles - 1)
    tile = (n_tiles - 1 - order) if reverse else order
    new = step % 2
    old = 1 - new
    rows = RG_STEPS * NB
    halo = 2 * NB

    @pl.when(step == 0)
    def _():
        h_scr[...] = h0_ref[...]
        a_scr[1] = jnp.ones((rows, D), F32)
        b_scr[1] = jnp.zeros((rows, D), F32)

    def coef_stage():
        if mode != "rev":
            def halo_rows(ref):
                xb = jnp.concatenate([ref[b] for b in range(NB)], axis=0)
                return jnp.dot(phalo_ref[...], xb, preferred_element_type=F32)

            xe_scr[0:halo, :] = jnp.where(tile > 0, halo_rows(xp_ref)[(BF16_ROWS - 2) * NB:BF16_ROWS * NB], 0.0)
            xe_scr[halo + rows:2 * halo + rows, :] = jnp.where(tile < n_tiles - 1, halo_rows(xn_ref)[0:halo], 0.0)
            for s in range(rows // RG_SUB):
                t0 = s * RG_SUB_STEPS
                xb = jnp.concatenate([x_ref[b, t0:t0 + RG_SUB_STEPS, :] for b in range(NB)], axis=0)
                xe_scr[halo + s * RG_SUB:halo + (s + 1) * RG_SUB, :] = jnp.dot(
                    pin_ref[...], xb, preferred_element_type=F32)
            cw = 0.5 * cw_ref[...]
            cb = 0.5 * cb_ref[...]

        lam2 = (-0.5 * RG_C * LOG2E) * _softplus(-lam_ref[...])

        ones = jnp.ones((RG_COEF, 128), BF16)

        for s in range(rows // RG_COEF):
            r0 = s * RG_COEF
            if mode == "rev":
                xh = xh_ref[r0:r0 + RG_COEF, :].astype(F32)
            else:
                xh = cb + cw[0:1] * xe_scr[r0:r0 + RG_COEF, :]
                xh = xh + cw[1:2] * xe_scr[r0 + NB:r0 + NB + RG_COEF, :]
                xh = xh + cw[2:3] * xe_scr[r0 + 2 * NB:r0 + 2 * NB + RG_COEF, :]
                xh = xh + cw[3:4] * xe_scr[r0 + 3 * NB:r0 + 3 * NB + RG_COEF, :]
            if mode == "fwd":
                xh_out[r0:r0 + RG_COEF, :] = xh.astype(BF16)
            for p in range(D // 128):
                lo, hi = p * 128, (p + 1) * 128
                xhp = xh[:, lo:hi]
                lhs = jnp.concatenate([xhp.astype(BF16), ones], axis=1)
                pre = jnp.dot(lhs, wbd_ref[p], preferred_element_type=F32)
                t_r = jnp.tanh(pre[:, 0:128])
                t_i = jnp.tanh(pre[:, 128:256])
                a = jnp.exp2(lam2[:, lo:hi] * (t_r + 1.0))
                y = jnp.maximum(1.0 - a * a, F32_TINY)
                gain = y * lax.rsqrt(y)
                a_scr[new, r0:r0 + RG_COEF, lo:hi] = a
                b_scr[new, r0:r0 + RG_COEF, lo:hi] = gain * ((t_i + 1.0) * xhp)

    if mode == "rev":
        coef_stage()
    else:
        pl.when(step < n_tiles)(coef_stage)

    h = h_scr[...]
    for t in range(RG_STEPS):
        r0 = ((RG_STEPS - 1 - t) if reverse else t) * NB
        h = a_scr[old, r0:r0 + NB, :] * h + b_scr[old, r0:r0 + NB, :]
        b_scr[old, r0:r0 + NB, :] = h
    h_scr[...] = h
    hlast_ref[...] = h

    if mode == "rev":
        for s in range(rows // RG_SUB):
            r0 = s * RG_SUB
            t0 = s * RG_SUB_STEPS
            hs = (b_scr[old, r0:r0 + RG_SUB, :] + hf_ref[r0:r0 + RG_SUB, :].astype(F32)).astype(BF16)
            hb = jnp.dot(pout_ref[...], hs, preferred_element_type=F32).astype(o_ref.dtype)
            for b in range(NB):
                o_ref[b, t0:t0 + RG_SUB_STEPS, :] = hb[b * RG_SUB_STEPS:(b + 1) * RG_SUB_STEPS]
    else:
        o_ref[...] = b_scr[old].astype(o_ref.dtype)


def _rg_call(src, h0, cw, cb, wbd, lam, *, reverse, mode, h_fwd=None, name):
    L = src.shape[0] // NB if mode == "rev" else src.shape[1]
    rows = RG_STEPS * NB
    n_tiles = L // RG_STEPS
    per = RG_STEPS // BF16_ROWS
    n_halo = L // BF16_ROWS
    def tile_at(lag):
        def tile(i):
            order = jnp.clip(i - lag, 0, n_tiles - 1)
            return (n_tiles - 1 - order) if reverse else order
        return tile

    tile, tile_old = tile_at(0), tile_at(1)
    bm_spec = pl.BlockSpec((NB, RG_STEPS, D), lambda i: (0, tile(i), 0))
    tm_spec = pl.BlockSpec((rows, D), lambda i: (tile(i), 0))
    tm_old_spec = pl.BlockSpec((rows, D), lambda i: (tile_old(i), 0))
    tm_shape = jax.ShapeDtypeStruct((L * NB, D), BF16)
    state_spec = pl.BlockSpec((NB, D), lambda i: (0, 0))
    state_shape = jax.ShapeDtypeStruct((NB, D), F32)
    scratch = [pltpu.VMEM((2, rows, D), F32), pltpu.VMEM((2, rows, D), F32), pltpu.VMEM((NB, D), F32)]
    if mode == "rev":
        in_specs = [tm_spec, _const_spec((NB, D)), _const_spec((D // 128, 256, 256)), _const_spec((1, D)),
                    tm_old_spec, _const_spec((RG_SUB, RG_SUB))]
        args = [src, h0, wbd, lam, h_fwd, _swap_perm(RG_SUB_STEPS, NB)]
        out_specs = [pl.BlockSpec((NB, RG_STEPS, D), lambda i: (0, tile_old(i), 0)), state_spec]
        out_shape = [jax.ShapeDtypeStruct((NB, L, D), BF16), state_shape]
    else:
        in_specs = [
            bm_spec,
            pl.BlockSpec((NB, BF16_ROWS, D), lambda i: (0, jnp.maximum(tile(i) * per - 1, 0), 0)),
            pl.BlockSpec((NB, BF16_ROWS, D), lambda i: (0, jnp.minimum((tile(i) + 1) * per, n_halo - 1), 0)),
            _const_spec((NB, D)),
            _const_spec((4, D)),
            _const_spec((1, D)),
            _const_spec((D // 128, 256, 256)),
            _const_spec((1, D)),
            _const_spec((RG_SUB, RG_SUB)),
            _const_spec((NB * BF16_ROWS, NB * BF16_ROWS)),
        ]
        args = [src, src, src, h0, cw, cb, wbd, lam, _swap_perm(NB, RG_SUB_STEPS), _swap_perm(NB, BF16_ROWS)]
        out_specs = [tm_old_spec] + ([tm_spec] if mode == "fwd" else []) + [state_spec]
        out_shape = [tm_shape] + ([tm_shape] if mode == "fwd" else []) + [state_shape]
        scratch = [pltpu.VMEM((rows + 4 * NB, D), F32)] + scratch
    return pl.pallas_call(
        functools.partial(_rg_kernel, reverse=reverse, n_tiles=n_tiles, mode=mode),
        grid=(n_tiles + 1,),
        in_specs=in_specs,
        out_specs=out_specs,
        out_shape=out_shape,
        scratch_shapes=scratch,
        compiler_params=_params(("arbitrary",)),
        name=name,
    )(*args)


ML_REV_LANE = 8


def _scan_lanes_both(x, op, fill):
    n = x.shape[1]
    lane = lax.broadcasted_iota(jnp.int32, x.shape, 1)
    is_prefix = lax.broadcasted_iota(jnp.int32, x.shape, 0) < ML_REV_LANE
    sh = 1
    while sh < n:
        before = jnp.where(lane >= sh, pltpu.roll(x, sh, axis=1), fill)
        after = jnp.where(lane < n - sh, pltpu.roll(x, n - sh, axis=1), fill)
        x = op(x, jnp.where(is_prefix, before, after))
        sh *= 2
    return x


def _ml_prep_kernel(*refs, n_tiles, n_tok, latent):
    L = ML_CHUNK
    n_chunks = n_tok // L
    if latent:
        (x_ref, xp_ref, xn_ref, perm_ref, cw_ref, cb_ref, dq_ref, dk_ref, dv_ref, wg_ref, bg_ref,
         q_ref, kt_ref, v_ref, xc_ref, gv_ref, grf_ref, grr_ref,
         xt_scr, wqg_scr, wv_scr, wkt_scr, g_scr) = refs
    else:
        (x_ref, cw_ref, cb_ref, dq_ref, dk_ref, dv_ref, wg_ref, bg_ref,
         q_ref, kt_ref, v_ref, xc_ref, gv_ref, grf_ref, grr_ref,
         xt_scr, wqg_scr, wv_scr, wkt_scr, g_scr) = refs
    step = pl.program_id(0)
    tile = jnp.minimum(step, NB * n_tiles - 1) % n_tiles
    first = step == 0
    slot_new = step % 2
    slot_old = 1 - slot_new

    @pl.when(first)
    def _():
        g_scr[1] = jnp.zeros(g_scr.shape[1:], F32)

    src_t = lax.broadcasted_iota(jnp.int32, (L, L), 0)
    dst_t = lax.broadcasted_iota(jnp.int32, (L, L), 1)
    tri_fwd = (src_t <= dst_t).astype(BF16)
    tri_rev = (src_t >= dst_t).astype(BF16)

    def gate_vectors(gates, ck):
        r0 = ck * L
        li_rows = gates[:, 0:128].T[0:2 * ML_REV_LANE]
        lf_rows = _log_sigmoid(gates[:, 128:256].T[0:2 * ML_REV_LANE])
        split = jnp.concatenate(_split3(lf_rows), axis=0)
        to_t = jnp.dot(split, tri_fwd, preferred_element_type=F32)
        from_t = jnp.dot(split, tri_rev, preferred_element_type=F32)
        cum_rows = jnp.concatenate([
            to_t[0:8] + to_t[BF16_ROWS:BF16_ROWS + 8] + to_t[2 * BF16_ROWS:2 * BF16_ROWS + 8],
            from_t[8:16] + from_t[BF16_ROWS + 8:2 * BF16_ROWS] + from_t[2 * BF16_ROWS + 8:3 * BF16_ROWS]], axis=0)
        g_rows = li_rows - cum_rows
        grf_ref[ck] = g_rows[0:ML_REV_LANE]
        grr_ref[ck] = g_rows[ML_REV_LANE:2 * ML_REV_LANE]
        pm_rows = _scan_lanes_both(g_rows, jnp.maximum, -jnp.inf)
        pad = jnp.zeros((128 - 2 * ML_REV_LANE, L), F32)
        gv_ref[0, r0:r0 + L, :] = jnp.concatenate([cum_rows, pad], axis=0).T
        gv_ref[1, r0:r0 + L, :] = jnp.concatenate([pm_rows, pad], axis=0).T

    @pl.when(first)
    def _():
        diff = lax.broadcasted_iota(jnp.int32, (DH, DH), 1) - lax.broadcasted_iota(jnp.int32, (DH, DH), 0)

        def dense(d_ref, h):
            diag = d_ref[h * DH:(h + 1) * DH, :]
            out = jnp.zeros((DH, DH), F32)
            for d in range(-3, 4):
                out = jnp.where(diff == d, diag[:, 3 + d:4 + d], out)
            return out

        dot = functools.partial(jnp.dot, preferred_element_type=F32)
        for h in range(HEADS):
            wq = dense(dq_ref, h).astype(BF16)
            wk_f32 = dense(dk_ref, h) * (DH ** -0.5)
            wk = wk_f32.astype(BF16)
            wv = dense(dv_ref, h).astype(BF16)
            wkt_scr[h] = wk_f32.T.astype(BF16)
            wqg_scr[h, :, 0:DH] = wq
            wqg_scr[h, :, DH:2 * DH] = (dot(wq, wg_ref[h]) + dot(wk, wg_ref[HEADS + h])).astype(BF16)
            wv_scr[h, :, 0:DH] = wv
            wv_scr[h, :, DH:2 * DH] = dot(wv, wg_ref[2 * HEADS + h]).astype(BF16)

    if not latent:
        xt_scr[8:16, :] = jnp.zeros((8, D), F32)
        xt_scr[16:16 + n_tok, :] = x_ref[...].astype(F32)
        xt_scr[16 + n_tok:24 + n_tok, :] = jnp.zeros((8, D), F32)
    else:
        for g in range(GRID_W // BF16_ROWS):
            xg = x_ref[g * BF16_ROWS:(g + 1) * BF16_ROWS].reshape(BF16_ROWS * ML_GROUP_COLS, D)
            yg = jnp.dot(perm_ref[...], xg, preferred_element_type=F32)
            for w in range(ML_GROUP_COLS):
                dst = 16 + w * GRID_W + g * BF16_ROWS
                xt_scr[dst:dst + BF16_ROWS, :] = yg[w * BF16_ROWS:(w + 1) * BF16_ROWS]
        last = ML_GROUP_COLS - 1
        prev = jnp.concatenate([xp_ref[0].astype(F32)[last:last + 1], xp_ref[1].astype(F32)[last:last + 1]], axis=0)
        xt_scr[14:16, :] = jnp.where(tile > 0, prev, 0.0)
        xt_scr[16 + n_tok:17 + n_tok, :] = jnp.where(tile < n_tiles - 1, xn_ref[0].astype(F32)[0:1], 0.0)

    for ck in range(n_chunks):
        gate_vectors(g_scr[slot_old, ck * L:(ck + 1) * L, :], ck)

    cw = cw_ref[...]
    for ck in range(n_chunks):
        r0 = ck * L
        xt = xt_scr[16 + r0:16 + r0 + L, :]
        xc = cb_ref[...] + cw[0:1] * xt_scr[14 + r0:14 + r0 + L, :] + cw[1:2] * xt_scr[15 + r0:15 + r0 + L, :]
        xc = xc + cw[2:3] * xt + cw[3:4] * xt_scr[17 + r0:17 + r0 + L, :]
        xc = _silu(xc)
        xc_ref[r0:r0 + L, :] = xc.astype(BF16)
        gates = jnp.zeros((L, 256), F32) + bg_ref[...]
        for h in range(HEADS):
            lo, hi = h * DH, (h + 1) * DH
            xc_h = xc[:, lo:hi].astype(BF16)
            qg = jnp.dot(xc_h, wqg_scr[h], preferred_element_type=F32)
            vg = jnp.dot(xt[:, lo:hi].astype(BF16), wv_scr[h], preferred_element_type=F32)
            gates = gates + qg[:, DH:2 * DH] + vg[:, DH:2 * DH]
            q_ref[r0:r0 + L, lo:hi] = qg[:, 0:DH].astype(BF16)
            kt = lax.dot_general(wkt_scr[h], xc_h, (((1,), (1,)), ((), ())), preferred_element_type=F32)
            kt_ref[lo:hi, r0:r0 + L] = kt.astype(BF16)
            v_ref[r0:r0 + L, lo:hi] = vg[:, 0:DH].astype(BF16)
        g_scr[slot_new, r0:r0 + L, :] = gates


def _ml_rec_kernel(*refs, reverse, n_tok, mode):
    L = ML_CHUNK
    n_chunks = n_tok // L
    lane0 = ML_REV_LANE if reverse else 0
    if mode == "ctx":
        (q_ref, kt_ref, v_ref, gv_ref, gr_ref, c_out, m_out, c_scr, m_scr) = refs
    elif mode == "fwd":
        (q_ref, kt_ref, v_ref, gv_ref, gr_ref, c0_ref, m0_ref, o_ref, c_scr, m_scr, h_scr) = refs
    else:
        (q_ref, kt_ref, v_ref, gv_ref, gr_ref, c0_ref, m0_ref, hf_ref, xc_ref, ng_ref, sk_ref,
         perm_ref, o_ref, c_scr, m_scr, h_scr) = refs

    @pl.when(pl.program_id(1) == 0)
    def _():
        if mode == "ctx":
            c_scr[...] = jnp.zeros_like(c_scr)
            m_scr[...] = jnp.zeros_like(m_scr)
        else:
            c_scr[...] = c0_ref[...]
            m_scr[...] = m0_ref[...]

    ones_lanes = jnp.ones((L, 128), BF16)
    if mode != "ctx":
        row_id = lax.broadcasted_iota(jnp.int32, (L, L), 0)
        col_id = lax.broadcasted_iota(jnp.int32, (L, L), 1)
        keep = (col_id >= row_id) if reverse else (col_id <= row_id)

    def stage_b(ck):
        r0 = ck * L
        m_prev = m_scr[0:1, :]
        edge = 0 if reverse else L - 1
        b_tot = gv_ref[0, r0 + edge:r0 + edge + 1, :]
        gmax = gv_ref[1, r0 + edge:r0 + edge + 1, :]
        m_new = jnp.maximum(b_tot + m_prev, b_tot + gmax)
        dec = jnp.exp(b_tot + m_prev - m_new)
        if mode != "ctx":
            cum = gv_ref[0, r0:r0 + L, :]
            inter = cum + m_prev
            m_t = jnp.maximum(inter, cum + gv_ref[1, r0:r0 + L, :])
            c_all = cum - m_t
            si_all = jnp.exp(inter - m_t)
            fl_all = jnp.exp(-m_t)
            g_row = gr_ref[ck]
        for h in range(HEADS):
            lo, hi = h * DH, (h + 1) * DH
            ln = lane0 + h
            q = q_ref[r0:r0 + L, lo:hi]
            kt = kt_ref[lo:hi, r0:r0 + L]
            v = jnp.concatenate([v_ref[r0:r0 + L, lo:hi], ones_lanes], axis=1)
            dec_h = dec[:, ln:ln + 1]
            c_old = c_scr[h]
            if mode != "ctx":
                si_c = si_all[:, ln:ln + 1]
                s = jnp.dot(q, kt, preferred_element_type=F32)
                arg = jnp.where(keep, c_all[:, ln:ln + 1] + g_row[h:h + 1, :], -jnp.inf)
                p = (s * jnp.exp(arg)).astype(BF16)
                both = jnp.dot(p, v, preferred_element_type=F32)
                both = both + si_c * jnp.dot(q, c_old.astype(BF16), preferred_element_type=F32)
                scale = 1.0 / jnp.maximum(jnp.abs(both[:, DH:DH + 128]), fl_all[:, ln:ln + 1])
                h_scr[r0:r0 + L, lo:hi] = both[:, 0:DH] * jnp.concatenate([scale, scale], axis=1)
            ws_row = jnp.exp((b_tot[:, ln:ln + 1] - m_new[:, ln:ln + 1]) + gr_ref[ck][h:h + 1, :]).astype(BF16)
            kw_t = kt * ws_row
            c_scr[h] = dec_h * c_old + jnp.dot(kw_t, v, preferred_element_type=F32)
        m_scr[...] = jnp.broadcast_to(m_new, m_scr.shape)

    for ci in range(n_chunks):
        stage_b((n_chunks - 1 - ci) if reverse else ci)

    a_rows = min(ML_A_ROWS, n_tok)
    if mode == "ctx":
        c_out[...] = c_scr[...]
        m_out[...] = m_scr[...]
    elif mode == "fwd":
        o_ref[...] = h_scr[...].astype(o_ref.dtype)
    else:
        for sb in range(n_tok // a_rows):
            r0 = sb * a_rows
            ht = h_scr[r0:r0 + a_rows, :] + hf_ref[r0:r0 + a_rows, :].astype(F32)
            for h in range(HEADS):
                lo, hi = h * DH, (h + 1) * DH
                hh = ht[:, lo:hi]
                mu = jnp.mean(hh, axis=-1, keepdims=True)
                var = jnp.mean(jnp.square(hh - mu), axis=-1, keepdims=True)
                hn = (hh - mu) * lax.rsqrt(var + EPS)
                xc = xc_ref[r0:r0 + a_rows, lo:hi].astype(F32)
                h_scr[r0:r0 + a_rows, lo:hi] = hn * ng_ref[:, lo:hi] + sk_ref[:, lo:hi] * xc
        for g in range(GRID_W // BF16_ROWS):
            zg = jnp.concatenate(
                [h_scr[w * GRID_W + g * BF16_ROWS:w * GRID_W + (g + 1) * BF16_ROWS, :] for w in range(ML_GROUP_COLS)],
                axis=0).astype(BF16)
            og = jnp.dot(perm_ref[...], zg, preferred_element_type=F32).astype(o_ref.dtype)
            o_ref[g * BF16_ROWS:(g + 1) * BF16_ROWS] = og.reshape(BF16_ROWS, ML_GROUP_COLS, D)


def _ml_weight_specs():
    return [
        _const_spec((4, D)),
        _const_spec((1, D)),
        _const_spec((D, 8)),
        _const_spec((D, 8)),
        _const_spec((D, 8)),
        _const_spec((3 * HEADS, DH, 256)),
        _const_spec((1, 256)),
    ]


def _ml_prep_call(mlx, weights, *, latent, name):
    scratch = [
        pltpu.VMEM(((ML_TILE if latent else CTX) + 32, D), F32),
        pltpu.VMEM((HEADS, DH, 2 * DH), BF16),
        pltpu.VMEM((HEADS, DH, 2 * DH), BF16),
        pltpu.VMEM((HEADS, DH, DH), BF16),
    ]
    n_tiles, n_tok, seq = (GRID_W // ML_GROUP_COLS, ML_TILE, SEQ) if latent else (1, CTX, CTX)
    n_flat = NB * n_tiles

    def at(lag):
        def split(step):
            flat = jnp.clip(step - lag, 0, n_flat - 1)
            return flat // n_tiles, flat % n_tiles
        return split

    new, old = at(0), at(1)
    if latent:
        in_specs = [
            pl.BlockSpec((None, GRID_W, ML_GROUP_COLS, D), lambda g: (new(g)[0], 0, new(g)[1], 0)),
            pl.BlockSpec((None, 2, ML_GROUP_COLS, D),
                         lambda g: (new(g)[0], GRID_W // 2 - 1, jnp.maximum(new(g)[1] - 1, 0), 0)),
            pl.BlockSpec((None, 1, ML_GROUP_COLS, D),
                         lambda g: (new(g)[0], 0, jnp.minimum(new(g)[1] + 1, n_tiles - 1), 0)),
            _const_spec((BF16_ROWS * ML_GROUP_COLS, BF16_ROWS * ML_GROUP_COLS)),
        ]
        args = [mlx, mlx, mlx, _swap_perm(BF16_ROWS, ML_GROUP_COLS)]
    else:
        in_specs = [pl.BlockSpec((None, CTX, D), lambda g: (new(g)[0], 0, 0))]
        args = [mlx]
    scratch.append(pltpu.VMEM((2, n_tok, 256), F32))
    tok_spec = pl.BlockSpec((None, n_tok, D), lambda g: (new(g)[0], new(g)[1], 0))
    kt_spec = pl.BlockSpec((None, D, n_tok), lambda g: (new(g)[0], 0, new(g)[1]))
    gv_spec = pl.BlockSpec((None, 2, n_tok, 128), lambda g: (old(g)[0], 0, old(g)[1], 0))
    row_spec = pl.BlockSpec((None, n_tok // ML_CHUNK, 8, ML_CHUNK), lambda g: (old(g)[0], old(g)[1], 0, 0))
    tok_shape = jax.ShapeDtypeStruct((NB, seq, D), BF16)
    row_shape = jax.ShapeDtypeStruct((NB, seq // ML_CHUNK, 8, ML_CHUNK), F32)
    return pl.pallas_call(
        functools.partial(_ml_prep_kernel, n_tiles=n_tiles, n_tok=n_tok, latent=latent),
        grid=(n_flat + 1,),
        in_specs=in_specs + _ml_weight_specs(),
        out_specs=[tok_spec, kt_spec, tok_spec, tok_spec, gv_spec, row_spec, row_spec],
        out_shape=[tok_shape, jax.ShapeDtypeStruct((NB, D, seq), BF16), tok_shape, tok_shape,
                   jax.ShapeDtypeStruct((NB, 2, seq, 128), F32), row_shape, row_shape],
        scratch_shapes=scratch,
        compiler_params=_params(("arbitrary",)),
        name=name,
    )(*args, *weights)


_STATE_SHAPES = [
    jax.ShapeDtypeStruct((NB, HEADS, DH, DH + 128), F32),
    jax.ShapeDtypeStruct((NB, 8, 128), F32),
]


def _ml_ctx_kernel(kt_ref, v_ref, gv_ref, grf_ref, grr_ref, cf_out, mf_out, cr_out, mr_out):
    L = ML_CHUNK
    ones_lanes = jnp.ones((L, 128), BF16)
    for reverse, gr_ref, c_out, m_out in ((False, grf_ref, cf_out, mf_out), (True, grr_ref, cr_out, mr_out)):
        lane0 = ML_REV_LANE if reverse else 0
        edge = 0 if reverse else L - 1
        b_tot = gv_ref[0, edge:edge + 1, :]
        m_new = jnp.maximum(b_tot, b_tot + gv_ref[1, edge:edge + 1, :])
        for h in range(HEADS):
            lo, hi = h * DH, (h + 1) * DH
            ln = lane0 + h
            v = jnp.concatenate([v_ref[:, lo:hi], ones_lanes], axis=1)
            ws_row = jnp.exp((b_tot[:, ln:ln + 1] - m_new[:, ln:ln + 1]) + gr_ref[0][h:h + 1, :]).astype(BF16)
            c_out[h] = jnp.dot(kt_ref[lo:hi, :] * ws_row, v, preferred_element_type=F32)
        m_out[...] = jnp.broadcast_to(m_new, m_out.shape)


def _ml_ctx_call(kt, v, gv, grf, grr):
    assert CTX == ML_CHUNK
    state_specs = [
        pl.BlockSpec((None, HEADS, DH, DH + 128), lambda b: (b, 0, 0, 0)),
        pl.BlockSpec((None, 8, 128), lambda b: (b, 0, 0)),
    ]
    row_spec = pl.BlockSpec((None, 1, 8, ML_CHUNK), lambda b: (b, 0, 0, 0))
    out = pl.pallas_call(
        _ml_ctx_kernel,
        grid=(NB,),
        in_specs=[
            pl.BlockSpec((None, D, CTX), lambda b: (b, 0, 0)),
            pl.BlockSpec((None, CTX, D), lambda b: (b, 0, 0)),
            pl.BlockSpec((None, 2, CTX, 128), lambda b: (b, 0, 0, 0)),
            row_spec, row_spec,
        ],
        out_specs=state_specs * 2,
        out_shape=list(_STATE_SHAPES) * 2,
        compiler_params=_params(("arbitrary",)),
        name="ml_ctx",
    )(kt, v, gv, grf, grr)
    return out[:2], out[2:]


def _ml_rec_call(q, kt, v, gv, g_rows, *, reverse, state=None, merge_with=None, name):
    seq = q.shape[1]
    n_tok = min(ML_TILE, seq)
    n_tiles = seq // n_tok
    n_chunks = n_tok // ML_CHUNK
    tile = (lambda j: n_tiles - 1 - j) if reverse else (lambda j: j)
    tok_spec = pl.BlockSpec((None, n_tok, D), lambda b, j: (b, tile(j), 0))
    state_specs = [
        pl.BlockSpec((None, HEADS, DH, DH + 128), lambda b, j: (b, 0, 0, 0)),
        pl.BlockSpec((None, 8, 128), lambda b, j: (b, 0, 0)),
    ]
    in_specs = [
        tok_spec, pl.BlockSpec((None, D, n_tok), lambda b, j: (b, 0, tile(j))), tok_spec,
        pl.BlockSpec((None, 2, n_tok, 128), lambda b, j: (b, 0, tile(j), 0)),
        pl.BlockSpec((None, n_chunks, 8, ML_CHUNK), lambda b, j: (b, tile(j), 0, 0)),
    ]
    args = [q, kt, v, gv, g_rows]
    scratch = [
        pltpu.VMEM((HEADS, DH, DH + 128), F32),
        pltpu.VMEM((8, 128), F32),
    ]
    if state is None:
        mode = "ctx"
        out_specs, out_shape = state_specs, _STATE_SHAPES
    else:
        in_specs += state_specs
        args += list(state)
        scratch.append(pltpu.VMEM((n_tok, D), F32))
        if merge_with is None:
            mode = "fwd"
            out_specs, out_shape = tok_spec, jax.ShapeDtypeStruct((NB, seq, D), BF16)
        else:
            mode = "rev"
            h_fwd, xc, norm_g, skip = merge_with
            in_specs += [tok_spec, tok_spec, _const_spec((1, D)), _const_spec((1, D)),
                         _const_spec((BF16_ROWS * ML_GROUP_COLS, BF16_ROWS * ML_GROUP_COLS))]
            args += [h_fwd, xc, norm_g, skip, _swap_perm(BF16_ROWS, ML_GROUP_COLS)]
            out_specs = pl.BlockSpec((None, GRID_W, ML_GROUP_COLS, D), lambda b, j: (b, 0, tile(j), 0))
            out_shape = jax.ShapeDtypeStruct((NB, GRID_W, GRID_W, D), BF16)
    return pl.pallas_call(
        functools.partial(_ml_rec_kernel, reverse=reverse, n_tok=n_tok, mode=mode),
        grid=(NB, n_tiles),
        in_specs=in_specs,
        out_specs=out_specs,
        out_shape=out_shape,
        scratch_shapes=scratch,
        compiler_params=_params(("arbitrary", "arbitrary")),
        name=name,
    )(*args)


def _final_kernel(x_ref, hrg_ref, grg_ref, hml_ref, smlo_ref, sgr_ref, sgm_ref, g1_ref, sh2_ref, sc2_ref, g2_ref,
                  n2_ref, nf_ref, wbr_ref, wbm_ref, wo_ref, wfi_ref, wfo_ref, o_ref, act_scr):
    dot = functools.partial(jnp.dot, preferred_element_type=F32)
    y_rg = (hrg_ref[...].astype(F32) * grg_ref[...].astype(F32)).astype(BF16)
    y_ml = (hml_ref[...].astype(F32) * smlo_ref[...].astype(F32)).astype(BF16)
    mix = sgr_ref[...].astype(F32) * dot(y_rg, wbr_ref[...])
    mix = mix + sgm_ref[...].astype(F32) * dot(y_ml, wbm_ref[...])
    x1 = x_ref[...] + g1_ref[...] * dot(mix.astype(BF16), wo_ref[...])
    ms = jnp.mean(x1 * x1, axis=-1, keepdims=True)
    hn = x1 * lax.rsqrt(ms + EPS) * n2_ref[...]
    hb = (hn * (1.0 + sc2_ref[...]) + sh2_ref[...]).astype(BF16)
    step = 256
    for c in range(D_FF // step):
        gate = dot(hb, wfi_ref[:, c * step:(c + 1) * step])
        up = dot(hb, wfi_ref[:, D_FF + c * step:D_FF + (c + 1) * step])
        act_scr[:, c * step:(c + 1) * step] = (_silu(gate) * up).astype(BF16)
    x2 = x1 + g2_ref[...] * dot(act_scr[...], wfo_ref[...])
    ms2 = jnp.mean(x2 * x2, axis=-1, keepdims=True)
    o_ref[...] = x2 * lax.rsqrt(ms2 + EPS) * nf_ref[...]


def _final_call(x, h_rg, grg, h_ml, smlo, sgr, sgm, mod3, norm2_g, final_g, wbr, wbm, wo, wfi, wfo):
    rows = FINAL_ROWS
    row_spec = pl.BlockSpec((None, rows, D), lambda b, i: (b, i, 0))
    mod_spec = lambda g: pl.BlockSpec((None, 1, D), lambda b, i: (b, 0, g))
    return pl.pallas_call(
        _final_kernel,
        grid=(NB, SEQ // rows),
        in_specs=[
            row_spec, row_spec, row_spec, row_spec, row_spec, row_spec, row_spec,
            mod_spec(2), mod_spec(3), mod_spec(4), mod_spec(5),
            _const_spec((1, D)), _const_spec((1, D)),
            _const_spec((D, D)), _const_spec((D, D)), _const_spec((D, D)),
            _const_spec((D, 2 * D_FF)), _const_spec((D_FF, D)),
        ],
        out_specs=row_spec,
        out_shape=jax.ShapeDtypeStruct((NB, SEQ, D), F32),
        scratch_shapes=[pltpu.VMEM((rows, D_FF), BF16)],
        compiler_params=_params(("arbitrary", "arbitrary")),
        name="final",
    )(x, h_rg, grg, h_ml, smlo, sgr, sgm, mod3, mod3, mod3, mod3, norm2_g, final_g, wbr, wbm, wo, wfi, wfo)


def _pair_blockdiag(w):
    w = w.reshape(8, 2, 64, 64)
    z = jnp.zeros((8, 64, 64), w.dtype)
    top = jnp.concatenate([w[:, 0], z], axis=2)
    bot = jnp.concatenate([z, w[:, 1]], axis=2)
    return jnp.concatenate([top, bot], axis=1)


def _rg_weights(wa, ba, wx, bx):
    w = jnp.concatenate([_pair_blockdiag(wa), _pair_blockdiag(wx)], axis=2).astype(BF16)
    bias = 0.5 * jnp.concatenate([ba.reshape(8, 1, 128), bx.reshape(8, 1, 128)], axis=2)
    b_hi = bias.astype(BF16)
    b_lo = (bias - b_hi.astype(F32)).astype(BF16)
    return (jnp.concatenate([w, b_hi, b_lo, jnp.zeros((8, 126, 256), BF16)], axis=1),)


def _block_diagonals(w):
    rows = [jnp.pad(w[:, i, :], ((0, 0), (3 - i, 1 + i))) for i in range(4)]
    return jnp.stack(rows, axis=1).reshape(D, 8)


def _gate_weights(wi, bi, wf, bf):
    def lanes(x):
        gap = [(0, 0)] * (x.ndim - 2)
        fwd = jnp.pad(x[0], gap + [(0, ML_REV_LANE - HEADS)])
        rev = jnp.pad(x[1], gap + [(0, 128 - ML_REV_LANE - HEADS)])
        return jnp.concatenate([fwd, rev], axis=-1)
    w = jnp.concatenate([lanes(wi), lanes(wf)], axis=-1)
    b = jnp.concatenate([lanes(bi), lanes(bf)], axis=-1).reshape(1, 256)
    return w.reshape(3 * HEADS, DH, 256).astype(BF16), b


def kernel(x, c, ctx, c_ctx, w_mod, b_mod, norm1_g, norm2_g, w_in, rg_conv_w, rg_conv_b, rg_wa, rg_ba, rg_wx,
           rg_bx, rg_lambda, ml_conv_w, ml_conv_b, ml_wq, ml_wk, ml_wv, ml_wi, ml_bi, ml_wf, ml_bf,
           ml_norm_g, ml_skip, w_branch_rg, w_branch_ml, w_out, w_ffn_in, w_ffn_out, final_norm_g):
    mod = _mod_call(c, c_ctx, w_mod[0], b_mod[0])
    mod3 = mod.reshape(2 * NB, 1, 6 * D)
    w_in_bf = w_in[0].astype(BF16)
    norm1 = norm1_g[0].reshape(1, D)

    rgx, grg, mlx, smlo, sgr, sgm = _proj_call(x, mod3, norm1, w_in_bf, ctx=False)
    rgx_c, mlx_c = _proj_call(ctx, mod3, norm1, w_in_bf, ctx=True)

    rg_cw = rg_conv_w[0]
    rg_cb = rg_conv_b[0].reshape(1, D)
    zero_h = jnp.zeros((NB, D), F32)
    rg_w = [_rg_weights(rg_wa[0, d], rg_ba[0, d], rg_wx[0, d], rg_bx[0, d]) + (rg_lambda[0, d].reshape(1, D),)
            for d in range(2)]
    _, h0_f = _rg_call(rgx_c, zero_h, rg_cw, rg_cb, *rg_w[0], reverse=False, mode="ctx", name="rg_ctx_fwd")
    _, h0_r = _rg_call(rgx_c, zero_h, rg_cw, rg_cb, *rg_w[1], reverse=True, mode="ctx", name="rg_ctx_rev")
    h_f, rg_xh, _ = _rg_call(rgx, h0_f, rg_cw, rg_cb, *rg_w[0], reverse=False, mode="fwd", name="rg_fwd")
    h_rg, _ = _rg_call(rg_xh, h0_r, rg_cw, rg_cb, *rg_w[1], reverse=True, mode="rev", h_fwd=h_f, name="rg_rev")

    ml_cw = ml_conv_w[0]
    ml_cb = ml_conv_b[0].reshape(1, D)
    ml_qkv = (_block_diagonals(ml_wq[0]), _block_diagonals(ml_wk[0]), _block_diagonals(ml_wv[0]))
    ml_w = (ml_cw, ml_cb) + ml_qkv + _gate_weights(ml_wi[0], ml_bi[0], ml_wf[0], ml_bf[0])
    q_c, kt_c, v_c, _, gv_c, grf_c, grr_c = _ml_prep_call(mlx_c, ml_w, latent=False, name="ml_prep_ctx")
    st_f, st_r = _ml_ctx_call(kt_c, v_c, gv_c, grf_c, grr_c)
    q_l, kt_l, v_l, xc_l, gv_l, grf_l, grr_l = _ml_prep_call(
        mlx.reshape(NB, GRID_W, GRID_W, D), ml_w, latent=True, name="ml_prep")
    hm_f = _ml_rec_call(q_l, kt_l, v_l, gv_l, grf_l, reverse=False, state=st_f, name="ml_fwd")
    h_ml = _ml_rec_call(q_l, kt_l, v_l, gv_l, grr_l, reverse=True, state=st_r,
                        merge_with=(hm_f, xc_l, ml_norm_g[0].reshape(1, D), ml_skip[0].reshape(1, D)), name="ml_rev")

    return _final_call(
        x, h_rg, grg, h_ml.reshape(NB, SEQ, D), smlo, sgr, sgm, mod3,
        norm2_g[0].reshape(1, D), final_norm_g.reshape(1, D),
        w_branch_rg[0].astype(BF16), w_branch_ml[0].astype(BF16), w_out[0].astype(BF16),
        w_ffn_in[0].astype(BF16), w_ffn_out[0].astype(BF16))
```
